```python
import jax, jax.numpy as jnp
from jax import lax
import numpy as np

D_MODEL = 4096
BATCH = 8
SEQ = 4096
DEPTH = 1

H_A = 16
DH_A = 128
H_B = 32
HKV_B = 4
G_B = H_B // HKV_B
DH_B = 64
WINDOW = 128
NUM_BUCKETS = 32
MAX_DISTANCE = 128
BLOCK = 128
D_FF = ((8 * D_MODEL // 3 + 255) // 256) * 256
EPS = 1e-6

W_QA = H_A * DH_A
W_KA = H_A * DH_A
W_VA = H_A * DH_A
W_FA = H_A
W_QB = H_B * DH_B
W_KB = HKV_B * DH_B
W_VB = HKV_B * DH_B
W_GA = D_MODEL
W_GB = D_MODEL
W_IN = W_QA + W_KA + W_VA + W_FA + W_QB + W_KB + W_VB + W_GA + W_GB

kernel_name = "fox_swa_sink_gated_hybrid_block"


def rms_norm(x, g):
    xf = x.astype(jnp.float32)
    y = xf * lax.rsqrt(jnp.mean(xf * xf, axis=-1, keepdims=True) + EPS)
    return (y * g.astype(jnp.float32)).astype(x.dtype)


def t5_bucket(dist):
    max_exact = NUM_BUCKETS // 2
    small = dist < max_exact
    large = max_exact + (np.log(np.maximum(dist, 1) / max_exact) / np.log(MAX_DISTANCE / max_exact)
                         * (NUM_BUCKETS - max_exact)).astype(np.int64)
    large = np.minimum(large, NUM_BUCKETS - 1)
    return np.where(small, dist, large)


def band_geometry(n_blocks):
    ql = np.arange(BLOCK)[:, None]
    kl = np.arange(2 * BLOCK)[None, :]
    dist = ql + BLOCK - kl
    in_window = (dist >= 0) & (dist < WINDOW)
    key_global = np.arange(n_blocks)[:, None, None] * BLOCK - BLOCK + kl[None]
    mask = in_window[None] & (key_global >= 0)
    bucket = t5_bucket(np.clip(dist, 0, None))
    return jnp.asarray(mask), jnp.asarray(bucket.astype(np.int32))


def forgetting_attention(q, k, v, f_logit):
    B, S, H, D = q.shape
    n_blocks = S // BLOCK
    log_f = jax.nn.log_sigmoid(f_logit.astype(jnp.float32))
    c = lax.cumsum(log_f, axis=1).transpose(0, 2, 1)
    key_pos = jnp.arange(S)
    scale = D ** -0.5

    def one_block(i):
        start = i * BLOCK
        q_blk = lax.dynamic_slice_in_dim(q, start, BLOCK, axis=1)
        c_q = lax.dynamic_slice_in_dim(c, start, BLOCK, axis=2)
        s = jnp.einsum('bqhd,bkhd->bhqk', q_blk, k, preferred_element_type=jnp.float32) * scale
        s = s + c_q[..., None] - c[:, :, None, :]
        q_pos = start + jnp.arange(BLOCK)
        causal = key_pos[None, :] <= q_pos[:, None]
        p = jax.nn.softmax(jnp.where(causal, s, -jnp.inf), axis=-1)
        return jnp.einsum('bhqk,bkhd->bqhd', p.astype(v.dtype), v)

    out = lax.map(one_block, jnp.arange(n_blocks))
    return out.transpose(1, 0, 2, 3, 4).reshape(B, S, H * D)


def sliding_window_sink_attention(q, k, v, sinks, rel_bias):
    B, S = q.shape[:2]
    n_blocks = S // BLOCK
    mask, bucket = band_geometry(n_blocks)
    bias = rel_bias.astype(jnp.float32)[bucket].transpose(2, 0, 1).reshape(HKV_B, G_B, BLOCK, 2 * BLOCK)
    qb = q.reshape(B, n_blocks, BLOCK, HKV_B, G_B, DH_B)

    def band(t):
        tp = jnp.pad(t, ((0, 0), (BLOCK, 0), (0, 0), (0, 0))).reshape(B, n_blocks + 1, BLOCK, HKV_B, DH_B)
        return jnp.concatenate([tp[:, :-1], tp[:, 1:]], axis=2)

    kb, vb = band(k), band(v)
    s = jnp.einsum('bnqhgd,bnkhd->bnhgqk', qb, kb, preferred_element_type=jnp.float32) * (DH_B ** -0.5)
    s = jnp.where(mask[None, :, None, None], s + bias, -jnp.inf)
    sink = sinks.astype(jnp.float32).reshape(1, 1, HKV_B, G_B, 1, 1)
    m = jnp.maximum(jnp.max(s, axis=-1, keepdims=True), sink)
    p = jnp.exp(s - m)
    p = p / (jnp.sum(p, axis=-1, keepdims=True) + jnp.exp(sink - m))
    o = jnp.einsum('bnhgqk,bnkhd->bnqhgd', p.astype(v.dtype), vb)
    return o.reshape(B, S, H_B * DH_B)


def _fwd_setup_inputs(seed: int = 0) -> dict:
    key = jax.random.key(seed)
    ks = jax.random.split(key, 16)
    f32 = jnp.float32

    def w(k, shape, fan_in):
        return jax.random.normal(k, shape, f32) * (fan_in ** -0.5)

    return {
        "x": jax.random.normal(ks[0], (BATCH, SEQ, D_MODEL), f32),
        "norm1_g": 1.0 + 0.02 * jax.random.normal(ks[1], (DEPTH, D_MODEL), f32),
        "w_in": w(ks[2], (DEPTH, D_MODEL, W_IN), D_MODEL),
        "b_forget": 0.1 * jax.random.normal(ks[3], (DEPTH, H_A), f32),
        "attn_sinks": 0.5 * jax.random.normal(ks[4], (DEPTH, H_B), f32),
        "rel_bias": 0.1 * jax.random.normal(ks[5], (NUM_BUCKETS, H_B), f32),
        "w_branch_a": w(ks[6], (DEPTH, H_A * DH_A, D_MODEL), H_A * DH_A),
        "w_branch_b": w(ks[7], (DEPTH, H_B * DH_B, D_MODEL), H_B * DH_B),
        "w_out": w(ks[8], (DEPTH, D_MODEL, D_MODEL), D_MODEL),
        "norm2_g": 1.0 + 0.02 * jax.random.normal(ks[9], (DEPTH, D_MODEL), f32),
        "w_ffn_gate": w(ks[10], (DEPTH, D_MODEL, D_FF), D_MODEL),
        "w_ffn_up": w(ks[11], (DEPTH, D_MODEL, D_FF), D_MODEL),
        "w_ffn_down": w(ks[12], (DEPTH, D_FF, D_MODEL), D_FF),
        "final_g": 1.0 + 0.02 * jax.random.normal(ks[13], (D_MODEL,), f32),
    }


def _fwd_reference(x, norm1_g, w_in, b_forget, attn_sinks, rel_bias, w_branch_a, w_branch_b,
              w_out, norm2_g, w_ffn_gate, w_ffn_up, w_ffn_down, final_g):
    B, S, _ = x.shape
    split_at = list(np.cumsum([W_QA, W_KA, W_VA, W_FA, W_QB, W_KB, W_VB, W_GA])[:])
    for l in range(DEPTH):
        h = rms_norm(x, norm1_g[l])
        proj = jnp.einsum('bsd,dn->bsn', h, w_in[l])
        qa, ka, va, fa, qb, kb, vb, ga, gb = jnp.split(proj, split_at, axis=-1)
        qa = qa.reshape(B, S, H_A, DH_A)
        ka = ka.reshape(B, S, H_A, DH_A)
        va = va.reshape(B, S, H_A, DH_A)
        fa = fa + b_forget[l]
        qb = qb.reshape(B, S, HKV_B, G_B, DH_B)
        kb = kb.reshape(B, S, HKV_B, DH_B)
        vb = vb.reshape(B, S, HKV_B, DH_B)

        ya = jnp.einsum('bsc,cd->bsd', forgetting_attention(qa, ka, va, fa), w_branch_a[l])
        yb = jnp.einsum('bsc,cd->bsd', sliding_window_sink_attention(qb, kb, vb, attn_sinks[l], rel_bias),
                        w_branch_b[l])
        mixed = jax.nn.sigmoid(ga) * ya + jax.nn.sigmoid(gb) * yb
        x = x + jnp.einsum('bsd,de->bse', mixed, w_out[l])

        h = rms_norm(x, norm2_g[l])
        hidden = jax.nn.silu(jnp.einsum('bsd,df->bsf', h, w_ffn_gate[l])) * jnp.einsum('bsd,df->bsf', h, w_ffn_up[l])
        x = x + jnp.einsum('bsf,fd->bsd', hidden, w_ffn_down[l])
    return rms_norm(x, final_g)


import jax as _jax
import jax.numpy as _jnp

TWIN_FORMAT = 'train_step'
FWD_PARAMS = ['x', 'norm1_g', 'w_in', 'b_forget', 'attn_sinks', 'rel_bias', 'w_branch_a', 'w_branch_b', 'w_out', 'norm2_g', 'w_ffn_gate', 'w_ffn_up', 'w_ffn_down', 'final_g']
TWIN_WEIGHTS = ['norm1_g', 'w_in', 'b_forget', 'attn_sinks', 'rel_bias', 'w_branch_a', 'w_branch_b', 'w_out', 'norm2_g', 'w_ffn_gate', 'w_ffn_up', 'w_ffn_down', 'final_g']
TWIN_DIFF_INPUT = 'x'
TWIN_INPUTS = ['x', 'norm1_g', 'w_in', 'b_forget', 'attn_sinks', 'rel_bias', 'w_branch_a', 'w_branch_b', 'w_out', 'norm2_g', 'w_ffn_gate', 'w_ffn_up', 'w_ffn_down', 'final_g', 'loss_target', 'm_norm1_g', 'm_w_in', 'm_b_forget', 'm_attn_sinks', 'm_rel_bias', 'm_w_branch_a', 'm_w_branch_b', 'm_w_out', 'm_norm2_g', 'm_w_ffn_gate', 'm_w_ffn_up', 'm_w_ffn_down', 'm_final_g', 'v_norm1_g', 'v_w_in', 'v_b_forget', 'v_attn_sinks', 'v_rel_bias', 'v_w_branch_a', 'v_w_branch_b', 'v_w_out', 'v_norm2_g', 'v_w_ffn_gate', 'v_w_ffn_up', 'v_w_ffn_down', 'v_final_g']
TWIN_OUTPUTS = ['loss', 'grad_x', 'grad_norm1_g', 'grad_w_in', 'grad_b_forget', 'grad_attn_sinks', 'grad_rel_bias', 'grad_w_branch_a', 'grad_w_branch_b', 'grad_w_out', 'grad_norm2_g', 'grad_w_ffn_gate', 'grad_w_ffn_up', 'grad_w_ffn_down', 'grad_final_g', 'delta_norm1_g', 'delta_w_in', 'delta_b_forget', 'delta_attn_sinks', 'delta_rel_bias', 'delta_w_branch_a', 'delta_w_branch_b', 'delta_w_out', 'delta_norm2_g', 'delta_w_ffn_gate', 'delta_w_ffn_up', 'delta_w_ffn_down', 'delta_final_g', 'new_m_norm1_g', 'new_m_w_in', 'new_m_b_forget', 'new_m_attn_sinks', 'new_m_rel_bias', 'new_m_w_branch_a', 'new_m_w_branch_b', 'new_m_w_out', 'new_m_norm2_g', 'new_m_w_ffn_gate', 'new_m_w_ffn_up', 'new_m_w_ffn_down', 'new_m_final_g', 'new_v_norm1_g', 'new_v_w_in', 'new_v_b_forget', 'new_v_attn_sinks', 'new_v_rel_bias', 'new_v_w_branch_a', 'new_v_w_branch_b', 'new_v_w_out', 'new_v_norm2_g', 'new_v_w_ffn_gate', 'new_v_w_ffn_up', 'new_v_w_ffn_down', 'new_v_final_g']
TWIN_LEAF_KINDS = {'loss': 'loss', 'grad_x': 'grad_x', 'grad_norm1_g': 'grad_w', 'grad_w_in': 'grad_w', 'grad_b_forget': 'grad_w', 'grad_attn_sinks': 'grad_w', 'grad_rel_bias': 'grad_w', 'grad_w_branch_a': 'grad_w', 'grad_w_branch_b': 'grad_w', 'grad_w_out': 'grad_w', 'grad_norm2_g': 'grad_w', 'grad_w_ffn_gate': 'grad_w', 'grad_w_ffn_up': 'grad_w', 'grad_w_ffn_down': 'grad_w', 'grad_final_g': 'grad_w', 'delta_norm1_g': 'delta_w', 'delta_w_in': 'delta_w', 'delta_b_forget': 'delta_w', 'delta_attn_sinks': 'delta_w', 'delta_rel_bias': 'delta_w', 'delta_w_branch_a': 'delta_w', 'delta_w_branch_b': 'delta_w', 'delta_w_out': 'delta_w', 'delta_norm2_g': 'delta_w', 'delta_w_ffn_gate': 'delta_w', 'delta_w_ffn_up': 'delta_w', 'delta_w_ffn_down': 'delta_w', 'delta_final_g': 'delta_w', 'new_m_norm1_g': 'new_m', 'new_m_w_in': 'new_m', 'new_m_b_forget': 'new_m', 'new_m_attn_sinks': 'new_m', 'new_m_rel_bias': 'new_m', 'new_m_w_branch_a': 'new_m', 'new_m_w_branch_b': 'new_m', 'new_m_w_out': 'new_m', 'new_m_norm2_g': 'new_m', 'new_m_w_ffn_gate': 'new_m', 'new_m_w_ffn_up': 'new_m', 'new_m_w_ffn_down': 'new_m', 'new_m_final_g': 'new_m', 'new_v_norm1_g': 'new_v', 'new_v_w_in': 'new_v', 'new_v_b_forget': 'new_v', 'new_v_attn_sinks': 'new_v', 'new_v_rel_bias': 'new_v', 'new_v_w_branch_a': 'new_v', 'new_v_w_branch_b': 'new_v', 'new_v_w_out': 'new_v', 'new_v_norm2_g': 'new_v', 'new_v_w_ffn_gate': 'new_v', 'new_v_w_ffn_up': 'new_v', 'new_v_w_ffn_down': 'new_v', 'new_v_final_g': 'new_v'}


def _forward(args):
    return _fwd_reference(*[args[k] for k in FWD_PARAMS])


def _output_shape():
    out = _jax.eval_shape(lambda: _forward(_fwd_setup_inputs(0)))
    return out.shape, out.dtype

N_MICROBATCH = 1
ADAM_LR = 0.001
ADAM_B1 = 0.9
ADAM_B2 = 0.999
ADAM_EPS = 1e-08
ADAM_WD = 0.01
ADAM_STEP = 10
PER_EXAMPLE_BATCH_AXIS = {'x': 0, 'loss_target': 0}
SHARED_INPUTS = []
_WEIGHT_DTYPES = {'norm1_g': _jnp.float32, 'w_in': _jnp.float32, 'b_forget': _jnp.float32, 'attn_sinks': _jnp.float32, 'rel_bias': _jnp.float32, 'w_branch_a': _jnp.float32, 'w_branch_b': _jnp.float32, 'w_out': _jnp.float32, 'norm2_g': _jnp.float32, 'w_ffn_gate': _jnp.float32, 'w_ffn_up': _jnp.float32, 'w_ffn_down': _jnp.float32, 'final_g': _jnp.float32}
MOMENT_SCALE = {'norm1_g': 2.384929e-02, 'w_in': 1.182981e-02, 'b_forget': 1.274657e-01, 'attn_sinks': 4.855037e-03, 'rel_bias': 8.495917e-03, 'w_branch_a': 1.677375e-02, 'w_branch_b': 5.094402e-03, 'w_out': 1.751371e-02, 'norm2_g': 3.103920e-02, 'w_ffn_gate': 1.366147e-02, 'w_ffn_up': 1.321435e-02, 'w_ffn_down': 2.170083e-02, 'final_g': 7.988618e+00}


def _to_microbatches(a, axis):
    t = _jnp.moveaxis(a, axis, 0)
    t = t.reshape((N_MICROBATCH, t.shape[0] // N_MICROBATCH) + t.shape[1:])
    return _jnp.moveaxis(t, 1, axis + 1)


def setup_inputs(seed: int = 0) -> dict:
    inp = _fwd_setup_inputs(seed)
    key = _jax.random.fold_in(_jax.random.key(seed), 7919)
    shape, _ = _output_shape()
    out = dict(inp)
    out["loss_target"] = _jax.random.normal(_jax.random.fold_in(key, 0), shape, _jnp.float32)
    for i, name in enumerate(TWIN_WEIGHTS):
        w = inp[name].astype(_jnp.float32)
        if MOMENT_SCALE is None:
            s = _jnp.sqrt(_jnp.mean(_jnp.square(w)) + 1e-30)
        else:
            s = MOMENT_SCALE[name]
        km, kv = _jax.random.split(_jax.random.fold_in(key, i + 1))
        out[name] = w
        out["m_" + name] = s * _jax.random.normal(km, w.shape, _jnp.float32)
        out["v_" + name] = (s * s) * _jax.random.uniform(kv, w.shape, _jnp.float32, 0.5, 1.5)
    if N_MICROBATCH > 1:
        for name, axis in PER_EXAMPLE_BATCH_AXIS.items():
            out[name] = _to_microbatches(out[name], axis)
    return {'x': out['x'], 'norm1_g': out['norm1_g'], 'w_in': out['w_in'], 'b_forget': out['b_forget'], 'attn_sinks': out['attn_sinks'], 'rel_bias': out['rel_bias'], 'w_branch_a': out['w_branch_a'], 'w_branch_b': out['w_branch_b'], 'w_out': out['w_out'], 'norm2_g': out['norm2_g'], 'w_ffn_gate': out['w_ffn_gate'], 'w_ffn_up': out['w_ffn_up'], 'w_ffn_down': out['w_ffn_down'], 'final_g': out['final_g'], 'loss_target': out['loss_target'], 'm_norm1_g': out['m_norm1_g'], 'm_w_in': out['m_w_in'], 'm_b_forget': out['m_b_forget'], 'm_attn_sinks': out['m_attn_sinks'], 'm_rel_bias': out['m_rel_bias'], 'm_w_branch_a': out['m_w_branch_a'], 'm_w_branch_b': out['m_w_branch_b'], 'm_w_out': out['m_w_out'], 'm_norm2_g': out['m_norm2_g'], 'm_w_ffn_gate': out['m_w_ffn_gate'], 'm_w_ffn_up': out['m_w_ffn_up'], 'm_w_ffn_down': out['m_w_ffn_down'], 'm_final_g': out['m_final_g'], 'v_norm1_g': out['v_norm1_g'], 'v_w_in': out['v_w_in'], 'v_b_forget': out['v_b_forget'], 'v_attn_sinks': out['v_attn_sinks'], 'v_rel_bias': out['v_rel_bias'], 'v_w_branch_a': out['v_w_branch_a'], 'v_w_branch_b': out['v_w_branch_b'], 'v_w_out': out['v_w_out'], 'v_norm2_g': out['v_norm2_g'], 'v_w_ffn_gate': out['v_w_ffn_gate'], 'v_w_ffn_up': out['v_w_ffn_up'], 'v_w_ffn_down': out['v_w_ffn_down'], 'v_final_g': out['v_final_g']}


def _loss(weights, diff, rest, loss_target):
    with _jax.named_scope("forward"):
        args = {**rest, TWIN_DIFF_INPUT: diff, **{k: w.astype(_WEIGHT_DTYPES[k]) for k, w in weights.items()}}
        y = _forward(args)
    with _jax.named_scope("loss_head"):
        err = _jnp.square(y.astype(_jnp.float32) - loss_target)
        return 0.5 * _jnp.sum(_jnp.mean(err, axis=-1)) if err.ndim else 0.5 * err


def _adamw(w, g, m, v):
    m = ADAM_B1 * m + (1.0 - ADAM_B1) * g
    v = ADAM_B2 * v + (1.0 - ADAM_B2) * _jnp.square(g)
    m_hat = m / (1.0 - ADAM_B1 ** ADAM_STEP)
    v_hat = v / (1.0 - ADAM_B2 ** ADAM_STEP)
    delta = -ADAM_LR * (m_hat / (_jnp.sqrt(v_hat) + ADAM_EPS) + ADAM_WD * w)
    return delta, m, v


def reference(x, norm1_g, w_in, b_forget, attn_sinks, rel_bias, w_branch_a, w_branch_b, w_out, norm2_g, w_ffn_gate, w_ffn_up, w_ffn_down, final_g, loss_target, m_norm1_g, m_w_in, m_b_forget, m_attn_sinks, m_rel_bias, m_w_branch_a, m_w_branch_b, m_w_out, m_norm2_g, m_w_ffn_gate, m_w_ffn_up, m_w_ffn_down, m_final_g, v_norm1_g, v_w_in, v_b_forget, v_attn_sinks, v_rel_bias, v_w_branch_a, v_w_branch_b, v_w_out, v_norm2_g, v_w_ffn_gate, v_w_ffn_up, v_w_ffn_down, v_final_g):
    given = dict(x=x, norm1_g=norm1_g, w_in=w_in, b_forget=b_forget, attn_sinks=attn_sinks, rel_bias=rel_bias, w_branch_a=w_branch_a, w_branch_b=w_branch_b, w_out=w_out, norm2_g=norm2_g, w_ffn_gate=w_ffn_gate, w_ffn_up=w_ffn_up, w_ffn_down=w_ffn_down, final_g=final_g, loss_target=loss_target, m_norm1_g=m_norm1_g, m_w_in=m_w_in, m_b_forget=m_b_forget, m_attn_sinks=m_attn_sinks, m_rel_bias=m_rel_bias, m_w_branch_a=m_w_branch_a, m_w_branch_b=m_w_branch_b, m_w_out=m_w_out, m_norm2_g=m_norm2_g, m_w_ffn_gate=m_w_ffn_gate, m_w_ffn_up=m_w_ffn_up, m_w_ffn_down=m_w_ffn_down, m_final_g=m_final_g, v_norm1_g=v_norm1_g, v_w_in=v_w_in, v_b_forget=v_b_forget, v_attn_sinks=v_attn_sinks, v_rel_bias=v_rel_bias, v_w_branch_a=v_w_branch_a, v_w_branch_b=v_w_branch_b, v_w_out=v_w_out, v_norm2_g=v_norm2_g, v_w_ffn_gate=v_w_ffn_gate, v_w_ffn_up=v_w_ffn_up, v_w_ffn_down=v_w_ffn_down, v_final_g=v_final_g)
    weights = {n: given[n] for n in TWIN_WEIGHTS}
    shared = {n: given[n] for n in SHARED_INPUTS}
    per_example = {n: given[n] for n in ['x']}
    grad_fn = _jax.value_and_grad(_loss, argnums=(0, 1))

    def one_microbatch(ex, loss_target):
        ex = dict(ex)
        diff = ex.pop(TWIN_DIFF_INPUT)
        return grad_fn(weights, diff, {**shared, **ex}, loss_target)

    if N_MICROBATCH == 1:
        loss, (grad_w, grad_x) = one_microbatch(per_example, given["loss_target"])
    else:
        def body(carry, xs):
            loss_sum, grad_sum = carry
            l_k, (gw_k, gx_k) = one_microbatch(xs[0], xs[1])
            with _jax.named_scope("update"):
                return (loss_sum + l_k, _jax.tree.map(_jnp.add, grad_sum, gw_k)), gx_k

        init = (_jnp.zeros((), _jnp.float32), _jax.tree.map(_jnp.zeros_like, weights))
        (loss, grad_w), grad_x = _jax.lax.scan(body, init, (per_example, given["loss_target"]))
    with _jax.named_scope("update"):
        delta_w, new_m, new_v = {}, {}, {}
        for n in TWIN_WEIGHTS:
            delta_w[n], new_m[n], new_v[n] = _adamw(weights[n], grad_w[n], given["m_" + n], given["v_" + n])
    return (loss, grad_x, *[grad_w[n] for n in TWIN_WEIGHTS], *[delta_w[n] for n in TWIN_WEIGHTS],
            *[new_m[n] for n in TWIN_WEIGHTS], *[new_v[n] for n in TWIN_WEIGHTS])
```

```python
import functools

import numpy as np
import jax
import jax.numpy as jnp
from jax import lax
from jax.experimental import pallas as pl
from jax.experimental.pallas import tpu as pltpu

N_DEV = 8
MESH_AXES = ("x", "y", "c")
DH_A = 128
DH_B = 64
SWA_BLOCK = 128
NUM_BUCKETS = 32
MAX_DISTANCE = 128
EPS = 1e-6
ADAM_LR = 0.001
ADAM_B1 = 0.9
ADAM_B2 = 0.999
ADAM_EPS = 1e-08
ADAM_WD = 0.01
ADAM_STEP = 10
NEG = -1e30
LANES = 128
BF16_TILE_ROWS = 16
V7X_VMEM_LIMIT = 56 * 1024 * 1024
F32 = jnp.float32
BF16 = jnp.bfloat16
MESH = pl.DeviceIdType.MESH
ANY = pl.BlockSpec(memory_space=pl.ANY)


def _round_up(n, m):
    return (n + m - 1) // m * m


def _pick(n, target, mult):
    best = None
    for d in range(mult, min(n, target) + 1, mult):
        if n % d == 0:
            best = d
    return n if best is None else best


def _pcall(kernel, *, name, dims=None, **kw):
    return pl.pallas_call(
        kernel, name=name,
        compiler_params=pltpu.CompilerParams(dimension_semantics=dims, vmem_limit_bytes=V7X_VMEM_LIMIT),
        **kw)


def _comm_call(kernel, *, name, **kw):
    return pl.pallas_call(kernel, name=name, **kw)


def _sigmoid(v):
    return 1.0 / (1.0 + jnp.exp(-v))


_DIMS = {"nn": (((1,), (0,)), ((), ())), "nt": (((1,), (1,)), ((), ())), "tn": (((0,), (0,)), ((), ()))}


def _dot(a, b, form="nn"):
    return lax.dot_general(a, b, _DIMS[form], preferred_element_type=F32)


def _matmul(pairs, form, *, bm, bn, tk=None, out_dtype=BF16, addend=None, name):
    a0, b0 = pairs[0]
    if form == "tn":
        K, M = a0.shape
    else:
        M, K = a0.shape
    N = b0.shape[0] if form == "nt" else b0.shape[1]
    tk = K if tk is None else tk
    bm, bn = min(bm, M), min(bn, N)
    assert M % bm == 0 and N % bn == 0 and K % tk == 0, (name, M, N, K, bm, bn, tk)
    nk = K // tk
    n_pairs = len(pairs)
    has_add = addend is not None
    if form == "tn":
        a_spec = pl.BlockSpec((tk, bm), lambda i, j, k: (k, i))
    else:
        a_spec = pl.BlockSpec((bm, tk), lambda i, j, k: (i, k))
    if form == "nt":
        b_spec = pl.BlockSpec((bn, tk), lambda i, j, k: (j, k))
    else:
        b_spec = pl.BlockSpec((tk, bn), lambda i, j, k: (k, j))
    o_spec = pl.BlockSpec((bm, bn), lambda i, j, k: (i, j))

    def kernel(*refs):
        ab = refs[:2 * n_pairs]
        c_ref = refs[2 * n_pairs] if has_add else None
        o_ref = refs[2 * n_pairs + has_add]

        def partial_sum():
            acc = _dot(ab[0][...], ab[1][...], form)
            for p in range(1, n_pairs):
                acc = acc + _dot(ab[2 * p][...], ab[2 * p + 1][...], form)
            return acc

        def finish(acc):
            if has_add:
                acc = acc + c_ref[...].astype(F32)
            o_ref[...] = acc.astype(o_ref.dtype)

        if nk == 1:
            finish(partial_sum())
        else:
            acc_ref = refs[-1]
            k = pl.program_id(2)

            @pl.when(k == 0)
            def _():
                acc_ref[...] = jnp.zeros_like(acc_ref)

            acc_ref[...] += partial_sum()

            @pl.when(k == nk - 1)
            def _():
                finish(acc_ref[...])

    operands, in_specs = [], []
    for a, b in pairs:
        operands += [a, b]
        in_specs += [a_spec, b_spec]
    if has_add:
        operands.append(addend)
        in_specs.append(o_spec)
    return _pcall(
        kernel, name=name, dims=("parallel", "parallel", "arbitrary"),
        out_shape=jax.ShapeDtypeStruct((M, N), out_dtype),
        grid=(M // bm, N // bn, nk), in_specs=in_specs, out_specs=o_spec,
        scratch_shapes=[pltpu.VMEM((bm, bn), F32)] if nk > 1 else [],
    )(*operands)


def _rms_fwd(x, g, *, name):
    T, D = x.shape
    br = _pick(T, 256, 8)

    def kernel(x_ref, g_ref, h_ref, r_ref):
        xv = x_ref[...]
        rstd = lax.rsqrt(jnp.mean(xv * xv, axis=-1, keepdims=True) + EPS)
        h_ref[...] = (xv * rstd * g_ref[...]).astype(BF16)
        r_ref[...] = rstd

    return _pcall(
        kernel, name=name, dims=("parallel",),
        out_shape=(jax.ShapeDtypeStruct((T, D), BF16), jax.ShapeDtypeStruct((T, 1), F32)),
        grid=(T // br,),
        in_specs=[pl.BlockSpec((br, D), lambda i: (i, 0)), pl.BlockSpec((1, D), lambda i: (0, 0))],
        out_specs=(pl.BlockSpec((br, D), lambda i: (i, 0)), pl.BlockSpec((br, 1), lambda i: (i, 0))),
    )(x, g)


def _rms_bwd(dres, dh, x, rstd, g, *, name):
    T, D = x.shape
    br = _pick(T, 256, 8)

    def kernel(dres_ref, dh_ref, x_ref, r_ref, g_ref, dx_ref, gg_ref):
        @pl.when(pl.program_id(0) == 0)
        def _():
            gg_ref[...] = jnp.zeros_like(gg_ref)

        rstd_v = r_ref[...]
        xhat = x_ref[...] * rstd_v
        dhv = dh_ref[...].astype(F32)
        dxhat = dhv * g_ref[...]
        proj = jnp.mean(dxhat * xhat, axis=-1, keepdims=True)
        dx_ref[...] = dres_ref[...] + rstd_v * (dxhat - xhat * proj)
        gg_ref[...] += jnp.sum(dhv * xhat, axis=0, keepdims=True)

    row = pl.BlockSpec((br, D), lambda i: (i, 0))
    vec = pl.BlockSpec((1, D), lambda i: (0, 0))
    return _pcall(
        kernel, name=name, dims=("arbitrary",),
        out_shape=(jax.ShapeDtypeStruct((T, D), F32), jax.ShapeDtypeStruct((1, D), F32)),
        grid=(T // br,),
        in_specs=[row, row, row, pl.BlockSpec((br, 1), lambda i: (i, 0)), vec],
        out_specs=(row, vec),
    )(dres, dh, x, rstd, g)


def _final_loss(x2, gf, target, *, name):
    T, D = x2.shape
    br = _pick(T, 256, 8)

    def kernel(x_ref, g_ref, t_ref, dx_ref, loss_ref, gg_ref):
        @pl.when(pl.program_id(0) == 0)
        def _():
            gg_ref[...] = jnp.zeros_like(gg_ref)
            loss_ref[...] = jnp.zeros_like(loss_ref)

        xv = x_ref[...]
        gv = g_ref[...]
        rstd = lax.rsqrt(jnp.mean(xv * xv, axis=-1, keepdims=True) + EPS)
        xhat = xv * rstd
        err = xhat * gv - t_ref[...]
        loss_ref[...] += 0.5 * jnp.sum(jnp.mean(err * err, axis=-1, keepdims=True), axis=0, keepdims=True)
        dy = err * (1.0 / D)
        dxhat = dy * gv
        proj = jnp.mean(dxhat * xhat, axis=-1, keepdims=True)
        dx_ref[...] = rstd * (dxhat - xhat * proj)
        gg_ref[...] += jnp.sum(dy * xhat, axis=0, keepdims=True)

    row = pl.BlockSpec((br, D), lambda i: (i, 0))
    vec = pl.BlockSpec((1, D), lambda i: (0, 0))
    return _pcall(
        kernel, name=name, dims=("arbitrary",),
        out_shape=(jax.ShapeDtypeStruct((T, D), F32), jax.ShapeDtypeStruct((1, 1), F32),
                   jax.ShapeDtypeStruct((1, D), F32)),
        grid=(T // br,),
        in_specs=[row, vec, row],
        out_specs=(row, pl.BlockSpec((1, 1), lambda i: (0, 0)), vec),
    )(x2, gf, target)


def _scan_rows(v, reverse):
    n = v.shape[0]
    row = lax.broadcasted_iota(jnp.int32, v.shape, 0)
    s = 1
    while s < n:
        if reverse:
            v = v + jnp.where(row < n - s, pltpu.roll(v, n - s, 0), 0.0)
        else:
            v = v + jnp.where(row >= s, pltpu.roll(v, s, 0), 0.0)
        s *= 2
    return v


def _fox_gate_fwd(f, bias, *, name):
    T = f.shape[0]
    tb = _pick(T, 512, 8)

    def kernel(f_ref, b_ref, c_ref, s_ref, carry):
        @pl.when(pl.program_id(0) == 0)
        def _():
            carry[...] = jnp.zeros_like(carry)

        fa = f_ref[...] + b_ref[...]
        z = jnp.exp(-jnp.abs(fa))
        log_f = jnp.minimum(fa, 0.0) - jnp.log(1.0 + z)
        s_ref[...] = jnp.where(fa >= 0, z, 1.0) / (1.0 + z)
        c = _scan_rows(log_f, False) + carry[...]
        c_ref[...] = c
        carry[...] = c[tb - 1:tb, :]

    blk = pl.BlockSpec((tb, LANES), lambda i: (i, 0))
    return _pcall(
        kernel, name=name, dims=("arbitrary",),
        out_shape=(jax.ShapeDtypeStruct((T, LANES), F32), jax.ShapeDtypeStruct((T, LANES), F32)),
        grid=(T // tb,),
        in_specs=[blk, pl.BlockSpec((1, LANES), lambda i: (0, 0))],
        out_specs=(blk, blk),
        scratch_shapes=[pltpu.VMEM((1, LANES), F32)],
    )(f, bias)


def _fox_gate_bwd(dc, sneg, *, name):
    T = dc.shape[0]
    tb = _pick(T, 512, 8)
    nb = T // tb

    def kernel(dc_ref, s_ref, df_ref, gb_ref, carry):
        @pl.when(pl.program_id(0) == 0)
        def _():
            carry[...] = jnp.zeros_like(carry)
            gb_ref[...] = jnp.zeros_like(gb_ref)

        dlog = _scan_rows(dc_ref[...], True) + carry[...]
        carry[...] = dlog[0:1, :]
        dfa = dlog * s_ref[...]
        df_ref[...] = dfa.astype(BF16)
        gb_ref[...] += jnp.sum(dfa, axis=0, keepdims=True)

    blk = pl.BlockSpec((tb, LANES), lambda i: (nb - 1 - i, 0))
    return _pcall(
        kernel, name=name, dims=("arbitrary",),
        out_shape=(jax.ShapeDtypeStruct((T, LANES), BF16), jax.ShapeDtypeStruct((1, LANES), F32)),
        grid=(nb,),
        in_specs=[blk, blk],
        out_specs=(blk, pl.BlockSpec((1, LANES), lambda i: (0, 0))),
        scratch_shapes=[pltpu.VMEM((1, LANES), F32)],
    )(dc, sneg)


def _fox_scores(q, k, cq, ck, i, j, tq, tk):
    s = _dot(q, k, "nt") * (DH_A ** -0.5) + cq - ck
    qpos = i * tq + lax.broadcasted_iota(jnp.int32, (tq, tk), 0)
    kpos = j * tk + lax.broadcasted_iota(jnp.int32, (tq, tk), 1)
    return s, kpos <= qpos


def _fox_fwd(proj, c_col, c_row, n_heads, *, name):
    T = proj.shape[0]
    H = n_heads
    tq = tk = _pick(T, 512, 128)
    nq = T // tq

    def kernel(q_ref, k_ref, v_ref, cq_ref, ck_ref, o_ref, lse_ref, m_sc, l_sc, acc_sc):
        i, j = pl.program_id(1), pl.program_id(2)

        @pl.when(j == 0)
        def _():
            m_sc[...] = jnp.full_like(m_sc, NEG)
            l_sc[...] = jnp.zeros_like(l_sc)
            acc_sc[...] = jnp.zeros_like(acc_sc)

        @pl.when(j <= i)
        def _():
            s, visible = _fox_scores(q_ref[...], k_ref[...], cq_ref[...], ck_ref[...], i, j, tq, tk)
            s = jnp.where(visible, s, NEG)
            m_prev = m_sc[...]
            m_new = jnp.maximum(m_prev, jnp.max(s, axis=-1, keepdims=True))
            alpha = jnp.exp(m_prev - m_new)
            p = jnp.where(visible, jnp.exp(s - m_new), 0.0)
            l_sc[...] = alpha * l_sc[...] + jnp.sum(p, axis=-1, keepdims=True)
            acc_sc[...] = alpha * acc_sc[...] + _dot(p.astype(BF16), v_ref[...])
            m_sc[...] = m_new

        @pl.when(j == i)
        def _():
            l = l_sc[...]
            o_ref[...] = (acc_sc[...] / l).astype(BF16)
            lse_ref[...] = m_sc[...] + jnp.log(l)

    return _pcall(
        kernel, name=name, dims=("parallel", "parallel", "arbitrary"),
        out_shape=(jax.ShapeDtypeStruct((T, H * DH_A), BF16), jax.ShapeDtypeStruct((H, T, 1), F32)),
        grid=(H, nq, nq),
        in_specs=[
            pl.BlockSpec((tq, DH_A), lambda h, i, j: (i, h)),
            pl.BlockSpec((tk, DH_A), lambda h, i, j: (jnp.minimum(j, i), H + h)),
            pl.BlockSpec((tk, DH_A), lambda h, i, j: (jnp.minimum(j, i), 2 * H + h)),
            pl.BlockSpec((None, tq, 1), lambda h, i, j: (h, i, 0)),
            pl.BlockSpec((None, 1, tk), lambda h, i, j: (h, 0, jnp.minimum(j, i))),
        ],
        out_specs=(pl.BlockSpec((tq, DH_A), lambda h, i, j: (i, h)),
                   pl.BlockSpec((None, tq, 1), lambda h, i, j: (h, i, 0))),
        scratch_shapes=[pltpu.VMEM((tq, 1), F32), pltpu.VMEM((tq, 1), F32), pltpu.VMEM((tq, DH_A), F32)],
    )(proj, proj, proj, c_col, c_row)


def _rowdot(a, b, n_heads, *, name):
    T = a.shape[0]
    tq = _pick(T, 512, 8)

    def kernel(a_ref, b_ref, o_ref):
        o_ref[...] = jnp.sum(a_ref[...].astype(F32) * b_ref[...].astype(F32), axis=-1, keepdims=True)

    blk = pl.BlockSpec((tq, DH_A), lambda h, i: (i, h))
    return _pcall(
        kernel, name=name, dims=("parallel", "parallel"),
        out_shape=jax.ShapeDtypeStruct((n_heads, T, 1), F32),
        grid=(n_heads, T // tq), in_specs=[blk, blk],
        out_specs=pl.BlockSpec((None, tq, 1), lambda h, i: (h, i, 0)),
    )(a, b)


def _fox_bwd(proj, do, lse, delta, c_col, c_row, n_heads, *, name):
    T = proj.shape[0]
    H = n_heads
    tq = tk = _pick(T, 512, 128)
    nq = T // tq
    scale = DH_A ** -0.5

    def kernel(q_ref, k_ref, v_ref, do_ref, lse_ref, dl_ref, cq_ref, ck_ref,
               dq_ref, dk_ref, dv_ref, dc_ref, dcq_ref, dk_sc, dv_sc, dc_sc):
        j, i = pl.program_id(1), pl.program_id(2)

        @pl.when((j == 0) & (i == 0))
        def _():
            dq_ref[...] = jnp.zeros_like(dq_ref)
            dcq_ref[...] = jnp.zeros_like(dcq_ref)

        @pl.when(i == 0)
        def _():
            dk_sc[...] = jnp.zeros_like(dk_sc)
            dv_sc[...] = jnp.zeros_like(dv_sc)
            dc_sc[...] = jnp.zeros_like(dc_sc)

        @pl.when(i >= j)
        def _():
            q, k, v, dov = q_ref[...], k_ref[...], v_ref[...], do_ref[...]
            s, visible = _fox_scores(q, k, cq_ref[...], ck_ref[...], i, j, tq, tk)
            p = jnp.where(visible, jnp.exp(s - lse_ref[...]), 0.0)
            dv_sc[...] += _dot(p.astype(BF16), dov, "tn")
            dp = _dot(dov, v, "nt")
            ds = p * (dp - dl_ref[...])
            dc_sc[...] -= jnp.sum(ds, axis=0, keepdims=True)
            dsb = ds.astype(BF16)
            dk_sc[...] += _dot(dsb, q, "tn") * scale
            rows = pl.ds(pl.multiple_of(i * tq, tq), tq)
            dq_ref[rows, :] += _dot(dsb, k) * scale
            dcq_ref[rows, :] += jnp.sum(ds, axis=1, keepdims=True)

        @pl.when(i == nq - 1)
        def _():
            dk_ref[...] = dk_sc[...].astype(BF16)
            dv_ref[...] = dv_sc[...].astype(BF16)
            dc_ref[...] = dc_sc[...]

    qi = lambda j, i: jnp.maximum(i, j)
    return _pcall(
        kernel, name=name, dims=("parallel", "arbitrary", "arbitrary"),
        out_shape=(jax.ShapeDtypeStruct((T, H * DH_A), F32), jax.ShapeDtypeStruct((T, H * DH_A), BF16),
                   jax.ShapeDtypeStruct((T, H * DH_A), BF16), jax.ShapeDtypeStruct((H, 1, T), F32),
                   jax.ShapeDtypeStruct((H, T, 1), F32)),
        grid=(H, nq, nq),
        in_specs=[
            pl.BlockSpec((tq, DH_A), lambda h, j, i: (qi(j, i), h)),
            pl.BlockSpec((tk, DH_A), lambda h, j, i: (j, H + h)),
            pl.BlockSpec((tk, DH_A), lambda h, j, i: (j, 2 * H + h)),
            pl.BlockSpec((tq, DH_A), lambda h, j, i: (qi(j, i), h)),
            pl.BlockSpec((None, tq, 1), lambda h, j, i: (h, qi(j, i), 0)),
            pl.BlockSpec((None, tq, 1), lambda h, j, i: (h, qi(j, i), 0)),
            pl.BlockSpec((None, tq, 1), lambda h, j, i: (h, qi(j, i), 0)),
            pl.BlockSpec((None, 1, tk), lambda h, j, i: (h, 0, j)),
        ],
        out_specs=(pl.BlockSpec((T, DH_A), lambda h, j, i: (0, h)),
                   pl.BlockSpec((tk, DH_A), lambda h, j, i: (j, h)),
                   pl.BlockSpec((tk, DH_A), lambda h, j, i: (j, h)),
                   pl.BlockSpec((None, 1, tk), lambda h, j, i: (h, 0, j)),
                   pl.BlockSpec((None, T, 1), lambda h, j, i: (h, 0, 0))),
        scratch_shapes=[pltpu.VMEM((tk, DH_A), F32), pltpu.VMEM((tk, DH_A), F32), pltpu.VMEM((1, tk), F32)],
    )(proj, proj, proj, do, lse, delta, c_col, c_row)


def _t5_bucket(dist):
    max_exact = NUM_BUCKETS // 2
    small = dist < max_exact
    large = max_exact + (np.log(np.maximum(dist, 1) / max_exact) / np.log(MAX_DISTANCE / max_exact)
                         * (NUM_BUCKETS - max_exact)).astype(np.int64)
    large = np.minimum(large, NUM_BUCKETS - 1)
    return np.where(small, dist, large)


def _bucket_onehot():
    ql = np.arange(SWA_BLOCK)[:, None]
    kl = np.arange(2 * SWA_BLOCK)[None, :]
    bucket = _t5_bucket(np.clip(ql + SWA_BLOCK - kl, 0, None))
    bucket = np.stack([bucket[:, :SWA_BLOCK], bucket[:, SWA_BLOCK:]], axis=0)
    onehot = (bucket.reshape(1, -1) == np.arange(NUM_BUCKETS)[:, None])
    return jnp.asarray(onehot, dtype=BF16)


def _split3(v):
    hi = v.astype(BF16)
    r1 = v - hi.astype(F32)
    mid = r1.astype(BF16)
    lo = (r1 - mid.astype(F32)).astype(BF16)
    return hi, mid, lo


def _bias_table(rel_bias, onehot, *, name):
    B, H = rel_bias.shape
    n = onehot.shape[1]
    bc = _pick(n, 8192, LANES)

    def kernel(rb_ref, oh_ref, o_ref):
        hi, mid, lo = _split3(rb_ref[...])
        oh = oh_ref[...]
        o_ref[...] = _dot(hi, oh, "tn") + _dot(mid, oh, "tn") + _dot(lo, oh, "tn")

    return _pcall(
        kernel, name=name, dims=("parallel",),
        out_shape=jax.ShapeDtypeStruct((H, n), F32), grid=(n // bc,),
        in_specs=[pl.BlockSpec((B, H), lambda i: (0, 0)), pl.BlockSpec((B, bc), lambda i: (0, i))],
        out_specs=pl.BlockSpec((H, bc), lambda i: (0, i)),
    )(rel_bias, onehot)


def _bias_table_bwd(dbias, onehot, *, name):
    H, n = dbias.shape
    B = onehot.shape[0]
    bc = _pick(n, 8192, LANES)

    def kernel(db_ref, oh_ref, o_ref):
        @pl.when(pl.program_id(0) == 0)
        def _():
            o_ref[...] = jnp.zeros_like(o_ref)

        hi, mid, lo = _split3(db_ref[...])
        oh = oh_ref[...]
        o_ref[...] += _dot(oh, hi, "nt") + _dot(oh, mid, "nt") + _dot(oh, lo, "nt")

    return _pcall(
        kernel, name=name, dims=("arbitrary",),
        out_shape=jax.ShapeDtypeStruct((B, H), F32), grid=(n // bc,),
        in_specs=[pl.BlockSpec((H, bc), lambda i: (0, i)), pl.BlockSpec((B, bc), lambda i: (0, i))],
        out_specs=pl.BlockSpec((B, H), lambda i: (0, 0)),
    )(dbias, onehot)


def _swa_probs(q2, kp, kc, bias_ref, lse, n, G):
    Q = SWA_BLOCK
    scale = DH_B ** -0.5
    sp = (_dot(q2, kp, "nt") * scale).reshape(G, Q, Q) + bias_ref[:, 0]
    sc = (_dot(q2, kc, "nt") * scale).reshape(G, Q, Q) + bias_ref[:, 1]
    row = lax.broadcasted_iota(jnp.int32, (G, Q, Q), 1)
    col = lax.broadcasted_iota(jnp.int32, (G, Q, Q), 2)
    vis_p = (col > row) & (n > 0)
    vis_c = col <= row
    sp = jnp.where(vis_p, sp, NEG)
    sc = jnp.where(vis_c, sc, NEG)
    if lse is None:
        return sp, sc, vis_p, vis_c
    pp = jnp.where(vis_p, jnp.exp(sp - lse), 0.0)
    pc = jnp.where(vis_c, jnp.exp(sc - lse), 0.0)
    return pp, pc


def _swa_fwd(q, k, v, bias, sinks, *, name):
    HKV, G, T, _ = q.shape
    Q = SWA_BLOCK
    N = T // Q

    def kernel(q_ref, kp_ref, kc_ref, vp_ref, vc_ref, b_ref, s_ref, o_ref, lse_ref):
        n = pl.program_id(1)
        q2 = q_ref[...].reshape(G * Q, DH_B)
        sp, sc, vis_p, vis_c = _swa_probs(q2, kp_ref[...], kc_ref[...], b_ref, None, n, G)
        sink = s_ref[...][:, :, 0:1]
        m = jnp.maximum(jnp.maximum(jnp.max(sp, axis=-1, keepdims=True), jnp.max(sc, axis=-1, keepdims=True)), sink)
        pp = jnp.where(vis_p, jnp.exp(sp - m), 0.0)
        pc = jnp.where(vis_c, jnp.exp(sc - m), 0.0)
        denom = jnp.sum(pp, axis=-1, keepdims=True) + jnp.sum(pc, axis=-1, keepdims=True) + jnp.exp(sink - m)
        o = _dot(pp.reshape(G * Q, Q).astype(BF16), vp_ref[...]) + _dot(pc.reshape(G * Q, Q).astype(BF16), vc_ref[...])
        o_ref[...] = (o.reshape(G, Q, DH_B) / denom).astype(BF16)
        lse_ref[...] = m + jnp.log(denom)

    prev = lambda h, n: (h, jnp.maximum(n - 1, 0), 0)
    cur = lambda h, n: (h, n, 0)
    kv = lambda f: pl.BlockSpec((None, Q, DH_B), f)
    return _pcall(
        kernel, name=name, dims=("parallel", "parallel"),
        out_shape=(jax.ShapeDtypeStruct((HKV, G, T, DH_B), BF16), jax.ShapeDtypeStruct((HKV, G, T, 1), F32)),
        grid=(HKV, N),
        in_specs=[pl.BlockSpec((None, G, Q, DH_B), lambda h, n: (h, 0, n, 0)),
                  kv(prev), kv(cur), kv(prev), kv(cur),
                  pl.BlockSpec((None, G, 2, Q, Q), lambda h, n: (h, 0, 0, 0, 0)),
                  pl.BlockSpec((None, G, 1, LANES), lambda h, n: (h, 0, 0, 0))],
        out_specs=(pl.BlockSpec((None, G, Q, DH_B), lambda h, n: (h, 0, n, 0)),
                   pl.BlockSpec((None, G, Q, 1), lambda h, n: (h, 0, n, 0))),
    )(q, k, k, v, v, bias, sinks)


def _swa_bwd(q, k, v, o, do, lse, bias, sinks, *, name):
    HKV, G, T, _ = q.shape
    Q = SWA_BLOCK
    N = T // Q
    scale = DH_B ** -0.5

    def kernel(q_ref, kp_ref, kc_ref, vp_ref, vc_ref, o_ref, do_ref, lse_ref, b_ref, s_ref,
               dq_ref, dk_ref, dv_ref, db_ref, ds_ref, ck_sc, cv_sc):
        n = pl.program_id(1)

        @pl.when(n == 0)
        def _():
            db_ref[...] = jnp.zeros_like(db_ref)
            ds_ref[...] = jnp.zeros_like(ds_ref)
            ck_sc[...] = jnp.zeros_like(ck_sc)
            cv_sc[...] = jnp.zeros_like(cv_sc)

        @pl.when(n < N)
        def _():
            q2 = q_ref[...].reshape(G * Q, DH_B)
            do3 = do_ref[...]
            do2 = do3.reshape(G * Q, DH_B)
            kp, kc, vp, vc = kp_ref[...], kc_ref[...], vp_ref[...], vc_ref[...]
            lse_v = lse_ref[...]
            pp, pc = _swa_probs(q2, kp, kc, b_ref, lse_v, n, G)
            delta = jnp.sum(do3.astype(F32) * o_ref[...].astype(F32), axis=-1, keepdims=True)
            dsp = pp * (_dot(do2, vp, "nt").reshape(G, Q, Q) - delta)
            dsc = pc * (_dot(do2, vc, "nt").reshape(G, Q, Q) - delta)
            p_sink = jnp.exp(s_ref[...][:, :, 0:1] - lse_v)
            ds_ref[...] += jnp.broadcast_to(-jnp.sum(p_sink * delta, axis=1, keepdims=True), (G, 1, LANES))
            db_ref[:, 0] += dsp
            db_ref[:, 1] += dsc
            dsp2 = dsp.reshape(G * Q, Q).astype(BF16)
            dsc2 = dsc.reshape(G * Q, Q).astype(BF16)
            dq = (_dot(dsp2, kp) + _dot(dsc2, kc)) * scale
            dq_ref[...] = dq.reshape(G, Q, DH_B).astype(BF16)
            pp2 = pp.reshape(G * Q, Q).astype(BF16)
            pc2 = pc.reshape(G * Q, Q).astype(BF16)
            dk_ref[...] = (ck_sc[...] + _dot(dsp2, q2, "tn") * scale).astype(BF16)
            dv_ref[...] = (cv_sc[...] + _dot(pp2, do2, "tn")).astype(BF16)
            ck_sc[...] = _dot(dsc2, q2, "tn") * scale
            cv_sc[...] = _dot(pc2, do2, "tn")

        @pl.when(n == N)
        def _():
            dk_ref[...] = ck_sc[...].astype(BF16)
            dv_ref[...] = cv_sc[...].astype(BF16)

    qn = lambda n: jnp.minimum(n, N - 1)
    prev = lambda h, n: (h, jnp.maximum(qn(n) - 1, 0), 0)
    cur = lambda h, n: (h, qn(n), 0)
    out_kv = lambda h, n: (h, jnp.maximum(n - 1, 0), 0)
    kv = lambda f: pl.BlockSpec((None, Q, DH_B), f)
    qspec = pl.BlockSpec((None, G, Q, DH_B), lambda h, n: (h, 0, qn(n), 0))
    bspec = pl.BlockSpec((None, G, 2, Q, Q), lambda h, n: (h, 0, 0, 0, 0))
    sspec = pl.BlockSpec((None, G, 1, LANES), lambda h, n: (h, 0, 0, 0))
    return _pcall(
        kernel, name=name, dims=("parallel", "arbitrary"),
        out_shape=(jax.ShapeDtypeStruct((HKV, G, T, DH_B), BF16), jax.ShapeDtypeStruct((HKV, T, DH_B), BF16),
                   jax.ShapeDtypeStruct((HKV, T, DH_B), BF16), jax.ShapeDtypeStruct((HKV, G, 2, Q, Q), F32),
                   jax.ShapeDtypeStruct((HKV, G, 1, LANES), F32)),
        grid=(HKV, N + 1),
        in_specs=[qspec, kv(prev), kv(cur), kv(prev), kv(cur), qspec, qspec,
                  pl.BlockSpec((None, G, Q, 1), lambda h, n: (h, 0, qn(n), 0)), bspec, sspec],
        out_specs=(qspec, kv(out_kv), kv(out_kv), bspec, sspec),
        scratch_shapes=[pltpu.VMEM((Q, DH_B), F32), pltpu.VMEM((Q, DH_B), F32)],
    )(q, k, k, v, v, o, do, lse, bias, sinks)


def _branch_mix(oa, ob, wa_t, wb_t, proj, gate_col, *, name):
    T = oa.shape[0]
    D = wa_t.shape[0]
    bm, bn = _pick(T, 1024, 8), _pick(D, 512, LANES)
    assert gate_col % bn == 0
    ga0, gb0 = gate_col // bn, (gate_col + D) // bn

    def kernel(oa_ref, ob_ref, wa_ref, wb_ref, ga_ref, gb_ref, mix_ref, ya_ref, yb_ref):
        ya = _dot(oa_ref[...], wa_ref[...], "nt")
        yb = _dot(ob_ref[...], wb_ref[...], "nt")
        mix = _sigmoid(ga_ref[...].astype(F32)) * ya + _sigmoid(gb_ref[...].astype(F32)) * yb
        mix_ref[...] = mix.astype(BF16)
        ya_ref[...] = ya.astype(BF16)
        yb_ref[...] = yb.astype(BF16)

    out = pl.BlockSpec((bm, bn), lambda i, j: (i, j))
    return _pcall(
        kernel, name=name, dims=("parallel", "parallel"),
        out_shape=tuple(jax.ShapeDtypeStruct((T, D), BF16) for _ in range(3)),
        grid=(T // bm, D // bn),
        in_specs=[pl.BlockSpec((bm, oa.shape[1]), lambda i, j: (i, 0)),
                  pl.BlockSpec((bm, ob.shape[1]), lambda i, j: (i, 0)),
                  pl.BlockSpec((bn, oa.shape[1]), lambda i, j: (j, 0)),
                  pl.BlockSpec((bn, ob.shape[1]), lambda i, j: (j, 0)),
                  pl.BlockSpec((bm, bn), lambda i, j: (i, ga0 + j)),
                  pl.BlockSpec((bm, bn), lambda i, j: (i, gb0 + j))],
        out_specs=(out, out, out),
    )(oa, ob, wa_t, wb_t, proj, proj)


def _mix_bwd(dmixed, ya, yb, proj, gate_col, *, name):
    T, D = dmixed.shape
    br, bc = _pick(T, 512, 8), _pick(D, 512, LANES)
    assert gate_col % bc == 0
    ga0, gb0 = gate_col // bc, (gate_col + D) // bc

    def kernel(dm_ref, ya_ref, yb_ref, ga_ref, gb_ref, dya_ref, dyb_ref, dga_ref, dgb_ref):
        dm = dm_ref[...].astype(F32)
        sa = _sigmoid(ga_ref[...].astype(F32))
        sb = _sigmoid(gb_ref[...].astype(F32))
        dya_ref[...] = (dm * sa).astype(BF16)
        dyb_ref[...] = (dm * sb).astype(BF16)
        dga_ref[...] = (dm * ya_ref[...].astype(F32) * sa * (1.0 - sa)).astype(BF16)
        dgb_ref[...] = (dm * yb_ref[...].astype(F32) * sb * (1.0 - sb)).astype(BF16)

    blk = pl.BlockSpec((br, bc), lambda i, j: (i, j))
    return _pcall(
        kernel, name=name, dims=("parallel", "parallel"),
        out_shape=tuple(jax.ShapeDtypeStruct((T, D), BF16) for _ in range(4)),
        grid=(T // br, D // bc),
        in_specs=[blk, blk, blk, pl.BlockSpec((br, bc), lambda i, j: (i, ga0 + j)),
                  pl.BlockSpec((br, bc), lambda i, j: (i, gb0 + j))],
        out_specs=(blk, blk, blk, blk),
    )(dmixed, ya, yb, proj, proj)


def _ffn_up(h, wg_t, wu_t, *, name):
    T, D = h.shape
    FP = wg_t.shape[0]
    bm, bn = _pick(T, 1024, 8), _pick(FP, 512, LANES)

    def kernel(h_ref, wg_ref, wu_ref, g_ref, u_ref, hid_ref):
        hv = h_ref[...]
        g = _dot(hv, wg_ref[...], "nt")
        u = _dot(hv, wu_ref[...], "nt")
        g_ref[...] = g.astype(BF16)
        u_ref[...] = u.astype(BF16)
        hid_ref[...] = (g * _sigmoid(g) * u).astype(BF16)

    out = pl.BlockSpec((bm, bn), lambda i, j: (i, j))
    wspec = pl.BlockSpec((bn, D), lambda i, j: (j, 0))
    return _pcall(
        kernel, name=name, dims=("parallel", "parallel"),
        out_shape=tuple(jax.ShapeDtypeStruct((T, FP), BF16) for _ in range(3)),
        grid=(T // bm, FP // bn),
        in_specs=[pl.BlockSpec((bm, D), lambda i, j: (i, 0)), wspec, wspec],
        out_specs=(out, out, out),
    )(h, wg_t, wu_t)


def _swiglu_bwd(dhid, gate, up, *, name):
    T, FP = dhid.shape
    br, bc = _pick(T, 512, 8), _pick(FP, 1024, LANES)

    def kernel(dh_ref, g_ref, u_ref, dg_ref, du_ref):
        dh = dh_ref[...].astype(F32)
        g = g_ref[...].astype(F32)
        u = u_ref[...].astype(F32)
        sg = _sigmoid(g)
        dg_ref[...] = (dh * u * sg * (1.0 + g * (1.0 - sg))).astype(BF16)
        du_ref[...] = (dh * g * sg).astype(BF16)

    blk = pl.BlockSpec((br, bc), lambda i, j: (i, j))
    return _pcall(
        kernel, name=name, dims=("parallel", "parallel"),
        out_shape=(jax.ShapeDtypeStruct((T, FP), BF16), jax.ShapeDtypeStruct((T, FP), BF16)),
        grid=(T // br, FP // bc), in_specs=[blk, blk, blk], out_specs=(blk, blk),
    )(dhid, gate, up)


def _adamw(w, g, m, v, *, name):
    Rw, Cw = w.shape
    br = _pick(Rw, max(8, (1 << 19) // Cw // 8 * 8), 8)

    def kernel(w_ref, g_ref, m_ref, v_ref, d_ref, nm_ref, nv_ref):
        gv = g_ref[...]
        nm = ADAM_B1 * m_ref[...] + (1.0 - ADAM_B1) * gv
        nv = ADAM_B2 * v_ref[...] + (1.0 - ADAM_B2) * (gv * gv)
        m_hat = nm / (1.0 - ADAM_B1 ** ADAM_STEP)
        v_hat = nv / (1.0 - ADAM_B2 ** ADAM_STEP)
        d_ref[...] = -ADAM_LR * (m_hat / (jnp.sqrt(v_hat) + ADAM_EPS) + ADAM_WD * w_ref[...])
        nm_ref[...] = nm
        nv_ref[...] = nv

    blk = pl.BlockSpec((br, Cw), lambda i: (i, 0))
    return _pcall(
        kernel, name=name, dims=("parallel",),
        out_shape=tuple(jax.ShapeDtypeStruct((Rw, Cw), F32) for _ in range(3)),
        grid=(Rw // br,), in_specs=[blk, blk, blk, blk], out_specs=(blk, blk, blk),
    )(w, g, m, v)


def _my_place():
    return lax.axis_index("x"), lax.axis_index("y"), lax.axis_index("c")


def _flip(v, bit):
    return 1 - v if bit else v


def _all_gather_rows(shard, *, pad_rows=0, name):
    R, C = shard.shape
    flat = R % BF16_TILE_ROWS == 0
    assert flat or pad_rows == 0
    out_shape = (N_DEV * R + pad_rows, C) if flat else (N_DEV, R, C)

    def body(*refs):
        if pad_rows:
            x_ref, z_ref, out_ref, send_sems, recv_sems, local_sems = refs
        else:
            x_ref, out_ref, send_sems, recv_sems, local_sems = refs
        x, y, c = _my_place()
        me, sibling = (x, y, c), (x, y, 1 - c)
        chips = [(1 - x, y), (x, 1 - y), (1 - x, 1 - y)]

        def block(px, py, pc):
            idx = 4 * px + 2 * py + pc
            if flat:
                return out_ref.at[pl.ds(pl.multiple_of(idx * R, BF16_TILE_ROWS), R), :]
            return out_ref.at[idx]

        def copy(k, blk, to, src=None):
            return pltpu.make_async_remote_copy(
                src_ref=block(*blk) if src is None else src, dst_ref=block(*blk),
                send_sem=send_sems.at[k], recv_sem=recv_sems.at[k], device_id=to, device_id_type=MESH)

        mine = pltpu.make_async_copy(x_ref, block(*me), local_sems.at[0])
        mine.start()
        if pad_rows:
            zero = pltpu.make_async_copy(z_ref, out_ref.at[pl.ds(N_DEV * R, pad_rows), :], local_sems.at[1])
            zero.start()
        first = [copy(0, me, sibling, src=x_ref)]
        first += [copy(1 + j, me, (*chip, c), src=x_ref) for j, chip in enumerate(chips)]
        for cp in first:
            cp.start()
        passed = [copy(4 + j, (*chip, c), sibling) for j, chip in enumerate(chips)]
        for j, chip in enumerate(chips):
            copy(1 + j, (*chip, c), me).wait_recv()
            passed[j].start()
        copy(0, sibling, me).wait_recv()
        for j, chip in enumerate(chips):
            copy(4 + j, (*chip, 1 - c), me).wait_recv()
        for cp in first + passed:
            cp.wait_send()
        mine.wait()
        if pad_rows:
            zero.wait()

    operands = [shard] + ([jnp.zeros((pad_rows, C), shard.dtype)] if pad_rows else [])
    return _comm_call(
        body, name=name,
        out_shape=jax.ShapeDtypeStruct(out_shape, shard.dtype),
        in_specs=[ANY] * len(operands), out_specs=ANY,
        scratch_shapes=[pltpu.SemaphoreType.DMA((7,)), pltpu.SemaphoreType.DMA((7,)), pltpu.SemaphoreType.DMA((2,))],
    )(*operands)


def _scatter_rows(g, R, *, name):
    flat = g.ndim == 2
    C = g.shape[-1]

    def body(g_ref, land_ref, send_sems, recv_sems, local_sem):
        x, y, c = _my_place()

        def block(px, py, pc):
            idx = 4 * px + 2 * py + pc
            if flat:
                return g_ref.at[pl.ds(pl.multiple_of(idx * R, BF16_TILE_ROWS), R), :]
            return g_ref.at[idx]

        mine = pltpu.make_async_copy(block(x, y, c), land_ref.at[0], local_sem)
        mine.start()
        copies = []
        for k in range(1, N_DEV):
            peer = (_flip(x, k & 4), _flip(y, k & 2), _flip(c, k & 1))
            copies.append(pltpu.make_async_remote_copy(
                src_ref=block(*peer), dst_ref=land_ref.at[k],
                send_sem=send_sems.at[k - 1], recv_sem=recv_sems.at[k - 1], device_id=peer, device_id_type=MESH))
        for cp in copies:
            cp.start()
        for cp in copies:
            cp.wait_recv()
        for cp in copies:
            cp.wait_send()
        mine.wait()

    return _comm_call(
        body, name=name,
        out_shape=jax.ShapeDtypeStruct((N_DEV, R, C), g.dtype),
        in_specs=[ANY], out_specs=ANY,
        scratch_shapes=[pltpu.SemaphoreType.DMA((7,)), pltpu.SemaphoreType.DMA((7,)), pltpu.SemaphoreType.DMA(())],
    )(g)


def _sum_slots(land, *, name):
    _, R, C = land.shape
    bc = _pick(C, max(LANES, (1 << 19) // R // LANES * LANES), LANES)

    def kernel(l_ref, o_ref):
        acc = l_ref[0].astype(F32)
        for k in range(1, N_DEV):
            acc = acc + l_ref[k].astype(F32)
        o_ref[...] = acc

    return _pcall(
        kernel, name=name, dims=("parallel",),
        out_shape=jax.ShapeDtypeStruct((R, C), F32), grid=(C // bc,),
        in_specs=[pl.BlockSpec((N_DEV, R, bc), lambda i: (0, 0, i))],
        out_specs=pl.BlockSpec((R, bc), lambda i: (0, i)),
    )(land)


def _all_reduce_small(v, *, name):
    rows, C = v.shape

    def body(v_ref, o_ref, land_ref, send_sems, recv_sems):
        x, y, c = _my_place()
        me = 4 * x + 2 * y + c
        copies = []
        for k in range(1, N_DEV):
            peer = (_flip(x, k & 4), _flip(y, k & 2), _flip(c, k & 1))
            copies.append(pltpu.make_async_remote_copy(
                src_ref=v_ref, dst_ref=land_ref.at[me],
                send_sem=send_sems.at[k - 1], recv_sem=recv_sems.at[k - 1], device_id=peer, device_id_type=MESH))
        for cp in copies:
            cp.start()
        land_ref[me] = v_ref[...]
        for k in range(1, N_DEV):
            peer_idx = 4 * _flip(x, k & 4) + 2 * _flip(y, k & 2) + _flip(c, k & 1)
            pltpu.make_async_remote_copy(
                src_ref=v_ref, dst_ref=land_ref.at[peer_idx],
                send_sem=send_sems.at[k - 1], recv_sem=recv_sems.at[k - 1],
                device_id=(x, y, c), device_id_type=MESH).wait_recv()
        for cp in copies:
            cp.wait_send()
        acc = land_ref[0]
        for s in range(1, N_DEV):
            acc = acc + land_ref[s]
        o_ref[...] = acc

    vm = pl.BlockSpec(memory_space=pltpu.VMEM)
    return _comm_call(
        body, name=name,
        out_shape=jax.ShapeDtypeStruct((rows, C), F32),
        in_specs=[vm], out_specs=vm,
        scratch_shapes=[pltpu.VMEM((N_DEV, rows, C), F32),
                        pltpu.SemaphoreType.DMA((7,)), pltpu.SemaphoreType.DMA((7,))],
    )(v)


def kernel(x, norm1_g, w_in, b_forget, attn_sinks, rel_bias, w_branch_a, w_branch_b, w_out, norm2_g, w_ffn_gate, w_ffn_up, w_ffn_down, final_g, loss_target, m_norm1_g, m_w_in, m_b_forget, m_attn_sinks, m_rel_bias, m_w_branch_a, m_w_branch_b, m_w_out, m_norm2_g, m_w_ffn_gate, m_w_ffn_up, m_w_ffn_down, m_final_g, v_norm1_g, v_w_in, v_b_forget, v_attn_sinks, v_rel_bias, v_w_branch_a, v_w_branch_b, v_w_out, v_norm2_g, v_w_ffn_gate, v_w_ffn_up, v_w_ffn_down, v_final_g):
    xs = x[0]
    T, D = xs.shape
    H_A, H_B = b_forget.shape[-1], attn_sinks.shape[-1]
    WA, QB = H_A * DH_A, H_B * DH_B
    R_IN = w_in.shape[-1]
    W_IN = N_DEV * R_IN
    KB = (W_IN - 3 * WA - H_A - QB - 2 * D) // 2
    HKV = KB // DH_B
    G = H_B // HKV
    N_BIG = W_IN - H_A
    GATE_COL = 3 * WA + QB + 2 * KB
    R_F = w_ffn_gate.shape[-1]
    F = N_DEV * R_F
    FP = _round_up(F, 512)
    assert H_A <= LANES and T % SWA_BLOCK == 0

    win_s = w_in[0].T.astype(BF16)
    wa_s = w_branch_a[0].T.astype(BF16)
    wb_s = w_branch_b[0].T.astype(BF16)
    wout_s = w_out[0].astype(BF16)
    wg_s = w_ffn_gate[0].T.astype(BF16)
    wu_s = w_ffn_up[0].T.astype(BF16)
    wd_s = w_ffn_down[0].astype(BF16)

    win_all = _all_gather_rows(win_s, name="ag_w_in").reshape(W_IN, D)
    w_big = jnp.concatenate([win_all[:3 * WA], win_all[3 * WA + H_A:]], axis=0)
    w_f = jnp.pad(win_all[3 * WA:3 * WA + H_A], ((0, LANES - H_A), (0, 0)))
    wa_t = _all_gather_rows(wa_s, name="ag_w_branch_a")
    wb_t = _all_gather_rows(wb_s, name="ag_w_branch_b")
    wout = _all_gather_rows(wout_s, name="ag_w_out")
    wg_t = _all_gather_rows(wg_s, pad_rows=FP - F, name="ag_w_ffn_gate")
    wu_t = _all_gather_rows(wu_s, pad_rows=FP - F, name="ag_w_ffn_up")
    wd = _all_gather_rows(wd_s, pad_rows=FP - F, name="ag_w_ffn_down")

    BM = 1024
    bn_big = _pick(N_BIG, 768, LANES)

    h1, rstd1 = _rms_fwd(xs, norm1_g, name="rms1_fwd")
    proj = _matmul([(h1, w_big)], "nt", bm=BM, bn=bn_big, name="proj_fwd")
    f_logit = _matmul([(h1, w_f)], "nt", bm=BM, bn=LANES, out_dtype=F32, name="forget_fwd")
    b_pad = jnp.pad(b_forget, ((0, 0), (0, LANES - H_A)))
    c_all, sneg = _fox_gate_fwd(f_logit, b_pad, name="fox_gate_fwd")
    c_heads = c_all[:, :H_A].T
    c_col, c_row = c_heads[:, :, None], c_heads[:, None, :]
    oa, lse_a = _fox_fwd(proj, c_col, c_row, H_A, name="fox_fwd")

    def heads_q(a):
        return a.reshape(T, HKV, G, DH_B).transpose(1, 2, 0, 3)

    def heads_kv(a):
        return a.reshape(T, HKV, DH_B).transpose(1, 0, 2)

    qb = heads_q(proj[:, 3 * WA:3 * WA + QB])
    kb = heads_kv(proj[:, 3 * WA + QB:3 * WA + QB + KB])
    vb = heads_kv(proj[:, 3 * WA + QB + KB:GATE_COL])
    onehot = _bucket_onehot()
    bias = _bias_table(rel_bias, onehot, name="rel_bias_table").reshape(HKV, G, 2, SWA_BLOCK, SWA_BLOCK)
    sinks_b = jnp.broadcast_to(attn_sinks.reshape(HKV, G, 1, 1), (HKV, G, 1, LANES))
    ob4, lse_b = _swa_fwd(qb, kb, vb, bias, sinks_b, name="swa_fwd")
    ob = ob4.transpose(2, 0, 1, 3).reshape(T, QB)

    mixed, ya, yb = _branch_mix(oa, ob, wa_t, wb_t, proj, GATE_COL, name="branch_mix")
    x1 = _matmul([(mixed, wout)], "nn", bm=BM, bn=512, out_dtype=F32, addend=xs, name="out_proj_fwd")

    h2, rstd2 = _rms_fwd(x1, norm2_g, name="rms2_fwd")
    gate, up, hidden = _ffn_up(h2, wg_t, wu_t, name="ffn_up")
    tk_f = _pick(FP, 2816, LANES)
    x2 = _matmul([(hidden, wd)], "nn", bm=BM, bn=1024, tk=tk_f, out_dtype=F32, addend=x1, name="ffn_down_fwd")

    dx2, loss_part, g_final = _final_loss(x2, final_g.reshape(1, D), loss_target[0], name="final_loss")
    loss = lax.psum(loss_part[0, 0], MESH_AXES)

    dx2_b = dx2.astype(BF16)
    dhid = _matmul([(dx2_b, wd)], "nt", bm=BM, bn=1024, name="ffn_down_bwd_x")
    g_wd = _matmul([(hidden, dx2_b)], "tn", bm=1024, bn=1024, name="ffn_down_bwd_w")
    dgate, dup = _swiglu_bwd(dhid, gate, up, name="swiglu_bwd")
    dh2 = _matmul([(dgate, wg_t), (dup, wu_t)], "nn", bm=BM, bn=1024, tk=_pick(FP, 1408, LANES), out_dtype=F32,
                  name="ffn_up_bwd_x")
    g_wg = _matmul([(dgate, h2)], "tn", bm=1024, bn=1024, name="ffn_gate_bwd_w")
    g_wu = _matmul([(dup, h2)], "tn", bm=1024, bn=1024, name="ffn_up_bwd_w")
    dx1, g_norm2 = _rms_bwd(dx2, dh2, x1, rstd2, norm2_g, name="rms2_bwd")

    dx1_b = dx1.astype(BF16)
    dmixed = _matmul([(dx1_b, wout)], "nt", bm=BM, bn=1024, name="out_proj_bwd_x")
    g_wout = _matmul([(mixed, dx1_b)], "tn", bm=1024, bn=1024, name="out_proj_bwd_w")
    dya, dyb, dga, dgb = _mix_bwd(dmixed, ya, yb, proj, GATE_COL, name="mix_bwd")
    doa = _matmul([(dya, wa_t)], "nn", bm=BM, bn=1024, name="branch_a_bwd_x")
    dob = _matmul([(dyb, wb_t)], "nn", bm=BM, bn=1024, name="branch_b_bwd_x")
    g_wa = _matmul([(dya, oa)], "tn", bm=1024, bn=1024, name="branch_a_bwd_w")
    g_wb = _matmul([(dyb, ob)], "tn", bm=1024, bn=1024, name="branch_b_bwd_w")

    delta_a = _rowdot(doa, oa, H_A, name="fox_delta")
    dqa, dka, dva, dc_key, dc_query = _fox_bwd(proj, doa, lse_a, delta_a, c_col, c_row, H_A, name="fox_bwd")
    dc = jnp.pad((dc_key[:, 0, :] + dc_query[:, :, 0]).T, ((0, 0), (0, LANES - H_A)))
    df, g_bf = _fox_gate_bwd(dc, sneg, name="fox_gate_bwd")

    dqb4, dkb3, dvb3, dbias, dsinks = _swa_bwd(qb, kb, vb, ob4, heads_q(dob), lse_b, bias, sinks_b, name="swa_bwd")
    g_relb = _bias_table_bwd(dbias.reshape(H_B, -1), onehot, name="rel_bias_table_bwd")
    dqb = dqb4.transpose(2, 0, 1, 3).reshape(T, QB)
    dkb = dkb3.transpose(1, 0, 2).reshape(T, KB)
    dvb = dvb3.transpose(1, 0, 2).reshape(T, KB)

    dproj = jnp.concatenate([dqa.astype(BF16), dka, dva, dqb, dkb, dvb, dga, dgb], axis=1)
    dh_f = _matmul([(df, w_f)], "nn", bm=BM, bn=1024, out_dtype=F32, name="forget_bwd_x")
    tk_big = _pick(N_BIG, 2816, LANES)
    dh1 = _matmul([(dproj, w_big)], "nn", bm=BM, bn=1024, tk=tk_big, out_dtype=F32, addend=dh_f, name="proj_bwd_x")
    g_wbig = _matmul([(dproj, h1)], "tn", bm=bn_big, bn=1024, name="proj_bwd_w")
    g_wf = _matmul([(df, h1)], "tn", bm=LANES, bn=1024, name="forget_bwd_w")
    grad_x, g_norm1 = _rms_bwd(dx1, dh1, xs, rstd1, norm1_g, name="rms1_bwd")

    g_win = jnp.concatenate([g_wbig[:3 * WA], g_wf[:H_A], g_wbig[3 * WA:]], axis=0).reshape(N_DEV, R_IN, D)

    def reduce(g, R, nm):
        return _sum_slots(_scatter_rows(g, R, name="rs_" + nm), name="sum_" + nm)

    gw_in = reduce(g_win, R_IN, "w_in").T
    gw_a = reduce(g_wa, D // N_DEV, "w_branch_a").T
    gw_b = reduce(g_wb, D // N_DEV, "w_branch_b").T
    gw_out = reduce(g_wout, D // N_DEV, "w_out")
    gw_g = reduce(g_wg, R_F, "w_ffn_gate").T
    gw_u = reduce(g_wu, R_F, "w_ffn_up").T
    gw_d = reduce(g_wd, R_F, "w_ffn_down")

    n_rb = NUM_BUCKETS * H_B
    assert D >= n_rb and D >= H_A + H_B
    small = jnp.concatenate([
        g_norm1, g_norm2, g_final,
        jnp.pad(jnp.concatenate([g_bf[:, :H_A], dsinks[:, :, 0, 0].reshape(1, H_B)], axis=1), ((0, 0), (0, D - H_A - H_B))),
        jnp.pad(g_relb.reshape(1, n_rb), ((0, 0), (0, D - n_rb))),
        jnp.zeros((3, D), F32)], axis=0)
    small = _all_reduce_small(small, name="ar_small")
    gs = {
        "norm1_g": small[0:1], "norm2_g": small[1:2], "final_g": small[2],
        "b_forget": small[3:4, :H_A], "attn_sinks": small[3:4, H_A:H_A + H_B],
        "rel_bias": small[4, :n_rb].reshape(NUM_BUCKETS, H_B),
    }

    def adam(w, g, m, v, nm):
        shape = w.shape
        two_d = lambda a: a.reshape(-1, shape[-1])
        d, nm_, nv_ = _adamw(two_d(w), two_d(g), two_d(m), two_d(v), name="adamw_" + nm)
        return g.reshape(shape), d.reshape(shape), nm_.reshape(shape), nv_.reshape(shape)

    def adam_small(ws, gsm, ms, vs):
        def pack(parts):
            rows = [jnp.pad(p.reshape(1, -1), ((0, 0), (0, D - p.size))) for p in parts]
            return jnp.concatenate(rows + [jnp.ones((8 - len(rows), D), F32)], axis=0)
        d, nm_, nv_ = _adamw(pack(ws), pack(gsm), pack(ms), pack(vs), name="adamw_small")
        unpack = lambda a: [a[i, :p.size].reshape(p.shape) for i, p in enumerate(ws)]
        return unpack(d), unpack(nm_), unpack(nv_)

    big = {
        "w_in": adam(w_in, gw_in[None], m_w_in, v_w_in, "w_in"),
        "w_branch_a": adam(w_branch_a, gw_a[None], m_w_branch_a, v_w_branch_a, "w_branch_a"),
        "w_branch_b": adam(w_branch_b, gw_b[None], m_w_branch_b, v_w_branch_b, "w_branch_b"),
        "w_out": adam(w_out, gw_out[None], m_w_out, v_w_out, "w_out"),
        "w_ffn_gate": adam(w_ffn_gate, gw_g[None], m_w_ffn_gate, v_w_ffn_gate, "w_ffn_gate"),
        "w_ffn_up": adam(w_ffn_up, gw_u[None], m_w_ffn_up, v_w_ffn_up, "w_ffn_up"),
        "w_ffn_down": adam(w_ffn_down, gw_d[None], m_w_ffn_down, v_w_ffn_down, "w_ffn_down"),
    }
    small_names = ["norm1_g", "b_forget", "attn_sinks", "rel_bias", "norm2_g", "final_g"]
    small_w = [norm1_g, b_forget, attn_sinks, rel_bias, norm2_g, final_g]
    small_g = [gs[n].reshape(w.shape) for n, w in zip(small_names, small_w)]
    small_m = [m_norm1_g, m_b_forget, m_attn_sinks, m_rel_bias, m_norm2_g, m_final_g]
    small_v = [v_norm1_g, v_b_forget, v_attn_sinks, v_rel_bias, v_norm2_g, v_final_g]
    sd, sm, sv = adam_small(small_w, small_g, small_m, small_v)
    for i, n in enumerate(small_names):
        big[n] = (small_g[i], sd[i], sm[i], sv[i])

    order = ["norm1_g", "w_in", "b_forget", "attn_sinks", "rel_bias", "w_branch_a", "w_branch_b", "w_out",
             "norm2_g", "w_ffn_gate", "w_ffn_up", "w_ffn_down", "final_g"]
    grads = [big[n][0] for n in order]
    deltas = [big[n][1] for n in order]
    new_m = [big[n][2] for n in order]
    new_v = [big[n][3] for n in order]
    return (loss, grad_x[None], *grads, *deltas, *new_m, *new_v)
```

```python
import functools

import numpy as np
import jax
import jax.numpy as jnp
from jax import lax
from jax.experimental import pallas as pl
from jax.experimental.pallas import tpu as pltpu

N_DEV = 8
MESH_AXES = ("x", "y", "c")
DH_A = 128
DH_B = 64
SWA_BLOCK = 128
NUM_BUCKETS = 32
MAX_DISTANCE = 128
EPS = 1e-6
ADAM_LR = 0.001
ADAM_B1 = 0.9
ADAM_B2 = 0.999
ADAM_EPS = 1e-08
ADAM_WD = 0.01
ADAM_STEP = 10
NEG = -1e30
LANES = 128
BF16_TILE_ROWS = 16
V7X_VMEM_LIMIT = 56 * 1024 * 1024
F32 = jnp.float32
BF16 = jnp.bfloat16
MESH = pl.DeviceIdType.MESH
ANY = pl.BlockSpec(memory_space=pl.ANY)


def _round_up(n, m):
    return (n + m - 1) // m * m


def _pick(n, target, mult):
    best = None
    for d in range(mult, min(n, target) + 1, mult):
        if n % d == 0:
            best = d
    return n if best is None else best


def _pcall(kernel, *, name, in_specs, dims=None, deps=(), **kw):
    deps = tuple(deps)
    if deps:
        inner, n_in, n_dep = kernel, len(in_specs), len(deps)

        def kernel(*refs):
            inner(*refs[:n_in], *refs[n_in + n_dep:])

        in_specs = list(in_specs) + [ANY] * n_dep
    call = pl.pallas_call(
        kernel, name=name, in_specs=in_specs,
        compiler_params=pltpu.CompilerParams(dimension_semantics=dims, vmem_limit_bytes=V7X_VMEM_LIMIT),
        **kw)
    return lambda *operands: call(*operands, *deps)


def _comm_call(kernel, *, name, **kw):
    return pl.pallas_call(kernel, name=name, **kw)


def _sigmoid(v):
    return 1.0 / (1.0 + jnp.exp(-v))


_DIMS = {"nn": (((1,), (0,)), ((), ())), "nt": (((1,), (1,)), ((), ())), "tn": (((0,), (0,)), ((), ()))}


def _dot(a, b, form="nn"):
    return lax.dot_general(a, b, _DIMS[form], preferred_element_type=F32)


def _matmul(pairs, form, *, bm, bn, tk=None, out_dtype=BF16, addend=None, deps=(), name):
    a0, b0 = pairs[0]
    if form == "tn":
        K, M = a0.shape
    else:
        M, K = a0.shape
    N = b0.shape[0] if form == "nt" else b0.shape[1]
    tk = K if tk is None else tk
    bm, bn = min(bm, M), min(bn, N)
    assert M % bm == 0 and N % bn == 0 and K % tk == 0, (name, M, N, K, bm, bn, tk)
    nk = K // tk
    n_pairs = len(pairs)
    has_add = addend is not None
    if form == "tn":
        a_spec = pl.BlockSpec((tk, bm), lambda i, j, k: (k, i))
    else:
        a_spec = pl.BlockSpec((bm, tk), lambda i, j, k: (i, k))
    if form == "nt":
        b_spec = pl.BlockSpec((bn, tk), lambda i, j, k: (j, k))
    else:
        b_spec = pl.BlockSpec((tk, bn), lambda i, j, k: (k, j))
    o_spec = pl.BlockSpec((bm, bn), lambda i, j, k: (i, j))

    def kernel(*refs):
        ab = refs[:2 * n_pairs]
        c_ref = refs[2 * n_pairs] if has_add else None
        o_ref = refs[2 * n_pairs + has_add]

        def partial_sum():
            acc = _dot(ab[0][...], ab[1][...], form)
            for p in range(1, n_pairs):
                acc = acc + _dot(ab[2 * p][...], ab[2 * p + 1][...], form)
            return acc

        def finish(acc):
            if has_add:
                acc = acc + c_ref[...].astype(F32)
            o_ref[...] = acc.astype(o_ref.dtype)

        if nk == 1:
            finish(partial_sum())
        else:
            acc_ref = refs[-1]
            k = pl.program_id(2)

            @pl.when(k == 0)
            def _():
                acc_ref[...] = jnp.zeros_like(acc_ref)

            acc_ref[...] += partial_sum()

            @pl.when(k == nk - 1)
            def _():
                finish(acc_ref[...])

    operands, in_specs = [], []
    for a, b in pairs:
        operands += [a, b]
        in_specs += [a_spec, b_spec]
    if has_add:
        operands.append(addend)
        in_specs.append(o_spec)
    return _pcall(
        kernel, name=name, dims=("parallel", "parallel", "arbitrary"), deps=deps,
        out_shape=jax.ShapeDtypeStruct((M, N), out_dtype),
        grid=(M // bm, N // bn, nk), in_specs=in_specs, out_specs=o_spec,
        scratch_shapes=[pltpu.VMEM((bm, bn), F32)] if nk > 1 else [],
    )(*operands)


def _rms_fwd(x, g, *, deps=(), name):
    T, D = x.shape
    br = _pick(T, 256, 8)

    def kernel(x_ref, g_ref, h_ref, r_ref):
        xv = x_ref[...]
        rstd = lax.rsqrt(jnp.mean(xv * xv, axis=-1, keepdims=True) + EPS)
        h_ref[...] = (xv * rstd * g_ref[...]).astype(BF16)
        r_ref[...] = rstd

    return _pcall(
        kernel, name=name, dims=("parallel",), deps=deps,
        out_shape=(jax.ShapeDtypeStruct((T, D), BF16), jax.ShapeDtypeStruct((T, 1), F32)),
        grid=(T // br,),
        in_specs=[pl.BlockSpec((br, D), lambda i: (i, 0)), pl.BlockSpec((1, D), lambda i: (0, 0))],
        out_specs=(pl.BlockSpec((br, D), lambda i: (i, 0)), pl.BlockSpec((br, 1), lambda i: (i, 0))),
    )(x, g)


def _rms_bwd(dres, dh, x, rstd, g, *, deps=(), name):
    T, D = x.shape
    br = _pick(T, 256, 8)

    def kernel(dres_ref, dh_ref, x_ref, r_ref, g_ref, dx_ref, gg_ref):
        @pl.when(pl.program_id(0) == 0)
        def _():
            gg_ref[...] = jnp.zeros_like(gg_ref)

        rstd_v = r_ref[...]
        xhat = x_ref[...] * rstd_v
        dhv = dh_ref[...].astype(F32)
        dxhat = dhv * g_ref[...]
        proj = jnp.mean(dxhat * xhat, axis=-1, keepdims=True)
        dx_ref[...] = dres_ref[...] + rstd_v * (dxhat - xhat * proj)
        gg_ref[...] += jnp.sum(dhv * xhat, axis=0, keepdims=True)

    row = pl.BlockSpec((br, D), lambda i: (i, 0))
    vec = pl.BlockSpec((1, D), lambda i: (0, 0))
    return _pcall(
        kernel, name=name, dims=("arbitrary",), deps=deps,
        out_shape=(jax.ShapeDtypeStruct((T, D), F32), jax.ShapeDtypeStruct((1, D), F32)),
        grid=(T // br,),
        in_specs=[row, row, row, pl.BlockSpec((br, 1), lambda i: (i, 0)), vec],
        out_specs=(row, vec),
    )(dres, dh, x, rstd, g)


def _final_loss(x2, gf, target, *, name):
    T, D = x2.shape
    br = _pick(T, 256, 8)

    def kernel(x_ref, g_ref, t_ref, dx_ref, loss_ref, gg_ref):
        @pl.when(pl.program_id(0) == 0)
        def _():
            gg_ref[...] = jnp.zeros_like(gg_ref)
            loss_ref[...] = jnp.zeros_like(loss_ref)

        xv = x_ref[...]
        gv = g_ref[...]
        rstd = lax.rsqrt(jnp.mean(xv * xv, axis=-1, keepdims=True) + EPS)
        xhat = xv * rstd
        err = xhat * gv - t_ref[...]
        loss_ref[...] += 0.5 * jnp.sum(jnp.mean(err * err, axis=-1, keepdims=True), axis=0, keepdims=True)
        dy = err * (1.0 / D)
        dxhat = dy * gv
        proj = jnp.mean(dxhat * xhat, axis=-1, keepdims=True)
        dx_ref[...] = rstd * (dxhat - xhat * proj)
        gg_ref[...] += jnp.sum(dy * xhat, axis=0, keepdims=True)

    row = pl.BlockSpec((br, D), lambda i: (i, 0))
    vec = pl.BlockSpec((1, D), lambda i: (0, 0))
    return _pcall(
        kernel, name=name, dims=("arbitrary",),
        out_shape=(jax.ShapeDtypeStruct((T, D), F32), jax.ShapeDtypeStruct((1, 1), F32),
                   jax.ShapeDtypeStruct((1, D), F32)),
        grid=(T // br,),
        in_specs=[row, vec, row],
        out_specs=(row, pl.BlockSpec((1, 1), lambda i: (0, 0)), vec),
    )(x2, gf, target)


def _scan_rows(v, reverse):
    n = v.shape[0]
    row = lax.broadcasted_iota(jnp.int32, v.shape, 0)
    s = 1
    while s < n:
        if reverse:
            v = v + jnp.where(row < n - s, pltpu.roll(v, n - s, 0), 0.0)
        else:
            v = v + jnp.where(row >= s, pltpu.roll(v, s, 0), 0.0)
        s *= 2
    return v


def _fox_gate_fwd(f, bias, *, name):
    T = f.shape[0]
    tb = _pick(T, 512, 8)

    def kernel(f_ref, b_ref, c_ref, s_ref, carry):
        @pl.when(pl.program_id(0) == 0)
        def _():
            carry[...] = jnp.zeros_like(carry)

        fa = f_ref[...] + b_ref[...]
        z = jnp.exp(-jnp.abs(fa))
        log_f = jnp.minimum(fa, 0.0) - jnp.log(1.0 + z)
        s_ref[...] = jnp.where(fa >= 0, z, 1.0) / (1.0 + z)
        c = _scan_rows(log_f, False) + carry[...]
        c_ref[...] = c
        carry[...] = c[tb - 1:tb, :]

    blk = pl.BlockSpec((tb, LANES), lambda i: (i, 0))
    return _pcall(
        kernel, name=name, dims=("arbitrary",),
        out_shape=(jax.ShapeDtypeStruct((T, LANES), F32), jax.ShapeDtypeStruct((T, LANES), F32)),
        grid=(T // tb,),
        in_specs=[blk, pl.BlockSpec((1, LANES), lambda i: (0, 0))],
        out_specs=(blk, blk),
        scratch_shapes=[pltpu.VMEM((1, LANES), F32)],
    )(f, bias)


def _fox_gate_bwd(dc, sneg, *, name):
    T = dc.shape[0]
    tb = _pick(T, 512, 8)
    nb = T // tb

    def kernel(dc_ref, s_ref, df_ref, gb_ref, carry):
        @pl.when(pl.program_id(0) == 0)
        def _():
            carry[...] = jnp.zeros_like(carry)
            gb_ref[...] = jnp.zeros_like(gb_ref)

        dlog = _scan_rows(dc_ref[...], True) + carry[...]
        carry[...] = dlog[0:1, :]
        dfa = dlog * s_ref[...]
        df_ref[...] = dfa.astype(BF16)
        gb_ref[...] += jnp.sum(dfa, axis=0, keepdims=True)

    blk = pl.BlockSpec((tb, LANES), lambda i: (nb - 1 - i, 0))
    return _pcall(
        kernel, name=name, dims=("arbitrary",),
        out_shape=(jax.ShapeDtypeStruct((T, LANES), BF16), jax.ShapeDtypeStruct((1, LANES), F32)),
        grid=(nb,),
        in_specs=[blk, blk],
        out_specs=(blk, pl.BlockSpec((1, LANES), lambda i: (0, 0))),
        scratch_shapes=[pltpu.VMEM((1, LANES), F32)],
    )(dc, sneg)


def _fox_scores(q, k, cq, ck, i, j, tq, tk):
    s = _dot(q, k, "nt") * (DH_A ** -0.5) + cq - ck
    qpos = i * tq + lax.broadcasted_iota(jnp.int32, (tq, tk), 0)
    kpos = j * tk + lax.broadcasted_iota(jnp.int32, (tq, tk), 1)
    return s, kpos <= qpos


def _fox_fwd(proj, c_col, c_row, n_heads, *, name):
    T = proj.shape[0]
    H = n_heads
    tq = tk = _pick(T, 512, 128)
    nq = T // tq

    def kernel(q_ref, k_ref, v_ref, cq_ref, ck_ref, o_ref, lse_ref, m_sc, l_sc, acc_sc):
        i, j = pl.program_id(1), pl.program_id(2)

        @pl.when(j == 0)
        def _():
            m_sc[...] = jnp.full_like(m_sc, NEG)
            l_sc[...] = jnp.zeros_like(l_sc)
            acc_sc[...] = jnp.zeros_like(acc_sc)

        @pl.when(j <= i)
        def _():
            s, visible = _fox_scores(q_ref[...], k_ref[...], cq_ref[...], ck_ref[...], i, j, tq, tk)
            s = jnp.where(visible, s, NEG)
            m_prev = m_sc[...]
            m_new = jnp.maximum(m_prev, jnp.max(s, axis=-1, keepdims=True))
            alpha = jnp.exp(m_prev - m_new)
            p = jnp.where(visible, jnp.exp(s - m_new), 0.0)
            l_sc[...] = alpha * l_sc[...] + jnp.sum(p, axis=-1, keepdims=True)
            acc_sc[...] = alpha * acc_sc[...] + _dot(p.astype(BF16), v_ref[...])
            m_sc[...] = m_new

        @pl.when(j == i)
        def _():
            l = l_sc[...]
            o_ref[...] = (acc_sc[...] / l).astype(BF16)
            lse_ref[...] = m_sc[...] + jnp.log(l)

    return _pcall(
        kernel, name=name, dims=("parallel", "parallel", "arbitrary"),
        out_shape=(jax.ShapeDtypeStruct((T, H * DH_A), BF16), jax.ShapeDtypeStruct((H, T, 1), F32)),
        grid=(H, nq, nq),
        in_specs=[
            pl.BlockSpec((tq, DH_A), lambda h, i, j: (i, h)),
            pl.BlockSpec((tk, DH_A), lambda h, i, j: (jnp.minimum(j, i), H + h)),
            pl.BlockSpec((tk, DH_A), lambda h, i, j: (jnp.minimum(j, i), 2 * H + h)),
            pl.BlockSpec((None, tq, 1), lambda h, i, j: (h, i, 0)),
            pl.BlockSpec((None, 1, tk), lambda h, i, j: (h, 0, jnp.minimum(j, i))),
        ],
        out_specs=(pl.BlockSpec((tq, DH_A), lambda h, i, j: (i, h)),
                   pl.BlockSpec((None, tq, 1), lambda h, i, j: (h, i, 0))),
        scratch_shapes=[pltpu.VMEM((tq, 1), F32), pltpu.VMEM((tq, 1), F32), pltpu.VMEM((tq, DH_A), F32)],
    )(proj, proj, proj, c_col, c_row)


def _rowdot(a, b, n_heads, *, name):
    T = a.shape[0]
    tq = _pick(T, 512, 8)

    def kernel(a_ref, b_ref, o_ref):
        o_ref[...] = jnp.sum(a_ref[...].astype(F32) * b_ref[...].astype(F32), axis=-1, keepdims=True)

    blk = pl.BlockSpec((tq, DH_A), lambda h, i: (i, h))
    return _pcall(
        kernel, name=name, dims=("parallel", "parallel"),
        out_shape=jax.ShapeDtypeStruct((n_heads, T, 1), F32),
        grid=(n_heads, T // tq), in_specs=[blk, blk],
        out_specs=pl.BlockSpec((None, tq, 1), lambda h, i: (h, i, 0)),
    )(a, b)


def _fox_bwd(proj, do, lse, delta, c_col, c_row, n_heads, *, name):
    T = proj.shape[0]
    H = n_heads
    tq = tk = _pick(T, 512, 128)
    nq = T // tq
    scale = DH_A ** -0.5

    def kernel(q_ref, k_ref, v_ref, do_ref, lse_ref, dl_ref, cq_ref, ck_ref,
               dq_ref, dk_ref, dv_ref, dc_ref, dcq_ref, dk_sc, dv_sc, dc_sc):
        j, i = pl.program_id(1), pl.program_id(2)

        @pl.when((j == 0) & (i == 0))
        def _():
            dq_ref[...] = jnp.zeros_like(dq_ref)
            dcq_ref[...] = jnp.zeros_like(dcq_ref)

        @pl.when(i == 0)
        def _():
            dk_sc[...] = jnp.zeros_like(dk_sc)
            dv_sc[...] = jnp.zeros_like(dv_sc)
            dc_sc[...] = jnp.zeros_like(dc_sc)

        @pl.when(i >= j)
        def _():
            q, k, v, dov = q_ref[...], k_ref[...], v_ref[...], do_ref[...]
            s, visible = _fox_scores(q, k, cq_ref[...], ck_ref[...], i, j, tq, tk)
            p = jnp.where(visible, jnp.exp(s - lse_ref[...]), 0.0)
            dv_sc[...] += _dot(p.astype(BF16), dov, "tn")
            dp = _dot(dov, v, "nt")
            ds = p * (dp - dl_ref[...])
            dc_sc[...] -= jnp.sum(ds, axis=0, keepdims=True)
            dsb = ds.astype(BF16)
            dk_sc[...] += _dot(dsb, q, "tn") * scale
            rows = pl.ds(pl.multiple_of(i * tq, tq), tq)
            dq_ref[rows, :] += _dot(dsb, k) * scale
            dcq_ref[rows, :] += jnp.sum(ds, axis=1, keepdims=True)

        @pl.when(i == nq - 1)
        def _():
            dk_ref[...] = dk_sc[...].astype(BF16)
            dv_ref[...] = dv_sc[...].astype(BF16)
            dc_ref[...] = dc_sc[...]

    qi = lambda j, i: jnp.maximum(i, j)
    return _pcall(
        kernel, name=name, dims=("parallel", "arbitrary", "arbitrary"),
        out_shape=(jax.ShapeDtypeStruct((T, H * DH_A), F32), jax.ShapeDtypeStruct((T, H * DH_A), BF16),
                   jax.ShapeDtypeStruct((T, H * DH_A), BF16), jax.ShapeDtypeStruct((H, 1, T), F32),
                   jax.ShapeDtypeStruct((H, T, 1), F32)),
        grid=(H, nq, nq),
        in_specs=[
            pl.BlockSpec((tq, DH_A), lambda h, j, i: (qi(j, i), h)),
            pl.BlockSpec((tk, DH_A), lambda h, j, i: (j, H + h)),
            pl.BlockSpec((tk, DH_A), lambda h, j, i: (j, 2 * H + h)),
            pl.BlockSpec((tq, DH_A), lambda h, j, i: (qi(j, i), h)),
            pl.BlockSpec((None, tq, 1), lambda h, j, i: (h, qi(j, i), 0)),
            pl.BlockSpec((None, tq, 1), lambda h, j, i: (h, qi(j, i), 0)),
            pl.BlockSpec((None, tq, 1), lambda h, j, i: (h, qi(j, i), 0)),
            pl.BlockSpec((None, 1, tk), lambda h, j, i: (h, 0, j)),
        ],
        out_specs=(pl.BlockSpec((T, DH_A), lambda h, j, i: (0, h)),
                   pl.BlockSpec((tk, DH_A), lambda h, j, i: (j, h)),
                   pl.BlockSpec((tk, DH_A), lambda h, j, i: (j, h)),
                   pl.BlockSpec((None, 1, tk), lambda h, j, i: (h, 0, j)),
                   pl.BlockSpec((None, T, 1), lambda h, j, i: (h, 0, 0))),
        scratch_shapes=[pltpu.VMEM((tk, DH_A), F32), pltpu.VMEM((tk, DH_A), F32), pltpu.VMEM((1, tk), F32)],
    )(proj, proj, proj, do, lse, delta, c_col, c_row)


def _t5_bucket(dist):
    max_exact = NUM_BUCKETS // 2
    small = dist < max_exact
    large = max_exact + (np.log(np.maximum(dist, 1) / max_exact) / np.log(MAX_DISTANCE / max_exact)
                         * (NUM_BUCKETS - max_exact)).astype(np.int64)
    large = np.minimum(large, NUM_BUCKETS - 1)
    return np.where(small, dist, large)


def _bucket_onehot():
    ql = np.arange(SWA_BLOCK)[:, None]
    kl = np.arange(2 * SWA_BLOCK)[None, :]
    bucket = _t5_bucket(np.clip(ql + SWA_BLOCK - kl, 0, None))
    bucket = np.stack([bucket[:, :SWA_BLOCK], bucket[:, SWA_BLOCK:]], axis=0)
    onehot = (bucket.reshape(1, -1) == np.arange(NUM_BUCKETS)[:, None])
    return jnp.asarray(onehot, dtype=BF16)


def _split3(v):
    hi = v.astype(BF16)
    r1 = v - hi.astype(F32)
    mid = r1.astype(BF16)
    lo = (r1 - mid.astype(F32)).astype(BF16)
    return hi, mid, lo


def _bias_table(rel_bias, onehot, *, name):
    B, H = rel_bias.shape
    n = onehot.shape[1]
    bc = _pick(n, 8192, LANES)

    def kernel(rb_ref, oh_ref, o_ref):
        hi, mid, lo = _split3(rb_ref[...])
        oh = oh_ref[...]
        o_ref[...] = _dot(hi, oh, "tn") + _dot(mid, oh, "tn") + _dot(lo, oh, "tn")

    return _pcall(
        kernel, name=name, dims=("parallel",),
        out_shape=jax.ShapeDtypeStruct((H, n), F32), grid=(n // bc,),
        in_specs=[pl.BlockSpec((B, H), lambda i: (0, 0)), pl.BlockSpec((B, bc), lambda i: (0, i))],
        out_specs=pl.BlockSpec((H, bc), lambda i: (0, i)),
    )(rel_bias, onehot)


def _bias_table_bwd(dbias, onehot, *, name):
    H, n = dbias.shape
    B = onehot.shape[0]
    bc = _pick(n, 8192, LANES)

    def kernel(db_ref, oh_ref, o_ref):
        @pl.when(pl.program_id(0) == 0)
        def _():
            o_ref[...] = jnp.zeros_like(o_ref)

        hi, mid, lo = _split3(db_ref[...])
        oh = oh_ref[...]
        o_ref[...] += _dot(oh, hi, "nt") + _dot(oh, mid, "nt") + _dot(oh, lo, "nt")

    return _pcall(
        kernel, name=name, dims=("arbitrary",),
        out_shape=jax.ShapeDtypeStruct((B, H), F32), grid=(n // bc,),
        in_specs=[pl.BlockSpec((H, bc), lambda i: (0, i)), pl.BlockSpec((B, bc), lambda i: (0, i))],
        out_specs=pl.BlockSpec((B, H), lambda i: (0, 0)),
    )(dbias, onehot)


def _swa_probs(q2, kp, kc, bias_ref, lse, n, G):
    Q = SWA_BLOCK
    scale = DH_B ** -0.5
    sp = (_dot(q2, kp, "nt") * scale).reshape(G, Q, Q) + bias_ref[:, 0]
    sc = (_dot(q2, kc, "nt") * scale).reshape(G, Q, Q) + bias_ref[:, 1]
    row = lax.broadcasted_iota(jnp.int32, (G, Q, Q), 1)
    col = lax.broadcasted_iota(jnp.int32, (G, Q, Q), 2)
    vis_p = (col > row) & (n > 0)
    vis_c = col <= row
    sp = jnp.where(vis_p, sp, NEG)
    sc = jnp.where(vis_c, sc, NEG)
    if lse is None:
        return sp, sc, vis_p, vis_c
    pp = jnp.where(vis_p, jnp.exp(sp - lse), 0.0)
    pc = jnp.where(vis_c, jnp.exp(sc - lse), 0.0)
    return pp, pc


def _swa_fwd(q, k, v, bias, sinks, *, name):
    HKV, G, T, _ = q.shape
    Q = SWA_BLOCK
    N = T // Q

    def kernel(q_ref, kp_ref, kc_ref, vp_ref, vc_ref, b_ref, s_ref, o_ref, lse_ref):
        n = pl.program_id(1)
        q2 = q_ref[...].reshape(G * Q, DH_B)
        sp, sc, vis_p, vis_c = _swa_probs(q2, kp_ref[...], kc_ref[...], b_ref, None, n, G)
        sink = s_ref[...][:, :, 0:1]
        m = jnp.maximum(jnp.maximum(jnp.max(sp, axis=-1, keepdims=True), jnp.max(sc, axis=-1, keepdims=True)), sink)
        pp = jnp.where(vis_p, jnp.exp(sp - m), 0.0)
        pc = jnp.where(vis_c, jnp.exp(sc - m), 0.0)
        denom = jnp.sum(pp, axis=-1, keepdims=True) + jnp.sum(pc, axis=-1, keepdims=True) + jnp.exp(sink - m)
        o = _dot(pp.reshape(G * Q, Q).astype(BF16), vp_ref[...]) + _dot(pc.reshape(G * Q, Q).astype(BF16), vc_ref[...])
        o_ref[...] = (o.reshape(G, Q, DH_B) / denom).astype(BF16)
        lse_ref[...] = m + jnp.log(denom)

    prev = lambda h, n: (h, jnp.maximum(n - 1, 0), 0)
    cur = lambda h, n: (h, n, 0)
    kv = lambda f: pl.BlockSpec((None, Q, DH_B), f)
    return _pcall(
        kernel, name=name, dims=("parallel", "parallel"),
        out_shape=(jax.ShapeDtypeStruct((HKV, G, T, DH_B), BF16), jax.ShapeDtypeStruct((HKV, G, T, 1), F32)),
        grid=(HKV, N),
        in_specs=[pl.BlockSpec((None, G, Q, DH_B), lambda h, n: (h, 0, n, 0)),
                  kv(prev), kv(cur), kv(prev), kv(cur),
                  pl.BlockSpec((None, G, 2, Q, Q), lambda h, n: (h, 0, 0, 0, 0)),
                  pl.BlockSpec((None, G, 1, LANES), lambda h, n: (h, 0, 0, 0))],
        out_specs=(pl.BlockSpec((None, G, Q, DH_B), lambda h, n: (h, 0, n, 0)),
                   pl.BlockSpec((None, G, Q, 1), lambda h, n: (h, 0, n, 0))),
    )(q, k, k, v, v, bias, sinks)


def _swa_bwd(q, k, v, o, do, lse, bias, sinks, *, name):
    HKV, G, T, _ = q.shape
    Q = SWA_BLOCK
    N = T // Q
    scale = DH_B ** -0.5

    def kernel(q_ref, kp_ref, kc_ref, vp_ref, vc_ref, o_ref, do_ref, lse_ref, b_ref, s_ref,
               dq_ref, dk_ref, dv_ref, db_ref, ds_ref, ck_sc, cv_sc):
        n = pl.program_id(1)

        @pl.when(n == 0)
        def _():
            db_ref[...] = jnp.zeros_like(db_ref)
            ds_ref[...] = jnp.zeros_like(ds_ref)
            ck_sc[...] = jnp.zeros_like(ck_sc)
            cv_sc[...] = jnp.zeros_like(cv_sc)

        @pl.when(n < N)
        def _():
            q2 = q_ref[...].reshape(G * Q, DH_B)
            do3 = do_ref[...]
            do2 = do3.reshape(G * Q, DH_B)
            kp, kc, vp, vc = kp_ref[...], kc_ref[...], vp_ref[...], vc_ref[...]
            lse_v = lse_ref[...]
            pp, pc = _swa_probs(q2, kp, kc, b_ref, lse_v, n, G)
            delta = jnp.sum(do3.astype(F32) * o_ref[...].astype(F32), axis=-1, keepdims=True)
            dsp = pp * (_dot(do2, vp, "nt").reshape(G, Q, Q) - delta)
            dsc = pc * (_dot(do2, vc, "nt").reshape(G, Q, Q) - delta)
            p_sink = jnp.exp(s_ref[...][:, :, 0:1] - lse_v)
            ds_ref[...] += jnp.broadcast_to(-jnp.sum(p_sink * delta, axis=1, keepdims=True), (G, 1, LANES))
            db_ref[:, 0] += dsp
            db_ref[:, 1] += dsc
            dsp2 = dsp.reshape(G * Q, Q).astype(BF16)
            dsc2 = dsc.reshape(G * Q, Q).astype(BF16)
            dq = (_dot(dsp2, kp) + _dot(dsc2, kc)) * scale
            dq_ref[...] = dq.reshape(G, Q, DH_B).astype(BF16)
            pp2 = pp.reshape(G * Q, Q).astype(BF16)
            pc2 = pc.reshape(G * Q, Q).astype(BF16)
            dk_ref[...] = (ck_sc[...] + _dot(dsp2, q2, "tn") * scale).astype(BF16)
            dv_ref[...] = (cv_sc[...] + _dot(pp2, do2, "tn")).astype(BF16)
            ck_sc[...] = _dot(dsc2, q2, "tn") * scale
            cv_sc[...] = _dot(pc2, do2, "tn")

        @pl.when(n == N)
        def _():
            dk_ref[...] = ck_sc[...].astype(BF16)
            dv_ref[...] = cv_sc[...].astype(BF16)

    qn = lambda n: jnp.minimum(n, N - 1)
    prev = lambda h, n: (h, jnp.maximum(qn(n) - 1, 0), 0)
    cur = lambda h, n: (h, qn(n), 0)
    out_kv = lambda h, n: (h, jnp.maximum(n - 1, 0), 0)
    kv = lambda f: pl.BlockSpec((None, Q, DH_B), f)
    qspec = pl.BlockSpec((None, G, Q, DH_B), lambda h, n: (h, 0, qn(n), 0))
    bspec = pl.BlockSpec((None, G, 2, Q, Q), lambda h, n: (h, 0, 0, 0, 0))
    sspec = pl.BlockSpec((None, G, 1, LANES), lambda h, n: (h, 0, 0, 0))
    return _pcall(
        kernel, name=name, dims=("parallel", "arbitrary"),
        out_shape=(jax.ShapeDtypeStruct((HKV, G, T, DH_B), BF16), jax.ShapeDtypeStruct((HKV, T, DH_B), BF16),
                   jax.ShapeDtypeStruct((HKV, T, DH_B), BF16), jax.ShapeDtypeStruct((HKV, G, 2, Q, Q), F32),
                   jax.ShapeDtypeStruct((HKV, G, 1, LANES), F32)),
        grid=(HKV, N + 1),
        in_specs=[qspec, kv(prev), kv(cur), kv(prev), kv(cur), qspec, qspec,
                  pl.BlockSpec((None, G, Q, 1), lambda h, n: (h, 0, qn(n), 0)), bspec, sspec],
        out_specs=(qspec, kv(out_kv), kv(out_kv), bspec, sspec),
        scratch_shapes=[pltpu.VMEM((Q, DH_B), F32), pltpu.VMEM((Q, DH_B), F32)],
    )(q, k, k, v, v, o, do, lse, bias, sinks)


def _branch_mix(oa, ob, wa_t, wb_t, proj, gate_col, *, name):
    T = oa.shape[0]
    D = wa_t.shape[0]
    bm, bn = _pick(T, 1024, 8), _pick(D, 512, LANES)
    assert gate_col % bn == 0
    ga0, gb0 = gate_col // bn, (gate_col + D) // bn

    def kernel(oa_ref, ob_ref, wa_ref, wb_ref, ga_ref, gb_ref, mix_ref, ya_ref, yb_ref):
        ya = _dot(oa_ref[...], wa_ref[...], "nt")
        yb = _dot(ob_ref[...], wb_ref[...], "nt")
        mix = _sigmoid(ga_ref[...].astype(F32)) * ya + _sigmoid(gb_ref[...].astype(F32)) * yb
        mix_ref[...] = mix.astype(BF16)
        ya_ref[...] = ya.astype(BF16)
        yb_ref[...] = yb.astype(BF16)

    out = pl.BlockSpec((bm, bn), lambda i, j: (i, j))
    return _pcall(
        kernel, name=name, dims=("parallel", "parallel"),
        out_shape=tuple(jax.ShapeDtypeStruct((T, D), BF16) for _ in range(3)),
        grid=(T // bm, D // bn),
        in_specs=[pl.BlockSpec((bm, oa.shape[1]), lambda i, j: (i, 0)),
                  pl.BlockSpec((bm, ob.shape[1]), lambda i, j: (i, 0)),
                  pl.BlockSpec((bn, oa.shape[1]), lambda i, j: (j, 0)),
                  pl.BlockSpec((bn, ob.shape[1]), lambda i, j: (j, 0)),
                  pl.BlockSpec((bm, bn), lambda i, j: (i, ga0 + j)),
                  pl.BlockSpec((bm, bn), lambda i, j: (i, gb0 + j))],
        out_specs=(out, out, out),
    )(oa, ob, wa_t, wb_t, proj, proj)


def _mix_bwd(dmixed, ya, yb, proj, gate_col, *, deps=(), name):
    T, D = dmixed.shape
    br, bc = _pick(T, 512, 8), _pick(D, 512, LANES)
    assert gate_col % bc == 0
    ga0, gb0 = gate_col // bc, (gate_col + D) // bc

    def kernel(dm_ref, ya_ref, yb_ref, ga_ref, gb_ref, dya_ref, dyb_ref, dga_ref, dgb_ref):
        dm = dm_ref[...].astype(F32)
        sa = _sigmoid(ga_ref[...].astype(F32))
        sb = _sigmoid(gb_ref[...].astype(F32))
        dya_ref[...] = (dm * sa).astype(BF16)
        dyb_ref[...] = (dm * sb).astype(BF16)
        dga_ref[...] = (dm * ya_ref[...].astype(F32) * sa * (1.0 - sa)).astype(BF16)
        dgb_ref[...] = (dm * yb_ref[...].astype(F32) * sb * (1.0 - sb)).astype(BF16)

    blk = pl.BlockSpec((br, bc), lambda i, j: (i, j))
    return _pcall(
        kernel, name=name, dims=("parallel", "parallel"), deps=deps,
        out_shape=tuple(jax.ShapeDtypeStruct((T, D), BF16) for _ in range(4)),
        grid=(T // br, D // bc),
        in_specs=[blk, blk, blk, pl.BlockSpec((br, bc), lambda i, j: (i, ga0 + j)),
                  pl.BlockSpec((br, bc), lambda i, j: (i, gb0 + j))],
        out_specs=(blk, blk, blk, blk),
    )(dmixed, ya, yb, proj, proj)


def _ffn_up(h, wg_t, wu_t, *, name):
    T, D = h.shape
    FP = wg_t.shape[0]
    bm, bn = _pick(T, 1024, 8), _pick(FP, 512, LANES)

    def kernel(h_ref, wg_ref, wu_ref, g_ref, u_ref, hid_ref):
        hv = h_ref[...]
        g = _dot(hv, wg_ref[...], "nt")
        u = _dot(hv, wu_ref[...], "nt")
        g_ref[...] = g.astype(BF16)
        u_ref[...] = u.astype(BF16)
        hid_ref[...] = (g * _sigmoid(g) * u).astype(BF16)

    out = pl.BlockSpec((bm, bn), lambda i, j: (i, j))
    wspec = pl.BlockSpec((bn, D), lambda i, j: (j, 0))
    return _pcall(
        kernel, name=name, dims=("parallel", "parallel"),
        out_shape=tuple(jax.ShapeDtypeStruct((T, FP), BF16) for _ in range(3)),
        grid=(T // bm, FP // bn),
        in_specs=[pl.BlockSpec((bm, D), lambda i, j: (i, 0)), wspec, wspec],
        out_specs=(out, out, out),
    )(h, wg_t, wu_t)


def _swiglu_bwd(dhid, gate, up, *, deps=(), name):
    T, FP = dhid.shape
    br, bc = _pick(T, 512, 8), _pick(FP, 1024, LANES)

    def kernel(dh_ref, g_ref, u_ref, dg_ref, du_ref):
        dh = dh_ref[...].astype(F32)
        g = g_ref[...].astype(F32)
        u = u_ref[...].astype(F32)
        sg = _sigmoid(g)
        dg_ref[...] = (dh * u * sg * (1.0 + g * (1.0 - sg))).astype(BF16)
        du_ref[...] = (dh * g * sg).astype(BF16)

    blk = pl.BlockSpec((br, bc), lambda i, j: (i, j))
    return _pcall(
        kernel, name=name, dims=("parallel", "parallel"), deps=deps,
        out_shape=(jax.ShapeDtypeStruct((T, FP), BF16), jax.ShapeDtypeStruct((T, FP), BF16)),
        grid=(T // br, FP // bc), in_specs=[blk, blk, blk], out_specs=(blk, blk),
    )(dhid, gate, up)


def _adamw(w, g, m, v, *, name):
    Rw, Cw = w.shape
    br = _pick(Rw, max(8, (1 << 19) // Cw // 8 * 8), 8)

    def kernel(w_ref, g_ref, m_ref, v_ref, d_ref, nm_ref, nv_ref):
        gv = g_ref[...]
        nm = ADAM_B1 * m_ref[...] + (1.0 - ADAM_B1) * gv
        nv = ADAM_B2 * v_ref[...] + (1.0 - ADAM_B2) * (gv * gv)
        m_hat = nm / (1.0 - ADAM_B1 ** ADAM_STEP)
        v_hat = nv / (1.0 - ADAM_B2 ** ADAM_STEP)
        d_ref[...] = -ADAM_LR * (m_hat / (jnp.sqrt(v_hat) + ADAM_EPS) + ADAM_WD * w_ref[...])
        nm_ref[...] = nm
        nv_ref[...] = nv

    blk = pl.BlockSpec((br, Cw), lambda i: (i, 0))
    return _pcall(
        kernel, name=name, dims=("parallel",),
        out_shape=tuple(jax.ShapeDtypeStruct((Rw, Cw), F32) for _ in range(3)),
        grid=(Rw // br,), in_specs=[blk, blk, blk, blk], out_specs=(blk, blk, blk),
    )(w, g, m, v)


def _my_place():
    return lax.axis_index("x"), lax.axis_index("y"), lax.axis_index("c")


def _flip(v, bit):
    return 1 - v if bit else v


def _all_gather_rows(shard, *, pad_rows=0, name):
    R, C = shard.shape
    flat = R % BF16_TILE_ROWS == 0
    assert flat or pad_rows == 0
    out_shape = (N_DEV * R + pad_rows, C) if flat else (N_DEV, R, C)

    def body(*refs):
        if pad_rows:
            x_ref, z_ref, out_ref, send_sems, recv_sems, local_sems = refs
        else:
            x_ref, out_ref, send_sems, recv_sems, local_sems = refs
        x, y, c = _my_place()
        me, sibling = (x, y, c), (x, y, 1 - c)
        chips = [(1 - x, y), (x, 1 - y), (1 - x, 1 - y)]

        def block(px, py, pc):
            idx = 4 * px + 2 * py + pc
            if flat:
                return out_ref.at[pl.ds(pl.multiple_of(idx * R, BF16_TILE_ROWS), R), :]
            return out_ref.at[idx]

        def copy(k, blk, to, src=None):
            return pltpu.make_async_remote_copy(
                src_ref=block(*blk) if src is None else src, dst_ref=block(*blk),
                send_sem=send_sems.at[k], recv_sem=recv_sems.at[k], device_id=to, device_id_type=MESH)

        mine = pltpu.make_async_copy(x_ref, block(*me), local_sems.at[0])
        mine.start()
        if pad_rows:
            zero = pltpu.make_async_copy(z_ref, out_ref.at[pl.ds(N_DEV * R, pad_rows), :], local_sems.at[1])
            zero.start()
        first = [copy(0, me, sibling, src=x_ref)]
        first += [copy(1 + j, me, (*chip, c), src=x_ref) for j, chip in enumerate(chips)]
        for cp in first:
            cp.start()
        passed = [copy(4 + j, (*chip, c), sibling) for j, chip in enumerate(chips)]
        for j, chip in enumerate(chips):
            copy(1 + j, (*chip, c), me).wait_recv()
            passed[j].start()
        copy(0, sibling, me).wait_recv()
        for j, chip in enumerate(chips):
            copy(4 + j, (*chip, 1 - c), me).wait_recv()
        for cp in first + passed:
            cp.wait_send()
        mine.wait()
        if pad_rows:
            zero.wait()

    operands = [shard] + ([jnp.zeros((pad_rows, C), shard.dtype)] if pad_rows else [])
    return _comm_call(
        body, name=name,
        out_shape=jax.ShapeDtypeStruct(out_shape, shard.dtype),
        in_specs=[ANY] * len(operands), out_specs=ANY,
        scratch_shapes=[pltpu.SemaphoreType.DMA((7,)), pltpu.SemaphoreType.DMA((7,)), pltpu.SemaphoreType.DMA((2,))],
    )(*operands)


HBM_SPEC = pl.BlockSpec(memory_space=pltpu.HBM)
SEM_SPEC = pl.BlockSpec(memory_space=pltpu.SEMAPHORE)
SPLIT_EFFECT = pltpu.SideEffectType.DATAFLOW_SIDE_EFFECTING
N_PEERS = N_DEV - 1


def _hbm(a):
    return pltpu.with_memory_space_constraint(a, pltpu.HBM)


def _peer(x, y, c, k):
    return _flip(x, k & 4), _flip(y, k & 2), _flip(c, k & 1)


def _row_block(ref, idx, R):
    if len(ref.shape) == 3:
        return ref.at[idx]
    return ref.at[pl.ds(pl.multiple_of(idx * R, BF16_TILE_ROWS), R), :]


def _split_copies(src_ref, land_ref, sems, R, gather):
    send_sems, recv_sems, local_sem = sems
    x, y, c = _my_place()
    me = 4 * x + 2 * y + c
    if gather:
        local = pltpu.make_async_copy(src_ref, _row_block(land_ref, me, R), local_sem.at[0])
    else:
        local = pltpu.make_async_copy(_row_block(src_ref, me, R), land_ref.at[0], local_sem.at[0])
    remote = []
    for k in range(1, N_DEV):
        px, py, pc = _peer(x, y, c, k)
        if gather:
            src, dst = src_ref, _row_block(land_ref, me, R)
            arrives = _row_block(land_ref, 4 * px + 2 * py + pc, R)
        else:
            src, dst = _row_block(src_ref, 4 * px + 2 * py + pc, R), land_ref.at[k]
            arrives = dst
        send = pltpu.make_async_remote_copy(src_ref=src, dst_ref=dst, send_sem=send_sems.at[k - 1],
                                            recv_sem=recv_sems.at[k - 1], device_id=(px, py, pc), device_id_type=MESH)
        recv = pltpu.make_async_remote_copy(src_ref=src, dst_ref=arrives, send_sem=send_sems.at[k - 1],
                                            recv_sem=recv_sems.at[k - 1], device_id=(px, py, pc), device_id_type=MESH)
        remote.append((send, recv))
    return local, remote


def _exchange_start(src, land, after, R, gather, *, name):
    def body(src_ref, land_ref, after_ref, send_sems, recv_sems, local_sem, src_thru, land_thru, token):
        local, remote = _split_copies(src_ref, land_ref, (send_sems, recv_sems, local_sem), R, gather)
        local.start()
        for send, _ in remote:
            send.start()
        token[...] = jnp.zeros_like(token)

    out = pl.pallas_call(
        body, name=name,
        out_shape=(pltpu.SemaphoreType.DMA((N_PEERS,)), pltpu.SemaphoreType.DMA((N_PEERS,)), pltpu.SemaphoreType.DMA((1,)),
                   pltpu.HBM(src.shape, src.dtype), pltpu.HBM(land.shape, land.dtype),
                   jax.ShapeDtypeStruct((8, LANES), F32)),
        in_specs=(HBM_SPEC, HBM_SPEC, ANY),
        out_specs=(SEM_SPEC, SEM_SPEC, SEM_SPEC, HBM_SPEC, HBM_SPEC, pl.BlockSpec(memory_space=pltpu.VMEM)),
        input_output_aliases={0: 3, 1: 4},
        compiler_params=pltpu.CompilerParams(has_side_effects=SPLIT_EFFECT),
    )(_hbm(src), _hbm(land), after)
    return out[:3], out[3], out[4], out[5]


def _exchange_wait(sems, src_thru, land_thru, after, R, gather, *, name):
    def body(src_ref, land_ref, send_sems, recv_sems, local_sem, after_ref, src_dead, got_ref):
        local, remote = _split_copies(src_ref, land_ref, (send_sems, recv_sems, local_sem), R, gather)
        for send, recv in remote:
            send.wait_send()
            recv.wait_recv()
        local.wait()

    return pl.pallas_call(
        body, name=name,
        out_shape=(pltpu.HBM(src_thru.shape, src_thru.dtype), pltpu.HBM(land_thru.shape, land_thru.dtype)),
        in_specs=(HBM_SPEC, HBM_SPEC, SEM_SPEC, SEM_SPEC, SEM_SPEC, ANY),
        out_specs=(HBM_SPEC, HBM_SPEC),
        input_output_aliases={0: 0, 1: 1},
        compiler_params=pltpu.CompilerParams(has_side_effects=SPLIT_EFFECT),
    )(src_thru, land_thru, *sems, after)[1]


def _gather_start(shard, after, *, pad_rows=0, name):
    R, C = shard.shape
    assert R % BF16_TILE_ROWS == 0
    if pad_rows:
        land = jnp.zeros((N_DEV * R + pad_rows, C), shard.dtype)
    else:
        land = lax.empty((N_DEV * R, C), shard.dtype)
    return _exchange_start(shard, land, after, R, True, name=name) + (R, True)


def _scatter_start(g, R, after, *, name):
    land = lax.empty((N_DEV, R, g.shape[-1]), g.dtype)
    return _exchange_start(g, land, after, R, False, name=name) + (R, False)


def _finish(handle, after, *, name):
    sems, src_thru, land_thru, _, R, gather = handle
    return _exchange_wait(sems, src_thru, land_thru, after, R, gather, name=name)


def _sum_slots(land, *, name):
    _, R, C = land.shape
    bc = _pick(C, max(LANES, (1 << 19) // R // LANES * LANES), LANES)

    def kernel(l_ref, o_ref):
        acc = l_ref[0].astype(F32)
        for k in range(1, N_DEV):
            acc = acc + l_ref[k].astype(F32)
        o_ref[...] = acc

    return _pcall(
        kernel, name=name, dims=("parallel",),
        out_shape=jax.ShapeDtypeStruct((R, C), F32), grid=(C // bc,),
        in_specs=[pl.BlockSpec((N_DEV, R, bc), lambda i: (0, 0, i))],
        out_specs=pl.BlockSpec((R, bc), lambda i: (0, i)),
    )(land)


def _all_reduce_small(v, *, name):
    rows, C = v.shape

    def body(v_ref, o_ref, land_ref, send_sems, recv_sems):
        x, y, c = _my_place()
        me = 4 * x + 2 * y + c
        copies = []
        for k in range(1, N_DEV):
            peer = (_flip(x, k & 4), _flip(y, k & 2), _flip(c, k & 1))
            copies.append(pltpu.make_async_remote_copy(
                src_ref=v_ref, dst_ref=land_ref.at[me],
                send_sem=send_sems.at[k - 1], recv_sem=recv_sems.at[k - 1], device_id=peer, device_id_type=MESH))
        for cp in copies:
            cp.start()
        land_ref[me] = v_ref[...]
        for k in range(1, N_DEV):
            peer_idx = 4 * _flip(x, k & 4) + 2 * _flip(y, k & 2) + _flip(c, k & 1)
            pltpu.make_async_remote_copy(
                src_ref=v_ref, dst_ref=land_ref.at[peer_idx],
                send_sem=send_sems.at[k - 1], recv_sem=recv_sems.at[k - 1],
                device_id=(x, y, c), device_id_type=MESH).wait_recv()
        for cp in copies:
            cp.wait_send()
        acc = land_ref[0]
        for s in range(1, N_DEV):
            acc = acc + land_ref[s]
        o_ref[...] = acc

    vm = pl.BlockSpec(memory_space=pltpu.VMEM)
    return _comm_call(
        body, name=name,
        out_shape=jax.ShapeDtypeStruct((rows, C), F32),
        in_specs=[vm], out_specs=vm,
        scratch_shapes=[pltpu.VMEM((N_DEV, rows, C), F32),
                        pltpu.SemaphoreType.DMA((7,)), pltpu.SemaphoreType.DMA((7,))],
    )(v)


def kernel(x, norm1_g, w_in, b_forget, attn_sinks, rel_bias, w_branch_a, w_branch_b, w_out, norm2_g, w_ffn_gate, w_ffn_up, w_ffn_down, final_g, loss_target, m_norm1_g, m_w_in, m_b_forget, m_attn_sinks, m_rel_bias, m_w_branch_a, m_w_branch_b, m_w_out, m_norm2_g, m_w_ffn_gate, m_w_ffn_up, m_w_ffn_down, m_final_g, v_norm1_g, v_w_in, v_b_forget, v_attn_sinks, v_rel_bias, v_w_branch_a, v_w_branch_b, v_w_out, v_norm2_g, v_w_ffn_gate, v_w_ffn_up, v_w_ffn_down, v_final_g):
    xs = x[0]
    T, D = xs.shape
    H_A, H_B = b_forget.shape[-1], attn_sinks.shape[-1]
    WA, QB = H_A * DH_A, H_B * DH_B
    R_IN = w_in.shape[-1]
    W_IN = N_DEV * R_IN
    KB = (W_IN - 3 * WA - H_A - QB - 2 * D) // 2
    HKV = KB // DH_B
    G = H_B // HKV
    N_BIG = W_IN - H_A
    GATE_COL = 3 * WA + QB + 2 * KB
    R_F = w_ffn_gate.shape[-1]
    F = N_DEV * R_F
    FP = _round_up(F, 512)
    assert H_A <= LANES and T % SWA_BLOCK == 0

    win_s = w_in[0].T.astype(BF16)
    wa_s = w_branch_a[0].T.astype(BF16)
    wb_s = w_branch_b[0].T.astype(BF16)
    wout_s = w_out[0].astype(BF16)
    wg_s = w_ffn_gate[0].T.astype(BF16)
    wu_s = w_ffn_up[0].T.astype(BF16)
    wd_s = w_ffn_down[0].astype(BF16)

    win_all = _all_gather_rows(win_s, name="ag_w_in").reshape(W_IN, D)
    w_big = jnp.concatenate([win_all[:3 * WA], win_all[3 * WA + H_A:]], axis=0)
    w_f = jnp.pad(win_all[3 * WA:3 * WA + H_A], ((0, LANES - H_A), (0, 0)))
    ag_wa = _gather_start(wa_s, win_all, name="ag_start_w_branch_a")
    ag_wb = _gather_start(wb_s, ag_wa[3], name="ag_start_w_branch_b")
    ag_wout = _gather_start(wout_s, ag_wb[3], name="ag_start_w_out")
    ag_wg = _gather_start(wg_s, ag_wout[3], pad_rows=FP - F, name="ag_start_w_ffn_gate")
    ag_wu = _gather_start(wu_s, ag_wg[3], pad_rows=FP - F, name="ag_start_w_ffn_up")
    ag_wd = _gather_start(wd_s, ag_wu[3], pad_rows=FP - F, name="ag_start_w_ffn_down")

    BM = 1024
    bn_big = _pick(N_BIG, 768, LANES)

    h1, rstd1 = _rms_fwd(xs, norm1_g, deps=[ag_wd[3]], name="rms1_fwd")
    proj = _matmul([(h1, w_big)], "nt", bm=BM, bn=bn_big, name="proj_fwd")
    f_logit = _matmul([(h1, w_f)], "nt", bm=BM, bn=LANES, out_dtype=F32, name="forget_fwd")
    b_pad = jnp.pad(b_forget, ((0, 0), (0, LANES - H_A)))
    c_all, sneg = _fox_gate_fwd(f_logit, b_pad, name="fox_gate_fwd")
    c_heads = c_all[:, :H_A].T
    c_col, c_row = c_heads[:, :, None], c_heads[:, None, :]
    oa, lse_a = _fox_fwd(proj, c_col, c_row, H_A, name="fox_fwd")

    def heads_q(a):
        return a.reshape(T, HKV, G, DH_B).transpose(1, 2, 0, 3)

    def heads_kv(a):
        return a.reshape(T, HKV, DH_B).transpose(1, 0, 2)

    qb = heads_q(proj[:, 3 * WA:3 * WA + QB])
    kb = heads_kv(proj[:, 3 * WA + QB:3 * WA + QB + KB])
    vb = heads_kv(proj[:, 3 * WA + QB + KB:GATE_COL])
    onehot = _bucket_onehot()
    bias = _bias_table(rel_bias, onehot, name="rel_bias_table").reshape(HKV, G, 2, SWA_BLOCK, SWA_BLOCK)
    sinks_b = jnp.broadcast_to(attn_sinks.reshape(HKV, G, 1, 1), (HKV, G, 1, LANES))
    ob4, lse_b = _swa_fwd(qb, kb, vb, bias, sinks_b, name="swa_fwd")
    ob = ob4.transpose(2, 0, 1, 3).reshape(T, QB)

    wa_t = _finish(ag_wa, ob, name="ag_wait_w_branch_a")
    wb_t = _finish(ag_wb, wa_t, name="ag_wait_w_branch_b")
    mixed, ya, yb = _branch_mix(oa, ob, wa_t, wb_t, proj, GATE_COL, name="branch_mix")
    wout = _finish(ag_wout, mixed, name="ag_wait_w_out")
    x1 = _matmul([(mixed, wout)], "nn", bm=BM, bn=512, out_dtype=F32, addend=xs, name="out_proj_fwd")

    h2, rstd2 = _rms_fwd(x1, norm2_g, name="rms2_fwd")
    wg_t = _finish(ag_wg, h2, name="ag_wait_w_ffn_gate")
    wu_t = _finish(ag_wu, wg_t, name="ag_wait_w_ffn_up")
    gate, up, hidden = _ffn_up(h2, wg_t, wu_t, name="ffn_up")
    wd = _finish(ag_wd, hidden, name="ag_wait_w_ffn_down")
    tk_f = _pick(FP, 2816, LANES)
    x2 = _matmul([(hidden, wd)], "nn", bm=BM, bn=1024, tk=tk_f, out_dtype=F32, addend=x1, name="ffn_down_fwd")

    dx2, loss_part, g_final = _final_loss(x2, final_g.reshape(1, D), loss_target[0], name="final_loss")
    loss = lax.psum(loss_part[0, 0], MESH_AXES)

    dx2_b = dx2.astype(BF16)
    dhid = _matmul([(dx2_b, wd)], "nt", bm=BM, bn=1024, name="ffn_down_bwd_x")
    g_wd = _matmul([(hidden, dx2_b)], "tn", bm=1024, bn=1024, name="ffn_down_bwd_w")
    rs_wd = _scatter_start(g_wd, R_F, loss_part, name="rs_start_w_ffn_down")
    dgate, dup = _swiglu_bwd(dhid, gate, up, deps=[rs_wd[3]], name="swiglu_bwd")
    g_wg = _matmul([(dgate, h2)], "tn", bm=1024, bn=1024, name="ffn_gate_bwd_w")
    rs_wg = _scatter_start(g_wg, R_F, rs_wd[3], name="rs_start_w_ffn_gate")
    g_wu = _matmul([(dup, h2)], "tn", bm=1024, bn=1024, deps=[rs_wg[3]], name="ffn_up_bwd_w")
    rs_wu = _scatter_start(g_wu, R_F, rs_wg[3], name="rs_start_w_ffn_up")
    dh2 = _matmul([(dgate, wg_t), (dup, wu_t)], "nn", bm=BM, bn=1024, tk=_pick(FP, 1408, LANES), out_dtype=F32,
                  deps=[rs_wu[3]], name="ffn_up_bwd_x")
    dx1, g_norm2 = _rms_bwd(dx2, dh2, x1, rstd2, norm2_g, name="rms2_bwd")

    dx1_b = dx1.astype(BF16)
    g_wout = _matmul([(mixed, dx1_b)], "tn", bm=1024, bn=1024, name="out_proj_bwd_w")
    rs_wout = _scatter_start(g_wout, D // N_DEV, rs_wu[3], name="rs_start_w_out")
    dmixed = _matmul([(dx1_b, wout)], "nt", bm=BM, bn=1024, deps=[rs_wout[3]], name="out_proj_bwd_x")
    dya, dyb, dga, dgb = _mix_bwd(dmixed, ya, yb, proj, GATE_COL, name="mix_bwd")
    g_wa = _matmul([(dya, oa)], "tn", bm=1024, bn=1024, name="branch_a_bwd_w")
    rs_wa = _scatter_start(g_wa, D // N_DEV, rs_wout[3], name="rs_start_w_branch_a")
    g_wb = _matmul([(dyb, ob)], "tn", bm=1024, bn=1024, deps=[rs_wa[3]], name="branch_b_bwd_w")
    rs_wb = _scatter_start(g_wb, D // N_DEV, rs_wa[3], name="rs_start_w_branch_b")
    doa = _matmul([(dya, wa_t)], "nn", bm=BM, bn=1024, deps=[rs_wb[3]], name="branch_a_bwd_x")
    dob = _matmul([(dyb, wb_t)], "nn", bm=BM, bn=1024, name="branch_b_bwd_x")

    delta_a = _rowdot(doa, oa, H_A, name="fox_delta")
    dqa, dka, dva, dc_key, dc_query = _fox_bwd(proj, doa, lse_a, delta_a, c_col, c_row, H_A, name="fox_bwd")
    dc = jnp.pad((dc_key[:, 0, :] + dc_query[:, :, 0]).T, ((0, 0), (0, LANES - H_A)))
    df, g_bf = _fox_gate_bwd(dc, sneg, name="fox_gate_bwd")

    dqb4, dkb3, dvb3, dbias, dsinks = _swa_bwd(qb, kb, vb, ob4, heads_q(dob), lse_b, bias, sinks_b, name="swa_bwd")
    g_relb = _bias_table_bwd(dbias.reshape(H_B, -1), onehot, name="rel_bias_table_bwd")
    dqb = dqb4.transpose(2, 0, 1, 3).reshape(T, QB)
    dkb = dkb3.transpose(1, 0, 2).reshape(T, KB)
    dvb = dvb3.transpose(1, 0, 2).reshape(T, KB)

    dproj = jnp.concatenate([dqa.astype(BF16), dka, dva, dqb, dkb, dvb, dga, dgb], axis=1)
    g_wbig = _matmul([(dproj, h1)], "tn", bm=bn_big, bn=1024, name="proj_bwd_w")
    g_wf = _matmul([(df, h1)], "tn", bm=LANES, bn=1024, name="forget_bwd_w")
    g_win = jnp.concatenate([g_wbig[:3 * WA], g_wf[:H_A], g_wbig[3 * WA:]], axis=0).reshape(N_DEV, R_IN, D)
    rs_win = _scatter_start(g_win, R_IN, rs_wb[3], name="rs_start_w_in")
    dh_f = _matmul([(df, w_f)], "nn", bm=BM, bn=1024, out_dtype=F32, deps=[rs_win[3]], name="forget_bwd_x")
    tk_big = _pick(N_BIG, 2816, LANES)
    dh1 = _matmul([(dproj, w_big)], "nn", bm=BM, bn=1024, tk=tk_big, out_dtype=F32, addend=dh_f, name="proj_bwd_x")
    grad_x, g_norm1 = _rms_bwd(dx1, dh1, xs, rstd1, norm1_g, name="rms1_bwd")

    def adam(w, g, m, v, nm):
        shape = w.shape
        two_d = lambda a: a.reshape(-1, shape[-1])
        d, nm_, nv_ = _adamw(two_d(w), two_d(g), two_d(m), two_d(v), name="adamw_" + nm)
        return g.reshape(shape), d.reshape(shape), nm_.reshape(shape), nv_.reshape(shape)

    def reduce(handle, after, nm):
        return _sum_slots(_finish(handle, after, name="rs_wait_" + nm), name="sum_" + nm)

    big = {}
    gw_d = reduce(rs_wd, grad_x, "w_ffn_down")
    big["w_ffn_down"] = adam(w_ffn_down, gw_d[None], m_w_ffn_down, v_w_ffn_down, "w_ffn_down")
    gw_g = reduce(rs_wg, big["w_ffn_down"][1], "w_ffn_gate").T
    big["w_ffn_gate"] = adam(w_ffn_gate, gw_g[None], m_w_ffn_gate, v_w_ffn_gate, "w_ffn_gate")
    gw_u = reduce(rs_wu, big["w_ffn_gate"][1], "w_ffn_up").T
    big["w_ffn_up"] = adam(w_ffn_up, gw_u[None], m_w_ffn_up, v_w_ffn_up, "w_ffn_up")
    gw_out = reduce(rs_wout, big["w_ffn_up"][1], "w_out")
    big["w_out"] = adam(w_out, gw_out[None], m_w_out, v_w_out, "w_out")
    gw_a = reduce(rs_wa, big["w_out"][1], "w_branch_a").T
    big["w_branch_a"] = adam(w_branch_a, gw_a[None], m_w_branch_a, v_w_branch_a, "w_branch_a")
    gw_b = reduce(rs_wb, big["w_branch_a"][1], "w_branch_b").T
    big["w_branch_b"] = adam(w_branch_b, gw_b[None], m_w_branch_b, v_w_branch_b, "w_branch_b")
    gw_in = reduce(rs_win, big["w_branch_b"][1], "w_in").T
    big["w_in"] = adam(w_in, gw_in[None], m_w_in, v_w_in, "w_in")

    n_rb = NUM_BUCKETS * H_B
    assert D >= n_rb and D >= H_A + H_B
    small = jnp.concatenate([
        g_norm1, g_norm2, g_final,
        jnp.pad(jnp.concatenate([g_bf[:, :H_A], dsinks[:, :, 0, 0].reshape(1, H_B)], axis=1), ((0, 0), (0, D - H_A - H_B))),
        jnp.pad(g_relb.reshape(1, n_rb), ((0, 0), (0, D - n_rb))),
        jnp.zeros((3, D), F32)], axis=0)
    small = _all_reduce_small(small, name="ar_small")
    gs = {
        "norm1_g": small[0:1], "norm2_g": small[1:2], "final_g": small[2],
        "b_forget": small[3:4, :H_A], "attn_sinks": small[3:4, H_A:H_A + H_B],
        "rel_bias": small[4, :n_rb].reshape(NUM_BUCKETS, H_B),
    }

    def adam_small(ws, gsm, ms, vs):
        def pack(parts):
            rows = [jnp.pad(p.reshape(1, -1), ((0, 0), (0, D - p.size))) for p in parts]
            return jnp.concatenate(rows + [jnp.ones((8 - len(rows), D), F32)], axis=0)
        d, nm_, nv_ = _adamw(pack(ws), pack(gsm), pack(ms), pack(vs), name="adamw_small")
        unpack = lambda a: [a[i, :p.size].reshape(p.shape) for i, p in enumerate(ws)]
        return unpack(d), unpack(nm_), unpack(nv_)

    small_names = ["norm1_g", "b_forget", "attn_sinks", "rel_bias", "norm2_g", "final_g"]
    small_w = [norm1_g, b_forget, attn_sinks, rel_bias, norm2_g, final_g]
    small_g = [gs[n].reshape(w.shape) for n, w in zip(small_names, small_w)]
    small_m = [m_norm1_g, m_b_forget, m_attn_sinks, m_rel_bias, m_norm2_g, m_final_g]
    small_v = [v_norm1_g, v_b_forget, v_attn_sinks, v_rel_bias, v_norm2_g, v_final_g]
    sd, sm, sv = adam_small(small_w, small_g, small_m, small_v)
    for i, n in enumerate(small_names):
        big[n] = (small_g[i], sd[i], sm[i], sv[i])

    order = ["norm1_g", "w_in", "b_forget", "attn_sinks", "rel_bias", "w_branch_a", "w_branch_b", "w_out",
             "norm2_g", "w_ffn_gate", "w_ffn_up", "w_ffn_down", "final_g"]
    grads = [big[n][0] for n in order]
    deltas = [big[n][1] for n in order]
    new_m = [big[n][2] for n in order]
    new_v = [big[n][3] for n in order]
    return (loss, grad_x[None], *grads, *deltas, *new_m, *new_v)
```

```python
import functools

import numpy as np
import jax
import jax.numpy as jnp
from jax import lax
from jax.experimental import pallas as pl
from jax.experimental.pallas import tpu as pltpu

N_DEV = 8
MESH_AXES = ("x", "y", "c")
DH_A = 128
DH_B = 64
SWA_BLOCK = 128
NUM_BUCKETS = 32
MAX_DISTANCE = 128
EPS = 1e-6
ADAM_LR = 0.001
ADAM_B1 = 0.9
ADAM_B2 = 0.999
ADAM_EPS = 1e-08
ADAM_WD = 0.01
ADAM_STEP = 10
NEG = -1e30
LANES = 128
BF16_TILE_ROWS = 16
V7X_VMEM_LIMIT = 56 * 1024 * 1024
F32 = jnp.float32
BF16 = jnp.bfloat16
MESH = pl.DeviceIdType.MESH
ANY = pl.BlockSpec(memory_space=pl.ANY)


def _round_up(n, m):
    return (n + m - 1) // m * m


def _pick(n, target, mult):
    best = None
    for d in range(mult, min(n, target) + 1, mult):
        if n % d == 0:
            best = d
    return n if best is None else best


def _pcall(kernel, *, name, in_specs, dims=None, deps=(), **kw):
    deps = tuple(deps)
    if deps:
        inner, n_in, n_dep = kernel, len(in_specs), len(deps)

        def kernel(*refs):
            inner(*refs[:n_in], *refs[n_in + n_dep:])

        in_specs = list(in_specs) + [ANY] * n_dep
    call = pl.pallas_call(
        kernel, name=name, in_specs=in_specs,
        compiler_params=pltpu.CompilerParams(dimension_semantics=dims, vmem_limit_bytes=V7X_VMEM_LIMIT),
        **kw)
    return lambda *operands: call(*operands, *deps)


def _comm_call(kernel, *, name, **kw):
    return pl.pallas_call(kernel, name=name, **kw)


def _sigmoid(v):
    return 1.0 / (1.0 + jnp.exp(-v))


_DIMS = {"nn": (((1,), (0,)), ((), ())), "nt": (((1,), (1,)), ((), ())), "tn": (((0,), (0,)), ((), ()))}


def _dot(a, b, form="nn"):
    return lax.dot_general(a, b, _DIMS[form], preferred_element_type=F32)


def _matmul(pairs, form, *, bm, bn, tk=None, out_dtype=BF16, addend=None, deps=(), name):
    a0, b0 = pairs[0]
    if form == "tn":
        K, M = a0.shape
    else:
        M, K = a0.shape
    N = b0.shape[0] if form == "nt" else b0.shape[1]
    tk = K if tk is None else tk
    bm, bn = min(bm, M), min(bn, N)
    assert M % bm == 0 and N % bn == 0 and K % tk == 0, (name, M, N, K, bm, bn, tk)
    nk = K // tk
    n_pairs = len(pairs)
    has_add = addend is not None
    if form == "tn":
        a_spec = pl.BlockSpec((tk, bm), lambda i, j, k: (k, i))
    else:
        a_spec = pl.BlockSpec((bm, tk), lambda i, j, k: (i, k))
    if form == "nt":
        b_spec = pl.BlockSpec((bn, tk), lambda i, j, k: (j, k))
    else:
        b_spec = pl.BlockSpec((tk, bn), lambda i, j, k: (k, j))
    o_spec = pl.BlockSpec((bm, bn), lambda i, j, k: (i, j))

    def kernel(*refs):
        ab = refs[:2 * n_pairs]
        c_ref = refs[2 * n_pairs] if has_add else None
        o_ref = refs[2 * n_pairs + has_add]

        def partial_sum():
            acc = _dot(ab[0][...], ab[1][...], form)
            for p in range(1, n_pairs):
                acc = acc + _dot(ab[2 * p][...], ab[2 * p + 1][...], form)
            return acc

        def finish(acc):
            if has_add:
                acc = acc + c_ref[...].astype(F32)
            o_ref[...] = acc.astype(o_ref.dtype)

        if nk == 1:
            finish(partial_sum())
        else:
            acc_ref = refs[-1]
            k = pl.program_id(2)

            @pl.when(k == 0)
            def _():
                acc_ref[...] = jnp.zeros_like(acc_ref)

            acc_ref[...] += partial_sum()

            @pl.when(k == nk - 1)
            def _():
                finish(acc_ref[...])

    operands, in_specs = [], []
    for a, b in pairs:
        operands += [a, b]
        in_specs += [a_spec, b_spec]
    if has_add:
        operands.append(addend)
        in_specs.append(o_spec)
    return _pcall(
        kernel, name=name, dims=("parallel", "parallel", "arbitrary"), deps=deps,
        out_shape=jax.ShapeDtypeStruct((M, N), out_dtype),
        grid=(M // bm, N // bn, nk), in_specs=in_specs, out_specs=o_spec,
        scratch_shapes=[pltpu.VMEM((bm, bn), F32)] if nk > 1 else [],
    )(*operands)


def _rms_fwd(x, g, *, deps=(), name):
    T, D = x.shape
    br = _pick(T, 256, 8)

    def kernel(x_ref, g_ref, h_ref, r_ref):
        xv = x_ref[...]
        rstd = lax.rsqrt(jnp.mean(xv * xv, axis=-1, keepdims=True) + EPS)
        h_ref[...] = (xv * rstd * g_ref[...]).astype(BF16)
        r_ref[...] = rstd

    return _pcall(
        kernel, name=name, dims=("parallel",), deps=deps,
        out_shape=(jax.ShapeDtypeStruct((T, D), BF16), jax.ShapeDtypeStruct((T, 1), F32)),
        grid=(T // br,),
        in_specs=[pl.BlockSpec((br, D), lambda i: (i, 0)), pl.BlockSpec((1, D), lambda i: (0, 0))],
        out_specs=(pl.BlockSpec((br, D), lambda i: (i, 0)), pl.BlockSpec((br, 1), lambda i: (i, 0))),
    )(x, g)


def _rms_bwd(dres, dh, x, rstd, g, *, deps=(), name):
    T, D = x.shape
    br = _pick(T, 256, 8)

    def kernel(dres_ref, dh_ref, x_ref, r_ref, g_ref, dx_ref, gg_ref):
        @pl.when(pl.program_id(0) == 0)
        def _():
            gg_ref[...] = jnp.zeros_like(gg_ref)

        rstd_v = r_ref[...]
        xhat = x_ref[...] * rstd_v
        dhv = dh_ref[...].astype(F32)
        dxhat = dhv * g_ref[...]
        proj = jnp.mean(dxhat * xhat, axis=-1, keepdims=True)
        dx_ref[...] = dres_ref[...] + rstd_v * (dxhat - xhat * proj)
        gg_ref[...] += jnp.sum(dhv * xhat, axis=0, keepdims=True)

    row = pl.BlockSpec((br, D), lambda i: (i, 0))
    vec = pl.BlockSpec((1, D), lambda i: (0, 0))
    return _pcall(
        kernel, name=name, dims=("arbitrary",), deps=deps,
        out_shape=(jax.ShapeDtypeStruct((T, D), F32), jax.ShapeDtypeStruct((1, D), F32)),
        grid=(T // br,),
        in_specs=[row, row, row, pl.BlockSpec((br, 1), lambda i: (i, 0)), vec],
        out_specs=(row, vec),
    )(dres, dh, x, rstd, g)


def _final_loss(x2, gf, target, *, name):
    T, D = x2.shape
    br = _pick(T, 256, 8)

    def kernel(x_ref, g_ref, t_ref, dx_ref, loss_ref, gg_ref):
        @pl.when(pl.program_id(0) == 0)
        def _():
            gg_ref[...] = jnp.zeros_like(gg_ref)
            loss_ref[...] = jnp.zeros_like(loss_ref)

        xv = x_ref[...]
        gv = g_ref[...]
        rstd = lax.rsqrt(jnp.mean(xv * xv, axis=-1, keepdims=True) + EPS)
        xhat = xv * rstd
        err = xhat * gv - t_ref[...]
        loss_ref[...] += 0.5 * jnp.sum(jnp.mean(err * err, axis=-1, keepdims=True), axis=0, keepdims=True)
        dy = err * (1.0 / D)
        dxhat = dy * gv
        proj = jnp.mean(dxhat * xhat, axis=-1, keepdims=True)
        dx_ref[...] = rstd * (dxhat - xhat * proj)
        gg_ref[...] += jnp.sum(dy * xhat, axis=0, keepdims=True)

    row = pl.BlockSpec((br, D), lambda i: (i, 0))
    vec = pl.BlockSpec((1, D), lambda i: (0, 0))
    return _pcall(
        kernel, name=name, dims=("arbitrary",),
        out_shape=(jax.ShapeDtypeStruct((T, D), F32), jax.ShapeDtypeStruct((1, 1), F32),
                   jax.ShapeDtypeStruct((1, D), F32)),
        grid=(T // br,),
        in_specs=[row, vec, row],
        out_specs=(row, pl.BlockSpec((1, 1), lambda i: (0, 0)), vec),
    )(x2, gf, target)


def _scan_rows(v, reverse):
    n = v.shape[0]
    row = lax.broadcasted_iota(jnp.int32, v.shape, 0)
    s = 1
    while s < n:
        if reverse:
            v = v + jnp.where(row < n - s, pltpu.roll(v, n - s, 0), 0.0)
        else:
            v = v + jnp.where(row >= s, pltpu.roll(v, s, 0), 0.0)
        s *= 2
    return v


def _fox_gate_fwd(f, bias, *, name):
    T = f.shape[0]
    tb = _pick(T, 512, 8)

    def kernel(f_ref, b_ref, c_ref, s_ref, carry):
        @pl.when(pl.program_id(0) == 0)
        def _():
            carry[...] = jnp.zeros_like(carry)

        fa = f_ref[...] + b_ref[...]
        z = jnp.exp(-jnp.abs(fa))
        log_f = jnp.minimum(fa, 0.0) - jnp.log(1.0 + z)
        s_ref[...] = jnp.where(fa >= 0, z, 1.0) / (1.0 + z)
        c = _scan_rows(log_f, False) + carry[...]
        c_ref[...] = c
        carry[...] = c[tb - 1:tb, :]

    blk = pl.BlockSpec((tb, LANES), lambda i: (i, 0))
    return _pcall(
        kernel, name=name, dims=("arbitrary",),
        out_shape=(jax.ShapeDtypeStruct((T, LANES), F32), jax.ShapeDtypeStruct((T, LANES), F32)),
        grid=(T // tb,),
        in_specs=[blk, pl.BlockSpec((1, LANES), lambda i: (0, 0))],
        out_specs=(blk, blk),
        scratch_shapes=[pltpu.VMEM((1, LANES), F32)],
    )(f, bias)


def _fox_gate_bwd(dc, sneg, *, name):
    T = dc.shape[0]
    tb = _pick(T, 512, 8)
    nb = T // tb

    def kernel(dc_ref, s_ref, df_ref, gb_ref, carry):
        @pl.when(pl.program_id(0) == 0)
        def _():
            carry[...] = jnp.zeros_like(carry)
            gb_ref[...] = jnp.zeros_like(gb_ref)

        dlog = _scan_rows(dc_ref[...], True) + carry[...]
        carry[...] = dlog[0:1, :]
        dfa = dlog * s_ref[...]
        df_ref[...] = dfa.astype(BF16)
        gb_ref[...] += jnp.sum(dfa, axis=0, keepdims=True)

    blk = pl.BlockSpec((tb, LANES), lambda i: (nb - 1 - i, 0))
    return _pcall(
        kernel, name=name, dims=("arbitrary",),
        out_shape=(jax.ShapeDtypeStruct((T, LANES), BF16), jax.ShapeDtypeStruct((1, LANES), F32)),
        grid=(nb,),
        in_specs=[blk, blk],
        out_specs=(blk, pl.BlockSpec((1, LANES), lambda i: (0, 0))),
        scratch_shapes=[pltpu.VMEM((1, LANES), F32)],
    )(dc, sneg)


def _fox_scores(q, k, cq, ck, diagonal):
    s = _dot(q, k, "nt") * (DH_A ** -0.5) + cq - ck
    if diagonal:
        row = lax.broadcasted_iota(jnp.int32, s.shape, 0)
        col = lax.broadcasted_iota(jnp.int32, s.shape, 1)
        s = jnp.where(col <= row, s, NEG)
    return s


def _fox_fwd(proj, c_col, c_row, n_heads, *, name):
    T = proj.shape[0]
    H = n_heads
    tq = tk = _pick(T, 512, 128)
    nq = T // tq

    def kernel(q_ref, k_ref, v_ref, cq_ref, ck_ref, o_ref, lse_ref, m_sc, l_sc, acc_sc):
        i, j = pl.program_id(1), pl.program_id(2)

        @pl.when(j == 0)
        def _():
            m_sc[...] = jnp.full_like(m_sc, NEG)
            l_sc[...] = jnp.zeros_like(l_sc)
            acc_sc[...] = jnp.zeros_like(acc_sc)

        def tile(diagonal):
            s = _fox_scores(q_ref[...], k_ref[...], cq_ref[...], ck_ref[...], diagonal)
            m_prev = m_sc[...]
            m_new = jnp.maximum(m_prev, jnp.max(s, axis=-1, keepdims=True))
            alpha = jnp.exp(m_prev - m_new)
            p = jnp.exp(s - m_new)
            l_sc[...] = alpha * l_sc[...] + jnp.sum(p, axis=-1, keepdims=True)
            acc_sc[...] = alpha * acc_sc[...] + _dot(p.astype(BF16), v_ref[...])
            m_sc[...] = m_new

        @pl.when(j < i)
        def _():
            tile(False)

        @pl.when(j == i)
        def _():
            tile(True)
            l = l_sc[...]
            o_ref[...] = (acc_sc[...] / l).astype(BF16)
            lse_ref[...] = m_sc[...] + jnp.log(l)

    return _pcall(
        kernel, name=name, dims=("parallel", "parallel", "arbitrary"),
        out_shape=(jax.ShapeDtypeStruct((T, H * DH_A), BF16), jax.ShapeDtypeStruct((H, T, 1), F32)),
        grid=(H, nq, nq),
        in_specs=[
            pl.BlockSpec((tq, DH_A), lambda h, i, j: (i, h)),
            pl.BlockSpec((tk, DH_A), lambda h, i, j: (jnp.minimum(j, i), H + h)),
            pl.BlockSpec((tk, DH_A), lambda h, i, j: (jnp.minimum(j, i), 2 * H + h)),
            pl.BlockSpec((None, tq, 1), lambda h, i, j: (h, i, 0)),
            pl.BlockSpec((None, 1, tk), lambda h, i, j: (h, 0, jnp.minimum(j, i))),
        ],
        out_specs=(pl.BlockSpec((tq, DH_A), lambda h, i, j: (i, h)),
                   pl.BlockSpec((None, tq, 1), lambda h, i, j: (h, i, 0))),
        scratch_shapes=[pltpu.VMEM((tq, 1), F32), pltpu.VMEM((tq, 1), F32), pltpu.VMEM((tq, DH_A), F32)],
    )(proj, proj, proj, c_col, c_row)


def _rowdot(a, b, n_heads, *, name):
    T = a.shape[0]
    tq = _pick(T, 512, 8)

    def kernel(a_ref, b_ref, o_ref):
        o_ref[...] = jnp.sum(a_ref[...].astype(F32) * b_ref[...].astype(F32), axis=-1, keepdims=True)

    blk = pl.BlockSpec((tq, DH_A), lambda h, i: (i, h))
    return _pcall(
        kernel, name=name, dims=("parallel", "parallel"),
        out_shape=jax.ShapeDtypeStruct((n_heads, T, 1), F32),
        grid=(n_heads, T // tq), in_specs=[blk, blk],
        out_specs=pl.BlockSpec((None, tq, 1), lambda h, i: (h, i, 0)),
    )(a, b)


def _fox_bwd(proj, do, lse, delta, c_col, c_row, n_heads, *, name):
    T = proj.shape[0]
    H = n_heads
    tq = tk = _pick(T, 512, 128)
    nq = T // tq
    scale = DH_A ** -0.5

    def kernel(q_ref, k_ref, v_ref, do_ref, lse_ref, dl_ref, cq_ref, ck_ref,
               dq_ref, dk_ref, dv_ref, dc_ref, dcq_ref, dk_sc, dv_sc, dc_sc):
        j, i = pl.program_id(1), pl.program_id(2)

        @pl.when((j == 0) & (i == 0))
        def _():
            dq_ref[...] = jnp.zeros_like(dq_ref)
            dcq_ref[...] = jnp.zeros_like(dcq_ref)

        @pl.when(i == 0)
        def _():
            dk_sc[...] = jnp.zeros_like(dk_sc)
            dv_sc[...] = jnp.zeros_like(dv_sc)
            dc_sc[...] = jnp.zeros_like(dc_sc)

        def tile(diagonal):
            q, k, v, dov = q_ref[...], k_ref[...], v_ref[...], do_ref[...]
            s = _fox_scores(q, k, cq_ref[...], ck_ref[...], diagonal)
            p = jnp.exp(s - lse_ref[...])
            dv_sc[...] += _dot(p.astype(BF16), dov, "tn")
            dp = _dot(dov, v, "nt")
            ds = p * (dp - dl_ref[...])
            dc_sc[...] -= jnp.sum(ds, axis=0, keepdims=True)
            dsb = ds.astype(BF16)
            dk_sc[...] += _dot(dsb, q, "tn") * scale
            rows = pl.ds(pl.multiple_of(i * tq, tq), tq)
            dq_ref[rows, :] += _dot(dsb, k) * scale
            dcq_ref[rows, :] += jnp.sum(ds, axis=1, keepdims=True)

        @pl.when(i > j)
        def _():
            tile(False)

        @pl.when(i == j)
        def _():
            tile(True)

        @pl.when(i == nq - 1)
        def _():
            dk_ref[...] = dk_sc[...].astype(BF16)
            dv_ref[...] = dv_sc[...].astype(BF16)
            dc_ref[...] = dc_sc[...]

    qi = lambda j, i: jnp.maximum(i, j)
    return _pcall(
        kernel, name=name, dims=("parallel", "arbitrary", "arbitrary"),
        out_shape=(jax.ShapeDtypeStruct((T, H * DH_A), F32), jax.ShapeDtypeStruct((T, H * DH_A), BF16),
                   jax.ShapeDtypeStruct((T, H * DH_A), BF16), jax.ShapeDtypeStruct((H, 1, T), F32),
                   jax.ShapeDtypeStruct((H, T, 1), F32)),
        grid=(H, nq, nq),
        in_specs=[
            pl.BlockSpec((tq, DH_A), lambda h, j, i: (qi(j, i), h)),
            pl.BlockSpec((tk, DH_A), lambda h, j, i: (j, H + h)),
            pl.BlockSpec((tk, DH_A), lambda h, j, i: (j, 2 * H + h)),
            pl.BlockSpec((tq, DH_A), lambda h, j, i: (qi(j, i), h)),
            pl.BlockSpec((None, tq, 1), lambda h, j, i: (h, qi(j, i), 0)),
            pl.BlockSpec((None, tq, 1), lambda h, j, i: (h, qi(j, i), 0)),
            pl.BlockSpec((None, tq, 1), lambda h, j, i: (h, qi(j, i), 0)),
            pl.BlockSpec((None, 1, tk), lambda h, j, i: (h, 0, j)),
        ],
        out_specs=(pl.BlockSpec((T, DH_A), lambda h, j, i: (0, h)),
                   pl.BlockSpec((tk, DH_A), lambda h, j, i: (j, h)),
                   pl.BlockSpec((tk, DH_A), lambda h, j, i: (j, h)),
                   pl.BlockSpec((None, 1, tk), lambda h, j, i: (h, 0, j)),
                   pl.BlockSpec((None, T, 1), lambda h, j, i: (h, 0, 0))),
        scratch_shapes=[pltpu.VMEM((tk, DH_A), F32), pltpu.VMEM((tk, DH_A), F32), pltpu.VMEM((1, tk), F32)],
    )(proj, proj, proj, do, lse, delta, c_col, c_row)


def _t5_bucket(dist):
    max_exact = NUM_BUCKETS // 2
    small = dist < max_exact
    large = max_exact + (np.log(np.maximum(dist, 1) / max_exact) / np.log(MAX_DISTANCE / max_exact)
                         * (NUM_BUCKETS - max_exact)).astype(np.int64)
    large = np.minimum(large, NUM_BUCKETS - 1)
    return np.where(small, dist, large)


def _bucket_onehot():
    ql = np.arange(SWA_BLOCK)[:, None]
    kl = np.arange(2 * SWA_BLOCK)[None, :]
    bucket = _t5_bucket(np.clip(ql + SWA_BLOCK - kl, 0, None))
    bucket = np.stack([bucket[:, :SWA_BLOCK], bucket[:, SWA_BLOCK:]], axis=0)
    onehot = (bucket.reshape(1, -1) == np.arange(NUM_BUCKETS)[:, None])
    return jnp.asarray(onehot, dtype=BF16)


def _split3(v):
    hi = v.astype(BF16)
    r1 = v - hi.astype(F32)
    mid = r1.astype(BF16)
    lo = (r1 - mid.astype(F32)).astype(BF16)
    return hi, mid, lo


def _bias_table(rel_bias, onehot, *, name):
    B, H = rel_bias.shape
    n = onehot.shape[1]
    bc = _pick(n, 8192, LANES)

    def kernel(rb_ref, oh_ref, o_ref):
        hi, mid, lo = _split3(rb_ref[...])
        oh = oh_ref[...]
        o_ref[...] = _dot(hi, oh, "tn") + _dot(mid, oh, "tn") + _dot(lo, oh, "tn")

    return _pcall(
        kernel, name=name, dims=("parallel",),
        out_shape=jax.ShapeDtypeStruct((H, n), F32), grid=(n // bc,),
        in_specs=[pl.BlockSpec((B, H), lambda i: (0, 0)), pl.BlockSpec((B, bc), lambda i: (0, i))],
        out_specs=pl.BlockSpec((H, bc), lambda i: (0, i)),
    )(rel_bias, onehot)


def _bias_table_bwd(dbias, onehot, *, name):
    H, n = dbias.shape
    B = onehot.shape[0]
    bc = _pick(n, 8192, LANES)

    def kernel(db_ref, oh_ref, o_ref):
        @pl.when(pl.program_id(0) == 0)
        def _():
            o_ref[...] = jnp.zeros_like(o_ref)

        hi, mid, lo = _split3(db_ref[...])
        oh = oh_ref[...]
        o_ref[...] += _dot(oh, hi, "nt") + _dot(oh, mid, "nt") + _dot(oh, lo, "nt")

    return _pcall(
        kernel, name=name, dims=("arbitrary",),
        out_shape=jax.ShapeDtypeStruct((B, H), F32), grid=(n // bc,),
        in_specs=[pl.BlockSpec((H, bc), lambda i: (0, i)), pl.BlockSpec((B, bc), lambda i: (0, i))],
        out_specs=pl.BlockSpec((B, H), lambda i: (0, 0)),
    )(dbias, onehot)


def _swa_probs(q2, kp, kc, bias_ref, lse, n, G):
    Q = SWA_BLOCK
    scale = DH_B ** -0.5
    sp = (_dot(q2, kp, "nt") * scale).reshape(G, Q, Q) + bias_ref[:, 0]
    sc = (_dot(q2, kc, "nt") * scale).reshape(G, Q, Q) + bias_ref[:, 1]
    row = lax.broadcasted_iota(jnp.int32, (G, Q, Q), 1)
    col = lax.broadcasted_iota(jnp.int32, (G, Q, Q), 2)
    vis_p = (col > row) & (n > 0)
    vis_c = col <= row
    sp = jnp.where(vis_p, sp, NEG)
    sc = jnp.where(vis_c, sc, NEG)
    if lse is None:
        return sp, sc, vis_p, vis_c
    pp = jnp.where(vis_p, jnp.exp(sp - lse), 0.0)
    pc = jnp.where(vis_c, jnp.exp(sc - lse), 0.0)
    return pp, pc


def _swa_fwd(q, k, v, bias, sinks, *, name):
    HKV, G, T, _ = q.shape
    Q = SWA_BLOCK
    N = T // Q

    def kernel(q_ref, kp_ref, kc_ref, vp_ref, vc_ref, b_ref, s_ref, o_ref, lse_ref):
        n = pl.program_id(1)
        q2 = q_ref[...].reshape(G * Q, DH_B)
        sp, sc, vis_p, vis_c = _swa_probs(q2, kp_ref[...], kc_ref[...], b_ref, None, n, G)
        sink = s_ref[...][:, :, 0:1]
        m = jnp.maximum(jnp.maximum(jnp.max(sp, axis=-1, keepdims=True), jnp.max(sc, axis=-1, keepdims=True)), sink)
        pp = jnp.where(vis_p, jnp.exp(sp - m), 0.0)
        pc = jnp.where(vis_c, jnp.exp(sc - m), 0.0)
        denom = jnp.sum(pp, axis=-1, keepdims=True) + jnp.sum(pc, axis=-1, keepdims=True) + jnp.exp(sink - m)
        o = _dot(pp.reshape(G * Q, Q).astype(BF16), vp_ref[...]) + _dot(pc.reshape(G * Q, Q).astype(BF16), vc_ref[...])
        o_ref[...] = (o.reshape(G, Q, DH_B) / denom).astype(BF16)
        lse_ref[...] = m + jnp.log(denom)

    prev = lambda h, n: (h, jnp.maximum(n - 1, 0), 0)
    cur = lambda h, n: (h, n, 0)
    kv = lambda f: pl.BlockSpec((None, Q, DH_B), f)
    return _pcall(
        kernel, name=name, dims=("parallel", "parallel"),
        out_shape=(jax.ShapeDtypeStruct((HKV, G, T, DH_B), BF16), jax.ShapeDtypeStruct((HKV, G, T, 1), F32)),
        grid=(HKV, N),
        in_specs=[pl.BlockSpec((None, G, Q, DH_B), lambda h, n: (h, 0, n, 0)),
                  kv(prev), kv(cur), kv(prev), kv(cur),
                  pl.BlockSpec((None, G, 2, Q, Q), lambda h, n: (h, 0, 0, 0, 0)),
                  pl.BlockSpec((None, G, 1, LANES), lambda h, n: (h, 0, 0, 0))],
        out_specs=(pl.BlockSpec((None, G, Q, DH_B), lambda h, n: (h, 0, n, 0)),
                   pl.BlockSpec((None, G, Q, 1), lambda h, n: (h, 0, n, 0))),
    )(q, k, k, v, v, bias, sinks)


def _swa_bwd(q, k, v, o, do, lse, bias, sinks, *, name):
    HKV, G, T, _ = q.shape
    Q = SWA_BLOCK
    N = T // Q
    scale = DH_B ** -0.5

    def kernel(q_ref, kp_ref, kc_ref, vp_ref, vc_ref, o_ref, do_ref, lse_ref, b_ref, s_ref,
               dq_ref, dk_ref, dv_ref, db_ref, ds_ref, ck_sc, cv_sc):
        n = pl.program_id(1)

        @pl.when(n == 0)
        def _():
            db_ref[...] = jnp.zeros_like(db_ref)
            ds_ref[...] = jnp.zeros_like(ds_ref)
            ck_sc[...] = jnp.zeros_like(ck_sc)
            cv_sc[...] = jnp.zeros_like(cv_sc)

        @pl.when(n < N)
        def _():
            q2 = q_ref[...].reshape(G * Q, DH_B)
            do3 = do_ref[...]
            do2 = do3.reshape(G * Q, DH_B)
            kp, kc, vp, vc = kp_ref[...], kc_ref[...], vp_ref[...], vc_ref[...]
            lse_v = lse_ref[...]
            pp, pc = _swa_probs(q2, kp, kc, b_ref, lse_v, n, G)
            delta = jnp.sum(do3.astype(F32) * o_ref[...].astype(F32), axis=-1, keepdims=True)
            dsp = pp * (_dot(do2, vp, "nt").reshape(G, Q, Q) - delta)
            dsc = pc * (_dot(do2, vc, "nt").reshape(G, Q, Q) - delta)
            p_sink = jnp.exp(s_ref[...][:, :, 0:1] - lse_v)
            ds_ref[...] += jnp.broadcast_to(-jnp.sum(p_sink * delta, axis=1, keepdims=True), (G, 1, LANES))
            db_ref[:, 0] += dsp
            db_ref[:, 1] += dsc
            dsp2 = dsp.reshape(G * Q, Q).astype(BF16)
            dsc2 = dsc.reshape(G * Q, Q).astype(BF16)
            dq = (_dot(dsp2, kp) + _dot(dsc2, kc)) * scale
            dq_ref[...] = dq.reshape(G, Q, DH_B).astype(BF16)
            pp2 = pp.reshape(G * Q, Q).astype(BF16)
            pc2 = pc.reshape(G * Q, Q).astype(BF16)
            dk_ref[...] = (ck_sc[...] + _dot(dsp2, q2, "tn") * scale).astype(BF16)
            dv_ref[...] = (cv_sc[...] + _dot(pp2, do2, "tn")).astype(BF16)
            ck_sc[...] = _dot(dsc2, q2, "tn") * scale
            cv_sc[...] = _dot(pc2, do2, "tn")

        @pl.when(n == N)
        def _():
            dk_ref[...] = ck_sc[...].astype(BF16)
            dv_ref[...] = cv_sc[...].astype(BF16)

    qn = lambda n: jnp.minimum(n, N - 1)
    prev = lambda h, n: (h, jnp.maximum(qn(n) - 1, 0), 0)
    cur = lambda h, n: (h, qn(n), 0)
    out_kv = lambda h, n: (h, jnp.maximum(n - 1, 0), 0)
    kv = lambda f: pl.BlockSpec((None, Q, DH_B), f)
    qspec = pl.BlockSpec((None, G, Q, DH_B), lambda h, n: (h, 0, qn(n), 0))
    bspec = pl.BlockSpec((None, G, 2, Q, Q), lambda h, n: (h, 0, 0, 0, 0))
    sspec = pl.BlockSpec((None, G, 1, LANES), lambda h, n: (h, 0, 0, 0))
    return _pcall(
        kernel, name=name, dims=("parallel", "arbitrary"),
        out_shape=(jax.ShapeDtypeStruct((HKV, G, T, DH_B), BF16), jax.ShapeDtypeStruct((HKV, T, DH_B), BF16),
                   jax.ShapeDtypeStruct((HKV, T, DH_B), BF16), jax.ShapeDtypeStruct((HKV, G, 2, Q, Q), F32),
                   jax.ShapeDtypeStruct((HKV, G, 1, LANES), F32)),
        grid=(HKV, N + 1),
        in_specs=[qspec, kv(prev), kv(cur), kv(prev), kv(cur), qspec, qspec,
                  pl.BlockSpec((None, G, Q, 1), lambda h, n: (h, 0, qn(n), 0)), bspec, sspec],
        out_specs=(qspec, kv(out_kv), kv(out_kv), bspec, sspec),
        scratch_shapes=[pltpu.VMEM((Q, DH_B), F32), pltpu.VMEM((Q, DH_B), F32)],
    )(q, k, k, v, v, o, do, lse, bias, sinks)


def _branch_mix(oa, ob, wa_t, wb_t, proj, gate_col, *, name):
    T = oa.shape[0]
    D = wa_t.shape[0]
    bm, bn = _pick(T, 1024, 8), _pick(D, 512, LANES)
    assert gate_col % bn == 0
    ga0, gb0 = gate_col // bn, (gate_col + D) // bn

    def kernel(oa_ref, ob_ref, wa_ref, wb_ref, ga_ref, gb_ref, mix_ref, ya_ref, yb_ref):
        ya = _dot(oa_ref[...], wa_ref[...], "nt")
        yb = _dot(ob_ref[...], wb_ref[...], "nt")
        mix = _sigmoid(ga_ref[...].astype(F32)) * ya + _sigmoid(gb_ref[...].astype(F32)) * yb
        mix_ref[...] = mix.astype(BF16)
        ya_ref[...] = ya.astype(BF16)
        yb_ref[...] = yb.astype(BF16)

    out = pl.BlockSpec((bm, bn), lambda i, j: (i, j))
    return _pcall(
        kernel, name=name, dims=("parallel", "parallel"),
        out_shape=tuple(jax.ShapeDtypeStruct((T, D), BF16) for _ in range(3)),
        grid=(T // bm, D // bn),
        in_specs=[pl.BlockSpec((bm, oa.shape[1]), lambda i, j: (i, 0)),
                  pl.BlockSpec((bm, ob.shape[1]), lambda i, j: (i, 0)),
                  pl.BlockSpec((bn, oa.shape[1]), lambda i, j: (j, 0)),
                  pl.BlockSpec((bn, ob.shape[1]), lambda i, j: (j, 0)),
                  pl.BlockSpec((bm, bn), lambda i, j: (i, ga0 + j)),
                  pl.BlockSpec((bm, bn), lambda i, j: (i, gb0 + j))],
        out_specs=(out, out, out),
    )(oa, ob, wa_t, wb_t, proj, proj)


def _mix_bwd(dmixed, ya, yb, proj, gate_col, *, deps=(), name):
    T, D = dmixed.shape
    br, bc = _pick(T, 512, 8), _pick(D, 512, LANES)
    assert gate_col % bc == 0
    ga0, gb0 = gate_col // bc, (gate_col + D) // bc

    def kernel(dm_ref, ya_ref, yb_ref, ga_ref, gb_ref, dya_ref, dyb_ref, dga_ref, dgb_ref):
        dm = dm_ref[...].astype(F32)
        sa = _sigmoid(ga_ref[...].astype(F32))
        sb = _sigmoid(gb_ref[...].astype(F32))
        dya_ref[...] = (dm * sa).astype(BF16)
        dyb_ref[...] = (dm * sb).astype(BF16)
        dga_ref[...] = (dm * ya_ref[...].astype(F32) * sa * (1.0 - sa)).astype(BF16)
        dgb_ref[...] = (dm * yb_ref[...].astype(F32) * sb * (1.0 - sb)).astype(BF16)

    blk = pl.BlockSpec((br, bc), lambda i, j: (i, j))
    return _pcall(
        kernel, name=name, dims=("parallel", "parallel"), deps=deps,
        out_shape=tuple(jax.ShapeDtypeStruct((T, D), BF16) for _ in range(4)),
        grid=(T // br, D // bc),
        in_specs=[blk, blk, blk, pl.BlockSpec((br, bc), lambda i, j: (i, ga0 + j)),
                  pl.BlockSpec((br, bc), lambda i, j: (i, gb0 + j))],
        out_specs=(blk, blk, blk, blk),
    )(dmixed, ya, yb, proj, proj)


def _ffn_up(h, wg_t, wu_t, *, name):
    T, D = h.shape
    FP = wg_t.shape[0]
    bm, bn = _pick(T, 1024, 8), _pick(FP, 512, LANES)

    def kernel(h_ref, wg_ref, wu_ref, g_ref, u_ref, hid_ref):
        hv = h_ref[...]
        g = _dot(hv, wg_ref[...], "nt")
        u = _dot(hv, wu_ref[...], "nt")
        g_ref[...] = g.astype(BF16)
        u_ref[...] = u.astype(BF16)
        hid_ref[...] = (g * _sigmoid(g) * u).astype(BF16)

    out = pl.BlockSpec((bm, bn), lambda i, j: (i, j))
    wspec = pl.BlockSpec((bn, D), lambda i, j: (j, 0))
    return _pcall(
        kernel, name=name, dims=("parallel", "parallel"),
        out_shape=tuple(jax.ShapeDtypeStruct((T, FP), BF16) for _ in range(3)),
        grid=(T // bm, FP // bn),
        in_specs=[pl.BlockSpec((bm, D), lambda i, j: (i, 0)), wspec, wspec],
        out_specs=(out, out, out),
    )(h, wg_t, wu_t)


def _swiglu_bwd(dhid, gate, up, *, deps=(), name):
    T, FP = dhid.shape
    br, bc = _pick(T, 512, 8), _pick(FP, 1024, LANES)

    def kernel(dh_ref, g_ref, u_ref, dg_ref, du_ref):
        dh = dh_ref[...].astype(F32)
        g = g_ref[...].astype(F32)
        u = u_ref[...].astype(F32)
        sg = _sigmoid(g)
        dg_ref[...] = (dh * u * sg * (1.0 + g * (1.0 - sg))).astype(BF16)
        du_ref[...] = (dh * g * sg).astype(BF16)

    blk = pl.BlockSpec((br, bc), lambda i, j: (i, j))
    return _pcall(
        kernel, name=name, dims=("parallel", "parallel"), deps=deps,
        out_shape=(jax.ShapeDtypeStruct((T, FP), BF16), jax.ShapeDtypeStruct((T, FP), BF16)),
        grid=(T // br, FP // bc), in_specs=[blk, blk, blk], out_specs=(blk, blk),
    )(dhid, gate, up)


def _adamw_update(w, g, m, v):
    nm = ADAM_B1 * m + (1.0 - ADAM_B1) * g
    nv = ADAM_B2 * v + (1.0 - ADAM_B2) * (g * g)
    m_hat = nm / (1.0 - ADAM_B1 ** ADAM_STEP)
    v_hat = nv / (1.0 - ADAM_B2 ** ADAM_STEP)
    return -ADAM_LR * (m_hat / (jnp.sqrt(v_hat) + ADAM_EPS) + ADAM_WD * w), nm, nv


def _adamw(w, g, m, v, *, name):
    Rw, Cw = w.shape
    br = _pick(Rw, max(8, (1 << 19) // Cw // 8 * 8), 8)

    def kernel(w_ref, g_ref, m_ref, v_ref, d_ref, nm_ref, nv_ref):
        d_ref[...], nm_ref[...], nv_ref[...] = _adamw_update(w_ref[...], g_ref[...], m_ref[...], v_ref[...])

    blk = pl.BlockSpec((br, Cw), lambda i: (i, 0))
    return _pcall(
        kernel, name=name, dims=("parallel",),
        out_shape=tuple(jax.ShapeDtypeStruct((Rw, Cw), F32) for _ in range(3)),
        grid=(Rw // br,), in_specs=[blk, blk, blk, blk], out_specs=(blk, blk, blk),
    )(w, g, m, v)


def _sum_adamw(land, w, m, v, *, name):
    _, R, C = land.shape
    bc = _pick(C, max(LANES, (1 << 18) // R // LANES * LANES), LANES)

    def kernel(l_ref, w_ref, m_ref, v_ref, g_ref, d_ref, nm_ref, nv_ref):
        g = l_ref[0].astype(F32)
        for k in range(1, N_DEV):
            g = g + l_ref[k].astype(F32)
        g_ref[...] = g
        d_ref[...], nm_ref[...], nv_ref[...] = _adamw_update(w_ref[...], g, m_ref[...], v_ref[...])

    blk = pl.BlockSpec((R, bc), lambda i: (0, i))
    return _pcall(
        kernel, name=name, dims=("parallel",),
        out_shape=tuple(jax.ShapeDtypeStruct((R, C), F32) for _ in range(4)),
        grid=(C // bc,),
        in_specs=[pl.BlockSpec((N_DEV, R, bc), lambda i: (0, 0, i)), blk, blk, blk],
        out_specs=(blk, blk, blk, blk),
    )(land, w, m, v)


def _my_place():
    return lax.axis_index("x"), lax.axis_index("y"), lax.axis_index("c")


def _flip(v, bit):
    return 1 - v if bit else v


def _all_gather_rows(shard, *, pad_rows=0, name):
    R, C = shard.shape
    flat = R % BF16_TILE_ROWS == 0
    assert flat or pad_rows == 0
    out_shape = (N_DEV * R + pad_rows, C) if flat else (N_DEV, R, C)

    def body(*refs):
        if pad_rows:
            x_ref, z_ref, out_ref, send_sems, recv_sems, local_sems = refs
        else:
            x_ref, out_ref, send_sems, recv_sems, local_sems = refs
        x, y, c = _my_place()
        me, sibling = (x, y, c), (x, y, 1 - c)
        chips = [(1 - x, y), (x, 1 - y), (1 - x, 1 - y)]

        def block(px, py, pc):
            idx = 4 * px + 2 * py + pc
            if flat:
                return out_ref.at[pl.ds(pl.multiple_of(idx * R, BF16_TILE_ROWS), R), :]
            return out_ref.at[idx]

        def copy(k, blk, to, src=None):
            return pltpu.make_async_remote_copy(
                src_ref=block(*blk) if src is None else src, dst_ref=block(*blk),
                send_sem=send_sems.at[k], recv_sem=recv_sems.at[k], device_id=to, device_id_type=MESH)

        mine = pltpu.make_async_copy(x_ref, block(*me), local_sems.at[0])
        mine.start()
        if pad_rows:
            zero = pltpu.make_async_copy(z_ref, out_ref.at[pl.ds(N_DEV * R, pad_rows), :], local_sems.at[1])
            zero.start()
        first = [copy(0, me, sibling, src=x_ref)]
        first += [copy(1 + j, me, (*chip, c), src=x_ref) for j, chip in enumerate(chips)]
        for cp in first:
            cp.start()
        passed = [copy(4 + j, (*chip, c), sibling) for j, chip in enumerate(chips)]
        for j, chip in enumerate(chips):
            copy(1 + j, (*chip, c), me).wait_recv()
            passed[j].start()
        copy(0, sibling, me).wait_recv()
        for j, chip in enumerate(chips):
            copy(4 + j, (*chip, 1 - c), me).wait_recv()
        for cp in first + passed:
            cp.wait_send()
        mine.wait()
        if pad_rows:
            zero.wait()

    operands = [shard] + ([jnp.zeros((pad_rows, C), shard.dtype)] if pad_rows else [])
    return _comm_call(
        body, name=name,
        out_shape=jax.ShapeDtypeStruct(out_shape, shard.dtype),
        in_specs=[ANY] * len(operands), out_specs=ANY,
        scratch_shapes=[pltpu.SemaphoreType.DMA((7,)), pltpu.SemaphoreType.DMA((7,)), pltpu.SemaphoreType.DMA((2,))],
    )(*operands)


HBM_SPEC = pl.BlockSpec(memory_space=pltpu.HBM)
SEM_SPEC = pl.BlockSpec(memory_space=pltpu.SEMAPHORE)
SPLIT_EFFECT = pltpu.SideEffectType.DATAFLOW_SIDE_EFFECTING
N_PEERS = N_DEV - 1


def _hbm(a):
    return pltpu.with_memory_space_constraint(a, pltpu.HBM)


def _peer(x, y, c, k):
    return _flip(x, k & 4), _flip(y, k & 2), _flip(c, k & 1)


def _row_block(ref, idx, R):
    if len(ref.shape) == 3:
        return ref.at[idx]
    return ref.at[pl.ds(pl.multiple_of(idx * R, BF16_TILE_ROWS), R), :]


def _split_copies(src_ref, land_ref, sems, R, gather):
    send_sems, recv_sems, local_sem = sems
    x, y, c = _my_place()
    me = 4 * x + 2 * y + c
    if gather:
        local = pltpu.make_async_copy(src_ref, _row_block(land_ref, me, R), local_sem.at[0])
    else:
        local = pltpu.make_async_copy(_row_block(src_ref, me, R), land_ref.at[0], local_sem.at[0])
    remote = []
    for k in range(1, N_DEV):
        px, py, pc = _peer(x, y, c, k)
        if gather:
            src, dst = src_ref, _row_block(land_ref, me, R)
            arrives = _row_block(land_ref, 4 * px + 2 * py + pc, R)
        else:
            src, dst = _row_block(src_ref, 4 * px + 2 * py + pc, R), land_ref.at[k]
            arrives = dst
        send = pltpu.make_async_remote_copy(src_ref=src, dst_ref=dst, send_sem=send_sems.at[k - 1],
                                            recv_sem=recv_sems.at[k - 1], device_id=(px, py, pc), device_id_type=MESH)
        recv = pltpu.make_async_remote_copy(src_ref=src, dst_ref=arrives, send_sem=send_sems.at[k - 1],
                                            recv_sem=recv_sems.at[k - 1], device_id=(px, py, pc), device_id_type=MESH)
        remote.append((send, recv))
    return local, remote


def _exchange_start(src, land, after, R, gather, *, name):
    def body(src_ref, land_ref, after_ref, send_sems, recv_sems, local_sem, src_thru, land_thru, token):
        local, remote = _split_copies(src_ref, land_ref, (send_sems, recv_sems, local_sem), R, gather)
        local.start()
        for send, _ in remote:
            send.start()
        token[...] = jnp.zeros_like(token)

    out = pl.pallas_call(
        body, name=name,
        out_shape=(pltpu.SemaphoreType.DMA((N_PEERS,)), pltpu.SemaphoreType.DMA((N_PEERS,)), pltpu.SemaphoreType.DMA((1,)),
                   pltpu.HBM(src.shape, src.dtype), pltpu.HBM(land.shape, land.dtype),
                   jax.ShapeDtypeStruct((8, LANES), F32)),
        in_specs=(HBM_SPEC, HBM_SPEC, ANY),
        out_specs=(SEM_SPEC, SEM_SPEC, SEM_SPEC, HBM_SPEC, HBM_SPEC, pl.BlockSpec(memory_space=pltpu.VMEM)),
        input_output_aliases={0: 3, 1: 4},
        compiler_params=pltpu.CompilerParams(has_side_effects=SPLIT_EFFECT),
    )(_hbm(src), _hbm(land), after)
    return out[:3], out[3], out[4], out[5]


def _exchange_wait(sems, src_thru, land_thru, after, R, gather, *, name):
    def body(src_ref, land_ref, send_sems, recv_sems, local_sem, after_ref, src_dead, got_ref):
        local, remote = _split_copies(src_ref, land_ref, (send_sems, recv_sems, local_sem), R, gather)
        for send, recv in remote:
            send.wait_send()
            recv.wait_recv()
        local.wait()

    return pl.pallas_call(
        body, name=name,
        out_shape=(pltpu.HBM(src_thru.shape, src_thru.dtype), pltpu.HBM(land_thru.shape, land_thru.dtype)),
        in_specs=(HBM_SPEC, HBM_SPEC, SEM_SPEC, SEM_SPEC, SEM_SPEC, ANY),
        out_specs=(HBM_SPEC, HBM_SPEC),
        input_output_aliases={0: 0, 1: 1},
        compiler_params=pltpu.CompilerParams(has_side_effects=SPLIT_EFFECT),
    )(src_thru, land_thru, *sems, after)[1]


def _gather_start(shard, after, *, pad_rows=0, name):
    R, C = shard.shape
    assert R % BF16_TILE_ROWS == 0
    land = lax.empty((N_DEV * R + pad_rows, C), shard.dtype)
    if pad_rows:
        land = lax.dynamic_update_slice(land, jnp.zeros((pad_rows, C), shard.dtype), (N_DEV * R, 0))
    return _exchange_start(shard, land, after, R, True, name=name) + (R, True)


def _scatter_start(g, R, after, *, name):
    land = lax.empty((N_DEV, R, g.shape[-1]), g.dtype)
    return _exchange_start(g, land, after, R, False, name=name) + (R, False)


def _finish(handle, after, *, name):
    sems, src_thru, land_thru, _, R, gather = handle
    return _exchange_wait(sems, src_thru, land_thru, after, R, gather, name=name)


def _all_reduce_small(v, *, name):
    rows, C = v.shape

    def body(v_ref, o_ref, land_ref, send_sems, recv_sems):
        x, y, c = _my_place()
        me = 4 * x + 2 * y + c
        copies = []
        for k in range(1, N_DEV):
            peer = (_flip(x, k & 4), _flip(y, k & 2), _flip(c, k & 1))
            copies.append(pltpu.make_async_remote_copy(
                src_ref=v_ref, dst_ref=land_ref.at[me],
                send_sem=send_sems.at[k - 1], recv_sem=recv_sems.at[k - 1], device_id=peer, device_id_type=MESH))
        for cp in copies:
            cp.start()
        land_ref[me] = v_ref[...]
        for k in range(1, N_DEV):
            peer_idx = 4 * _flip(x, k & 4) + 2 * _flip(y, k & 2) + _flip(c, k & 1)
            pltpu.make_async_remote_copy(
                src_ref=v_ref, dst_ref=land_ref.at[peer_idx],
                send_sem=send_sems.at[k - 1], recv_sem=recv_sems.at[k - 1],
                device_id=(x, y, c), device_id_type=MESH).wait_recv()
        for cp in copies:
            cp.wait_send()
        acc = land_ref[0]
        for s in range(1, N_DEV):
            acc = acc + land_ref[s]
        o_ref[...] = acc

    vm = pl.BlockSpec(memory_space=pltpu.VMEM)
    return _comm_call(
        body, name=name,
        out_shape=jax.ShapeDtypeStruct((rows, C), F32),
        in_specs=[vm], out_specs=vm,
        scratch_shapes=[pltpu.VMEM((N_DEV, rows, C), F32),
                        pltpu.SemaphoreType.DMA((7,)), pltpu.SemaphoreType.DMA((7,))],
    )(v)


def kernel(x, norm1_g, w_in, b_forget, attn_sinks, rel_bias, w_branch_a, w_branch_b, w_out, norm2_g, w_ffn_gate, w_ffn_up, w_ffn_down, final_g, loss_target, m_norm1_g, m_w_in, m_b_forget, m_attn_sinks, m_rel_bias, m_w_branch_a, m_w_branch_b, m_w_out, m_norm2_g, m_w_ffn_gate, m_w_ffn_up, m_w_ffn_down, m_final_g, v_norm1_g, v_w_in, v_b_forget, v_attn_sinks, v_rel_bias, v_w_branch_a, v_w_branch_b, v_w_out, v_norm2_g, v_w_ffn_gate, v_w_ffn_up, v_w_ffn_down, v_final_g):
    xs = x[0]
    T, D = xs.shape
    H_A, H_B = b_forget.shape[-1], attn_sinks.shape[-1]
    WA, QB = H_A * DH_A, H_B * DH_B
    R_IN = w_in.shape[-1]
    W_IN = N_DEV * R_IN
    KB = (W_IN - 3 * WA - H_A - QB - 2 * D) // 2
    HKV = KB // DH_B
    G = H_B // HKV
    N_BIG = W_IN - H_A
    GATE_COL = 3 * WA + QB + 2 * KB
    R_F = w_ffn_gate.shape[-1]
    F = N_DEV * R_F
    FP = _round_up(F, 512)
    assert H_A <= LANES and T % SWA_BLOCK == 0

    win_s = w_in[0].T.astype(BF16)
    wa_s = w_branch_a[0].T.astype(BF16)
    wb_s = w_branch_b[0].T.astype(BF16)
    wout_s = w_out[0].astype(BF16)
    wg_s = w_ffn_gate[0].T.astype(BF16)
    wu_s = w_ffn_up[0].T.astype(BF16)
    wd_s = w_ffn_down[0].astype(BF16)

    win_all = _all_gather_rows(win_s, name="ag_w_in").reshape(W_IN, D)
    w_big = jnp.concatenate([win_all[:3 * WA], win_all[3 * WA + H_A:]], axis=0)
    w_f = jnp.pad(win_all[3 * WA:3 * WA + H_A], ((0, LANES - H_A), (0, 0)))
    ag_wa = _gather_start(wa_s, win_all, name="ag_start_w_branch_a")
    ag_wb = _gather_start(wb_s, ag_wa[3], name="ag_start_w_branch_b")
    ag_wout = _gather_start(wout_s, ag_wb[3], name="ag_start_w_out")
    ag_wg = _gather_start(wg_s, ag_wout[3], pad_rows=FP - F, name="ag_start_w_ffn_gate")
    ag_wu = _gather_start(wu_s, ag_wg[3], pad_rows=FP - F, name="ag_start_w_ffn_up")
    ag_wd = _gather_start(wd_s, ag_wu[3], pad_rows=FP - F, name="ag_start_w_ffn_down")

    BM = 1024
    bn_big = _pick(N_BIG, 768, LANES)

    h1, rstd1 = _rms_fwd(xs, norm1_g, deps=[ag_wd[3]], name="rms1_fwd")
    proj = _matmul([(h1, w_big)], "nt", bm=BM, bn=bn_big, name="proj_fwd")
    f_logit = _matmul([(h1, w_f)], "nt", bm=BM, bn=LANES, out_dtype=F32, name="forget_fwd")
    b_pad = jnp.pad(b_forget, ((0, 0), (0, LANES - H_A)))
    c_all, sneg = _fox_gate_fwd(f_logit, b_pad, name="fox_gate_fwd")
    c_heads = c_all[:, :H_A].T
    c_col, c_row = c_heads[:, :, None], c_heads[:, None, :]
    oa, lse_a = _fox_fwd(proj, c_col, c_row, H_A, name="fox_fwd")

    def heads_q(a):
        return a.reshape(T, HKV, G, DH_B).transpose(1, 2, 0, 3)

    def heads_kv(a):
        return a.reshape(T, HKV, DH_B).transpose(1, 0, 2)

    qb = heads_q(proj[:, 3 * WA:3 * WA + QB])
    kb = heads_kv(proj[:, 3 * WA + QB:3 * WA + QB + KB])
    vb = heads_kv(proj[:, 3 * WA + QB + KB:GATE_COL])
    onehot = _bucket_onehot()
    bias = _bias_table(rel_bias, onehot, name="rel_bias_table").reshape(HKV, G, 2, SWA_BLOCK, SWA_BLOCK)
    sinks_b = jnp.broadcast_to(attn_sinks.reshape(HKV, G, 1, 1), (HKV, G, 1, LANES))
    ob4, lse_b = _swa_fwd(qb, kb, vb, bias, sinks_b, name="swa_fwd")
    ob = ob4.transpose(2, 0, 1, 3).reshape(T, QB)

    wa_t = _finish(ag_wa, ob, name="ag_wait_w_branch_a")
    wb_t = _finish(ag_wb, wa_t, name="ag_wait_w_branch_b")
    mixed, ya, yb = _branch_mix(oa, ob, wa_t, wb_t, proj, GATE_COL, name="branch_mix")
    wout = _finish(ag_wout, mixed, name="ag_wait_w_out")
    x1 = _matmul([(mixed, wout)], "nn", bm=BM, bn=512, out_dtype=F32, addend=xs, name="out_proj_fwd")

    h2, rstd2 = _rms_fwd(x1, norm2_g, name="rms2_fwd")
    wg_t = _finish(ag_wg, h2, name="ag_wait_w_ffn_gate")
    wu_t = _finish(ag_wu, wg_t, name="ag_wait_w_ffn_up")
    gate, up, hidden = _ffn_up(h2, wg_t, wu_t, name="ffn_up")
    wd = _finish(ag_wd, hidden, name="ag_wait_w_ffn_down")
    tk_f = _pick(FP, 2816, LANES)
    x2 = _matmul([(hidden, wd)], "nn", bm=BM, bn=1024, tk=tk_f, out_dtype=F32, addend=x1, name="ffn_down_fwd")

    dx2, loss_part, g_final = _final_loss(x2, final_g.reshape(1, D), loss_target[0], name="final_loss")
    loss = lax.psum(loss_part[0, 0], MESH_AXES)

    dx2_b = dx2.astype(BF16)
    dhid = _matmul([(dx2_b, wd)], "nt", bm=BM, bn=1024, name="ffn_down_bwd_x")
    g_wd = _matmul([(hidden, dx2_b)], "tn", bm=1024, bn=1024, name="ffn_down_bwd_w")
    rs_wd = _scatter_start(g_wd, R_F, loss_part, name="rs_start_w_ffn_down")
    dgate, dup = _swiglu_bwd(dhid, gate, up, deps=[rs_wd[3]], name="swiglu_bwd")
    g_wg = _matmul([(dgate, h2)], "tn", bm=1024, bn=1024, name="ffn_gate_bwd_w")
    rs_wg = _scatter_start(g_wg, R_F, rs_wd[3], name="rs_start_w_ffn_gate")
    g_wu = _matmul([(dup, h2)], "tn", bm=1024, bn=1024, deps=[rs_wg[3]], name="ffn_up_bwd_w")
    rs_wu = _scatter_start(g_wu, R_F, rs_wg[3], name="rs_start_w_ffn_up")
    dh2 = _matmul([(dgate, wg_t), (dup, wu_t)], "nn", bm=BM, bn=1024, tk=_pick(FP, 1408, LANES), out_dtype=F32,
                  deps=[rs_wu[3]], name="ffn_up_bwd_x")
    dx1, g_norm2 = _rms_bwd(dx2, dh2, x1, rstd2, norm2_g, name="rms2_bwd")

    dx1_b = dx1.astype(BF16)
    g_wout = _matmul([(mixed, dx1_b)], "tn", bm=1024, bn=1024, name="out_proj_bwd_w")
    rs_wout = _scatter_start(g_wout, D // N_DEV, rs_wu[3], name="rs_start_w_out")
    dmixed = _matmul([(dx1_b, wout)], "nt", bm=BM, bn=1024, deps=[rs_wout[3]], name="out_proj_bwd_x")
    dya, dyb, dga, dgb = _mix_bwd(dmixed, ya, yb, proj, GATE_COL, name="mix_bwd")
    g_wa = _matmul([(dya, oa)], "tn", bm=1024, bn=1024, name="branch_a_bwd_w")
    rs_wa = _scatter_start(g_wa, D // N_DEV, rs_wout[3], name="rs_start_w_branch_a")
    g_wb = _matmul([(dyb, ob)], "tn", bm=1024, bn=1024, deps=[rs_wa[3]], name="branch_b_bwd_w")
    rs_wb = _scatter_start(g_wb, D // N_DEV, rs_wa[3], name="rs_start_w_branch_b")
    doa = _matmul([(dya, wa_t)], "nn", bm=BM, bn=1024, deps=[rs_wb[3]], name="branch_a_bwd_x")
    dob = _matmul([(dyb, wb_t)], "nn", bm=BM, bn=1024, name="branch_b_bwd_x")

    delta_a = _rowdot(doa, oa, H_A, name="fox_delta")
    dqa, dka, dva, dc_key, dc_query = _fox_bwd(proj, doa, lse_a, delta_a, c_col, c_row, H_A, name="fox_bwd")
    dc = jnp.pad((dc_key[:, 0, :] + dc_query[:, :, 0]).T, ((0, 0), (0, LANES - H_A)))
    df, g_bf = _fox_gate_bwd(dc, sneg, name="fox_gate_bwd")

    dqb4, dkb3, dvb3, dbias, dsinks = _swa_bwd(qb, kb, vb, ob4, heads_q(dob), lse_b, bias, sinks_b, name="swa_bwd")
    g_relb = _bias_table_bwd(dbias.reshape(H_B, -1), onehot, name="rel_bias_table_bwd")
    dqb = dqb4.transpose(2, 0, 1, 3).reshape(T, QB)
    dkb = dkb3.transpose(1, 0, 2).reshape(T, KB)
    dvb = dvb3.transpose(1, 0, 2).reshape(T, KB)

    dproj = jnp.concatenate([dqa.astype(BF16), dka, dva, dqb, dkb, dvb, dga, dgb], axis=1)
    g_wbig = _matmul([(dproj, h1)], "tn", bm=bn_big, bn=1024, name="proj_bwd_w")
    g_wf = _matmul([(df, h1)], "tn", bm=LANES, bn=1024, name="forget_bwd_w")
    g_win = jnp.concatenate([g_wbig[:3 * WA], g_wf[:H_A], g_wbig[3 * WA:]], axis=0).reshape(N_DEV, R_IN, D)
    rs_win = _scatter_start(g_win, R_IN, rs_wb[3], name="rs_start_w_in")
    dh_f = _matmul([(df, w_f)], "nn", bm=BM, bn=1024, out_dtype=F32, deps=[rs_win[3]], name="forget_bwd_x")
    tk_big = _pick(N_BIG, 2816, LANES)
    dh1 = _matmul([(dproj, w_big)], "nn", bm=BM, bn=1024, tk=tk_big, out_dtype=F32, addend=dh_f, name="proj_bwd_x")
    grad_x, g_norm1 = _rms_bwd(dx1, dh1, xs, rstd1, norm1_g, name="rms1_bwd")

    def update(handle, after, w, m, v, transposed, nm):
        take = (lambda a: a[0].T) if transposed else (lambda a: a[0])
        give = (lambda a: a.T[None]) if transposed else (lambda a: a[None])
        land = _finish(handle, after, name="rs_wait_" + nm)
        return tuple(give(o) for o in _sum_adamw(land, take(w), take(m), take(v), name="adamw_" + nm))

    big = {}
    big["w_ffn_down"] = update(rs_wd, grad_x, w_ffn_down, m_w_ffn_down, v_w_ffn_down, False, "w_ffn_down")
    big["w_ffn_gate"] = update(rs_wg, big["w_ffn_down"][1], w_ffn_gate, m_w_ffn_gate, v_w_ffn_gate, True, "w_ffn_gate")
    big["w_ffn_up"] = update(rs_wu, big["w_ffn_gate"][1], w_ffn_up, m_w_ffn_up, v_w_ffn_up, True, "w_ffn_up")
    big["w_out"] = update(rs_wout, big["w_ffn_up"][1], w_out, m_w_out, v_w_out, False, "w_out")
    big["w_branch_a"] = update(rs_wa, big["w_out"][1], w_branch_a, m_w_branch_a, v_w_branch_a, True, "w_branch_a")
    big["w_branch_b"] = update(rs_wb, big["w_branch_a"][1], w_branch_b, m_w_branch_b, v_w_branch_b, True, "w_branch_b")
    big["w_in"] = update(rs_win, big["w_branch_b"][1], w_in, m_w_in, v_w_in, True, "w_in")

    n_rb = NUM_BUCKETS * H_B
    assert D >= n_rb and D >= H_A + H_B
    small = jnp.concatenate([
        g_norm1, g_norm2, g_final,
        jnp.pad(jnp.concatenate([g_bf[:, :H_A], dsinks[:, :, 0, 0].reshape(1, H_B)], axis=1), ((0, 0), (0, D - H_A - H_B))),
        jnp.pad(g_relb.reshape(1, n_rb), ((0, 0), (0, D - n_rb))),
        jnp.zeros((3, D), F32)], axis=0)
    small = _all_reduce_small(small, name="ar_small")
    gs = {
        "norm1_g": small[0:1], "norm2_g": small[1:2], "final_g": small[2],
        "b_forget": small[3:4, :H_A], "attn_sinks": small[3:4, H_A:H_A + H_B],
        "rel_bias": small[4, :n_rb].reshape(NUM_BUCKETS, H_B),
    }

    def adam_small(ws, gsm, ms, vs):
        def pack(parts):
            rows = [jnp.pad(p.reshape(1, -1), ((0, 0), (0, D - p.size))) for p in parts]
            return jnp.concatenate(rows + [jnp.ones((8 - len(rows), D), F32)], axis=0)
        d, nm_, nv_ = _adamw(pack(ws), pack(gsm), pack(ms), pack(vs), name="adamw_small")
        unpack = lambda a: [a[i, :p.size].reshape(p.shape) for i, p in enumerate(ws)]
        return unpack(d), unpack(nm_), unpack(nv_)

    small_names = ["norm1_g", "b_forget", "attn_sinks", "rel_bias", "norm2_g", "final_g"]
    small_w = [norm1_g, b_forget, attn_sinks, rel_bias, norm2_g, final_g]
    small_g = [gs[n].reshape(w.shape) for n, w in zip(small_names, small_w)]
    small_m = [m_norm1_g, m_b_forget, m_attn_sinks, m_rel_bias, m_norm2_g, m_final_g]
    small_v = [v_norm1_g, v_b_forget, v_attn_sinks, v_rel_bias, v_norm2_g, v_final_g]
    sd, sm, sv = adam_small(small_w, small_g, small_m, small_v)
    for i, n in enumerate(small_names):
        big[n] = (small_g[i], sd[i], sm[i], sv[i])

    order = ["norm1_g", "w_in", "b_forget", "attn_sinks", "rel_bias", "w_branch_a", "w_branch_b", "w_out",
             "norm2_g", "w_ffn_gate", "w_ffn_up", "w_ffn_down", "final_g"]
    grads = [big[n][0] for n in order]
    deltas = [big[n][1] for n in order]
    new_m = [big[n][2] for n in order]
    new_v = [big[n][3] for n in order]
    return (loss, grad_x[None], *grads, *deltas, *new_m, *new_v)
```

```python
import functools

import numpy as np
import jax
import jax.numpy as jnp
from jax import lax
from jax.experimental import pallas as pl
from jax.experimental.pallas import tpu as pltpu

N_DEV = 8
MESH_AXES = ("x", "y", "c")
DH_A = 128
DH_B = 64
SWA_BLOCK = 128
FOX_TILE = 1024
NUM_BUCKETS = 32
MAX_DISTANCE = 128
EPS = 1e-6
ADAM_LR = 0.001
ADAM_B1 = 0.9
ADAM_B2 = 0.999
ADAM_EPS = 1e-08
ADAM_WD = 0.01
ADAM_STEP = 10
NEG = -1e30
LANES = 128
BF16_TILE_ROWS = 16
V7X_VMEM_LIMIT = 56 * 1024 * 1024
F32 = jnp.float32
BF16 = jnp.bfloat16
MESH = pl.DeviceIdType.MESH
ANY = pl.BlockSpec(memory_space=pl.ANY)


def _round_up(n, m):
    return (n + m - 1) // m * m


def _pick(n, target, mult):
    best = None
    for d in range(mult, min(n, target) + 1, mult):
        if n % d == 0:
            best = d
    return n if best is None else best


def _pcall(kernel, *, name, in_specs, dims=None, deps=(), **kw):
    deps = tuple(deps)
    if deps:
        inner, n_in, n_dep = kernel, len(in_specs), len(deps)

        def kernel(*refs):
            inner(*refs[:n_in], *refs[n_in + n_dep:])

        in_specs = list(in_specs) + [ANY] * n_dep
    call = pl.pallas_call(
        kernel, name=name, in_specs=in_specs,
        compiler_params=pltpu.CompilerParams(dimension_semantics=dims, vmem_limit_bytes=V7X_VMEM_LIMIT),
        **kw)
    return lambda *operands: call(*operands, *deps)


def _comm_call(kernel, *, name, **kw):
    return pl.pallas_call(kernel, name=name, **kw)


def _sigmoid(v):
    return 1.0 / (1.0 + jnp.exp(-v))


_DIMS = {"nn": (((1,), (0,)), ((), ())), "nt": (((1,), (1,)), ((), ())), "tn": (((0,), (0,)), ((), ()))}


def _dot(a, b, form="nn"):
    return lax.dot_general(a, b, _DIMS[form], preferred_element_type=F32)


def _matmul(pairs, form, *, bm, bn, tk=None, out_dtype=BF16, addend=None, extras=(), epilogue=None, n_out=1, deps=(),
            name):
    a0, b0 = pairs[0]
    if form == "tn":
        K, M = a0.shape
    else:
        M, K = a0.shape
    N = b0.shape[0] if form == "nt" else b0.shape[1]
    tk = K if tk is None else tk
    bm, bn = min(bm, M), min(bn, N)
    assert M % bm == 0 and N % bn == 0 and K % tk == 0, (name, M, N, K, bm, bn, tk)
    nk = K // tk
    n_pairs = len(pairs)
    has_add = addend is not None
    if form == "tn":
        a_spec = pl.BlockSpec((tk, bm), lambda i, j, k: (k, i))
    else:
        a_spec = pl.BlockSpec((bm, tk), lambda i, j, k: (i, k))
    if form == "nt":
        b_spec = pl.BlockSpec((bn, tk), lambda i, j, k: (j, k))
    else:
        b_spec = pl.BlockSpec((tk, bn), lambda i, j, k: (k, j))
    o_spec = pl.BlockSpec((bm, bn), lambda i, j, k: (i, j))
    n_extra = len(extras)
    n_in = 2 * n_pairs + has_add + n_extra

    def kernel(*refs):
        ab = refs[:2 * n_pairs]
        c_ref = refs[2 * n_pairs] if has_add else None
        extra_refs = refs[2 * n_pairs + has_add:n_in]
        o_refs = refs[n_in:n_in + n_out]

        def partial_sum():
            acc = _dot(ab[0][...], ab[1][...], form)
            for p in range(1, n_pairs):
                acc = acc + _dot(ab[2 * p][...], ab[2 * p + 1][...], form)
            return acc

        def finish(acc):
            if has_add:
                acc = acc + c_ref[...].astype(F32)
            outs = (acc,) if epilogue is None else epilogue(acc, *[r[...] for r in extra_refs])
            for o_ref, val in zip(o_refs, outs, strict=True):
                o_ref[...] = val.astype(o_ref.dtype)

        if nk == 1:
            finish(partial_sum())
        else:
            acc_ref = refs[-1]
            k = pl.program_id(2)

            @pl.when(k == 0)
            def _():
                acc_ref[...] = jnp.zeros_like(acc_ref)

            acc_ref[...] += partial_sum()

            @pl.when(k == nk - 1)
            def _():
                finish(acc_ref[...])

    operands, in_specs = [], []
    for a, b in pairs:
        operands += [a, b]
        in_specs += [a_spec, b_spec]
    if has_add:
        operands.append(addend)
        in_specs.append(o_spec)
    for arr, first in extras:
        operands.append(arr)
        in_specs.append(pl.BlockSpec((bm, bn), lambda i, j, k, first=first: (i, first + j)))
    out_shape = tuple(jax.ShapeDtypeStruct((M, N), out_dtype) for _ in range(n_out))
    out = _pcall(
        kernel, name=name, dims=("parallel", "parallel", "arbitrary"), deps=deps,
        out_shape=out_shape, grid=(M // bm, N // bn, nk), in_specs=in_specs, out_specs=(o_spec,) * n_out,
        scratch_shapes=[pltpu.VMEM((bm, bn), F32)] if nk > 1 else [],
    )(*operands)
    return out[0] if epilogue is None else out


def _rms_fwd(x, g, *, deps=(), name):
    T, D = x.shape
    br = _pick(T, 256, 8)

    def kernel(x_ref, g_ref, h_ref, r_ref):
        xv = x_ref[...]
        rstd = lax.rsqrt(jnp.mean(xv * xv, axis=-1, keepdims=True) + EPS)
        h_ref[...] = (xv * rstd * g_ref[...]).astype(BF16)
        r_ref[...] = rstd

    return _pcall(
        kernel, name=name, dims=("parallel",), deps=deps,
        out_shape=(jax.ShapeDtypeStruct((T, D), BF16), jax.ShapeDtypeStruct((T, 1), F32)),
        grid=(T // br,),
        in_specs=[pl.BlockSpec((br, D), lambda i: (i, 0)), pl.BlockSpec((1, D), lambda i: (0, 0))],
        out_specs=(pl.BlockSpec((br, D), lambda i: (i, 0)), pl.BlockSpec((br, 1), lambda i: (i, 0))),
    )(x, g)


def _rms_bwd(dres, dh, x, rstd, g, *, deps=(), name):
    T, D = x.shape
    br = _pick(T, 256, 8)

    def kernel(dres_ref, dh_ref, x_ref, r_ref, g_ref, dx_ref, gg_ref):
        @pl.when(pl.program_id(0) == 0)
        def _():
            gg_ref[...] = jnp.zeros_like(gg_ref)

        rstd_v = r_ref[...]
        xhat = x_ref[...] * rstd_v
        dhv = dh_ref[...].astype(F32)
        dxhat = dhv * g_ref[...]
        proj = jnp.mean(dxhat * xhat, axis=-1, keepdims=True)
        dx_ref[...] = dres_ref[...] + rstd_v * (dxhat - xhat * proj)
        gg_ref[...] += jnp.sum(dhv * xhat, axis=0, keepdims=True)

    row = pl.BlockSpec((br, D), lambda i: (i, 0))
    vec = pl.BlockSpec((1, D), lambda i: (0, 0))
    return _pcall(
        kernel, name=name, dims=("arbitrary",), deps=deps,
        out_shape=(jax.ShapeDtypeStruct((T, D), F32), jax.ShapeDtypeStruct((1, D), F32)),
        grid=(T // br,),
        in_specs=[row, row, row, pl.BlockSpec((br, 1), lambda i: (i, 0)), vec],
        out_specs=(row, vec),
    )(dres, dh, x, rstd, g)


def _final_loss(x2, gf, target, *, name):
    T, D = x2.shape
    br = _pick(T, 256, 8)

    def kernel(x_ref, g_ref, t_ref, dx_ref, loss_ref, gg_ref):
        @pl.when(pl.program_id(0) == 0)
        def _():
            gg_ref[...] = jnp.zeros_like(gg_ref)
            loss_ref[...] = jnp.zeros_like(loss_ref)

        xv = x_ref[...]
        gv = g_ref[...]
        rstd = lax.rsqrt(jnp.mean(xv * xv, axis=-1, keepdims=True) + EPS)
        xhat = xv * rstd
        err = xhat * gv - t_ref[...]
        loss_ref[...] += 0.5 * jnp.sum(jnp.mean(err * err, axis=-1, keepdims=True), axis=0, keepdims=True)
        dy = err * (1.0 / D)
        dxhat = dy * gv
        proj = jnp.mean(dxhat * xhat, axis=-1, keepdims=True)
        dx_ref[...] = rstd * (dxhat - xhat * proj)
        gg_ref[...] += jnp.sum(dy * xhat, axis=0, keepdims=True)

    row = pl.BlockSpec((br, D), lambda i: (i, 0))
    vec = pl.BlockSpec((1, D), lambda i: (0, 0))
    return _pcall(
        kernel, name=name, dims=("arbitrary",),
        out_shape=(jax.ShapeDtypeStruct((T, D), F32), jax.ShapeDtypeStruct((1, 1), F32),
                   jax.ShapeDtypeStruct((1, D), F32)),
        grid=(T // br,),
        in_specs=[row, vec, row],
        out_specs=(row, pl.BlockSpec((1, 1), lambda i: (0, 0)), vec),
    )(x2, gf, target)


def _scan_rows(v, reverse):
    n = v.shape[0]
    row = lax.broadcasted_iota(jnp.int32, v.shape, 0)
    s = 1
    while s < n:
        if reverse:
            v = v + jnp.where(row < n - s, pltpu.roll(v, n - s, 0), 0.0)
        else:
            v = v + jnp.where(row >= s, pltpu.roll(v, s, 0), 0.0)
        s *= 2
    return v


def _fox_gate_fwd(f, bias, *, name):
    T = f.shape[0]
    tb = _pick(T, 512, 8)

    def kernel(f_ref, b_ref, c_ref, s_ref, carry):
        @pl.when(pl.program_id(0) == 0)
        def _():
            carry[...] = jnp.zeros_like(carry)

        fa = f_ref[...] + b_ref[...]
        z = jnp.exp(-jnp.abs(fa))
        log_f = jnp.minimum(fa, 0.0) - jnp.log(1.0 + z)
        s_ref[...] = jnp.where(fa >= 0, z, 1.0) / (1.0 + z)
        c = _scan_rows(log_f, False) + carry[...]
        c_ref[...] = c
        carry[...] = c[tb - 1:tb, :]

    blk = pl.BlockSpec((tb, LANES), lambda i: (i, 0))
    return _pcall(
        kernel, name=name, dims=("arbitrary",),
        out_shape=(jax.ShapeDtypeStruct((T, LANES), F32), jax.ShapeDtypeStruct((T, LANES), F32)),
        grid=(T // tb,),
        in_specs=[blk, pl.BlockSpec((1, LANES), lambda i: (0, 0))],
        out_specs=(blk, blk),
        scratch_shapes=[pltpu.VMEM((1, LANES), F32)],
    )(f, bias)


def _fox_gate_bwd(dc, sneg, *, name):
    T = dc.shape[0]
    tb = _pick(T, 512, 8)
    nb = T // tb

    def kernel(dc_ref, s_ref, df_ref, gb_ref, carry):
        @pl.when(pl.program_id(0) == 0)
        def _():
            carry[...] = jnp.zeros_like(carry)
            gb_ref[...] = jnp.zeros_like(gb_ref)

        dlog = _scan_rows(dc_ref[...], True) + carry[...]
        carry[...] = dlog[0:1, :]
        dfa = dlog * s_ref[...]
        df_ref[...] = dfa.astype(BF16)
        gb_ref[...] += jnp.sum(dfa, axis=0, keepdims=True)

    blk = pl.BlockSpec((tb, LANES), lambda i: (nb - 1 - i, 0))
    return _pcall(
        kernel, name=name, dims=("arbitrary",),
        out_shape=(jax.ShapeDtypeStruct((T, LANES), BF16), jax.ShapeDtypeStruct((1, LANES), F32)),
        grid=(nb,),
        in_specs=[blk, blk],
        out_specs=(blk, pl.BlockSpec((1, LANES), lambda i: (0, 0))),
        scratch_shapes=[pltpu.VMEM((1, LANES), F32)],
    )(dc, sneg)


def _fox_scores(q, k, cq, ck, diagonal):
    s = _dot(q, k, "nt") * (DH_A ** -0.5) + cq - ck
    if diagonal:
        row = lax.broadcasted_iota(jnp.int32, s.shape, 0)
        col = lax.broadcasted_iota(jnp.int32, s.shape, 1)
        s = jnp.where(col <= row, s, NEG)
    return s


def _fox_fwd(proj, c_col, c_row, n_heads, *, name):
    T = proj.shape[0]
    H = n_heads
    tq = tk = _pick(T, FOX_TILE, 128)
    nq = T // tq

    def kernel(q_ref, k_ref, v_ref, cq_ref, ck_ref, o_ref, lse_ref, m_sc, l_sc, acc_sc):
        i, j = pl.program_id(1), pl.program_id(2)

        @pl.when(j == 0)
        def _():
            m_sc[...] = jnp.full_like(m_sc, NEG)
            l_sc[...] = jnp.zeros_like(l_sc)
            acc_sc[...] = jnp.zeros_like(acc_sc)

        def tile(diagonal):
            s = _fox_scores(q_ref[...], k_ref[...], cq_ref[...], ck_ref[...], diagonal)
            m_prev = m_sc[...]
            m_new = jnp.maximum(m_prev, jnp.max(s, axis=-1, keepdims=True))
            alpha = jnp.exp(m_prev - m_new)
            p = jnp.exp(s - m_new)
            l_sc[...] = alpha * l_sc[...] + jnp.sum(p, axis=-1, keepdims=True)
            acc_sc[...] = alpha * acc_sc[...] + _dot(p.astype(BF16), v_ref[...])
            m_sc[...] = m_new

        @pl.when(j < i)
        def _():
            tile(False)

        @pl.when(j == i)
        def _():
            tile(True)
            l = l_sc[...]
            o_ref[...] = (acc_sc[...] / l).astype(BF16)
            lse_ref[...] = m_sc[...] + jnp.log(l)

    return _pcall(
        kernel, name=name, dims=("parallel", "parallel", "arbitrary"),
        out_shape=(jax.ShapeDtypeStruct((T, H * DH_A), BF16), jax.ShapeDtypeStruct((H, T, 1), F32)),
        grid=(H, nq, nq),
        in_specs=[
            pl.BlockSpec((tq, DH_A), lambda h, i, j: (i, h)),
            pl.BlockSpec((tk, DH_A), lambda h, i, j: (jnp.minimum(j, i), H + h)),
            pl.BlockSpec((tk, DH_A), lambda h, i, j: (jnp.minimum(j, i), 2 * H + h)),
            pl.BlockSpec((None, tq, 1), lambda h, i, j: (h, i, 0)),
            pl.BlockSpec((None, 1, tk), lambda h, i, j: (h, 0, jnp.minimum(j, i))),
        ],
        out_specs=(pl.BlockSpec((tq, DH_A), lambda h, i, j: (i, h)),
                   pl.BlockSpec((None, tq, 1), lambda h, i, j: (h, i, 0))),
        scratch_shapes=[pltpu.VMEM((tq, 1), F32), pltpu.VMEM((tq, 1), F32), pltpu.VMEM((tq, DH_A), F32)],
    )(proj, proj, proj, c_col, c_row)


def _rowdot(a, b, n_heads, *, name):
    T = a.shape[0]
    tq = _pick(T, 512, 8)

    def kernel(a_ref, b_ref, o_ref):
        o_ref[...] = jnp.sum(a_ref[...].astype(F32) * b_ref[...].astype(F32), axis=-1, keepdims=True)

    blk = pl.BlockSpec((tq, DH_A), lambda h, i: (i, h))
    return _pcall(
        kernel, name=name, dims=("parallel", "parallel"),
        out_shape=jax.ShapeDtypeStruct((n_heads, T, 1), F32),
        grid=(n_heads, T // tq), in_specs=[blk, blk],
        out_specs=pl.BlockSpec((None, tq, 1), lambda h, i: (h, i, 0)),
    )(a, b)


def _fox_bwd(proj, do, lse, delta, c_col, c_row, n_heads, *, name):
    T = proj.shape[0]
    H = n_heads
    tq = tk = _pick(T, FOX_TILE, 128)
    nq = T // tq
    scale = DH_A ** -0.5

    def kernel(q_ref, k_ref, v_ref, do_ref, lse_ref, dl_ref, cq_ref, ck_ref,
               dq_ref, dk_ref, dv_ref, dc_ref, dcq_ref, dk_sc, dv_sc, dc_sc):
        j, i = pl.program_id(1), pl.program_id(2)

        @pl.when((j == 0) & (i == 0))
        def _():
            dq_ref[...] = jnp.zeros_like(dq_ref)
            dcq_ref[...] = jnp.zeros_like(dcq_ref)

        @pl.when(i == 0)
        def _():
            dk_sc[...] = jnp.zeros_like(dk_sc)
            dv_sc[...] = jnp.zeros_like(dv_sc)
            dc_sc[...] = jnp.zeros_like(dc_sc)

        def tile(diagonal):
            q, k, v, dov = q_ref[...], k_ref[...], v_ref[...], do_ref[...]
            s = _fox_scores(q, k, cq_ref[...], ck_ref[...], diagonal)
            p = jnp.exp(s - lse_ref[...])
            dv_sc[...] += _dot(p.astype(BF16), dov, "tn")
            dp = _dot(dov, v, "nt")
            ds = p * (dp - dl_ref[...])
            dc_sc[...] -= jnp.sum(ds, axis=0, keepdims=True)
            dsb = ds.astype(BF16)
            dk_sc[...] += _dot(dsb, q, "tn") * scale
            rows = pl.ds(pl.multiple_of(i * tq, tq), tq)
            dq_ref[rows, :] += _dot(dsb, k) * scale
            dcq_ref[rows, :] += jnp.sum(ds, axis=1, keepdims=True)

        @pl.when(i > j)
        def _():
            tile(False)

        @pl.when(i == j)
        def _():
            tile(True)

        @pl.when(i == nq - 1)
        def _():
            dk_ref[...] = dk_sc[...].astype(BF16)
            dv_ref[...] = dv_sc[...].astype(BF16)
            dc_ref[...] = dc_sc[...]

    qi = lambda j, i: jnp.maximum(i, j)
    return _pcall(
        kernel, name=name, dims=("parallel", "arbitrary", "arbitrary"),
        out_shape=(jax.ShapeDtypeStruct((T, H * DH_A), F32), jax.ShapeDtypeStruct((T, H * DH_A), BF16),
                   jax.ShapeDtypeStruct((T, H * DH_A), BF16), jax.ShapeDtypeStruct((H, 1, T), F32),
                   jax.ShapeDtypeStruct((H, T, 1), F32)),
        grid=(H, nq, nq),
        in_specs=[
            pl.BlockSpec((tq, DH_A), lambda h, j, i: (qi(j, i), h)),
            pl.BlockSpec((tk, DH_A), lambda h, j, i: (j, H + h)),
            pl.BlockSpec((tk, DH_A), lambda h, j, i: (j, 2 * H + h)),
            pl.BlockSpec((tq, DH_A), lambda h, j, i: (qi(j, i), h)),
            pl.BlockSpec((None, tq, 1), lambda h, j, i: (h, qi(j, i), 0)),
            pl.BlockSpec((None, tq, 1), lambda h, j, i: (h, qi(j, i), 0)),
            pl.BlockSpec((None, tq, 1), lambda h, j, i: (h, qi(j, i), 0)),
            pl.BlockSpec((None, 1, tk), lambda h, j, i: (h, 0, j)),
        ],
        out_specs=(pl.BlockSpec((T, DH_A), lambda h, j, i: (0, h)),
                   pl.BlockSpec((tk, DH_A), lambda h, j, i: (j, h)),
                   pl.BlockSpec((tk, DH_A), lambda h, j, i: (j, h)),
                   pl.BlockSpec((None, 1, tk), lambda h, j, i: (h, 0, j)),
                   pl.BlockSpec((None, T, 1), lambda h, j, i: (h, 0, 0))),
        scratch_shapes=[pltpu.VMEM((tk, DH_A), F32), pltpu.VMEM((tk, DH_A), F32), pltpu.VMEM((1, tk), F32)],
    )(proj, proj, proj, do, lse, delta, c_col, c_row)


def _t5_bucket(dist):
    max_exact = NUM_BUCKETS // 2
    small = dist < max_exact
    large = max_exact + (np.log(np.maximum(dist, 1) / max_exact) / np.log(MAX_DISTANCE / max_exact)
                         * (NUM_BUCKETS - max_exact)).astype(np.int64)
    large = np.minimum(large, NUM_BUCKETS - 1)
    return np.where(small, dist, large)


def _bucket_onehot():
    ql = np.arange(SWA_BLOCK)[:, None]
    kl = np.arange(2 * SWA_BLOCK)[None, :]
    bucket = _t5_bucket(np.clip(ql + SWA_BLOCK - kl, 0, None))
    bucket = np.stack([bucket[:, :SWA_BLOCK], bucket[:, SWA_BLOCK:]], axis=0)
    onehot = (bucket.reshape(1, -1) == np.arange(NUM_BUCKETS)[:, None])
    return jnp.asarray(onehot, dtype=BF16)


def _split3(v):
    hi = v.astype(BF16)
    r1 = v - hi.astype(F32)
    mid = r1.astype(BF16)
    lo = (r1 - mid.astype(F32)).astype(BF16)
    return hi, mid, lo


def _bias_table(rel_bias, onehot, *, name):
    B, H = rel_bias.shape
    n = onehot.shape[1]
    bc = _pick(n, 8192, LANES)

    def kernel(rb_ref, oh_ref, o_ref):
        hi, mid, lo = _split3(rb_ref[...])
        oh = oh_ref[...]
        o_ref[...] = _dot(hi, oh, "tn") + _dot(mid, oh, "tn") + _dot(lo, oh, "tn")

    return _pcall(
        kernel, name=name, dims=("parallel",),
        out_shape=jax.ShapeDtypeStruct((H, n), F32), grid=(n // bc,),
        in_specs=[pl.BlockSpec((B, H), lambda i: (0, 0)), pl.BlockSpec((B, bc), lambda i: (0, i))],
        out_specs=pl.BlockSpec((H, bc), lambda i: (0, i)),
    )(rel_bias, onehot)


def _bias_table_bwd(dbias, onehot, *, name):
    H, n = dbias.shape
    B = onehot.shape[0]
    bc = _pick(n, 8192, LANES)

    def kernel(db_ref, oh_ref, o_ref):
        @pl.when(pl.program_id(0) == 0)
        def _():
            o_ref[...] = jnp.zeros_like(o_ref)

        hi, mid, lo = _split3(db_ref[...])
        oh = oh_ref[...]
        o_ref[...] += _dot(oh, hi, "nt") + _dot(oh, mid, "nt") + _dot(oh, lo, "nt")

    return _pcall(
        kernel, name=name, dims=("arbitrary",),
        out_shape=jax.ShapeDtypeStruct((B, H), F32), grid=(n // bc,),
        in_specs=[pl.BlockSpec((H, bc), lambda i: (0, i)), pl.BlockSpec((B, bc), lambda i: (0, i))],
        out_specs=pl.BlockSpec((B, H), lambda i: (0, 0)),
    )(dbias, onehot)


def _swa_probs(q2, kp, kc, bias_ref, lse, n, G):
    Q = SWA_BLOCK
    scale = DH_B ** -0.5
    sp = (_dot(q2, kp, "nt") * scale).reshape(G, Q, Q) + bias_ref[:, 0]
    sc = (_dot(q2, kc, "nt") * scale).reshape(G, Q, Q) + bias_ref[:, 1]
    row = lax.broadcasted_iota(jnp.int32, (G, Q, Q), 1)
    col = lax.broadcasted_iota(jnp.int32, (G, Q, Q), 2)
    vis_p = (col > row) & (n > 0)
    vis_c = col <= row
    sp = jnp.where(vis_p, sp, NEG)
    sc = jnp.where(vis_c, sc, NEG)
    if lse is None:
        return sp, sc, vis_p, vis_c
    pp = jnp.where(vis_p, jnp.exp(sp - lse), 0.0)
    pc = jnp.where(vis_c, jnp.exp(sc - lse), 0.0)
    return pp, pc


def _swa_fwd(q, k, v, bias, sinks, *, name):
    HKV, G, T, _ = q.shape
    Q = SWA_BLOCK
    N = T // Q

    def kernel(q_ref, kp_ref, kc_ref, vp_ref, vc_ref, b_ref, s_ref, o_ref, lse_ref):
        n = pl.program_id(1)
        q2 = q_ref[...].reshape(G * Q, DH_B)
        sp, sc, vis_p, vis_c = _swa_probs(q2, kp_ref[...], kc_ref[...], b_ref, None, n, G)
        sink = s_ref[...][:, :, 0:1]
        m = jnp.maximum(jnp.maximum(jnp.max(sp, axis=-1, keepdims=True), jnp.max(sc, axis=-1, keepdims=True)), sink)
        pp = jnp.where(vis_p, jnp.exp(sp - m), 0.0)
        pc = jnp.where(vis_c, jnp.exp(sc - m), 0.0)
        denom = jnp.sum(pp, axis=-1, keepdims=True) + jnp.sum(pc, axis=-1, keepdims=True) + jnp.exp(sink - m)
        o = _dot(pp.reshape(G * Q, Q).astype(BF16), vp_ref[...]) + _dot(pc.reshape(G * Q, Q).astype(BF16), vc_ref[...])
        o_ref[...] = (o.reshape(G, Q, DH_B) / denom).astype(BF16)
        lse_ref[...] = m + jnp.log(denom)

    prev = lambda h, n: (h, jnp.maximum(n - 1, 0), 0)
    cur = lambda h, n: (h, n, 0)
    kv = lambda f: pl.BlockSpec((None, Q, DH_B), f)
    return _pcall(
        kernel, name=name, dims=("parallel", "parallel"),
        out_shape=(jax.ShapeDtypeStruct((HKV, G, T, DH_B), BF16), jax.ShapeDtypeStruct((HKV, G, T, 1), F32)),
        grid=(HKV, N),
        in_specs=[pl.BlockSpec((None, G, Q, DH_B), lambda h, n: (h, 0, n, 0)),
                  kv(prev), kv(cur), kv(prev), kv(cur),
                  pl.BlockSpec((None, G, 2, Q, Q), lambda h, n: (h, 0, 0, 0, 0)),
                  pl.BlockSpec((None, G, 1, LANES), lambda h, n: (h, 0, 0, 0))],
        out_specs=(pl.BlockSpec((None, G, Q, DH_B), lambda h, n: (h, 0, n, 0)),
                   pl.BlockSpec((None, G, Q, 1), lambda h, n: (h, 0, n, 0))),
    )(q, k, k, v, v, bias, sinks)


def _swa_bwd(q, k, v, o, do, lse, bias, sinks, *, name):
    HKV, G, T, _ = q.shape
    Q = SWA_BLOCK
    N = T // Q
    scale = DH_B ** -0.5

    def kernel(q_ref, kp_ref, kc_ref, vp_ref, vc_ref, o_ref, do_ref, lse_ref, b_ref, s_ref,
               dq_ref, dk_ref, dv_ref, db_ref, ds_ref, ck_sc, cv_sc):
        n = pl.program_id(1)

        @pl.when(n == 0)
        def _():
            db_ref[...] = jnp.zeros_like(db_ref)
            ds_ref[...] = jnp.zeros_like(ds_ref)
            ck_sc[...] = jnp.zeros_like(ck_sc)
            cv_sc[...] = jnp.zeros_like(cv_sc)

        @pl.when(n < N)
        def _():
            q2 = q_ref[...].reshape(G * Q, DH_B)
            do3 = do_ref[...]
            do2 = do3.reshape(G * Q, DH_B)
            kp, kc, vp, vc = kp_ref[...], kc_ref[...], vp_ref[...], vc_ref[...]
            lse_v = lse_ref[...]
            pp, pc = _swa_probs(q2, kp, kc, b_ref, lse_v, n, G)
            delta = jnp.sum(do3.astype(F32) * o_ref[...].astype(F32), axis=-1, keepdims=True)
            dsp = pp * (_dot(do2, vp, "nt").reshape(G, Q, Q) - delta)
            dsc = pc * (_dot(do2, vc, "nt").reshape(G, Q, Q) - delta)
            p_sink = jnp.exp(s_ref[...][:, :, 0:1] - lse_v)
            ds_ref[...] += jnp.broadcast_to(-jnp.sum(p_sink * delta, axis=1, keepdims=True), (G, 1, LANES))
            db_ref[:, 0] += dsp
            db_ref[:, 1] += dsc
            dsp2 = dsp.reshape(G * Q, Q).astype(BF16)
            dsc2 = dsc.reshape(G * Q, Q).astype(BF16)
            dq = (_dot(dsp2, kp) + _dot(dsc2, kc)) * scale
            dq_ref[...] = dq.reshape(G, Q, DH_B).astype(BF16)
            pp2 = pp.reshape(G * Q, Q).astype(BF16)
            pc2 = pc.reshape(G * Q, Q).astype(BF16)
            dk_ref[...] = (ck_sc[...] + _dot(dsp2, q2, "tn") * scale).astype(BF16)
            dv_ref[...] = (cv_sc[...] + _dot(pp2, do2, "tn")).astype(BF16)
            ck_sc[...] = _dot(dsc2, q2, "tn") * scale
            cv_sc[...] = _dot(pc2, do2, "tn")

        @pl.when(n == N)
        def _():
            dk_ref[...] = ck_sc[...].astype(BF16)
            dv_ref[...] = cv_sc[...].astype(BF16)

    qn = lambda n: jnp.minimum(n, N - 1)
    prev = lambda h, n: (h, jnp.maximum(qn(n) - 1, 0), 0)
    cur = lambda h, n: (h, qn(n), 0)
    out_kv = lambda h, n: (h, jnp.maximum(n - 1, 0), 0)
    kv = lambda f: pl.BlockSpec((None, Q, DH_B), f)
    qspec = pl.BlockSpec((None, G, Q, DH_B), lambda h, n: (h, 0, qn(n), 0))
    bspec = pl.BlockSpec((None, G, 2, Q, Q), lambda h, n: (h, 0, 0, 0, 0))
    sspec = pl.BlockSpec((None, G, 1, LANES), lambda h, n: (h, 0, 0, 0))
    return _pcall(
        kernel, name=name, dims=("parallel", "arbitrary"),
        out_shape=(jax.ShapeDtypeStruct((HKV, G, T, DH_B), BF16), jax.ShapeDtypeStruct((HKV, T, DH_B), BF16),
                   jax.ShapeDtypeStruct((HKV, T, DH_B), BF16), jax.ShapeDtypeStruct((HKV, G, 2, Q, Q), F32),
                   jax.ShapeDtypeStruct((HKV, G, 1, LANES), F32)),
        grid=(HKV, N + 1),
        in_specs=[qspec, kv(prev), kv(cur), kv(prev), kv(cur), qspec, qspec,
                  pl.BlockSpec((None, G, Q, 1), lambda h, n: (h, 0, qn(n), 0)), bspec, sspec],
        out_specs=(qspec, kv(out_kv), kv(out_kv), bspec, sspec),
        scratch_shapes=[pltpu.VMEM((Q, DH_B), F32), pltpu.VMEM((Q, DH_B), F32)],
    )(q, k, k, v, v, o, do, lse, bias, sinks)


def _branch_mix(oa, ob, wa_t, wb_t, proj, gate_col, *, name):
    T = oa.shape[0]
    D = wa_t.shape[0]
    bm, bn = _pick(T, 1024, 8), _pick(D, 512, LANES)
    assert gate_col % bn == 0
    ga0, gb0 = gate_col // bn, (gate_col + D) // bn

    def kernel(oa_ref, ob_ref, wa_ref, wb_ref, ga_ref, gb_ref, mix_ref, ya_ref, yb_ref):
        ya = _dot(oa_ref[...], wa_ref[...], "nt")
        yb = _dot(ob_ref[...], wb_ref[...], "nt")
        mix = _sigmoid(ga_ref[...].astype(F32)) * ya + _sigmoid(gb_ref[...].astype(F32)) * yb
        mix_ref[...] = mix.astype(BF16)
        ya_ref[...] = ya.astype(BF16)
        yb_ref[...] = yb.astype(BF16)

    out = pl.BlockSpec((bm, bn), lambda i, j: (i, j))
    return _pcall(
        kernel, name=name, dims=("parallel", "parallel"),
        out_shape=tuple(jax.ShapeDtypeStruct((T, D), BF16) for _ in range(3)),
        grid=(T // bm, D // bn),
        in_specs=[pl.BlockSpec((bm, oa.shape[1]), lambda i, j: (i, 0)),
                  pl.BlockSpec((bm, ob.shape[1]), lambda i, j: (i, 0)),
                  pl.BlockSpec((bn, oa.shape[1]), lambda i, j: (j, 0)),
                  pl.BlockSpec((bn, ob.shape[1]), lambda i, j: (j, 0)),
                  pl.BlockSpec((bm, bn), lambda i, j: (i, ga0 + j)),
                  pl.BlockSpec((bm, bn), lambda i, j: (i, gb0 + j))],
        out_specs=(out, out, out),
    )(oa, ob, wa_t, wb_t, proj, proj)


def _mix_bwd_tile(dmixed, ya, yb, ga, gb):
    sa = _sigmoid(ga.astype(F32))
    sb = _sigmoid(gb.astype(F32))
    return (dmixed * sa, dmixed * sb,
            dmixed * ya.astype(F32) * sa * (1.0 - sa), dmixed * yb.astype(F32) * sb * (1.0 - sb))


def _ffn_up(h, wg_t, wu_t, *, name):
    T, D = h.shape
    FP = wg_t.shape[0]
    bm, bn = _pick(T, 1024, 8), _pick(FP, 512, LANES)

    def kernel(h_ref, wg_ref, wu_ref, g_ref, u_ref, hid_ref):
        hv = h_ref[...]
        g = _dot(hv, wg_ref[...], "nt")
        u = _dot(hv, wu_ref[...], "nt")
        g_ref[...] = g.astype(BF16)
        u_ref[...] = u.astype(BF16)
        hid_ref[...] = (g * _sigmoid(g) * u).astype(BF16)

    out = pl.BlockSpec((bm, bn), lambda i, j: (i, j))
    wspec = pl.BlockSpec((bn, D), lambda i, j: (j, 0))
    return _pcall(
        kernel, name=name, dims=("parallel", "parallel"),
        out_shape=tuple(jax.ShapeDtypeStruct((T, FP), BF16) for _ in range(3)),
        grid=(T // bm, FP // bn),
        in_specs=[pl.BlockSpec((bm, D), lambda i, j: (i, 0)), wspec, wspec],
        out_specs=(out, out, out),
    )(h, wg_t, wu_t)


def _swiglu_bwd_tile(dhid, gate, up):
    g = gate.astype(F32)
    sg = _sigmoid(g)
    return dhid * up.astype(F32) * sg * (1.0 + g * (1.0 - sg)), dhid * g * sg


def _adamw_update(w, g, m, v):
    nm = ADAM_B1 * m + (1.0 - ADAM_B1) * g
    nv = ADAM_B2 * v + (1.0 - ADAM_B2) * (g * g)
    m_hat = nm / (1.0 - ADAM_B1 ** ADAM_STEP)
    v_hat = nv / (1.0 - ADAM_B2 ** ADAM_STEP)
    return -ADAM_LR * (m_hat / (jnp.sqrt(v_hat) + ADAM_EPS) + ADAM_WD * w), nm, nv


def _adamw(w, g, m, v, *, name):
    Rw, Cw = w.shape
    br = _pick(Rw, max(8, (1 << 19) // Cw // 8 * 8), 8)

    def kernel(w_ref, g_ref, m_ref, v_ref, d_ref, nm_ref, nv_ref):
        d_ref[...], nm_ref[...], nv_ref[...] = _adamw_update(w_ref[...], g_ref[...], m_ref[...], v_ref[...])

    blk = pl.BlockSpec((br, Cw), lambda i: (i, 0))
    return _pcall(
        kernel, name=name, dims=("parallel",),
        out_shape=tuple(jax.ShapeDtypeStruct((Rw, Cw), F32) for _ in range(3)),
        grid=(Rw // br,), in_specs=[blk, blk, blk, blk], out_specs=(blk, blk, blk),
    )(w, g, m, v)


def _sum_adamw(land, w, m, v, *, name):
    _, R, C = land.shape
    bc = _pick(C, max(LANES, (1 << 18) // R // LANES * LANES), LANES)

    def kernel(l_ref, w_ref, m_ref, v_ref, g_ref, d_ref, nm_ref, nv_ref):
        g = l_ref[0].astype(F32)
        for k in range(1, N_DEV):
            g = g + l_ref[k].astype(F32)
        g_ref[...] = g
        d_ref[...], nm_ref[...], nv_ref[...] = _adamw_update(w_ref[...], g, m_ref[...], v_ref[...])

    blk = pl.BlockSpec((R, bc), lambda i: (0, i))
    return _pcall(
        kernel, name=name, dims=("parallel",),
        out_shape=tuple(jax.ShapeDtypeStruct((R, C), F32) for _ in range(4)),
        grid=(C // bc,),
        in_specs=[pl.BlockSpec((N_DEV, R, bc), lambda i: (0, 0, i)), blk, blk, blk],
        out_specs=(blk, blk, blk, blk),
    )(land, w, m, v)


def _my_place():
    return lax.axis_index("x"), lax.axis_index("y"), lax.axis_index("c")


def _flip(v, bit):
    return 1 - v if bit else v


def _all_gather_rows(shard, *, pad_rows=0, name):
    R, C = shard.shape
    flat = R % BF16_TILE_ROWS == 0
    assert flat or pad_rows == 0
    out_shape = (N_DEV * R + pad_rows, C) if flat else (N_DEV, R, C)

    def body(*refs):
        if pad_rows:
            x_ref, z_ref, out_ref, send_sems, recv_sems, local_sems = refs
        else:
            x_ref, out_ref, send_sems, recv_sems, local_sems = refs
        x, y, c = _my_place()
        me, sibling = (x, y, c), (x, y, 1 - c)
        chips = [(1 - x, y), (x, 1 - y), (1 - x, 1 - y)]

        def block(px, py, pc):
            idx = 4 * px + 2 * py + pc
            if flat:
                return out_ref.at[pl.ds(pl.multiple_of(idx * R, BF16_TILE_ROWS), R), :]
            return out_ref.at[idx]

        def copy(k, blk, to, src=None):
            return pltpu.make_async_remote_copy(
                src_ref=block(*blk) if src is None else src, dst_ref=block(*blk),
                send_sem=send_sems.at[k], recv_sem=recv_sems.at[k], device_id=to, device_id_type=MESH)

        mine = pltpu.make_async_copy(x_ref, block(*me), local_sems.at[0])
        mine.start()
        if pad_rows:
            zero = pltpu.make_async_copy(z_ref, out_ref.at[pl.ds(N_DEV * R, pad_rows), :], local_sems.at[1])
            zero.start()
        first = [copy(0, me, sibling, src=x_ref)]
        first += [copy(1 + j, me, (*chip, c), src=x_ref) for j, chip in enumerate(chips)]
        for cp in first:
            cp.start()
        passed = [copy(4 + j, (*chip, c), sibling) for j, chip in enumerate(chips)]
        for j, chip in enumerate(chips):
            copy(1 + j, (*chip, c), me).wait_recv()
            passed[j].start()
        copy(0, sibling, me).wait_recv()
        for j, chip in enumerate(chips):
            copy(4 + j, (*chip, 1 - c), me).wait_recv()
        for cp in first + passed:
            cp.wait_send()
        mine.wait()
        if pad_rows:
            zero.wait()

    operands = [shard] + ([jnp.zeros((pad_rows, C), shard.dtype)] if pad_rows else [])
    return _comm_call(
        body, name=name,
        out_shape=jax.ShapeDtypeStruct(out_shape, shard.dtype),
        in_specs=[ANY] * len(operands), out_specs=ANY,
        scratch_shapes=[pltpu.SemaphoreType.DMA((7,)), pltpu.SemaphoreType.DMA((7,)), pltpu.SemaphoreType.DMA((2,))],
    )(*operands)


HBM_SPEC = pl.BlockSpec(memory_space=pltpu.HBM)
SEM_SPEC = pl.BlockSpec(memory_space=pltpu.SEMAPHORE)
SPLIT_EFFECT = pltpu.SideEffectType.DATAFLOW_SIDE_EFFECTING
N_PEERS = N_DEV - 1


def _hbm(a):
    return pltpu.with_memory_space_constraint(a, pltpu.HBM)


def _peer(x, y, c, k):
    return _flip(x, k & 4), _flip(y, k & 2), _flip(c, k & 1)


def _row_block(ref, idx, R):
    if len(ref.shape) == 3:
        return ref.at[idx]
    return ref.at[pl.ds(pl.multiple_of(idx * R, BF16_TILE_ROWS), R), :]


def _split_copies(src_ref, land_ref, sems, R, gather):
    send_sems, recv_sems, local_sem = sems
    x, y, c = _my_place()
    me = 4 * x + 2 * y + c
    if gather:
        local = pltpu.make_async_copy(src_ref, _row_block(land_ref, me, R), local_sem.at[0])
    else:
        local = pltpu.make_async_copy(_row_block(src_ref, me, R), land_ref.at[0], local_sem.at[0])
    remote = []
    for k in range(1, N_DEV):
        px, py, pc = _peer(x, y, c, k)
        if gather:
            src, dst = src_ref, _row_block(land_ref, me, R)
            arrives = _row_block(land_ref, 4 * px + 2 * py + pc, R)
        else:
            src, dst = _row_block(src_ref, 4 * px + 2 * py + pc, R), land_ref.at[k]
            arrives = dst
        send = pltpu.make_async_remote_copy(src_ref=src, dst_ref=dst, send_sem=send_sems.at[k - 1],
                                            recv_sem=recv_sems.at[k - 1], device_id=(px, py, pc), device_id_type=MESH)
        recv = pltpu.make_async_remote_copy(src_ref=src, dst_ref=arrives, send_sem=send_sems.at[k - 1],
                                            recv_sem=recv_sems.at[k - 1], device_id=(px, py, pc), device_id_type=MESH)
        remote.append((send, recv))
    return local, remote


def _exchange_start(src, land, after, R, gather, *, name):
    def body(src_ref, land_ref, after_ref, send_sems, recv_sems, local_sem, src_thru, land_thru, token):
        local, remote = _split_copies(src_ref, land_ref, (send_sems, recv_sems, local_sem), R, gather)
        local.start()
        for send, _ in remote:
            send.start()
        token[...] = jnp.zeros_like(token)

    out = pl.pallas_call(
        body, name=name,
        out_shape=(pltpu.SemaphoreType.DMA((N_PEERS,)), pltpu.SemaphoreType.DMA((N_PEERS,)), pltpu.SemaphoreType.DMA((1,)),
                   pltpu.HBM(src.shape, src.dtype), pltpu.HBM(land.shape, land.dtype),
                   jax.ShapeDtypeStruct((8, LANES), F32)),
        in_specs=(HBM_SPEC, HBM_SPEC, ANY),
        out_specs=(SEM_SPEC, SEM_SPEC, SEM_SPEC, HBM_SPEC, HBM_SPEC, pl.BlockSpec(memory_space=pltpu.VMEM)),
        input_output_aliases={0: 3, 1: 4},
        compiler_params=pltpu.CompilerParams(has_side_effects=SPLIT_EFFECT),
    )(_hbm(src), _hbm(land), after)
    return out[:3], out[3], out[4], out[5]


def _exchange_wait(sems, src_thru, land_thru, after, R, gather, *, name):
    def body(src_ref, land_ref, send_sems, recv_sems, local_sem, after_ref, src_dead, got_ref):
        local, remote = _split_copies(src_ref, land_ref, (send_sems, recv_sems, local_sem), R, gather)
        for send, recv in remote:
            send.wait_send()
            recv.wait_recv()
        local.wait()

    return pl.pallas_call(
        body, name=name,
        out_shape=(pltpu.HBM(src_thru.shape, src_thru.dtype), pltpu.HBM(land_thru.shape, land_thru.dtype)),
        in_specs=(HBM_SPEC, HBM_SPEC, SEM_SPEC, SEM_SPEC, SEM_SPEC, ANY),
        out_specs=(HBM_SPEC, HBM_SPEC),
        input_output_aliases={0: 0, 1: 1},
        compiler_params=pltpu.CompilerParams(has_side_effects=SPLIT_EFFECT),
    )(src_thru, land_thru, *sems, after)[1]


def _gather_start(shard, after, *, pad_rows=0, name):
    R, C = shard.shape
    assert R % BF16_TILE_ROWS == 0
    land = lax.empty((N_DEV * R + pad_rows, C), shard.dtype)
    if pad_rows:
        land = lax.dynamic_update_slice(land, jnp.zeros((pad_rows, C), shard.dtype), (N_DEV * R, 0))
    return _exchange_start(shard, land, after, R, True, name=name) + (R, True)


def _scatter_start(g, R, after, *, name):
    land = lax.empty((N_DEV, R, g.shape[-1]), g.dtype)
    return _exchange_start(g, land, after, R, False, name=name) + (R, False)


def _finish(handle, after, *, name):
    sems, src_thru, land_thru, _, R, gather = handle
    return _exchange_wait(sems, src_thru, land_thru, after, R, gather, name=name)


def _all_reduce_small(v, *, name):
    rows, C = v.shape

    def body(v_ref, o_ref, land_ref, send_sems, recv_sems):
        x, y, c = _my_place()
        me = 4 * x + 2 * y + c
        copies = []
        for k in range(1, N_DEV):
            peer = (_flip(x, k & 4), _flip(y, k & 2), _flip(c, k & 1))
            copies.append(pltpu.make_async_remote_copy(
                src_ref=v_ref, dst_ref=land_ref.at[me],
                send_sem=send_sems.at[k - 1], recv_sem=recv_sems.at[k - 1], device_id=peer, device_id_type=MESH))
        for cp in copies:
            cp.start()
        land_ref[me] = v_ref[...]
        for k in range(1, N_DEV):
            peer_idx = 4 * _flip(x, k & 4) + 2 * _flip(y, k & 2) + _flip(c, k & 1)
            pltpu.make_async_remote_copy(
                src_ref=v_ref, dst_ref=land_ref.at[peer_idx],
                send_sem=send_sems.at[k - 1], recv_sem=recv_sems.at[k - 1],
                device_id=(x, y, c), device_id_type=MESH).wait_recv()
        for cp in copies:
            cp.wait_send()
        acc = land_ref[0]
        for s in range(1, N_DEV):
            acc = acc + land_ref[s]
        o_ref[...] = acc

    vm = pl.BlockSpec(memory_space=pltpu.VMEM)
    return _comm_call(
        body, name=name,
        out_shape=jax.ShapeDtypeStruct((rows, C), F32),
        in_specs=[vm], out_specs=vm,
        scratch_shapes=[pltpu.VMEM((N_DEV, rows, C), F32),
                        pltpu.SemaphoreType.DMA((7,)), pltpu.SemaphoreType.DMA((7,))],
    )(v)


def kernel(x, norm1_g, w_in, b_forget, attn_sinks, rel_bias, w_branch_a, w_branch_b, w_out, norm2_g, w_ffn_gate, w_ffn_up, w_ffn_down, final_g, loss_target, m_norm1_g, m_w_in, m_b_forget, m_attn_sinks, m_rel_bias, m_w_branch_a, m_w_branch_b, m_w_out, m_norm2_g, m_w_ffn_gate, m_w_ffn_up, m_w_ffn_down, m_final_g, v_norm1_g, v_w_in, v_b_forget, v_attn_sinks, v_rel_bias, v_w_branch_a, v_w_branch_b, v_w_out, v_norm2_g, v_w_ffn_gate, v_w_ffn_up, v_w_ffn_down, v_final_g):
    xs = x[0]
    T, D = xs.shape
    H_A, H_B = b_forget.shape[-1], attn_sinks.shape[-1]
    WA, QB = H_A * DH_A, H_B * DH_B
    R_IN = w_in.shape[-1]
    W_IN = N_DEV * R_IN
    KB = (W_IN - 3 * WA - H_A - QB - 2 * D) // 2
    HKV = KB // DH_B
    G = H_B // HKV
    N_BIG = W_IN - H_A
    GATE_COL = 3 * WA + QB + 2 * KB
    R_F = w_ffn_gate.shape[-1]
    F = N_DEV * R_F
    FP = _round_up(F, 512)
    assert H_A <= LANES and T % SWA_BLOCK == 0

    win_s = w_in[0].T.astype(BF16)
    wa_s = w_branch_a[0].T.astype(BF16)
    wb_s = w_branch_b[0].T.astype(BF16)
    wout_s = w_out[0].astype(BF16)
    wg_s = w_ffn_gate[0].T.astype(BF16)
    wu_s = w_ffn_up[0].T.astype(BF16)
    wd_s = w_ffn_down[0].astype(BF16)

    win_all = _all_gather_rows(win_s, name="ag_w_in").reshape(W_IN, D)
    w_big = jnp.concatenate([win_all[:3 * WA], win_all[3 * WA + H_A:]], axis=0)
    w_f = jnp.pad(win_all[3 * WA:3 * WA + H_A], ((0, LANES - H_A), (0, 0)))
    ag_wa = _gather_start(wa_s, win_all, name="ag_start_w_branch_a")
    ag_wb = _gather_start(wb_s, ag_wa[3], name="ag_start_w_branch_b")
    ag_wout = _gather_start(wout_s, ag_wb[3], name="ag_start_w_out")
    ag_wg = _gather_start(wg_s, ag_wout[3], pad_rows=FP - F, name="ag_start_w_ffn_gate")
    ag_wu = _gather_start(wu_s, ag_wg[3], pad_rows=FP - F, name="ag_start_w_ffn_up")
    ag_wd = _gather_start(wd_s, ag_wu[3], pad_rows=FP - F, name="ag_start_w_ffn_down")

    BM = 1024
    bn_big = _pick(N_BIG, 768, LANES)

    h1, rstd1 = _rms_fwd(xs, norm1_g, deps=[ag_wd[3]], name="rms1_fwd")
    proj = _matmul([(h1, w_big)], "nt", bm=BM, bn=bn_big, name="proj_fwd")
    f_logit = _matmul([(h1, w_f)], "nt", bm=BM, bn=LANES, out_dtype=F32, name="forget_fwd")
    b_pad = jnp.pad(b_forget, ((0, 0), (0, LANES - H_A)))
    c_all, sneg = _fox_gate_fwd(f_logit, b_pad, name="fox_gate_fwd")
    c_heads = c_all[:, :H_A].T
    c_col, c_row = c_heads[:, :, None], c_heads[:, None, :]
    oa, lse_a = _fox_fwd(proj, c_col, c_row, H_A, name="fox_fwd")

    def heads_q(a):
        return a.reshape(T, HKV, G, DH_B).transpose(1, 2, 0, 3)

    def heads_kv(a):
        return a.reshape(T, HKV, DH_B).transpose(1, 0, 2)

    qb = heads_q(proj[:, 3 * WA:3 * WA + QB])
    kb = heads_kv(proj[:, 3 * WA + QB:3 * WA + QB + KB])
    vb = heads_kv(proj[:, 3 * WA + QB + KB:GATE_COL])
    onehot = _bucket_onehot()
    bias = _bias_table(rel_bias, onehot, name="rel_bias_table").reshape(HKV, G, 2, SWA_BLOCK, SWA_BLOCK)
    sinks_b = jnp.broadcast_to(attn_sinks.reshape(HKV, G, 1, 1), (HKV, G, 1, LANES))
    ob4, lse_b = _swa_fwd(qb, kb, vb, bias, sinks_b, name="swa_fwd")
    ob = ob4.transpose(2, 0, 1, 3).reshape(T, QB)

    wa_t = _finish(ag_wa, ob, name="ag_wait_w_branch_a")
    wb_t = _finish(ag_wb, wa_t, name="ag_wait_w_branch_b")
    mixed, ya, yb = _branch_mix(oa, ob, wa_t, wb_t, proj, GATE_COL, name="branch_mix")
    wout = _finish(ag_wout, mixed, name="ag_wait_w_out")
    x1 = _matmul([(mixed, wout)], "nn", bm=BM, bn=512, out_dtype=F32, addend=xs, name="out_proj_fwd")

    h2, rstd2 = _rms_fwd(x1, norm2_g, name="rms2_fwd")
    wg_t = _finish(ag_wg, h2, name="ag_wait_w_ffn_gate")
    wu_t = _finish(ag_wu, wg_t, name="ag_wait_w_ffn_up")
    gate, up, hidden = _ffn_up(h2, wg_t, wu_t, name="ffn_up")
    wd = _finish(ag_wd, hidden, name="ag_wait_w_ffn_down")
    tk_f = _pick(FP, 2816, LANES)
    x2 = _matmul([(hidden, wd)], "nn", bm=BM, bn=1024, tk=tk_f, out_dtype=F32, addend=x1, name="ffn_down_fwd")

    dx2, loss_part, g_final = _final_loss(x2, final_g.reshape(1, D), loss_target[0], name="final_loss")
    loss = lax.psum(loss_part[0, 0], MESH_AXES)

    dx2_b = dx2.astype(BF16)
    g_wd = _matmul([(hidden, dx2_b)], "tn", bm=1024, bn=1024, name="ffn_down_bwd_w")
    rs_wd = _scatter_start(g_wd, R_F, loss_part, name="rs_start_w_ffn_down")
    dgate, dup = _matmul([(dx2_b, wd)], "nt", bm=BM, bn=512, extras=[(gate, 0), (up, 0)], epilogue=_swiglu_bwd_tile,
                         n_out=2, deps=[rs_wd[3]], name="ffn_down_bwd_x")
    g_wg = _matmul([(dgate, h2)], "tn", bm=1024, bn=1024, name="ffn_gate_bwd_w")
    rs_wg = _scatter_start(g_wg, R_F, rs_wd[3], name="rs_start_w_ffn_gate")
    g_wu = _matmul([(dup, h2)], "tn", bm=1024, bn=1024, deps=[rs_wg[3]], name="ffn_up_bwd_w")
    rs_wu = _scatter_start(g_wu, R_F, rs_wg[3], name="rs_start_w_ffn_up")
    dh2 = _matmul([(dgate, wg_t), (dup, wu_t)], "nn", bm=BM, bn=1024, tk=_pick(FP, 1408, LANES), out_dtype=F32,
                  deps=[rs_wu[3]], name="ffn_up_bwd_x")
    dx1, g_norm2 = _rms_bwd(dx2, dh2, x1, rstd2, norm2_g, name="rms2_bwd")

    dx1_b = dx1.astype(BF16)
    g_wout = _matmul([(mixed, dx1_b)], "tn", bm=1024, bn=1024, name="out_proj_bwd_w")
    rs_wout = _scatter_start(g_wout, D // N_DEV, rs_wu[3], name="rs_start_w_out")
    bn_mix = _pick(D, 256, LANES)
    assert GATE_COL % bn_mix == 0
    dya, dyb, dga, dgb = _matmul(
        [(dx1_b, wout)], "nt", bm=BM, bn=bn_mix, epilogue=_mix_bwd_tile, n_out=4, deps=[rs_wout[3]],
        extras=[(ya, 0), (yb, 0), (proj, GATE_COL // bn_mix), (proj, (GATE_COL + D) // bn_mix)], name="out_proj_bwd_x")
    g_wa = _matmul([(dya, oa)], "tn", bm=1024, bn=1024, name="branch_a_bwd_w")
    rs_wa = _scatter_start(g_wa, D // N_DEV, rs_wout[3], name="rs_start_w_branch_a")
    g_wb = _matmul([(dyb, ob)], "tn", bm=1024, bn=1024, deps=[rs_wa[3]], name="branch_b_bwd_w")
    rs_wb = _scatter_start(g_wb, D // N_DEV, rs_wa[3], name="rs_start_w_branch_b")
    doa = _matmul([(dya, wa_t)], "nn", bm=BM, bn=1024, deps=[rs_wb[3]], name="branch_a_bwd_x")
    dob = _matmul([(dyb, wb_t)], "nn", bm=BM, bn=1024, name="branch_b_bwd_x")

    delta_a = _rowdot(doa, oa, H_A, name="fox_delta")
    dqa, dka, dva, dc_key, dc_query = _fox_bwd(proj, doa, lse_a, delta_a, c_col, c_row, H_A, name="fox_bwd")
    dc = jnp.pad((dc_key[:, 0, :] + dc_query[:, :, 0]).T, ((0, 0), (0, LANES - H_A)))
    df, g_bf = _fox_gate_bwd(dc, sneg, name="fox_gate_bwd")

    dqb4, dkb3, dvb3, dbias, dsinks = _swa_bwd(qb, kb, vb, ob4, heads_q(dob), lse_b, bias, sinks_b, name="swa_bwd")
    g_relb = _bias_table_bwd(dbias.reshape(H_B, -1), onehot, name="rel_bias_table_bwd")
    dqb = dqb4.transpose(2, 0, 1, 3).reshape(T, QB)
    dkb = dkb3.transpose(1, 0, 2).reshape(T, KB)
    dvb = dvb3.transpose(1, 0, 2).reshape(T, KB)

    dproj = jnp.concatenate([dqa.astype(BF16), dka, dva, dqb, dkb, dvb, dga, dgb], axis=1)
    g_wbig = _matmul([(dproj, h1)], "tn", bm=bn_big, bn=1024, name="proj_bwd_w")
    g_wf = _matmul([(df, h1)], "tn", bm=LANES, bn=1024, name="forget_bwd_w")
    g_win = jnp.concatenate([g_wbig[:3 * WA], g_wf[:H_A], g_wbig[3 * WA:]], axis=0).reshape(N_DEV, R_IN, D)
    rs_win = _scatter_start(g_win, R_IN, rs_wb[3], name="rs_start_w_in")
    dh_f = _matmul([(df, w_f)], "nn", bm=BM, bn=1024, out_dtype=F32, deps=[rs_win[3]], name="forget_bwd_x")
    tk_big = _pick(N_BIG, 2816, LANES)
    dh1 = _matmul([(dproj, w_big)], "nn", bm=BM, bn=1024, tk=tk_big, out_dtype=F32, addend=dh_f, name="proj_bwd_x")
    grad_x, g_norm1 = _rms_bwd(dx1, dh1, xs, rstd1, norm1_g, name="rms1_bwd")

    def update(handle, after, w, m, v, transposed, nm):
        take = (lambda a: a[0].T) if transposed else (lambda a: a[0])
        give = (lambda a: a.T[None]) if transposed else (lambda a: a[None])
        land = _finish(handle, after, name="rs_wait_" + nm)
        return tuple(give(o) for o in _sum_adamw(land, take(w), take(m), take(v), name="adamw_" + nm))

    big = {}
    big["w_ffn_down"] = update(rs_wd, grad_x, w_ffn_down, m_w_ffn_down, v_w_ffn_down, False, "w_ffn_down")
    big["w_ffn_gate"] = update(rs_wg, big["w_ffn_down"][1], w_ffn_gate, m_w_ffn_gate, v_w_ffn_gate, True, "w_ffn_gate")
    big["w_ffn_up"] = update(rs_wu, big["w_ffn_gate"][1], w_ffn_up, m_w_ffn_up, v_w_ffn_up, True, "w_ffn_up")
    big["w_out"] = update(rs_wout, big["w_ffn_up"][1], w_out, m_w_out, v_w_out, False, "w_out")
    big["w_branch_a"] = update(rs_wa, big["w_out"][1], w_branch_a, m_w_branch_a, v_w_branch_a, True, "w_branch_a")
    big["w_branch_b"] = update(rs_wb, big["w_branch_a"][1], w_branch_b, m_w_branch_b, v_w_branch_b, True, "w_branch_b")
    big["w_in"] = update(rs_win, big["w_branch_b"][1], w_in, m_w_in, v_w_in, True, "w_in")

    n_rb = NUM_BUCKETS * H_B
    assert D >= n_rb and D >= H_A + H_B
    small = jnp.concatenate([
        g_norm1, g_norm2, g_final,
        jnp.pad(jnp.concatenate([g_bf[:, :H_A], dsinks[:, :, 0, 0].reshape(1, H_B)], axis=1), ((0, 0), (0, D - H_A - H_B))),
        jnp.pad(g_relb.reshape(1, n_rb), ((0, 0), (0, D - n_rb))),
        jnp.zeros((3, D), F32)], axis=0)
    small = _all_reduce_small(small, name="ar_small")
    gs = {
        "norm1_g": small[0:1], "norm2_g": small[1:2], "final_g": small[2],
        "b_forget": small[3:4, :H_A], "attn_sinks": small[3:4, H_A:H_A + H_B],
        "rel_bias": small[4, :n_rb].reshape(NUM_BUCKETS, H_B),
    }

    def adam_small(ws, gsm, ms, vs):
        def pack(parts):
            rows = [jnp.pad(p.reshape(1, -1), ((0, 0), (0, D - p.size))) for p in parts]
            return jnp.concatenate(rows + [jnp.ones((8 - len(rows), D), F32)], axis=0)
        d, nm_, nv_ = _adamw(pack(ws), pack(gsm), pack(ms), pack(vs), name="adamw_small")
        unpack = lambda a: [a[i, :p.size].reshape(p.shape) for i, p in enumerate(ws)]
        return unpack(d), unpack(nm_), unpack(nv_)

    small_names = ["norm1_g", "b_forget", "attn_sinks", "rel_bias", "norm2_g", "final_g"]
    small_w = [norm1_g, b_forget, attn_sinks, rel_bias, norm2_g, final_g]
    small_g = [gs[n].reshape(w.shape) for n, w in zip(small_names, small_w)]
    small_m = [m_norm1_g, m_b_forget, m_attn_sinks, m_rel_bias, m_norm2_g, m_final_g]
    small_v = [v_norm1_g, v_b_forget, v_attn_sinks, v_rel_bias, v_norm2_g, v_final_g]
    sd, sm, sv = adam_small(small_w, small_g, small_m, small_v)
    for i, n in enumerate(small_names):
        big[n] = (small_g[i], sd[i], sm[i], sv[i])

    order = ["norm1_g", "w_in", "b_forget", "attn_sinks", "rel_bias", "w_branch_a", "w_branch_b", "w_out",
             "norm2_g", "w_ffn_gate", "w_ffn_up", "w_ffn_down", "final_g"]
    grads = [big[n][0] for n in order]
    deltas = [big[n][1] for n in order]
    new_m = [big[n][2] for n in order]
    new_v = [big[n][3] for n in order]
    return (loss, grad_x[None], *grads, *deltas, *new_m, *new_v)
```

```python
import functools

import numpy as np
import jax
import jax.numpy as jnp
from jax import lax
from jax.experimental import pallas as pl
from jax.experimental.pallas import tpu as pltpu

N_DEV = 8
MESH_AXES = ("x", "y", "c")
DH_A = 128
DH_B = 64
SWA_BLOCK = 128
FOX_TILE = 1024
NUM_BUCKETS = 32
MAX_DISTANCE = 128
EPS = 1e-6
ADAM_LR = 0.001
ADAM_B1 = 0.9
ADAM_B2 = 0.999
ADAM_EPS = 1e-08
ADAM_WD = 0.01
ADAM_STEP = 10
NEG = -1e30
LANES = 128
BF16_TILE_ROWS = 16
V7X_VMEM_LIMIT = 56 * 1024 * 1024
F32 = jnp.float32
BF16 = jnp.bfloat16
MESH = pl.DeviceIdType.MESH
ANY = pl.BlockSpec(memory_space=pl.ANY)


def _round_up(n, m):
    return (n + m - 1) // m * m


def _pick(n, target, mult):
    best = None
    for d in range(mult, min(n, target) + 1, mult):
        if n % d == 0:
            best = d
    return n if best is None else best


def _pcall(kernel, *, name, in_specs, dims=None, deps=(), **kw):
    deps = tuple(deps)
    if deps:
        inner, n_in, n_dep = kernel, len(in_specs), len(deps)

        def kernel(*refs):
            inner(*refs[:n_in], *refs[n_in + n_dep:])

        in_specs = list(in_specs) + [ANY] * n_dep
    call = pl.pallas_call(
        kernel, name=name, in_specs=in_specs,
        compiler_params=pltpu.CompilerParams(dimension_semantics=dims, vmem_limit_bytes=V7X_VMEM_LIMIT),
        **kw)
    return lambda *operands: call(*operands, *deps)


def _comm_call(kernel, *, name, **kw):
    return pl.pallas_call(kernel, name=name, **kw)


def _sigmoid(v):
    return 1.0 / (1.0 + jnp.exp(-v))


_DIMS = {"nn": (((1,), (0,)), ((), ())), "nt": (((1,), (1,)), ((), ())), "tn": (((0,), (0,)), ((), ()))}


def _dot(a, b, form="nn"):
    return lax.dot_general(a, b, _DIMS[form], preferred_element_type=F32)


def _matmul(pairs, form, *, bm, bn, tk=None, out_dtype=BF16, addend=None, extras=(), epilogue=None, n_out=1, deps=(),
            name):
    a0, b0 = pairs[0]
    if form == "tn":
        K, M = a0.shape
    else:
        M, K = a0.shape
    N = b0.shape[0] if form == "nt" else b0.shape[1]
    tk = K if tk is None else tk
    bm, bn = min(bm, M), min(bn, N)
    assert M % bm == 0 and N % bn == 0 and K % tk == 0, (name, M, N, K, bm, bn, tk)
    nk = K // tk
    n_pairs = len(pairs)
    has_add = addend is not None
    if form == "tn":
        a_spec = pl.BlockSpec((tk, bm), lambda i, j, k: (k, i))
    else:
        a_spec = pl.BlockSpec((bm, tk), lambda i, j, k: (i, k))
    if form == "nt":
        b_spec = pl.BlockSpec((bn, tk), lambda i, j, k: (j, k))
    else:
        b_spec = pl.BlockSpec((tk, bn), lambda i, j, k: (k, j))
    o_spec = pl.BlockSpec((bm, bn), lambda i, j, k: (i, j))
    n_extra = len(extras)
    n_in = 2 * n_pairs + has_add + n_extra

    def kernel(*refs):
        ab = refs[:2 * n_pairs]
        c_ref = refs[2 * n_pairs] if has_add else None
        extra_refs = refs[2 * n_pairs + has_add:n_in]
        o_refs = refs[n_in:n_in + n_out]

        def partial_sum():
            acc = _dot(ab[0][...], ab[1][...], form)
            for p in range(1, n_pairs):
                acc = acc + _dot(ab[2 * p][...], ab[2 * p + 1][...], form)
            return acc

        def finish(acc):
            if has_add:
                acc = acc + c_ref[...].astype(F32)
            outs = (acc,) if epilogue is None else epilogue(acc, *[r[...] for r in extra_refs])
            for o_ref, val in zip(o_refs, outs, strict=True):
                o_ref[...] = val.astype(o_ref.dtype)

        if nk == 1:
            finish(partial_sum())
        else:
            acc_ref = refs[-1]
            k = pl.program_id(2)

            @pl.when(k == 0)
            def _():
                acc_ref[...] = jnp.zeros_like(acc_ref)

            acc_ref[...] += partial_sum()

            @pl.when(k == nk - 1)
            def _():
                finish(acc_ref[...])

    operands, in_specs = [], []
    for a, b in pairs:
        operands += [a, b]
        in_specs += [a_spec, b_spec]
    if has_add:
        operands.append(addend)
        in_specs.append(o_spec)
    for arr, first in extras:
        operands.append(arr)
        in_specs.append(pl.BlockSpec((bm, bn), lambda i, j, k, first=first: (i, first + j)))
    out_shape = tuple(jax.ShapeDtypeStruct((M, N), out_dtype) for _ in range(n_out))
    out = _pcall(
        kernel, name=name, dims=("parallel", "parallel", "arbitrary"), deps=deps,
        out_shape=out_shape, grid=(M // bm, N // bn, nk), in_specs=in_specs, out_specs=(o_spec,) * n_out,
        scratch_shapes=[pltpu.VMEM((bm, bn), F32)] if nk > 1 else [],
    )(*operands)
    return out[0] if epilogue is None else out


def _rms_fwd(x, g, *, deps=(), name):
    T, D = x.shape
    br = _pick(T, 256, 8)

    def kernel(x_ref, g_ref, h_ref, r_ref):
        xv = x_ref[...]
        rstd = lax.rsqrt(jnp.mean(xv * xv, axis=-1, keepdims=True) + EPS)
        h_ref[...] = (xv * rstd * g_ref[...]).astype(BF16)
        r_ref[...] = rstd

    return _pcall(
        kernel, name=name, dims=("parallel",), deps=deps,
        out_shape=(jax.ShapeDtypeStruct((T, D), BF16), jax.ShapeDtypeStruct((T, 1), F32)),
        grid=(T // br,),
        in_specs=[pl.BlockSpec((br, D), lambda i: (i, 0)), pl.BlockSpec((1, D), lambda i: (0, 0))],
        out_specs=(pl.BlockSpec((br, D), lambda i: (i, 0)), pl.BlockSpec((br, 1), lambda i: (i, 0))),
    )(x, g)


def _rms_bwd(dres, dh, x, rstd, g, *, deps=(), name):
    T, D = x.shape
    br = _pick(T, 256, 8)

    def kernel(dres_ref, dh_ref, x_ref, r_ref, g_ref, dx_ref, gg_ref):
        @pl.when(pl.program_id(0) == 0)
        def _():
            gg_ref[...] = jnp.zeros_like(gg_ref)

        rstd_v = r_ref[...]
        xhat = x_ref[...] * rstd_v
        dhv = dh_ref[...].astype(F32)
        dxhat = dhv * g_ref[...]
        proj = jnp.mean(dxhat * xhat, axis=-1, keepdims=True)
        dx_ref[...] = dres_ref[...] + rstd_v * (dxhat - xhat * proj)
        gg_ref[...] += jnp.sum(dhv * xhat, axis=0, keepdims=True)

    row = pl.BlockSpec((br, D), lambda i: (i, 0))
    vec = pl.BlockSpec((1, D), lambda i: (0, 0))
    return _pcall(
        kernel, name=name, dims=("arbitrary",), deps=deps,
        out_shape=(jax.ShapeDtypeStruct((T, D), F32), jax.ShapeDtypeStruct((1, D), F32)),
        grid=(T // br,),
        in_specs=[row, row, row, pl.BlockSpec((br, 1), lambda i: (i, 0)), vec],
        out_specs=(row, vec),
    )(dres, dh, x, rstd, g)


def _final_loss(x2, gf, target, *, name):
    T, D = x2.shape
    br = _pick(T, 256, 8)

    def kernel(x_ref, g_ref, t_ref, dx_ref, loss_ref, gg_ref):
        @pl.when(pl.program_id(0) == 0)
        def _():
            gg_ref[...] = jnp.zeros_like(gg_ref)
            loss_ref[...] = jnp.zeros_like(loss_ref)

        xv = x_ref[...]
        gv = g_ref[...]
        rstd = lax.rsqrt(jnp.mean(xv * xv, axis=-1, keepdims=True) + EPS)
        xhat = xv * rstd
        err = xhat * gv - t_ref[...]
        loss_ref[...] += 0.5 * jnp.sum(jnp.mean(err * err, axis=-1, keepdims=True), axis=0, keepdims=True)
        dy = err * (1.0 / D)
        dxhat = dy * gv
        proj = jnp.mean(dxhat * xhat, axis=-1, keepdims=True)
        dx_ref[...] = rstd * (dxhat - xhat * proj)
        gg_ref[...] += jnp.sum(dy * xhat, axis=0, keepdims=True)

    row = pl.BlockSpec((br, D), lambda i: (i, 0))
    vec = pl.BlockSpec((1, D), lambda i: (0, 0))
    return _pcall(
        kernel, name=name, dims=("arbitrary",),
        out_shape=(jax.ShapeDtypeStruct((T, D), F32), jax.ShapeDtypeStruct((1, 1), F32),
                   jax.ShapeDtypeStruct((1, D), F32)),
        grid=(T // br,),
        in_specs=[row, vec, row],
        out_specs=(row, pl.BlockSpec((1, 1), lambda i: (0, 0)), vec),
    )(x2, gf, target)


def _scan_rows(v, reverse):
    n = v.shape[0]
    row = lax.broadcasted_iota(jnp.int32, v.shape, 0)
    s = 1
    while s < n:
        if reverse:
            v = v + jnp.where(row < n - s, pltpu.roll(v, n - s, 0), 0.0)
        else:
            v = v + jnp.where(row >= s, pltpu.roll(v, s, 0), 0.0)
        s *= 2
    return v


def _fox_gate_fwd(f, bias, *, name):
    T = f.shape[0]
    tb = _pick(T, 512, 8)

    def kernel(f_ref, b_ref, c_ref, s_ref, carry):
        @pl.when(pl.program_id(0) == 0)
        def _():
            carry[...] = jnp.zeros_like(carry)

        fa = f_ref[...] + b_ref[...]
        z = jnp.exp(-jnp.abs(fa))
        log_f = jnp.minimum(fa, 0.0) - jnp.log(1.0 + z)
        s_ref[...] = jnp.where(fa >= 0, z, 1.0) / (1.0 + z)
        c = _scan_rows(log_f, False) + carry[...]
        c_ref[...] = c
        carry[...] = c[tb - 1:tb, :]

    blk = pl.BlockSpec((tb, LANES), lambda i: (i, 0))
    return _pcall(
        kernel, name=name, dims=("arbitrary",),
        out_shape=(jax.ShapeDtypeStruct((T, LANES), F32), jax.ShapeDtypeStruct((T, LANES), F32)),
        grid=(T // tb,),
        in_specs=[blk, pl.BlockSpec((1, LANES), lambda i: (0, 0))],
        out_specs=(blk, blk),
        scratch_shapes=[pltpu.VMEM((1, LANES), F32)],
    )(f, bias)


def _fox_gate_bwd(dc, sneg, *, name):
    T = dc.shape[0]
    tb = _pick(T, 512, 8)
    nb = T // tb

    def kernel(dc_ref, s_ref, df_ref, gb_ref, carry):
        @pl.when(pl.program_id(0) == 0)
        def _():
            carry[...] = jnp.zeros_like(carry)
            gb_ref[...] = jnp.zeros_like(gb_ref)

        dlog = _scan_rows(dc_ref[...], True) + carry[...]
        carry[...] = dlog[0:1, :]
        dfa = dlog * s_ref[...]
        df_ref[...] = dfa.astype(BF16)
        gb_ref[...] += jnp.sum(dfa, axis=0, keepdims=True)

    blk = pl.BlockSpec((tb, LANES), lambda i: (nb - 1 - i, 0))
    return _pcall(
        kernel, name=name, dims=("arbitrary",),
        out_shape=(jax.ShapeDtypeStruct((T, LANES), BF16), jax.ShapeDtypeStruct((1, LANES), F32)),
        grid=(nb,),
        in_specs=[blk, blk],
        out_specs=(blk, pl.BlockSpec((1, LANES), lambda i: (0, 0))),
        scratch_shapes=[pltpu.VMEM((1, LANES), F32)],
    )(dc, sneg)


def _fox_scores(q, k, cq, ck, diagonal):
    s = _dot(q, k, "nt") * (DH_A ** -0.5) + cq - ck
    if diagonal:
        row = lax.broadcasted_iota(jnp.int32, s.shape, 0)
        col = lax.broadcasted_iota(jnp.int32, s.shape, 1)
        s = jnp.where(col <= row, s, NEG)
    return s


def _fox_fwd(proj, c_col, c_row, n_heads, *, deps=(), name):
    T = proj.shape[0]
    H = n_heads
    tq = tk = _pick(T, FOX_TILE, 128)
    nq = T // tq

    def kernel(q_ref, k_ref, v_ref, cq_ref, ck_ref, o_ref, lse_ref, m_sc, l_sc, acc_sc):
        i, j = pl.program_id(1), pl.program_id(2)

        @pl.when(j == 0)
        def _():
            m_sc[...] = jnp.full_like(m_sc, NEG)
            l_sc[...] = jnp.zeros_like(l_sc)
            acc_sc[...] = jnp.zeros_like(acc_sc)

        def tile(diagonal):
            s = _fox_scores(q_ref[...], k_ref[...], cq_ref[...], ck_ref[...], diagonal)
            m_prev = m_sc[...]
            m_new = jnp.maximum(m_prev, jnp.max(s, axis=-1, keepdims=True))
            alpha = jnp.exp(m_prev - m_new)
            p = jnp.exp(s - m_new)
            l_sc[...] = alpha * l_sc[...] + jnp.sum(p, axis=-1, keepdims=True)
            acc_sc[...] = alpha * acc_sc[...] + _dot(p.astype(BF16), v_ref[...])
            m_sc[...] = m_new

        @pl.when(j < i)
        def _():
            tile(False)

        @pl.when(j == i)
        def _():
            tile(True)
            l = l_sc[...]
            o_ref[...] = (acc_sc[...] / l).astype(BF16)
            lse_ref[...] = m_sc[...] + jnp.log(l)

    return _pcall(
        kernel, name=name, dims=("parallel", "parallel", "arbitrary"), deps=deps,
        out_shape=(jax.ShapeDtypeStruct((T, H * DH_A), BF16), jax.ShapeDtypeStruct((H, T, 1), F32)),
        grid=(H, nq, nq),
        in_specs=[
            pl.BlockSpec((tq, DH_A), lambda h, i, j: (i, h)),
            pl.BlockSpec((tk, DH_A), lambda h, i, j: (jnp.minimum(j, i), H + h)),
            pl.BlockSpec((tk, DH_A), lambda h, i, j: (jnp.minimum(j, i), 2 * H + h)),
            pl.BlockSpec((None, tq, 1), lambda h, i, j: (h, i, 0)),
            pl.BlockSpec((None, 1, tk), lambda h, i, j: (h, 0, jnp.minimum(j, i))),
        ],
        out_specs=(pl.BlockSpec((tq, DH_A), lambda h, i, j: (i, h)),
                   pl.BlockSpec((None, tq, 1), lambda h, i, j: (h, i, 0))),
        scratch_shapes=[pltpu.VMEM((tq, 1), F32), pltpu.VMEM((tq, 1), F32), pltpu.VMEM((tq, DH_A), F32)],
    )(proj, proj, proj, c_col, c_row)


def _rowdot(a, b, n_heads, *, name):
    T = a.shape[0]
    tq = _pick(T, 512, 8)

    def kernel(a_ref, b_ref, o_ref):
        o_ref[...] = jnp.sum(a_ref[...].astype(F32) * b_ref[...].astype(F32), axis=-1, keepdims=True)

    blk = pl.BlockSpec((tq, DH_A), lambda h, i: (i, h))
    return _pcall(
        kernel, name=name, dims=("parallel", "parallel"),
        out_shape=jax.ShapeDtypeStruct((n_heads, T, 1), F32),
        grid=(n_heads, T // tq), in_specs=[blk, blk],
        out_specs=pl.BlockSpec((None, tq, 1), lambda h, i: (h, i, 0)),
    )(a, b)


def _fox_bwd(proj, do, lse, delta, c_col, c_row, n_heads, *, name):
    T = proj.shape[0]
    H = n_heads
    tq = tk = _pick(T, FOX_TILE, 128)
    nq = T // tq
    scale = DH_A ** -0.5

    def kernel(q_ref, k_ref, v_ref, do_ref, lse_ref, dl_ref, cq_ref, ck_ref,
               dq_ref, dk_ref, dv_ref, dc_ref, dcq_ref, dk_sc, dv_sc, dc_sc):
        j, i = pl.program_id(1), pl.program_id(2)

        @pl.when((j == 0) & (i == 0))
        def _():
            dq_ref[...] = jnp.zeros_like(dq_ref)
            dcq_ref[...] = jnp.zeros_like(dcq_ref)

        @pl.when(i == 0)
        def _():
            dk_sc[...] = jnp.zeros_like(dk_sc)
            dv_sc[...] = jnp.zeros_like(dv_sc)
            dc_sc[...] = jnp.zeros_like(dc_sc)

        def tile(diagonal):
            q, k, v, dov = q_ref[...], k_ref[...], v_ref[...], do_ref[...]
            s = _fox_scores(q, k, cq_ref[...], ck_ref[...], diagonal)
            p = jnp.exp(s - lse_ref[...])
            dv_sc[...] += _dot(p.astype(BF16), dov, "tn")
            dp = _dot(dov, v, "nt")
            ds = p * (dp - dl_ref[...])
            dc_sc[...] -= jnp.sum(ds, axis=0, keepdims=True)
            dsb = ds.astype(BF16)
            dk_sc[...] += _dot(dsb, q, "tn") * scale
            rows = pl.ds(pl.multiple_of(i * tq, tq), tq)
            dq_ref[rows, :] += _dot(dsb, k) * scale
            dcq_ref[rows, :] += jnp.sum(ds, axis=1, keepdims=True)

        @pl.when(i > j)
        def _():
            tile(False)

        @pl.when(i == j)
        def _():
            tile(True)

        @pl.when(i == nq - 1)
        def _():
            dk_ref[...] = dk_sc[...].astype(BF16)
            dv_ref[...] = dv_sc[...].astype(BF16)
            dc_ref[...] = dc_sc[...]

    qi = lambda j, i: jnp.maximum(i, j)
    return _pcall(
        kernel, name=name, dims=("parallel", "arbitrary", "arbitrary"),
        out_shape=(jax.ShapeDtypeStruct((T, H * DH_A), F32), jax.ShapeDtypeStruct((T, H * DH_A), BF16),
                   jax.ShapeDtypeStruct((T, H * DH_A), BF16), jax.ShapeDtypeStruct((H, 1, T), F32),
                   jax.ShapeDtypeStruct((H, T, 1), F32)),
        grid=(H, nq, nq),
        in_specs=[
            pl.BlockSpec((tq, DH_A), lambda h, j, i: (qi(j, i), h)),
            pl.BlockSpec((tk, DH_A), lambda h, j, i: (j, H + h)),
            pl.BlockSpec((tk, DH_A), lambda h, j, i: (j, 2 * H + h)),
            pl.BlockSpec((tq, DH_A), lambda h, j, i: (qi(j, i), h)),
            pl.BlockSpec((None, tq, 1), lambda h, j, i: (h, qi(j, i), 0)),
            pl.BlockSpec((None, tq, 1), lambda h, j, i: (h, qi(j, i), 0)),
            pl.BlockSpec((None, tq, 1), lambda h, j, i: (h, qi(j, i), 0)),
            pl.BlockSpec((None, 1, tk), lambda h, j, i: (h, 0, j)),
        ],
        out_specs=(pl.BlockSpec((T, DH_A), lambda h, j, i: (0, h)),
                   pl.BlockSpec((tk, DH_A), lambda h, j, i: (j, h)),
                   pl.BlockSpec((tk, DH_A), lambda h, j, i: (j, h)),
                   pl.BlockSpec((None, 1, tk), lambda h, j, i: (h, 0, j)),
                   pl.BlockSpec((None, T, 1), lambda h, j, i: (h, 0, 0))),
        scratch_shapes=[pltpu.VMEM((tk, DH_A), F32), pltpu.VMEM((tk, DH_A), F32), pltpu.VMEM((1, tk), F32)],
    )(proj, proj, proj, do, lse, delta, c_col, c_row)


def _t5_bucket(dist):
    max_exact = NUM_BUCKETS // 2
    small = dist < max_exact
    large = max_exact + (np.log(np.maximum(dist, 1) / max_exact) / np.log(MAX_DISTANCE / max_exact)
                         * (NUM_BUCKETS - max_exact)).astype(np.int64)
    large = np.minimum(large, NUM_BUCKETS - 1)
    return np.where(small, dist, large)


def _bucket_onehot():
    ql = np.arange(SWA_BLOCK)[:, None]
    kl = np.arange(2 * SWA_BLOCK)[None, :]
    bucket = _t5_bucket(np.clip(ql + SWA_BLOCK - kl, 0, None))
    bucket = np.stack([bucket[:, :SWA_BLOCK], bucket[:, SWA_BLOCK:]], axis=0)
    onehot = (bucket.reshape(1, -1) == np.arange(NUM_BUCKETS)[:, None])
    return jnp.asarray(onehot, dtype=BF16)


def _split3(v):
    hi = v.astype(BF16)
    r1 = v - hi.astype(F32)
    mid = r1.astype(BF16)
    lo = (r1 - mid.astype(F32)).astype(BF16)
    return hi, mid, lo


def _bias_table(rel_bias, onehot, *, name):
    B, H = rel_bias.shape
    n = onehot.shape[1]
    bc = _pick(n, 8192, LANES)

    def kernel(rb_ref, oh_ref, o_ref):
        hi, mid, lo = _split3(rb_ref[...])
        oh = oh_ref[...]
        o_ref[...] = _dot(hi, oh, "tn") + _dot(mid, oh, "tn") + _dot(lo, oh, "tn")

    return _pcall(
        kernel, name=name, dims=("parallel",),
        out_shape=jax.ShapeDtypeStruct((H, n), F32), grid=(n // bc,),
        in_specs=[pl.BlockSpec((B, H), lambda i: (0, 0)), pl.BlockSpec((B, bc), lambda i: (0, i))],
        out_specs=pl.BlockSpec((H, bc), lambda i: (0, i)),
    )(rel_bias, onehot)


def _bias_table_bwd(dbias, onehot, *, name):
    H, n = dbias.shape
    B = onehot.shape[0]
    bc = _pick(n, 8192, LANES)

    def kernel(db_ref, oh_ref, o_ref):
        @pl.when(pl.program_id(0) == 0)
        def _():
            o_ref[...] = jnp.zeros_like(o_ref)

        hi, mid, lo = _split3(db_ref[...])
        oh = oh_ref[...]
        o_ref[...] += _dot(oh, hi, "nt") + _dot(oh, mid, "nt") + _dot(oh, lo, "nt")

    return _pcall(
        kernel, name=name, dims=("arbitrary",),
        out_shape=jax.ShapeDtypeStruct((B, H), F32), grid=(n // bc,),
        in_specs=[pl.BlockSpec((H, bc), lambda i: (0, i)), pl.BlockSpec((B, bc), lambda i: (0, i))],
        out_specs=pl.BlockSpec((B, H), lambda i: (0, 0)),
    )(dbias, onehot)


def _swa_probs(q2, kp, kc, bias_ref, lse, n, G):
    Q = SWA_BLOCK
    scale = DH_B ** -0.5
    sp = (_dot(q2, kp, "nt") * scale).reshape(G, Q, Q) + bias_ref[:, 0]
    sc = (_dot(q2, kc, "nt") * scale).reshape(G, Q, Q) + bias_ref[:, 1]
    row = lax.broadcasted_iota(jnp.int32, (G, Q, Q), 1)
    col = lax.broadcasted_iota(jnp.int32, (G, Q, Q), 2)
    vis_p = (col > row) & (n > 0)
    vis_c = col <= row
    sp = jnp.where(vis_p, sp, NEG)
    sc = jnp.where(vis_c, sc, NEG)
    if lse is None:
        return sp, sc, vis_p, vis_c
    pp = jnp.where(vis_p, jnp.exp(sp - lse), 0.0)
    pc = jnp.where(vis_c, jnp.exp(sc - lse), 0.0)
    return pp, pc


def _swa_fwd(q, k, v, bias, sinks, *, deps=(), name):
    HKV, G, T, _ = q.shape
    Q = SWA_BLOCK
    N = T // Q

    def kernel(q_ref, kp_ref, kc_ref, vp_ref, vc_ref, b_ref, s_ref, o_ref, lse_ref):
        n = pl.program_id(1)
        q2 = q_ref[...].reshape(G * Q, DH_B)
        sp, sc, vis_p, vis_c = _swa_probs(q2, kp_ref[...], kc_ref[...], b_ref, None, n, G)
        sink = s_ref[...][:, :, 0:1]
        m = jnp.maximum(jnp.maximum(jnp.max(sp, axis=-1, keepdims=True), jnp.max(sc, axis=-1, keepdims=True)), sink)
        pp = jnp.where(vis_p, jnp.exp(sp - m), 0.0)
        pc = jnp.where(vis_c, jnp.exp(sc - m), 0.0)
        denom = jnp.sum(pp, axis=-1, keepdims=True) + jnp.sum(pc, axis=-1, keepdims=True) + jnp.exp(sink - m)
        o = _dot(pp.reshape(G * Q, Q).astype(BF16), vp_ref[...]) + _dot(pc.reshape(G * Q, Q).astype(BF16), vc_ref[...])
        o_ref[...] = (o.reshape(G, Q, DH_B) / denom).astype(BF16)
        lse_ref[...] = m + jnp.log(denom)

    prev = lambda h, n: (h, jnp.maximum(n - 1, 0), 0)
    cur = lambda h, n: (h, n, 0)
    kv = lambda f: pl.BlockSpec((None, Q, DH_B), f)
    return _pcall(
        kernel, name=name, dims=("parallel", "parallel"), deps=deps,
        out_shape=(jax.ShapeDtypeStruct((HKV, G, T, DH_B), BF16), jax.ShapeDtypeStruct((HKV, G, T, 1), F32)),
        grid=(HKV, N),
        in_specs=[pl.BlockSpec((None, G, Q, DH_B), lambda h, n: (h, 0, n, 0)),
                  kv(prev), kv(cur), kv(prev), kv(cur),
                  pl.BlockSpec((None, G, 2, Q, Q), lambda h, n: (h, 0, 0, 0, 0)),
                  pl.BlockSpec((None, G, 1, LANES), lambda h, n: (h, 0, 0, 0))],
        out_specs=(pl.BlockSpec((None, G, Q, DH_B), lambda h, n: (h, 0, n, 0)),
                   pl.BlockSpec((None, G, Q, 1), lambda h, n: (h, 0, n, 0))),
    )(q, k, k, v, v, bias, sinks)


def _swa_bwd(q, k, v, o, do, lse, bias, sinks, *, name):
    HKV, G, T, _ = q.shape
    Q = SWA_BLOCK
    N = T // Q
    scale = DH_B ** -0.5

    def kernel(q_ref, kp_ref, kc_ref, vp_ref, vc_ref, o_ref, do_ref, lse_ref, b_ref, s_ref,
               dq_ref, dk_ref, dv_ref, db_ref, ds_ref, ck_sc, cv_sc):
        n = pl.program_id(1)

        @pl.when(n == 0)
        def _():
            db_ref[...] = jnp.zeros_like(db_ref)
            ds_ref[...] = jnp.zeros_like(ds_ref)
            ck_sc[...] = jnp.zeros_like(ck_sc)
            cv_sc[...] = jnp.zeros_like(cv_sc)

        @pl.when(n < N)
        def _():
            q2 = q_ref[...].reshape(G * Q, DH_B)
            do3 = do_ref[...]
            do2 = do3.reshape(G * Q, DH_B)
            kp, kc, vp, vc = kp_ref[...], kc_ref[...], vp_ref[...], vc_ref[...]
            lse_v = lse_ref[...]
            pp, pc = _swa_probs(q2, kp, kc, b_ref, lse_v, n, G)
            delta = jnp.sum(do3.astype(F32) * o_ref[...].astype(F32), axis=-1, keepdims=True)
            dsp = pp * (_dot(do2, vp, "nt").reshape(G, Q, Q) - delta)
            dsc = pc * (_dot(do2, vc, "nt").reshape(G, Q, Q) - delta)
            p_sink = jnp.exp(s_ref[...][:, :, 0:1] - lse_v)
            ds_ref[...] += jnp.broadcast_to(-jnp.sum(p_sink * delta, axis=1, keepdims=True), (G, 1, LANES))
            db_ref[:, 0] += dsp
            db_ref[:, 1] += dsc
            dsp2 = dsp.reshape(G * Q, Q).astype(BF16)
            dsc2 = dsc.reshape(G * Q, Q).astype(BF16)
            dq = (_dot(dsp2, kp) + _dot(dsc2, kc)) * scale
            dq_ref[...] = dq.reshape(G, Q, DH_B).astype(BF16)
            pp2 = pp.reshape(G * Q, Q).astype(BF16)
            pc2 = pc.reshape(G * Q, Q).astype(BF16)
            dk_ref[...] = (ck_sc[...] + _dot(dsp2, q2, "tn") * scale).astype(BF16)
            dv_ref[...] = (cv_sc[...] + _dot(pp2, do2, "tn")).astype(BF16)
            ck_sc[...] = _dot(dsc2, q2, "tn") * scale
            cv_sc[...] = _dot(pc2, do2, "tn")

        @pl.when(n == N)
        def _():
            dk_ref[...] = ck_sc[...].astype(BF16)
            dv_ref[...] = cv_sc[...].astype(BF16)

    qn = lambda n: jnp.minimum(n, N - 1)
    prev = lambda h, n: (h, jnp.maximum(qn(n) - 1, 0), 0)
    cur = lambda h, n: (h, qn(n), 0)
    out_kv = lambda h, n: (h, jnp.maximum(n - 1, 0), 0)
    kv = lambda f: pl.BlockSpec((None, Q, DH_B), f)
    qspec = pl.BlockSpec((None, G, Q, DH_B), lambda h, n: (h, 0, qn(n), 0))
    bspec = pl.BlockSpec((None, G, 2, Q, Q), lambda h, n: (h, 0, 0, 0, 0))
    sspec = pl.BlockSpec((None, G, 1, LANES), lambda h, n: (h, 0, 0, 0))
    return _pcall(
        kernel, name=name, dims=("parallel", "arbitrary"),
        out_shape=(jax.ShapeDtypeStruct((HKV, G, T, DH_B), BF16), jax.ShapeDtypeStruct((HKV, T, DH_B), BF16),
                   jax.ShapeDtypeStruct((HKV, T, DH_B), BF16), jax.ShapeDtypeStruct((HKV, G, 2, Q, Q), F32),
                   jax.ShapeDtypeStruct((HKV, G, 1, LANES), F32)),
        grid=(HKV, N + 1),
        in_specs=[qspec, kv(prev), kv(cur), kv(prev), kv(cur), qspec, qspec,
                  pl.BlockSpec((None, G, Q, 1), lambda h, n: (h, 0, qn(n), 0)), bspec, sspec],
        out_specs=(qspec, kv(out_kv), kv(out_kv), bspec, sspec),
        scratch_shapes=[pltpu.VMEM((Q, DH_B), F32), pltpu.VMEM((Q, DH_B), F32)],
    )(q, k, k, v, v, o, do, lse, bias, sinks)


def _branch_mix(oa, ob, wa_t, wb_t, proj, gate_col, *, deps=(), name):
    T = oa.shape[0]
    D = wa_t.shape[0]
    bm, bn = _pick(T, 1024, 8), _pick(D, 512, LANES)
    assert gate_col % bn == 0
    ga0, gb0 = gate_col // bn, (gate_col + D) // bn

    def kernel(oa_ref, ob_ref, wa_ref, wb_ref, ga_ref, gb_ref, mix_ref, ya_ref, yb_ref):
        ya = _dot(oa_ref[...], wa_ref[...], "nt")
        yb = _dot(ob_ref[...], wb_ref[...], "nt")
        mix = _sigmoid(ga_ref[...].astype(F32)) * ya + _sigmoid(gb_ref[...].astype(F32)) * yb
        mix_ref[...] = mix.astype(BF16)
        ya_ref[...] = ya.astype(BF16)
        yb_ref[...] = yb.astype(BF16)

    out = pl.BlockSpec((bm, bn), lambda i, j: (i, j))
    return _pcall(
        kernel, name=name, dims=("parallel", "parallel"), deps=deps,
        out_shape=tuple(jax.ShapeDtypeStruct((T, D), BF16) for _ in range(3)),
        grid=(T // bm, D // bn),
        in_specs=[pl.BlockSpec((bm, oa.shape[1]), lambda i, j: (i, 0)),
                  pl.BlockSpec((bm, ob.shape[1]), lambda i, j: (i, 0)),
                  pl.BlockSpec((bn, oa.shape[1]), lambda i, j: (j, 0)),
                  pl.BlockSpec((bn, ob.shape[1]), lambda i, j: (j, 0)),
                  pl.BlockSpec((bm, bn), lambda i, j: (i, ga0 + j)),
                  pl.BlockSpec((bm, bn), lambda i, j: (i, gb0 + j))],
        out_specs=(out, out, out),
    )(oa, ob, wa_t, wb_t, proj, proj)


def _mix_bwd_tile(dmixed, ya, yb, ga, gb):
    sa = _sigmoid(ga.astype(F32))
    sb = _sigmoid(gb.astype(F32))
    return (dmixed * sa, dmixed * sb,
            dmixed * ya.astype(F32) * sa * (1.0 - sa), dmixed * yb.astype(F32) * sb * (1.0 - sb))


def _ffn_up(h, wg_t, wu_t, *, name):
    T, D = h.shape
    FP = wg_t.shape[0]
    bm, bn = _pick(T, 1024, 8), _pick(FP, 512, LANES)

    def kernel(h_ref, wg_ref, wu_ref, g_ref, u_ref, hid_ref):
        hv = h_ref[...]
        g = _dot(hv, wg_ref[...], "nt")
        u = _dot(hv, wu_ref[...], "nt")
        g_ref[...] = g.astype(BF16)
        u_ref[...] = u.astype(BF16)
        hid_ref[...] = (g * _sigmoid(g) * u).astype(BF16)

    out = pl.BlockSpec((bm, bn), lambda i, j: (i, j))
    wspec = pl.BlockSpec((bn, D), lambda i, j: (j, 0))
    return _pcall(
        kernel, name=name, dims=("parallel", "parallel"),
        out_shape=tuple(jax.ShapeDtypeStruct((T, FP), BF16) for _ in range(3)),
        grid=(T // bm, FP // bn),
        in_specs=[pl.BlockSpec((bm, D), lambda i, j: (i, 0)), wspec, wspec],
        out_specs=(out, out, out),
    )(h, wg_t, wu_t)


def _swiglu_bwd_tile(dhid, gate, up):
    g = gate.astype(F32)
    sg = _sigmoid(g)
    return dhid * up.astype(F32) * sg * (1.0 + g * (1.0 - sg)), dhid * g * sg


def _adamw_update(w, g, m, v):
    nm = ADAM_B1 * m + (1.0 - ADAM_B1) * g
    nv = ADAM_B2 * v + (1.0 - ADAM_B2) * (g * g)
    m_hat = nm / (1.0 - ADAM_B1 ** ADAM_STEP)
    v_hat = nv / (1.0 - ADAM_B2 ** ADAM_STEP)
    return -ADAM_LR * (m_hat / (jnp.sqrt(v_hat) + ADAM_EPS) + ADAM_WD * w), nm, nv


def _adamw(w, g, m, v, *, name):
    Rw, Cw = w.shape
    br = _pick(Rw, max(8, (1 << 19) // Cw // 8 * 8), 8)

    def kernel(w_ref, g_ref, m_ref, v_ref, d_ref, nm_ref, nv_ref):
        d_ref[...], nm_ref[...], nv_ref[...] = _adamw_update(w_ref[...], g_ref[...], m_ref[...], v_ref[...])

    blk = pl.BlockSpec((br, Cw), lambda i: (i, 0))
    return _pcall(
        kernel, name=name, dims=("parallel",),
        out_shape=tuple(jax.ShapeDtypeStruct((Rw, Cw), F32) for _ in range(3)),
        grid=(Rw // br,), in_specs=[blk, blk, blk, blk], out_specs=(blk, blk, blk),
    )(w, g, m, v)


def _sum_adamw(land, w, m, v, *, name):
    _, R, C = land.shape
    bc = _pick(C, max(LANES, (1 << 18) // R // LANES * LANES), LANES)

    def kernel(l_ref, w_ref, m_ref, v_ref, g_ref, d_ref, nm_ref, nv_ref):
        g = l_ref[0].astype(F32)
        for k in range(1, N_DEV):
            g = g + l_ref[k].astype(F32)
        g_ref[...] = g
        d_ref[...], nm_ref[...], nv_ref[...] = _adamw_update(w_ref[...], g, m_ref[...], v_ref[...])

    blk = pl.BlockSpec((R, bc), lambda i: (0, i))
    return _pcall(
        kernel, name=name, dims=("parallel",),
        out_shape=tuple(jax.ShapeDtypeStruct((R, C), F32) for _ in range(4)),
        grid=(C // bc,),
        in_specs=[pl.BlockSpec((N_DEV, R, bc), lambda i: (0, 0, i)), blk, blk, blk],
        out_specs=(blk, blk, blk, blk),
    )(land, w, m, v)


def _my_place():
    return lax.axis_index("x"), lax.axis_index("y"), lax.axis_index("c")


def _flip(v, bit):
    return 1 - v if bit else v


def _all_gather_rows(shard, *, pad_rows=0, name):
    R, C = shard.shape
    flat = R % BF16_TILE_ROWS == 0
    assert flat or pad_rows == 0
    out_shape = (N_DEV * R + pad_rows, C) if flat else (N_DEV, R, C)

    def body(*refs):
        if pad_rows:
            x_ref, z_ref, out_ref, send_sems, recv_sems, local_sems = refs
        else:
            x_ref, out_ref, send_sems, recv_sems, local_sems = refs
        x, y, c = _my_place()
        me, sibling = (x, y, c), (x, y, 1 - c)
        chips = [(1 - x, y), (x, 1 - y), (1 - x, 1 - y)]

        def block(px, py, pc):
            idx = 4 * px + 2 * py + pc
            if flat:
                return out_ref.at[pl.ds(pl.multiple_of(idx * R, BF16_TILE_ROWS), R), :]
            return out_ref.at[idx]

        def copy(k, blk, to, src=None):
            return pltpu.make_async_remote_copy(
                src_ref=block(*blk) if src is None else src, dst_ref=block(*blk),
                send_sem=send_sems.at[k], recv_sem=recv_sems.at[k], device_id=to, device_id_type=MESH)

        mine = pltpu.make_async_copy(x_ref, block(*me), local_sems.at[0])
        mine.start()
        if pad_rows:
            zero = pltpu.make_async_copy(z_ref, out_ref.at[pl.ds(N_DEV * R, pad_rows), :], local_sems.at[1])
            zero.start()
        first = [copy(0, me, sibling, src=x_ref)]
        first += [copy(1 + j, me, (*chip, c), src=x_ref) for j, chip in enumerate(chips)]
        for cp in first:
            cp.start()
        passed = [copy(4 + j, (*chip, c), sibling) for j, chip in enumerate(chips)]
        for j, chip in enumerate(chips):
            copy(1 + j, (*chip, c), me).wait_recv()
            passed[j].start()
        copy(0, sibling, me).wait_recv()
        for j, chip in enumerate(chips):
            copy(4 + j, (*chip, 1 - c), me).wait_recv()
        for cp in first + passed:
            cp.wait_send()
        mine.wait()
        if pad_rows:
            zero.wait()

    operands = [shard] + ([jnp.zeros((pad_rows, C), shard.dtype)] if pad_rows else [])
    return _comm_call(
        body, name=name,
        out_shape=jax.ShapeDtypeStruct(out_shape, shard.dtype),
        in_specs=[ANY] * len(operands), out_specs=ANY,
        scratch_shapes=[pltpu.SemaphoreType.DMA((7,)), pltpu.SemaphoreType.DMA((7,)), pltpu.SemaphoreType.DMA((2,))],
    )(*operands)


HBM_SPEC = pl.BlockSpec(memory_space=pltpu.HBM)
SEM_SPEC = pl.BlockSpec(memory_space=pltpu.SEMAPHORE)
SPLIT_EFFECT = pltpu.SideEffectType.DATAFLOW_SIDE_EFFECTING
N_PEERS = N_DEV - 1


def _hbm(a):
    return pltpu.with_memory_space_constraint(a, pltpu.HBM)


def _peer(x, y, c, k):
    return _flip(x, k & 4), _flip(y, k & 2), _flip(c, k & 1)


def _row_block(ref, idx, R):
    if len(ref.shape) == 3:
        return ref.at[idx]
    return ref.at[pl.ds(pl.multiple_of(idx * R, BF16_TILE_ROWS), R), :]


def _remote_pair(src, dst, arrives, send_sem, recv_sem, to):
    send = pltpu.make_async_remote_copy(src_ref=src, dst_ref=dst, send_sem=send_sem, recv_sem=recv_sem,
                                        device_id=to, device_id_type=MESH)
    recv = pltpu.make_async_remote_copy(src_ref=src, dst_ref=arrives, send_sem=send_sem, recv_sem=recv_sem,
                                        device_id=to, device_id_type=MESH)
    return send, recv


def _split_call(body, arrays, sems, n_through, out_sems, after, token, *, name):
    out_shape = [pltpu.SemaphoreType.DMA((n,)) for n in out_sems]
    out_shape += [pltpu.HBM(a.shape, a.dtype) for a in arrays[:n_through]]
    out_specs = [SEM_SPEC] * len(out_sems) + [HBM_SPEC] * n_through
    if token:
        out_shape.append(jax.ShapeDtypeStruct((8, LANES), F32))
        out_specs.append(pl.BlockSpec(memory_space=pltpu.VMEM))
    out = pl.pallas_call(
        body, name=name, out_shape=tuple(out_shape),
        in_specs=tuple([HBM_SPEC] * len(arrays) + [SEM_SPEC] * len(sems) + [ANY]),
        out_specs=tuple(out_specs),
        input_output_aliases={t: len(out_sems) + t for t in range(n_through)},
        compiler_params=pltpu.CompilerParams(has_side_effects=SPLIT_EFFECT),
    )(*[_hbm(a) for a in arrays], *sems, after)
    n_s = len(out_sems)
    return tuple(out[:n_s]), tuple(out[n_s:n_s + n_through]), (out[-1] if token else None)


def _scatter_copies(src_ref, land_ref, sems, R):
    send_sems, recv_sems, local_sem = sems
    x, y, c = _my_place()
    local = pltpu.make_async_copy(_row_block(src_ref, 4 * x + 2 * y + c, R), land_ref.at[0], local_sem.at[0])
    remote = []
    for k in range(1, N_DEV):
        px, py, pc = _peer(x, y, c, k)
        remote.append(_remote_pair(_row_block(src_ref, 4 * px + 2 * py + pc, R), land_ref.at[k], land_ref.at[k],
                                   send_sems.at[k - 1], recv_sems.at[k - 1], (px, py, pc)))
    return local, remote


def _scatter_start(g, R, after, *, name):
    land = lax.empty((N_DEV, R, g.shape[-1]), g.dtype)

    def body(src_ref, land_ref, after_ref, send_sems, recv_sems, local_sem, src_thru, land_thru, token):
        local, remote = _scatter_copies(src_ref, land_ref, (send_sems, recv_sems, local_sem), R)
        local.start()
        for send, _ in remote:
            send.start()
        token[...] = jnp.zeros_like(token)

    sems, (src_thru, land_thru), token = _split_call(body, [g, land], [], 2, (N_PEERS, N_PEERS, 1), after, True, name=name)
    return sems, src_thru, land_thru, token, R


def _scatter_finish(handle, after, *, name):
    sems, src_thru, land_thru, _, R = handle

    def body(src_ref, land_ref, send_sems, recv_sems, local_sem, after_ref, src_dead, got_ref):
        local, remote = _scatter_copies(src_ref, land_ref, (send_sems, recv_sems, local_sem), R)
        for send, recv in remote:
            send.wait_send()
            recv.wait_recv()
        local.wait()

    return _split_call(body, [src_thru, land_thru], sems, 2, (), after, False, name=name)[1][1]


def _gather_copies(src_ref, land_ref, sems, R):
    send_sems, recv_sems, local_sem = sems
    x, y, c = _my_place()
    mine = _row_block(land_ref, 4 * x + 2 * y + c, R)
    local = pltpu.make_async_copy(src_ref, mine, local_sem.at[0])
    peers = [(x, y, 1 - c), (1 - x, y, c), (x, 1 - y, c), (1 - x, 1 - y, c)]
    remote = [_remote_pair(src_ref, mine, _row_block(land_ref, 4 * px + 2 * py + pc, R),
                           send_sems.at[k], recv_sems.at[k], (px, py, pc)) for k, (px, py, pc) in enumerate(peers)]
    return local, remote


def _forward_copies(land_ref, sems, R):
    send_sems, recv_sems = sems
    x, y, c = _my_place()
    pairs = []
    for j, (px, py) in enumerate([(1 - x, y), (x, 1 - y), (1 - x, 1 - y)]):
        blk = _row_block(land_ref, 4 * px + 2 * py + c, R)
        pairs.append(_remote_pair(blk, blk, _row_block(land_ref, 4 * px + 2 * py + (1 - c), R),
                                  send_sems.at[j], recv_sems.at[j], (x, y, 1 - c)))
    return pairs


def _gather_start(shard, after, *, pad_rows=0, name):
    R, C = shard.shape
    assert R % BF16_TILE_ROWS == 0
    land = lax.empty((N_DEV * R + pad_rows, C), shard.dtype)
    if pad_rows:
        land = lax.dynamic_update_slice(land, jnp.zeros((pad_rows, C), shard.dtype), (N_DEV * R, 0))

    def body(src_ref, land_ref, after_ref, send_sems, recv_sems, local_sem, src_thru, land_thru, token):
        local, remote = _gather_copies(src_ref, land_ref, (send_sems, recv_sems, local_sem), R)
        local.start()
        for send, _ in remote:
            send.start()
        token[...] = jnp.zeros_like(token)

    sems, (src_thru, land_thru), token = _split_call(body, [shard, land], [], 2, (4, 4, 1), after, True, name=name)
    return sems, src_thru, land_thru, token, R


def _gather_forward(handle, after, *, name):
    sems, src_thru, land_thru, _, R = handle

    def arrived(land_ref, src_ref, send_sems, recv_sems, local_sem, after_ref, land_out):
        _, remote = _gather_copies(src_ref, land_ref, (send_sems, recv_sems, local_sem), R)
        for _, recv in remote[1:]:
            recv.wait_recv()

    def pass_on(land_ref, after_ref, send_sems, recv_sems, land_out, token):
        for send, _ in _forward_copies(land_ref, (send_sems, recv_sems), R):
            send.start()
        token[...] = jnp.zeros_like(token)

    _, (land1,), _ = _split_call(arrived, [land_thru, src_thru], sems, 1, (), after, False, name=name + "_arrived")
    sems2, (land2,), token = _split_call(pass_on, [land1], [], 1, (3, 3), src_thru, True, name=name + "_pass_on")
    return sems, sems2, src_thru, land2, token, R


def _gather_finish(handle2, after, *, name):
    sems, sems2, src_thru, land_thru, _, R = handle2

    def body(land_ref, src_ref, send_sems, recv_sems, local_sem, send2, recv2, after_ref, land_out):
        local, remote = _gather_copies(src_ref, land_ref, (send_sems, recv_sems, local_sem), R)
        for send, _ in remote:
            send.wait_send()
        remote[0][1].wait_recv()
        local.wait()
        for send, recv in _forward_copies(land_ref, (send2, recv2), R):
            send.wait_send()
            recv.wait_recv()

    return _split_call(body, [land_thru, src_thru], [*sems, *sems2], 1, (), after, False, name=name)[1][0]


def _all_reduce_small(v, *, name):
    rows, C = v.shape

    def body(v_ref, o_ref, land_ref, send_sems, recv_sems):
        x, y, c = _my_place()
        me = 4 * x + 2 * y + c
        copies = []
        for k in range(1, N_DEV):
            peer = (_flip(x, k & 4), _flip(y, k & 2), _flip(c, k & 1))
            copies.append(pltpu.make_async_remote_copy(
                src_ref=v_ref, dst_ref=land_ref.at[me],
                send_sem=send_sems.at[k - 1], recv_sem=recv_sems.at[k - 1], device_id=peer, device_id_type=MESH))
        for cp in copies:
            cp.start()
        land_ref[me] = v_ref[...]
        for k in range(1, N_DEV):
            peer_idx = 4 * _flip(x, k & 4) + 2 * _flip(y, k & 2) + _flip(c, k & 1)
            pltpu.make_async_remote_copy(
                src_ref=v_ref, dst_ref=land_ref.at[peer_idx],
                send_sem=send_sems.at[k - 1], recv_sem=recv_sems.at[k - 1],
                device_id=(x, y, c), device_id_type=MESH).wait_recv()
        for cp in copies:
            cp.wait_send()
        acc = land_ref[0]
        for s in range(1, N_DEV):
            acc = acc + land_ref[s]
        o_ref[...] = acc

    vm = pl.BlockSpec(memory_space=pltpu.VMEM)
    return _comm_call(
        body, name=name,
        out_shape=jax.ShapeDtypeStruct((rows, C), F32),
        in_specs=[vm], out_specs=vm,
        scratch_shapes=[pltpu.VMEM((N_DEV, rows, C), F32),
                        pltpu.SemaphoreType.DMA((7,)), pltpu.SemaphoreType.DMA((7,))],
    )(v)


def kernel(x, norm1_g, w_in, b_forget, attn_sinks, rel_bias, w_branch_a, w_branch_b, w_out, norm2_g, w_ffn_gate, w_ffn_up, w_ffn_down, final_g, loss_target, m_norm1_g, m_w_in, m_b_forget, m_attn_sinks, m_rel_bias, m_w_branch_a, m_w_branch_b, m_w_out, m_norm2_g, m_w_ffn_gate, m_w_ffn_up, m_w_ffn_down, m_final_g, v_norm1_g, v_w_in, v_b_forget, v_attn_sinks, v_rel_bias, v_w_branch_a, v_w_branch_b, v_w_out, v_norm2_g, v_w_ffn_gate, v_w_ffn_up, v_w_ffn_down, v_final_g):
    xs = x[0]
    T, D = xs.shape
    H_A, H_B = b_forget.shape[-1], attn_sinks.shape[-1]
    WA, QB = H_A * DH_A, H_B * DH_B
    R_IN = w_in.shape[-1]
    W_IN = N_DEV * R_IN
    KB = (W_IN - 3 * WA - H_A - QB - 2 * D) // 2
    HKV = KB // DH_B
    G = H_B // HKV
    N_BIG = W_IN - H_A
    GATE_COL = 3 * WA + QB + 2 * KB
    R_F = w_ffn_gate.shape[-1]
    F = N_DEV * R_F
    FP = _round_up(F, 512)
    assert H_A <= LANES and T % SWA_BLOCK == 0

    win_s = w_in[0].T.astype(BF16)
    wa_s = w_branch_a[0].T.astype(BF16)
    wb_s = w_branch_b[0].T.astype(BF16)
    wout_s = w_out[0].astype(BF16)
    wg_s = w_ffn_gate[0].T.astype(BF16)
    wu_s = w_ffn_up[0].T.astype(BF16)
    wd_s = w_ffn_down[0].astype(BF16)

    win_all = _all_gather_rows(win_s, name="ag_w_in").reshape(W_IN, D)
    w_big = jnp.concatenate([win_all[:3 * WA], win_all[3 * WA + H_A:]], axis=0)
    w_f = jnp.pad(win_all[3 * WA:3 * WA + H_A], ((0, LANES - H_A), (0, 0)))
    ag_wa = _gather_start(wa_s, win_all, name="ag_start_w_branch_a")
    ag_wb = _gather_start(wb_s, ag_wa[3], name="ag_start_w_branch_b")
    ag_wout = _gather_start(wout_s, ag_wb[3], name="ag_start_w_out")
    ag_wg = _gather_start(wg_s, ag_wout[3], pad_rows=FP - F, name="ag_start_w_ffn_gate")
    ag_wu = _gather_start(wu_s, ag_wg[3], pad_rows=FP - F, name="ag_start_w_ffn_up")
    ag_wd = _gather_start(wd_s, ag_wu[3], pad_rows=FP - F, name="ag_start_w_ffn_down")

    BM = 1024
    bn_big = _pick(N_BIG, 768, LANES)

    h1, rstd1 = _rms_fwd(xs, norm1_g, deps=[ag_wd[3]], name="rms1_fwd")
    proj = _matmul([(h1, w_big)], "nt", bm=BM, bn=bn_big, name="proj_fwd")
    f_logit = _matmul([(h1, w_f)], "nt", bm=BM, bn=LANES, out_dtype=F32, name="forget_fwd")
    b_pad = jnp.pad(b_forget, ((0, 0), (0, LANES - H_A)))
    c_all, sneg = _fox_gate_fwd(f_logit, b_pad, name="fox_gate_fwd")
    c_heads = c_all[:, :H_A].T
    c_col, c_row = c_heads[:, :, None], c_heads[:, None, :]
    fw_wa = _gather_forward(ag_wa, proj, name="ag_w_branch_a")
    fw_wb = _gather_forward(ag_wb, fw_wa[4], name="ag_w_branch_b")
    fw_wout = _gather_forward(ag_wout, fw_wb[4], name="ag_w_out")
    oa, lse_a = _fox_fwd(proj, c_col, c_row, H_A, deps=[fw_wout[4]], name="fox_fwd")
    fw_wg = _gather_forward(ag_wg, oa, name="ag_w_ffn_gate")
    fw_wu = _gather_forward(ag_wu, fw_wg[4], name="ag_w_ffn_up")

    def heads_q(a):
        return a.reshape(T, HKV, G, DH_B).transpose(1, 2, 0, 3)

    def heads_kv(a):
        return a.reshape(T, HKV, DH_B).transpose(1, 0, 2)

    qb = heads_q(proj[:, 3 * WA:3 * WA + QB])
    kb = heads_kv(proj[:, 3 * WA + QB:3 * WA + QB + KB])
    vb = heads_kv(proj[:, 3 * WA + QB + KB:GATE_COL])
    onehot = _bucket_onehot()
    bias = _bias_table(rel_bias, onehot, name="rel_bias_table").reshape(HKV, G, 2, SWA_BLOCK, SWA_BLOCK)
    sinks_b = jnp.broadcast_to(attn_sinks.reshape(HKV, G, 1, 1), (HKV, G, 1, LANES))
    ob4, lse_b = _swa_fwd(qb, kb, vb, bias, sinks_b, deps=[fw_wu[4]], name="swa_fwd")
    ob = ob4.transpose(2, 0, 1, 3).reshape(T, QB)
    fw_wd = _gather_forward(ag_wd, ob, name="ag_w_ffn_down")

    wa_t = _gather_finish(fw_wa, fw_wd[4], name="ag_wait_w_branch_a")
    wb_t = _gather_finish(fw_wb, wa_t, name="ag_wait_w_branch_b")
    mixed, ya, yb = _branch_mix(oa, ob, wa_t, wb_t, proj, GATE_COL, name="branch_mix")
    wout = _gather_finish(fw_wout, mixed, name="ag_wait_w_out")
    x1 = _matmul([(mixed, wout)], "nn", bm=BM, bn=512, out_dtype=F32, addend=xs, name="out_proj_fwd")

    h2, rstd2 = _rms_fwd(x1, norm2_g, name="rms2_fwd")
    wg_t = _gather_finish(fw_wg, h2, name="ag_wait_w_ffn_gate")
    wu_t = _gather_finish(fw_wu, wg_t, name="ag_wait_w_ffn_up")
    gate, up, hidden = _ffn_up(h2, wg_t, wu_t, name="ffn_up")
    wd = _gather_finish(fw_wd, hidden, name="ag_wait_w_ffn_down")
    tk_f = _pick(FP, 2816, LANES)
    x2 = _matmul([(hidden, wd)], "nn", bm=BM, bn=1024, tk=tk_f, out_dtype=F32, addend=x1, name="ffn_down_fwd")

    dx2, loss_part, g_final = _final_loss(x2, final_g.reshape(1, D), loss_target[0], name="final_loss")
    loss = lax.psum(loss_part[0, 0], MESH_AXES)

    dx2_b = dx2.astype(BF16)
    g_wd = _matmul([(hidden, dx2_b)], "tn", bm=1024, bn=1024, name="ffn_down_bwd_w")
    rs_wd = _scatter_start(g_wd, R_F, loss_part, name="rs_start_w_ffn_down")
    dgate, dup = _matmul([(dx2_b, wd)], "nt", bm=BM, bn=512, extras=[(gate, 0), (up, 0)], epilogue=_swiglu_bwd_tile,
                         n_out=2, deps=[rs_wd[3]], name="ffn_down_bwd_x")
    g_wg = _matmul([(dgate, h2)], "tn", bm=1024, bn=1024, name="ffn_gate_bwd_w")
    rs_wg = _scatter_start(g_wg, R_F, rs_wd[3], name="rs_start_w_ffn_gate")
    g_wu = _matmul([(dup, h2)], "tn", bm=1024, bn=1024, deps=[rs_wg[3]], name="ffn_up_bwd_w")
    rs_wu = _scatter_start(g_wu, R_F, rs_wg[3], name="rs_start_w_ffn_up")
    dh2 = _matmul([(dgate, wg_t), (dup, wu_t)], "nn", bm=BM, bn=1024, tk=_pick(FP, 1408, LANES), out_dtype=F32,
                  deps=[rs_wu[3]], name="ffn_up_bwd_x")
    dx1, g_norm2 = _rms_bwd(dx2, dh2, x1, rstd2, norm2_g, name="rms2_bwd")

    dx1_b = dx1.astype(BF16)
    g_wout = _matmul([(mixed, dx1_b)], "tn", bm=1024, bn=1024, name="out_proj_bwd_w")
    rs_wout = _scatter_start(g_wout, D // N_DEV, rs_wu[3], name="rs_start_w_out")
    bn_mix = _pick(D, 256, LANES)
    assert GATE_COL % bn_mix == 0
    dya, dyb, dga, dgb = _matmul(
        [(dx1_b, wout)], "nt", bm=BM, bn=bn_mix, epilogue=_mix_bwd_tile, n_out=4, deps=[rs_wout[3]],
        extras=[(ya, 0), (yb, 0), (proj, GATE_COL // bn_mix), (proj, (GATE_COL + D) // bn_mix)], name="out_proj_bwd_x")
    g_wa = _matmul([(dya, oa)], "tn", bm=1024, bn=1024, name="branch_a_bwd_w")
    rs_wa = _scatter_start(g_wa, D // N_DEV, rs_wout[3], name="rs_start_w_branch_a")
    g_wb = _matmul([(dyb, ob)], "tn", bm=1024, bn=1024, deps=[rs_wa[3]], name="branch_b_bwd_w")
    rs_wb = _scatter_start(g_wb, D // N_DEV, rs_wa[3], name="rs_start_w_branch_b")
    doa = _matmul([(dya, wa_t)], "nn", bm=BM, bn=1024, deps=[rs_wb[3]], name="branch_a_bwd_x")
    dob = _matmul([(dyb, wb_t)], "nn", bm=BM, bn=1024, name="branch_b_bwd_x")

    delta_a = _rowdot(doa, oa, H_A, name="fox_delta")
    dqa, dka, dva, dc_key, dc_query = _fox_bwd(proj, doa, lse_a, delta_a, c_col, c_row, H_A, name="fox_bwd")
    dc = jnp.pad((dc_key[:, 0, :] + dc_query[:, :, 0]).T, ((0, 0), (0, LANES - H_A)))
    df, g_bf = _fox_gate_bwd(dc, sneg, name="fox_gate_bwd")

    dqb4, dkb3, dvb3, dbias, dsinks = _swa_bwd(qb, kb, vb, ob4, heads_q(dob), lse_b, bias, sinks_b, name="swa_bwd")
    g_relb = _bias_table_bwd(dbias.reshape(H_B, -1), onehot, name="rel_bias_table_bwd")
    dqb = dqb4.transpose(2, 0, 1, 3).reshape(T, QB)
    dkb = dkb3.transpose(1, 0, 2).reshape(T, KB)
    dvb = dvb3.transpose(1, 0, 2).reshape(T, KB)

    dproj = jnp.concatenate([dqa.astype(BF16), dka, dva, dqb, dkb, dvb, dga, dgb], axis=1)
    g_wbig = _matmul([(dproj, h1)], "tn", bm=bn_big, bn=1024, name="proj_bwd_w")
    g_wf = _matmul([(df, h1)], "tn", bm=LANES, bn=1024, name="forget_bwd_w")
    g_win = jnp.concatenate([g_wbig[:3 * WA], g_wf[:H_A], g_wbig[3 * WA:]], axis=0).reshape(N_DEV, R_IN, D)
    rs_win = _scatter_start(g_win, R_IN, rs_wb[3], name="rs_start_w_in")
    dh_f = _matmul([(df, w_f)], "nn", bm=BM, bn=1024, out_dtype=F32, deps=[rs_win[3]], name="forget_bwd_x")
    tk_big = _pick(N_BIG, 2816, LANES)
    dh1 = _matmul([(dproj, w_big)], "nn", bm=BM, bn=1024, tk=tk_big, out_dtype=F32, addend=dh_f, name="proj_bwd_x")
    grad_x, g_norm1 = _rms_bwd(dx1, dh1, xs, rstd1, norm1_g, name="rms1_bwd")

    def update(handle, after, w, m, v, transposed, nm):
        take = (lambda a: a[0].T) if transposed else (lambda a: a[0])
        give = (lambda a: a.T[None]) if transposed else (lambda a: a[None])
        land = _scatter_finish(handle, after, name="rs_wait_" + nm)
        return tuple(give(o) for o in _sum_adamw(land, take(w), take(m), take(v), name="adamw_" + nm))

    big = {}
    big["w_ffn_down"] = update(rs_wd, grad_x, w_ffn_down, m_w_ffn_down, v_w_ffn_down, False, "w_ffn_down")
    big["w_ffn_gate"] = update(rs_wg, big["w_ffn_down"][1], w_ffn_gate, m_w_ffn_gate, v_w_ffn_gate, True, "w_ffn_gate")
    big["w_ffn_up"] = update(rs_wu, big["w_ffn_gate"][1], w_ffn_up, m_w_ffn_up, v_w_ffn_up, True, "w_ffn_up")
    big["w_out"] = update(rs_wout, big["w_ffn_up"][1], w_out, m_w_out, v_w_out, False, "w_out")
    big["w_branch_a"] = update(rs_wa, big["w_out"][1], w_branch_a, m_w_branch_a, v_w_branch_a, True, "w_branch_a")
    big["w_branch_b"] = update(rs_wb, big["w_branch_a"][1], w_branch_b, m_w_branch_b, v_w_branch_b, True, "w_branch_b")
    big["w_in"] = update(rs_win, big["w_branch_b"][1], w_in, m_w_in, v_w_in, True, "w_in")

    n_rb = NUM_BUCKETS * H_B
    assert D >= n_rb and D >= H_A + H_B
    small = jnp.concatenate([
        g_norm1, g_norm2, g_final,
        jnp.pad(jnp.concatenate([g_bf[:, :H_A], dsinks[:, :, 0, 0].reshape(1, H_B)], axis=1), ((0, 0), (0, D - H_A - H_B))),
        jnp.pad(g_relb.reshape(1, n_rb), ((0, 0), (0, D - n_rb))),
        jnp.zeros((3, D), F32)], axis=0)
    small = _all_reduce_small(small, name="ar_small")
    gs = {
        "norm1_g": small[0:1], "norm2_g": small[1:2], "final_g": small[2],
        "b_forget": small[3:4, :H_A], "attn_sinks": small[3:4, H_A:H_A + H_B],
        "rel_bias": small[4, :n_rb].reshape(NUM_BUCKETS, H_B),
    }

    def adam_small(ws, gsm, ms, vs):
        def pack(parts):
            rows = [jnp.pad(p.reshape(1, -1), ((0, 0), (0, D - p.size))) for p in parts]
            return jnp.concatenate(rows + [jnp.ones((8 - len(rows), D), F32)], axis=0)
        d, nm_, nv_ = _adamw(pack(ws), pack(gsm), pack(ms), pack(vs), name="adamw_small")
        unpack = lambda a: [a[i, :p.size].reshape(p.shape) for i, p in enumerate(ws)]
        return unpack(d), unpack(nm_), unpack(nv_)

    small_names = ["norm1_g", "b_forget", "attn_sinks", "rel_bias", "norm2_g", "final_g"]
    small_w = [norm1_g, b_forget, attn_sinks, rel_bias, norm2_g, final_g]
    small_g = [gs[n].reshape(w.shape) for n, w in zip(small_names, small_w)]
    small_m = [m_norm1_g, m_b_forget, m_attn_sinks, m_rel_bias, m_norm2_g, m_final_g]
    small_v = [v_norm1_g, v_b_forget, v_attn_sinks, v_rel_bias, v_norm2_g, v_final_g]
    sd, sm, sv = adam_small(small_w, small_g, small_m, small_v)
    for i, n in enumerate(small_names):
        big[n] = (small_g[i], sd[i], sm[i], sv[i])

    order = ["norm1_g", "w_in", "b_forget", "attn_sinks", "rel_bias", "w_branch_a", "w_branch_b", "w_out",
             "norm2_g", "w_ffn_gate", "w_ffn_up", "w_ffn_down", "final_g"]
    grads = [big[n][0] for n in order]
    deltas = [big[n][1] for n in order]
    new_m = [big[n][2] for n in order]
    new_v = [big[n][3] for n in order]
    return (loss, grad_x[None], *grads, *deltas, *new_m, *new_v)
```

```python
import math

import numpy as np
import jax
import jax.numpy as jnp
from jax import lax
from jax.experimental import pallas as pl
from jax.experimental.pallas import tpu as pltpu

N_DEV = 8
MESH_AXES = ("x", "y", "c")
DH_A = 128
DH_B = 64
SWA_BLOCK = 128
FOX_TILE = 1024
NUM_BUCKETS = 32
MAX_DISTANCE = 128
EPS = 1e-6
ADAM_LR = 0.001
ADAM_B1 = 0.9
ADAM_B2 = 0.999
ADAM_EPS = 1e-08
ADAM_WD = 0.01
ADAM_STEP = 10
NEG = -1e30
LANES = 128
BF16_TILE_ROWS = 16
V7X_VMEM_LIMIT = 56 * 1024 * 1024
F32 = jnp.float32
BF16 = jnp.bfloat16
MESH = pl.DeviceIdType.MESH
ANY = pl.BlockSpec(memory_space=pl.ANY)


def _round_up(n, m):
    return (n + m - 1) // m * m


def _pick(n, target, mult):
    best = None
    for d in range(mult, min(n, target) + 1, mult):
        if n % d == 0:
            best = d
    return n if best is None else best


def _pcall(kernel, *, name, in_specs, dims=None, deps=(), **kw):
    deps = tuple(deps)
    if deps:
        inner, n_in, n_dep = kernel, len(in_specs), len(deps)

        def kernel(*refs):
            inner(*refs[:n_in], *refs[n_in + n_dep:])

        in_specs = list(in_specs) + [ANY] * n_dep
    call = pl.pallas_call(
        kernel, name=name, in_specs=in_specs,
        compiler_params=pltpu.CompilerParams(dimension_semantics=dims, vmem_limit_bytes=V7X_VMEM_LIMIT),
        **kw)
    return lambda *operands: call(*operands, *deps)


def _comm_call(kernel, *, name, **kw):
    return pl.pallas_call(kernel, name=name, **kw)


def _sigmoid(v):
    return 1.0 / (1.0 + jnp.exp(-v))


_DIMS = {"nn": (((1,), (0,)), ((), ())), "nt": (((1,), (1,)), ((), ())), "tn": (((0,), (0,)), ((), ()))}


def _dot(a, b, form="nn"):
    return lax.dot_general(a, b, _DIMS[form], preferred_element_type=F32)


def _matmul(pairs, form, *, bm, bn, tk=None, out_dtype=BF16, addend=None, extras=(), epilogue=None, n_out=1, deps=(),
            b_hole=None, out_hole=None, name):
    a0, b0 = pairs[0]
    if form == "tn":
        K, M = a0.shape
    else:
        M, K = a0.shape
    N = b0.shape[0] if form == "nt" else b0.shape[1]
    if b_hole is not None and form == "nt":
        N -= b_hole[1]
    tk = K if tk is None else tk
    bm, bn = min(bm, M), min(bn, N)
    assert M % bm == 0 and N % bn == 0 and K % tk == 0, (name, M, N, K, bm, bn, tk)
    nk = K // tk
    n_pairs = len(pairs)
    has_add = addend is not None

    def stored_row(first, hole):
        return pl.multiple_of(first + jnp.where(first >= hole[0], hole[1], 0), BF16_TILE_ROWS)

    if form == "tn":
        a_spec = pl.BlockSpec((tk, bm), lambda i, j, k: (k, i))
    else:
        a_spec = pl.BlockSpec((bm, tk), lambda i, j, k: (i, k))
    if b_hole is not None:
        assert len(pairs) == 1 and b_hole[0] % (bn if form == "nt" else tk) == 0 and b_hole[1] % BF16_TILE_ROWS == 0
    if form == "nt":
        if b_hole is None:
            b_spec = pl.BlockSpec((bn, tk), lambda i, j, k: (j, k))
        else:
            b_spec = pl.BlockSpec((pl.Element(bn), pl.Element(tk)),
                                  lambda i, j, k: (stored_row(j * bn, b_hole), pl.multiple_of(k * tk, LANES)))
    elif b_hole is None or form == "tn":
        assert b_hole is None
        b_spec = pl.BlockSpec((tk, bn), lambda i, j, k: (k, j))
    else:
        b_spec = pl.BlockSpec((pl.Element(tk), pl.Element(bn)),
                              lambda i, j, k: (stored_row(k * tk, b_hole), pl.multiple_of(j * bn, LANES)))
    if out_hole is None:
        o_spec = pl.BlockSpec((bm, bn), lambda i, j, k: (i, j))
    else:
        assert out_hole[0] % bm == 0 and out_hole[1] % BF16_TILE_ROWS == 0 and epilogue is None and not has_add
        o_spec = pl.BlockSpec((pl.Element(bm), pl.Element(bn)),
                              lambda i, j, k: (stored_row(i * bm, out_hole), pl.multiple_of(j * bn, LANES)))
    m_stored = M + (out_hole[1] if out_hole is not None else 0)
    n_extra = len(extras)
    n_in = 2 * n_pairs + has_add + n_extra

    def kernel(*refs):
        ab = refs[:2 * n_pairs]
        c_ref = refs[2 * n_pairs] if has_add else None
        extra_refs = refs[2 * n_pairs + has_add:n_in]
        o_refs = refs[n_in:n_in + n_out]

        def partial_sum():
            acc = _dot(ab[0][...], ab[1][...], form)
            for p in range(1, n_pairs):
                acc = acc + _dot(ab[2 * p][...], ab[2 * p + 1][...], form)
            return acc

        def finish(acc):
            if has_add:
                acc = acc + c_ref[...].astype(F32)
            outs = (acc,) if epilogue is None else epilogue(acc, *[r[...] for r in extra_refs])
            for o_ref, val in zip(o_refs, outs, strict=True):
                o_ref[...] = val.astype(o_ref.dtype)

        if nk == 1:
            finish(partial_sum())
        else:
            acc_ref = refs[-1]
            k = pl.program_id(2)

            @pl.when(k == 0)
            def _():
                acc_ref[...] = jnp.zeros_like(acc_ref)

            acc_ref[...] += partial_sum()

            @pl.when(k == nk - 1)
            def _():
                finish(acc_ref[...])

    operands, in_specs = [], []
    for a, b in pairs:
        operands += [a, b]
        in_specs += [a_spec, b_spec]
    if has_add:
        operands.append(addend)
        in_specs.append(o_spec)
    for arr, first in extras:
        operands.append(arr)
        in_specs.append(pl.BlockSpec((bm, bn), lambda i, j, k, first=first: (i, first + j)))
    out_shape = tuple(jax.ShapeDtypeStruct((m_stored, N), out_dtype) for _ in range(n_out))
    out = _pcall(
        kernel, name=name, dims=("parallel", "parallel", "arbitrary"), deps=deps,
        out_shape=out_shape, grid=(M // bm, N // bn, nk), in_specs=in_specs, out_specs=(o_spec,) * n_out,
        scratch_shapes=[pltpu.VMEM((bm, bn), F32)] if nk > 1 else [],
    )(*operands)
    return out[0] if epilogue is None else out


def _rms_fwd(x, g, *, deps=(), name):
    T, D = x.shape
    br = _pick(T, 256, 8)

    def kernel(x_ref, g_ref, h_ref, r_ref):
        xv = x_ref[...]
        rstd = lax.rsqrt(jnp.mean(xv * xv, axis=-1, keepdims=True) + EPS)
        h_ref[...] = (xv * rstd * g_ref[...]).astype(BF16)
        r_ref[...] = rstd

    return _pcall(
        kernel, name=name, dims=("parallel",), deps=deps,
        out_shape=(jax.ShapeDtypeStruct((T, D), BF16), jax.ShapeDtypeStruct((T, 1), F32)),
        grid=(T // br,),
        in_specs=[pl.BlockSpec((br, D), lambda i: (i, 0)), pl.BlockSpec((1, D), lambda i: (0, 0))],
        out_specs=(pl.BlockSpec((br, D), lambda i: (i, 0)), pl.BlockSpec((br, 1), lambda i: (i, 0))),
    )(x, g)


def _rms_bwd(dres, dh, x, rstd, g, *, deps=(), name):
    T, D = x.shape
    br = _pick(T, 256, 8)

    def kernel(dres_ref, dh_ref, x_ref, r_ref, g_ref, dx_ref, gg_ref):
        @pl.when(pl.program_id(0) == 0)
        def _():
            gg_ref[...] = jnp.zeros_like(gg_ref)

        rstd_v = r_ref[...]
        xhat = x_ref[...] * rstd_v
        dhv = dh_ref[...].astype(F32)
        dxhat = dhv * g_ref[...]
        proj = jnp.mean(dxhat * xhat, axis=-1, keepdims=True)
        dx_ref[...] = dres_ref[...] + rstd_v * (dxhat - xhat * proj)
        gg_ref[...] += jnp.sum(dhv * xhat, axis=0, keepdims=True)

    row = pl.BlockSpec((br, D), lambda i: (i, 0))
    vec = pl.BlockSpec((1, D), lambda i: (0, 0))
    return _pcall(
        kernel, name=name, dims=("arbitrary",), deps=deps,
        out_shape=(jax.ShapeDtypeStruct((T, D), F32), jax.ShapeDtypeStruct((1, D), F32)),
        grid=(T // br,),
        in_specs=[row, row, row, pl.BlockSpec((br, 1), lambda i: (i, 0)), vec],
        out_specs=(row, vec),
    )(dres, dh, x, rstd, g)


def _final_loss(x2, gf, target, *, name):
    T, D = x2.shape
    br = _pick(T, 256, 8)

    def kernel(x_ref, g_ref, t_ref, dx_ref, loss_ref, gg_ref):
        @pl.when(pl.program_id(0) == 0)
        def _():
            gg_ref[...] = jnp.zeros_like(gg_ref)
            loss_ref[...] = jnp.zeros_like(loss_ref)

        xv = x_ref[...]
        gv = g_ref[...]
        rstd = lax.rsqrt(jnp.mean(xv * xv, axis=-1, keepdims=True) + EPS)
        xhat = xv * rstd
        err = xhat * gv - t_ref[...]
        loss_ref[...] += 0.5 * jnp.sum(jnp.mean(err * err, axis=-1, keepdims=True), axis=0, keepdims=True)
        dy = err * (1.0 / D)
        dxhat = dy * gv
        proj = jnp.mean(dxhat * xhat, axis=-1, keepdims=True)
        dx_ref[...] = rstd * (dxhat - xhat * proj)
        gg_ref[...] += jnp.sum(dy * xhat, axis=0, keepdims=True)

    row = pl.BlockSpec((br, D), lambda i: (i, 0))
    vec = pl.BlockSpec((1, D), lambda i: (0, 0))
    return _pcall(
        kernel, name=name, dims=("arbitrary",),
        out_shape=(jax.ShapeDtypeStruct((T, D), F32), jax.ShapeDtypeStruct((1, 1), F32),
                   jax.ShapeDtypeStruct((1, D), F32)),
        grid=(T // br,),
        in_specs=[row, vec, row],
        out_specs=(row, pl.BlockSpec((1, 1), lambda i: (0, 0)), vec),
    )(x2, gf, target)


def _scan_rows(v, reverse):
    n = v.shape[0]
    row = lax.broadcasted_iota(jnp.int32, v.shape, 0)
    s = 1
    while s < n:
        if reverse:
            v = v + jnp.where(row < n - s, pltpu.roll(v, n - s, 0), 0.0)
        else:
            v = v + jnp.where(row >= s, pltpu.roll(v, s, 0), 0.0)
        s *= 2
    return v


def _fox_gate_fwd(f, bias, *, name):
    T = f.shape[0]
    tb = _pick(T, 512, 8)

    def kernel(f_ref, b_ref, c_ref, s_ref, carry):
        @pl.when(pl.program_id(0) == 0)
        def _():
            carry[...] = jnp.zeros_like(carry)

        fa = f_ref[...] + b_ref[...]
        z = jnp.exp(-jnp.abs(fa))
        log_f = jnp.minimum(fa, 0.0) - jnp.log(1.0 + z)
        s_ref[...] = jnp.where(fa >= 0, z, 1.0) / (1.0 + z)
        c = _scan_rows(log_f, False) + carry[...]
        c_ref[...] = c
        carry[...] = c[tb - 1:tb, :]

    blk = pl.BlockSpec((tb, LANES), lambda i: (i, 0))
    return _pcall(
        kernel, name=name, dims=("arbitrary",),
        out_shape=(jax.ShapeDtypeStruct((T, LANES), F32), jax.ShapeDtypeStruct((T, LANES), F32)),
        grid=(T // tb,),
        in_specs=[blk, pl.BlockSpec((1, LANES), lambda i: (0, 0))],
        out_specs=(blk, blk),
        scratch_shapes=[pltpu.VMEM((1, LANES), F32)],
    )(f, bias)


def _fox_gate_bwd(dc, sneg, *, name):
    T = dc.shape[0]
    tb = _pick(T, 512, 8)
    nb = T // tb

    def kernel(dc_ref, s_ref, df_ref, gb_ref, carry):
        @pl.when(pl.program_id(0) == 0)
        def _():
            carry[...] = jnp.zeros_like(carry)
            gb_ref[...] = jnp.zeros_like(gb_ref)

        dlog = _scan_rows(dc_ref[...], True) + carry[...]
        carry[...] = dlog[0:1, :]
        dfa = dlog * s_ref[...]
        df_ref[...] = dfa.astype(BF16)
        gb_ref[...] += jnp.sum(dfa, axis=0, keepdims=True)

    blk = pl.BlockSpec((tb, LANES), lambda i: (nb - 1 - i, 0))
    return _pcall(
        kernel, name=name, dims=("arbitrary",),
        out_shape=(jax.ShapeDtypeStruct((T, LANES), BF16), jax.ShapeDtypeStruct((1, LANES), F32)),
        grid=(nb,),
        in_specs=[blk, blk],
        out_specs=(blk, pl.BlockSpec((1, LANES), lambda i: (0, 0))),
        scratch_shapes=[pltpu.VMEM((1, LANES), F32)],
    )(dc, sneg)


def _fox_scores(q, k, cq, ck, diagonal):
    s = _dot(q, k, "nt") * (DH_A ** -0.5) + cq - ck
    if diagonal:
        row = lax.broadcasted_iota(jnp.int32, s.shape, 0)
        col = lax.broadcasted_iota(jnp.int32, s.shape, 1)
        s = jnp.where(col <= row, s, NEG)
    return s


def _fox_fwd(proj, c_col, c_row, n_heads, *, deps=(), name):
    T = proj.shape[0]
    H = n_heads
    tq = tk = _pick(T, FOX_TILE, 128)
    nq = T // tq

    def kernel(q_ref, k_ref, v_ref, cq_ref, ck_ref, o_ref, lse_ref, m_sc, l_sc, acc_sc):
        i, j = pl.program_id(1), pl.program_id(2)

        @pl.when(j == 0)
        def _():
            m_sc[...] = jnp.full_like(m_sc, NEG)
            l_sc[...] = jnp.zeros_like(l_sc)
            acc_sc[...] = jnp.zeros_like(acc_sc)

        def tile(diagonal):
            s = _fox_scores(q_ref[...], k_ref[...], cq_ref[...], ck_ref[...], diagonal)
            m_prev = m_sc[...]
            m_new = jnp.maximum(m_prev, jnp.max(s, axis=-1, keepdims=True))
            alpha = jnp.exp(m_prev - m_new)
            p = jnp.exp(s - m_new)
            l_sc[...] = alpha * l_sc[...] + jnp.sum(p, axis=-1, keepdims=True)
            acc_sc[...] = alpha * acc_sc[...] + _dot(p.astype(BF16), v_ref[...])
            m_sc[...] = m_new

        @pl.when(j < i)
        def _():
            tile(False)

        @pl.when(j == i)
        def _():
            tile(True)
            l = l_sc[...]
            o_ref[...] = (acc_sc[...] / l).astype(BF16)
            lse_ref[...] = m_sc[...] + jnp.log(l)

    return _pcall(
        kernel, name=name, dims=("parallel", "parallel", "arbitrary"), deps=deps,
        out_shape=(jax.ShapeDtypeStruct((T, H * DH_A), BF16), jax.ShapeDtypeStruct((H, T, 1), F32)),
        grid=(H, nq, nq),
        in_specs=[
            pl.BlockSpec((tq, DH_A), lambda h, i, j: (i, h)),
            pl.BlockSpec((tk, DH_A), lambda h, i, j: (jnp.minimum(j, i), H + h)),
            pl.BlockSpec((tk, DH_A), lambda h, i, j: (jnp.minimum(j, i), 2 * H + h)),
            pl.BlockSpec((None, tq, 1), lambda h, i, j: (h, i, 0)),
            pl.BlockSpec((None, 1, tk), lambda h, i, j: (h, 0, jnp.minimum(j, i))),
        ],
        out_specs=(pl.BlockSpec((tq, DH_A), lambda h, i, j: (i, h)),
                   pl.BlockSpec((None, tq, 1), lambda h, i, j: (h, i, 0))),
        scratch_shapes=[pltpu.VMEM((tq, 1), F32), pltpu.VMEM((tq, 1), F32), pltpu.VMEM((tq, DH_A), F32)],
    )(proj, proj, proj, c_col, c_row)


def _rowdot(a, b, n_heads, *, name):
    T = a.shape[0]
    tq = _pick(T, 512, 8)

    def kernel(a_ref, b_ref, o_ref):
        o_ref[...] = jnp.sum(a_ref[...].astype(F32) * b_ref[...].astype(F32), axis=-1, keepdims=True)

    blk = pl.BlockSpec((tq, DH_A), lambda h, i: (i, h))
    return _pcall(
        kernel, name=name, dims=("parallel", "parallel"),
        out_shape=jax.ShapeDtypeStruct((n_heads, T, 1), F32),
        grid=(n_heads, T // tq), in_specs=[blk, blk],
        out_specs=pl.BlockSpec((None, tq, 1), lambda h, i: (h, i, 0)),
    )(a, b)


def _fox_bwd(proj, do, lse, delta, c_col, c_row, n_heads, *, name):
    T = proj.shape[0]
    H = n_heads
    tq = tk = _pick(T, FOX_TILE, 128)
    nq = T // tq
    scale = DH_A ** -0.5

    def kernel(q_ref, k_ref, v_ref, do_ref, lse_ref, dl_ref, cq_ref, ck_ref,
               dq_ref, dk_ref, dv_ref, dc_ref, dcq_ref, dk_sc, dv_sc, dc_sc):
        j, i = pl.program_id(1), pl.program_id(2)

        @pl.when((j == 0) & (i == 0))
        def _():
            dq_ref[...] = jnp.zeros_like(dq_ref)
            dcq_ref[...] = jnp.zeros_like(dcq_ref)

        @pl.when(i == 0)
        def _():
            dk_sc[...] = jnp.zeros_like(dk_sc)
            dv_sc[...] = jnp.zeros_like(dv_sc)
            dc_sc[...] = jnp.zeros_like(dc_sc)

        def tile(diagonal):
            q, k, v, dov = q_ref[...], k_ref[...], v_ref[...], do_ref[...]
            s = _fox_scores(q, k, cq_ref[...], ck_ref[...], diagonal)
            p = jnp.exp(s - lse_ref[...])
            dv_sc[...] += _dot(p.astype(BF16), dov, "tn")
            dp = _dot(dov, v, "nt")
            ds = p * (dp - dl_ref[...])
            dc_sc[...] -= jnp.sum(ds, axis=0, keepdims=True)
            dsb = ds.astype(BF16)
            dk_sc[...] += _dot(dsb, q, "tn") * scale
            rows = pl.ds(pl.multiple_of(i * tq, tq), tq)
            dq_ref[rows, :] += _dot(dsb, k) * scale
            dcq_ref[rows, :] += jnp.sum(ds, axis=1, keepdims=True)

        @pl.when(i > j)
        def _():
            tile(False)

        @pl.when(i == j)
        def _():
            tile(True)

        @pl.when(i == nq - 1)
        def _():
            dk_ref[...] = dk_sc[...].astype(BF16)
            dv_ref[...] = dv_sc[...].astype(BF16)
            dc_ref[...] = dc_sc[...]

    qi = lambda j, i: jnp.maximum(i, j)
    return _pcall(
        kernel, name=name, dims=("parallel", "arbitrary", "arbitrary"),
        out_shape=(jax.ShapeDtypeStruct((T, H * DH_A), F32), jax.ShapeDtypeStruct((T, H * DH_A), BF16),
                   jax.ShapeDtypeStruct((T, H * DH_A), BF16), jax.ShapeDtypeStruct((H, 1, T), F32),
                   jax.ShapeDtypeStruct((H, T, 1), F32)),
        grid=(H, nq, nq),
        in_specs=[
            pl.BlockSpec((tq, DH_A), lambda h, j, i: (qi(j, i), h)),
            pl.BlockSpec((tk, DH_A), lambda h, j, i: (j, H + h)),
            pl.BlockSpec((tk, DH_A), lambda h, j, i: (j, 2 * H + h)),
            pl.BlockSpec((tq, DH_A), lambda h, j, i: (qi(j, i), h)),
            pl.BlockSpec((None, tq, 1), lambda h, j, i: (h, qi(j, i), 0)),
            pl.BlockSpec((None, tq, 1), lambda h, j, i: (h, qi(j, i), 0)),
            pl.BlockSpec((None, tq, 1), lambda h, j, i: (h, qi(j, i), 0)),
            pl.BlockSpec((None, 1, tk), lambda h, j, i: (h, 0, j)),
        ],
        out_specs=(pl.BlockSpec((T, DH_A), lambda h, j, i: (0, h)),
                   pl.BlockSpec((tk, DH_A), lambda h, j, i: (j, h)),
                   pl.BlockSpec((tk, DH_A), lambda h, j, i: (j, h)),
                   pl.BlockSpec((None, 1, tk), lambda h, j, i: (h, 0, j)),
                   pl.BlockSpec((None, T, 1), lambda h, j, i: (h, 0, 0))),
        scratch_shapes=[pltpu.VMEM((tk, DH_A), F32), pltpu.VMEM((tk, DH_A), F32), pltpu.VMEM((1, tk), F32)],
    )(proj, proj, proj, do, lse, delta, c_col, c_row)


def _t5_bucket(dist):
    max_exact = NUM_BUCKETS // 2
    small = dist < max_exact
    large = max_exact + (np.log(np.maximum(dist, 1) / max_exact) / np.log(MAX_DISTANCE / max_exact)
                         * (NUM_BUCKETS - max_exact)).astype(np.int64)
    large = np.minimum(large, NUM_BUCKETS - 1)
    return np.where(small, dist, large)


def _bucket_onehot():
    ql = np.arange(SWA_BLOCK)[:, None]
    kl = np.arange(2 * SWA_BLOCK)[None, :]
    bucket = _t5_bucket(np.clip(ql + SWA_BLOCK - kl, 0, None))
    bucket = np.stack([bucket[:, :SWA_BLOCK], bucket[:, SWA_BLOCK:]], axis=0)
    onehot = (bucket.reshape(1, -1) == np.arange(NUM_BUCKETS)[:, None])
    return jnp.asarray(onehot, dtype=BF16)


def _split3(v):
    hi = v.astype(BF16)
    r1 = v - hi.astype(F32)
    mid = r1.astype(BF16)
    lo = (r1 - mid.astype(F32)).astype(BF16)
    return hi, mid, lo


def _bias_table(rel_bias, onehot, *, name):
    B, H = rel_bias.shape
    n = onehot.shape[1]
    bc = _pick(n, 8192, LANES)

    def kernel(rb_ref, oh_ref, o_ref):
        hi, mid, lo = _split3(rb_ref[...])
        oh = oh_ref[...]
        o_ref[...] = _dot(hi, oh, "tn") + _dot(mid, oh, "tn") + _dot(lo, oh, "tn")

    return _pcall(
        kernel, name=name, dims=("parallel",),
        out_shape=jax.ShapeDtypeStruct((H, n), F32), grid=(n // bc,),
        in_specs=[pl.BlockSpec((B, H), lambda i: (0, 0)), pl.BlockSpec((B, bc), lambda i: (0, i))],
        out_specs=pl.BlockSpec((H, bc), lambda i: (0, i)),
    )(rel_bias, onehot)


def _bias_table_bwd(dbias, onehot, *, name):
    H, n = dbias.shape
    B = onehot.shape[0]
    bc = _pick(n, 8192, LANES)

    def kernel(db_ref, oh_ref, o_ref):
        @pl.when(pl.program_id(0) == 0)
        def _():
            o_ref[...] = jnp.zeros_like(o_ref)

        hi, mid, lo = _split3(db_ref[...])
        oh = oh_ref[...]
        o_ref[...] += _dot(oh, hi, "nt") + _dot(oh, mid, "nt") + _dot(oh, lo, "nt")

    return _pcall(
        kernel, name=name, dims=("arbitrary",),
        out_shape=jax.ShapeDtypeStruct((B, H), F32), grid=(n // bc,),
        in_specs=[pl.BlockSpec((H, bc), lambda i: (0, i)), pl.BlockSpec((B, bc), lambda i: (0, i))],
        out_specs=pl.BlockSpec((B, H), lambda i: (0, 0)),
    )(dbias, onehot)


def _swa_probs(q2, kp, kc, bias_ref, lse, n, G):
    Q = SWA_BLOCK
    scale = DH_B ** -0.5
    sp = (_dot(q2, kp, "nt") * scale).reshape(G, Q, Q) + bias_ref[:, 0]
    sc = (_dot(q2, kc, "nt") * scale).reshape(G, Q, Q) + bias_ref[:, 1]
    row = lax.broadcasted_iota(jnp.int32, (G, Q, Q), 1)
    col = lax.broadcasted_iota(jnp.int32, (G, Q, Q), 2)
    vis_p = (col > row) & (n > 0)
    vis_c = col <= row
    sp = jnp.where(vis_p, sp, NEG)
    sc = jnp.where(vis_c, sc, NEG)
    if lse is None:
        return sp, sc, vis_p, vis_c
    pp = jnp.where(vis_p, jnp.exp(sp - lse), 0.0)
    pc = jnp.where(vis_c, jnp.exp(sc - lse), 0.0)
    return pp, pc


def _swa_fwd(q, k, v, bias, sinks, *, deps=(), name):
    HKV, G, T, _ = q.shape
    Q = SWA_BLOCK
    N = T // Q

    def kernel(q_ref, kp_ref, kc_ref, vp_ref, vc_ref, b_ref, s_ref, o_ref, lse_ref):
        n = pl.program_id(1)
        q2 = q_ref[...].reshape(G * Q, DH_B)
        sp, sc, vis_p, vis_c = _swa_probs(q2, kp_ref[...], kc_ref[...], b_ref, None, n, G)
        sink = s_ref[...][:, :, 0:1]
        m = jnp.maximum(jnp.maximum(jnp.max(sp, axis=-1, keepdims=True), jnp.max(sc, axis=-1, keepdims=True)), sink)
        pp = jnp.where(vis_p, jnp.exp(sp - m), 0.0)
        pc = jnp.where(vis_c, jnp.exp(sc - m), 0.0)
        denom = jnp.sum(pp, axis=-1, keepdims=True) + jnp.sum(pc, axis=-1, keepdims=True) + jnp.exp(sink - m)
        o = _dot(pp.reshape(G * Q, Q).astype(BF16), vp_ref[...]) + _dot(pc.reshape(G * Q, Q).astype(BF16), vc_ref[...])
        o_ref[...] = (o.reshape(G, Q, DH_B) / denom).astype(BF16)
        lse_ref[...] = m + jnp.log(denom)

    prev = lambda h, n: (h, jnp.maximum(n - 1, 0), 0)
    cur = lambda h, n: (h, n, 0)
    kv = lambda f: pl.BlockSpec((None, Q, DH_B), f)
    return _pcall(
        kernel, name=name, dims=("parallel", "parallel"), deps=deps,
        out_shape=(jax.ShapeDtypeStruct((HKV, G, T, DH_B), BF16), jax.ShapeDtypeStruct((HKV, G, T, 1), F32)),
        grid=(HKV, N),
        in_specs=[pl.BlockSpec((None, G, Q, DH_B), lambda h, n: (h, 0, n, 0)),
                  kv(prev), kv(cur), kv(prev), kv(cur),
                  pl.BlockSpec((None, G, 2, Q, Q), lambda h, n: (h, 0, 0, 0, 0)),
                  pl.BlockSpec((None, G, 1, LANES), lambda h, n: (h, 0, 0, 0))],
        out_specs=(pl.BlockSpec((None, G, Q, DH_B), lambda h, n: (h, 0, n, 0)),
                   pl.BlockSpec((None, G, Q, 1), lambda h, n: (h, 0, n, 0))),
    )(q, k, k, v, v, bias, sinks)


def _swa_bwd(q, k, v, o, do, lse, bias, sinks, *, name):
    HKV, G, T, _ = q.shape
    Q = SWA_BLOCK
    N = T // Q
    scale = DH_B ** -0.5

    def kernel(q_ref, kp_ref, kc_ref, vp_ref, vc_ref, o_ref, do_ref, lse_ref, b_ref, s_ref,
               dq_ref, dk_ref, dv_ref, db_ref, ds_ref, ck_sc, cv_sc):
        n = pl.program_id(1)

        @pl.when(n == 0)
        def _():
            db_ref[...] = jnp.zeros_like(db_ref)
            ds_ref[...] = jnp.zeros_like(ds_ref)
            ck_sc[...] = jnp.zeros_like(ck_sc)
            cv_sc[...] = jnp.zeros_like(cv_sc)

        @pl.when(n < N)
        def _():
            q2 = q_ref[...].reshape(G * Q, DH_B)
            do3 = do_ref[...]
            do2 = do3.reshape(G * Q, DH_B)
            kp, kc, vp, vc = kp_ref[...], kc_ref[...], vp_ref[...], vc_ref[...]
            lse_v = lse_ref[...]
            pp, pc = _swa_probs(q2, kp, kc, b_ref, lse_v, n, G)
            delta = jnp.sum(do3.astype(F32) * o_ref[...].astype(F32), axis=-1, keepdims=True)
            dsp = pp * (_dot(do2, vp, "nt").reshape(G, Q, Q) - delta)
            dsc = pc * (_dot(do2, vc, "nt").reshape(G, Q, Q) - delta)
            p_sink = jnp.exp(s_ref[...][:, :, 0:1] - lse_v)
            ds_ref[...] += jnp.broadcast_to(-jnp.sum(p_sink * delta, axis=1, keepdims=True), (G, 1, LANES))
            db_ref[:, 0] += dsp
            db_ref[:, 1] += dsc
            dsp2 = dsp.reshape(G * Q, Q).astype(BF16)
            dsc2 = dsc.reshape(G * Q, Q).astype(BF16)
            dq = (_dot(dsp2, kp) + _dot(dsc2, kc)) * scale
            dq_ref[...] = dq.reshape(G, Q, DH_B).astype(BF16)
            pp2 = pp.reshape(G * Q, Q).astype(BF16)
            pc2 = pc.reshape(G * Q, Q).astype(BF16)
            dk_ref[...] = (ck_sc[...] + _dot(dsp2, q2, "tn") * scale).astype(BF16)
            dv_ref[...] = (cv_sc[...] + _dot(pp2, do2, "tn")).astype(BF16)
            ck_sc[...] = _dot(dsc2, q2, "tn") * scale
            cv_sc[...] = _dot(pc2, do2, "tn")

        @pl.when(n == N)
        def _():
            dk_ref[...] = ck_sc[...].astype(BF16)
            dv_ref[...] = cv_sc[...].astype(BF16)

    qn = lambda n: jnp.minimum(n, N - 1)
    prev = lambda h, n: (h, jnp.maximum(qn(n) - 1, 0), 0)
    cur = lambda h, n: (h, qn(n), 0)
    out_kv = lambda h, n: (h, jnp.maximum(n - 1, 0), 0)
    kv = lambda f: pl.BlockSpec((None, Q, DH_B), f)
    qspec = pl.BlockSpec((None, G, Q, DH_B), lambda h, n: (h, 0, qn(n), 0))
    bspec = pl.BlockSpec((None, G, 2, Q, Q), lambda h, n: (h, 0, 0, 0, 0))
    sspec = pl.BlockSpec((None, G, 1, LANES), lambda h, n: (h, 0, 0, 0))
    return _pcall(
        kernel, name=name, dims=("parallel", "arbitrary"),
        out_shape=(jax.ShapeDtypeStruct((HKV, G, T, DH_B), BF16), jax.ShapeDtypeStruct((HKV, T, DH_B), BF16),
                   jax.ShapeDtypeStruct((HKV, T, DH_B), BF16), jax.ShapeDtypeStruct((HKV, G, 2, Q, Q), F32),
                   jax.ShapeDtypeStruct((HKV, G, 1, LANES), F32)),
        grid=(HKV, N + 1),
        in_specs=[qspec, kv(prev), kv(cur), kv(prev), kv(cur), qspec, qspec,
                  pl.BlockSpec((None, G, Q, 1), lambda h, n: (h, 0, qn(n), 0)), bspec, sspec],
        out_specs=(qspec, kv(out_kv), kv(out_kv), bspec, sspec),
        scratch_shapes=[pltpu.VMEM((Q, DH_B), F32), pltpu.VMEM((Q, DH_B), F32)],
    )(q, k, k, v, v, o, do, lse, bias, sinks)


def _branch_mix(oa, ob, wa_t, wb_t, proj, gate_col, *, deps=(), name):
    T = oa.shape[0]
    D = wa_t.shape[0]
    bm, bn = _pick(T, 1024, 8), _pick(D, 512, LANES)
    assert gate_col % bn == 0
    ga0, gb0 = gate_col // bn, (gate_col + D) // bn

    def kernel(oa_ref, ob_ref, wa_ref, wb_ref, ga_ref, gb_ref, mix_ref, ya_ref, yb_ref):
        ya = _dot(oa_ref[...], wa_ref[...], "nt")
        yb = _dot(ob_ref[...], wb_ref[...], "nt")
        mix = _sigmoid(ga_ref[...].astype(F32)) * ya + _sigmoid(gb_ref[...].astype(F32)) * yb
        mix_ref[...] = mix.astype(BF16)
        ya_ref[...] = ya.astype(BF16)
        yb_ref[...] = yb.astype(BF16)

    out = pl.BlockSpec((bm, bn), lambda i, j: (i, j))
    return _pcall(
        kernel, name=name, dims=("parallel", "parallel"), deps=deps,
        out_shape=tuple(jax.ShapeDtypeStruct((T, D), BF16) for _ in range(3)),
        grid=(T // bm, D // bn),
        in_specs=[pl.BlockSpec((bm, oa.shape[1]), lambda i, j: (i, 0)),
                  pl.BlockSpec((bm, ob.shape[1]), lambda i, j: (i, 0)),
                  pl.BlockSpec((bn, oa.shape[1]), lambda i, j: (j, 0)),
                  pl.BlockSpec((bn, ob.shape[1]), lambda i, j: (j, 0)),
                  pl.BlockSpec((bm, bn), lambda i, j: (i, ga0 + j)),
                  pl.BlockSpec((bm, bn), lambda i, j: (i, gb0 + j))],
        out_specs=(out, out, out),
    )(oa, ob, wa_t, wb_t, proj, proj)


def _mix_bwd_tile(dmixed, ya, yb, ga, gb):
    sa = _sigmoid(ga.astype(F32))
    sb = _sigmoid(gb.astype(F32))
    return (dmixed * sa, dmixed * sb,
            dmixed * ya.astype(F32) * sa * (1.0 - sa), dmixed * yb.astype(F32) * sb * (1.0 - sb))


def _ffn_up(h, wg_t, wu_t, *, name):
    T, D = h.shape
    FP = wg_t.shape[0]
    bm, bn = _pick(T, 1024, 8), _pick(FP, 512, LANES)

    def kernel(h_ref, wg_ref, wu_ref, g_ref, u_ref, hid_ref):
        hv = h_ref[...]
        g = _dot(hv, wg_ref[...], "nt")
        u = _dot(hv, wu_ref[...], "nt")
        g_ref[...] = g.astype(BF16)
        u_ref[...] = u.astype(BF16)
        hid_ref[...] = (g * _sigmoid(g) * u).astype(BF16)

    out = pl.BlockSpec((bm, bn), lambda i, j: (i, j))
    wspec = pl.BlockSpec((bn, D), lambda i, j: (j, 0))
    return _pcall(
        kernel, name=name, dims=("parallel", "parallel"),
        out_shape=tuple(jax.ShapeDtypeStruct((T, FP), BF16) for _ in range(3)),
        grid=(T // bm, FP // bn),
        in_specs=[pl.BlockSpec((bm, D), lambda i, j: (i, 0)), wspec, wspec],
        out_specs=(out, out, out),
    )(h, wg_t, wu_t)


def _swiglu_bwd_tile(dhid, gate, up):
    g = gate.astype(F32)
    sg = _sigmoid(g)
    return dhid * up.astype(F32) * sg * (1.0 + g * (1.0 - sg)), dhid * g * sg


def _adamw_update(w, g, m, v):
    nm = ADAM_B1 * m + (1.0 - ADAM_B1) * g
    nv = ADAM_B2 * v + (1.0 - ADAM_B2) * (g * g)
    m_hat = nm / (1.0 - ADAM_B1 ** ADAM_STEP)
    v_hat = nv / (1.0 - ADAM_B2 ** ADAM_STEP)
    return -ADAM_LR * (m_hat / (jnp.sqrt(v_hat) + ADAM_EPS) + ADAM_WD * w), nm, nv


def _adamw(w, g, m, v, *, name):
    Rw, Cw = w.shape
    br = _pick(Rw, max(8, (1 << 19) // Cw // 8 * 8), 8)

    def kernel(w_ref, g_ref, m_ref, v_ref, d_ref, nm_ref, nv_ref):
        d_ref[...], nm_ref[...], nv_ref[...] = _adamw_update(w_ref[...], g_ref[...], m_ref[...], v_ref[...])

    blk = pl.BlockSpec((br, Cw), lambda i: (i, 0))
    return _pcall(
        kernel, name=name, dims=("parallel",),
        out_shape=tuple(jax.ShapeDtypeStruct((Rw, Cw), F32) for _ in range(3)),
        grid=(Rw // br,), in_specs=[blk, blk, blk, blk], out_specs=(blk, blk, blk),
    )(w, g, m, v)


def _sum_adamw(land, w, m, v, *, name):
    _, R, C = land.shape
    bc = _pick(C, max(LANES, (1 << 18) // R // LANES * LANES), LANES)

    def kernel(l_ref, w_ref, m_ref, v_ref, g_ref, d_ref, nm_ref, nv_ref):
        g = l_ref[0].astype(F32)
        for k in range(1, N_DEV):
            g = g + l_ref[k].astype(F32)
        g_ref[...] = g
        d_ref[...], nm_ref[...], nv_ref[...] = _adamw_update(w_ref[...], g, m_ref[...], v_ref[...])

    blk = pl.BlockSpec((R, bc), lambda i: (0, i))
    return _pcall(
        kernel, name=name, dims=("parallel",),
        out_shape=tuple(jax.ShapeDtypeStruct((R, C), F32) for _ in range(4)),
        grid=(C // bc,),
        in_specs=[pl.BlockSpec((N_DEV, R, bc), lambda i: (0, 0, i)), blk, blk, blk],
        out_specs=(blk, blk, blk, blk),
    )(land, w, m, v)


def _my_place():
    return lax.axis_index("x"), lax.axis_index("y"), lax.axis_index("c")


def _flip(v, bit):
    return 1 - v if bit else v


def _all_gather_rows(shard, *, pad_rows=0, name):
    R, C = shard.shape
    flat = R % BF16_TILE_ROWS == 0
    assert flat or pad_rows == 0
    out_shape = (N_DEV * R + pad_rows, C) if flat else (N_DEV, R, C)

    def body(*refs):
        if pad_rows:
            x_ref, z_ref, out_ref, send_sems, recv_sems, local_sems = refs
        else:
            x_ref, out_ref, send_sems, recv_sems, local_sems = refs
        x, y, c = _my_place()
        me, sibling = (x, y, c), (x, y, 1 - c)
        chips = [(1 - x, y), (x, 1 - y), (1 - x, 1 - y)]

        def block(px, py, pc):
            idx = 4 * px + 2 * py + pc
            if flat:
                return out_ref.at[pl.ds(pl.multiple_of(idx * R, BF16_TILE_ROWS), R), :]
            return out_ref.at[idx]

        def copy(k, blk, to, src=None):
            return pltpu.make_async_remote_copy(
                src_ref=block(*blk) if src is None else src, dst_ref=block(*blk),
                send_sem=send_sems.at[k], recv_sem=recv_sems.at[k], device_id=to, device_id_type=MESH)

        mine = pltpu.make_async_copy(x_ref, block(*me), local_sems.at[0])
        mine.start()
        if pad_rows:
            zero = pltpu.make_async_copy(z_ref, out_ref.at[pl.ds(N_DEV * R, pad_rows), :], local_sems.at[1])
            zero.start()
        first = [copy(0, me, sibling, src=x_ref)]
        first += [copy(1 + j, me, (*chip, c), src=x_ref) for j, chip in enumerate(chips)]
        for cp in first:
            cp.start()
        passed = [copy(4 + j, (*chip, c), sibling) for j, chip in enumerate(chips)]
        for j, chip in enumerate(chips):
            copy(1 + j, (*chip, c), me).wait_recv()
            passed[j].start()
        copy(0, sibling, me).wait_recv()
        for j, chip in enumerate(chips):
            copy(4 + j, (*chip, 1 - c), me).wait_recv()
        for cp in first + passed:
            cp.wait_send()
        mine.wait()
        if pad_rows:
            zero.wait()

    operands = [shard] + ([jnp.zeros((pad_rows, C), shard.dtype)] if pad_rows else [])
    return _comm_call(
        body, name=name,
        out_shape=jax.ShapeDtypeStruct(out_shape, shard.dtype),
        in_specs=[ANY] * len(operands), out_specs=ANY,
        scratch_shapes=[pltpu.SemaphoreType.DMA((7,)), pltpu.SemaphoreType.DMA((7,)), pltpu.SemaphoreType.DMA((2,))],
    )(*operands)


HBM_SPEC = pl.BlockSpec(memory_space=pltpu.HBM)
SEM_SPEC = pl.BlockSpec(memory_space=pltpu.SEMAPHORE)
SPLIT_EFFECT = pltpu.SideEffectType.DATAFLOW_SIDE_EFFECTING
N_PEERS = N_DEV - 1


def _hbm(a):
    return pltpu.with_memory_space_constraint(a, pltpu.HBM)


def _peer(x, y, c, k):
    return _flip(x, k & 4), _flip(y, k & 2), _flip(c, k & 1)


def _row_block(ref, idx, R):
    if len(ref.shape) == 3:
        return ref.at[idx]
    return ref.at[pl.ds(pl.multiple_of(idx * R, BF16_TILE_ROWS), R), :]


def _remote_pair(src, dst, arrives, send_sem, recv_sem, to):
    send = pltpu.make_async_remote_copy(src_ref=src, dst_ref=dst, send_sem=send_sem, recv_sem=recv_sem,
                                        device_id=to, device_id_type=MESH)
    recv = pltpu.make_async_remote_copy(src_ref=src, dst_ref=arrives, send_sem=send_sem, recv_sem=recv_sem,
                                        device_id=to, device_id_type=MESH)
    return send, recv


def _split_call(body, arrays, sems, n_through, out_sems, after, token, *, name):
    out_shape = [pltpu.SemaphoreType.DMA((n,)) for n in out_sems]
    out_shape += [pltpu.HBM(a.shape, a.dtype) for a in arrays[:n_through]]
    out_specs = [SEM_SPEC] * len(out_sems) + [HBM_SPEC] * n_through
    if token:
        out_shape.append(jax.ShapeDtypeStruct((8, LANES), F32))
        out_specs.append(pl.BlockSpec(memory_space=pltpu.VMEM))
    out = pl.pallas_call(
        body, name=name, out_shape=tuple(out_shape),
        in_specs=tuple([HBM_SPEC] * len(arrays) + [SEM_SPEC] * len(sems) + [ANY]),
        out_specs=tuple(out_specs),
        input_output_aliases={t: len(out_sems) + t for t in range(n_through)},
        compiler_params=pltpu.CompilerParams(has_side_effects=SPLIT_EFFECT),
    )(*[_hbm(a) for a in arrays], *sems, after)
    n_s = len(out_sems)
    return tuple(out[:n_s]), tuple(out[n_s:n_s + n_through]), (out[-1] if token else None)


def _scatter_copies(src_ref, land_ref, sems, R):
    send_sems, recv_sems, local_sem = sems
    x, y, c = _my_place()
    local = pltpu.make_async_copy(_row_block(src_ref, 4 * x + 2 * y + c, R), land_ref.at[0], local_sem.at[0])
    remote = []
    for k in range(1, N_DEV):
        px, py, pc = _peer(x, y, c, k)
        remote.append(_remote_pair(_row_block(src_ref, 4 * px + 2 * py + pc, R), land_ref.at[k], land_ref.at[k],
                                   send_sems.at[k - 1], recv_sems.at[k - 1], (px, py, pc)))
    return local, remote


def _scatter_start(g, R, after, *, name):
    land = lax.empty((N_DEV, R, g.shape[-1]), g.dtype)

    def body(src_ref, land_ref, after_ref, send_sems, recv_sems, local_sem, src_thru, land_thru, token):
        local, remote = _scatter_copies(src_ref, land_ref, (send_sems, recv_sems, local_sem), R)
        local.start()
        for send, _ in remote:
            send.start()
        token[...] = jnp.zeros_like(token)

    sems, (src_thru, land_thru), token = _split_call(body, [g, land], [], 2, (N_PEERS, N_PEERS, 1), after, True, name=name)
    return sems, src_thru, land_thru, token, R


def _scatter_finish(handle, after, *, name):
    sems, src_thru, land_thru, _, R = handle

    def body(src_ref, land_ref, send_sems, recv_sems, local_sem, after_ref, src_dead, got_ref):
        local, remote = _scatter_copies(src_ref, land_ref, (send_sems, recv_sems, local_sem), R)
        for send, recv in remote:
            send.wait_send()
            recv.wait_recv()
        local.wait()

    return _split_call(body, [src_thru, land_thru], sems, 2, (), after, False, name=name)[1][1]


def _gather_copies(src_ref, land_ref, sems, R):
    send_sems, recv_sems, local_sem = sems
    x, y, c = _my_place()
    mine = _row_block(land_ref, 4 * x + 2 * y + c, R)
    local = pltpu.make_async_copy(src_ref, mine, local_sem.at[0])
    peers = [(x, y, 1 - c), (1 - x, y, c), (x, 1 - y, c), (1 - x, 1 - y, c)]
    remote = [_remote_pair(src_ref, mine, _row_block(land_ref, 4 * px + 2 * py + pc, R),
                           send_sems.at[k], recv_sems.at[k], (px, py, pc)) for k, (px, py, pc) in enumerate(peers)]
    return local, remote


def _forward_copies(land_ref, sems, R):
    send_sems, recv_sems = sems
    x, y, c = _my_place()
    pairs = []
    for j, (px, py) in enumerate([(1 - x, y), (x, 1 - y), (1 - x, 1 - y)]):
        blk = _row_block(land_ref, 4 * px + 2 * py + c, R)
        pairs.append(_remote_pair(blk, blk, _row_block(land_ref, 4 * px + 2 * py + (1 - c), R),
                                  send_sems.at[j], recv_sems.at[j], (x, y, 1 - c)))
    return pairs


def _gather_start(shard, after, *, pad_rows=0, name):
    R, C = shard.shape
    assert R % BF16_TILE_ROWS == 0
    land = lax.empty((N_DEV * R + pad_rows, C), shard.dtype)
    if pad_rows:
        land = lax.dynamic_update_slice(land, jnp.zeros((pad_rows, C), shard.dtype), (N_DEV * R, 0))

    def body(src_ref, land_ref, after_ref, send_sems, recv_sems, local_sem, src_thru, land_thru, token):
        local, remote = _gather_copies(src_ref, land_ref, (send_sems, recv_sems, local_sem), R)
        local.start()
        for send, _ in remote:
            send.start()
        token[...] = jnp.zeros_like(token)

    sems, (src_thru, land_thru), token = _split_call(body, [shard, land], [], 2, (4, 4, 1), after, True, name=name)
    return sems, src_thru, land_thru, token, R


def _gather_forward(handle, after, *, name):
    sems, src_thru, land_thru, _, R = handle

    def arrived(land_ref, src_ref, send_sems, recv_sems, local_sem, after_ref, land_out):
        _, remote = _gather_copies(src_ref, land_ref, (send_sems, recv_sems, local_sem), R)
        for _, recv in remote[1:]:
            recv.wait_recv()

    def pass_on(land_ref, after_ref, send_sems, recv_sems, land_out, token):
        for send, _ in _forward_copies(land_ref, (send_sems, recv_sems), R):
            send.start()
        token[...] = jnp.zeros_like(token)

    _, (land1,), _ = _split_call(arrived, [land_thru, src_thru], sems, 1, (), after, False, name=name + "_arrived")
    sems2, (land2,), token = _split_call(pass_on, [land1], [], 1, (3, 3), src_thru, True, name=name + "_pass_on")
    return sems, sems2, src_thru, land2, token, R


def _gather_finish(handle2, after, *, name):
    sems, sems2, src_thru, land_thru, _, R = handle2

    def body(land_ref, src_ref, send_sems, recv_sems, local_sem, send2, recv2, after_ref, land_out):
        local, remote = _gather_copies(src_ref, land_ref, (send_sems, recv_sems, local_sem), R)
        for send, _ in remote:
            send.wait_send()
        remote[0][1].wait_recv()
        local.wait()
        for send, recv in _forward_copies(land_ref, (send2, recv2), R):
            send.wait_send()
            recv.wait_recv()

    return _split_call(body, [land_thru, src_thru], [*sems, *sems2], 1, (), after, False, name=name)[1][0]


def _all_reduce_small(v, *, name):
    rows, C = v.shape

    def body(v_ref, o_ref, land_ref, send_sems, recv_sems):
        x, y, c = _my_place()
        me = 4 * x + 2 * y + c
        copies = []
        for k in range(1, N_DEV):
            peer = (_flip(x, k & 4), _flip(y, k & 2), _flip(c, k & 1))
            copies.append(pltpu.make_async_remote_copy(
                src_ref=v_ref, dst_ref=land_ref.at[me],
                send_sem=send_sems.at[k - 1], recv_sem=recv_sems.at[k - 1], device_id=peer, device_id_type=MESH))
        for cp in copies:
            cp.start()
        land_ref[me] = v_ref[...]
        for k in range(1, N_DEV):
            peer_idx = 4 * _flip(x, k & 4) + 2 * _flip(y, k & 2) + _flip(c, k & 1)
            pltpu.make_async_remote_copy(
                src_ref=v_ref, dst_ref=land_ref.at[peer_idx],
                send_sem=send_sems.at[k - 1], recv_sem=recv_sems.at[k - 1],
                device_id=(x, y, c), device_id_type=MESH).wait_recv()
        for cp in copies:
            cp.wait_send()
        acc = land_ref[0]
        for s in range(1, N_DEV):
            acc = acc + land_ref[s]
        o_ref[...] = acc

    vm = pl.BlockSpec(memory_space=pltpu.VMEM)
    return _comm_call(
        body, name=name,
        out_shape=jax.ShapeDtypeStruct((rows, C), F32),
        in_specs=[vm], out_specs=vm,
        scratch_shapes=[pltpu.VMEM((N_DEV, rows, C), F32),
                        pltpu.SemaphoreType.DMA((7,)), pltpu.SemaphoreType.DMA((7,))],
    )(v)


def kernel(x, norm1_g, w_in, b_forget, attn_sinks, rel_bias, w_branch_a, w_branch_b, w_out, norm2_g, w_ffn_gate, w_ffn_up, w_ffn_down, final_g, loss_target, m_norm1_g, m_w_in, m_b_forget, m_attn_sinks, m_rel_bias, m_w_branch_a, m_w_branch_b, m_w_out, m_norm2_g, m_w_ffn_gate, m_w_ffn_up, m_w_ffn_down, m_final_g, v_norm1_g, v_w_in, v_b_forget, v_attn_sinks, v_rel_bias, v_w_branch_a, v_w_branch_b, v_w_out, v_norm2_g, v_w_ffn_gate, v_w_ffn_up, v_w_ffn_down, v_final_g):
    xs = x[0]
    T, D = xs.shape
    H_A, H_B = b_forget.shape[-1], attn_sinks.shape[-1]
    WA, QB = H_A * DH_A, H_B * DH_B
    R_IN = w_in.shape[-1]
    W_IN = N_DEV * R_IN
    KB = (W_IN - 3 * WA - H_A - QB - 2 * D) // 2
    HKV = KB // DH_B
    G = H_B // HKV
    N_BIG = W_IN - H_A
    GATE_COL = 3 * WA + QB + 2 * KB
    R_F = w_ffn_gate.shape[-1]
    F = N_DEV * R_F
    FP = _round_up(F, 512)
    assert H_A <= LANES and T % SWA_BLOCK == 0

    win_s = w_in[0].T.astype(BF16)
    wa_s = w_branch_a[0].T.astype(BF16)
    wb_s = w_branch_b[0].T.astype(BF16)
    wout_s = w_out[0].astype(BF16)
    wg_s = w_ffn_gate[0].T.astype(BF16)
    wu_s = w_ffn_up[0].T.astype(BF16)
    wd_s = w_ffn_down[0].astype(BF16)

    win_all = _all_gather_rows(win_s, name="ag_w_in").reshape(W_IN, D)
    f_rows = (3 * WA, H_A)
    w_f = jnp.pad(win_all[3 * WA:3 * WA + H_A], ((0, LANES - H_A), (0, 0)))
    ag_wa = _gather_start(wa_s, win_all, name="ag_start_w_branch_a")
    ag_wb = _gather_start(wb_s, ag_wa[3], name="ag_start_w_branch_b")
    ag_wout = _gather_start(wout_s, ag_wb[3], name="ag_start_w_out")
    ag_wg = _gather_start(wg_s, ag_wout[3], pad_rows=FP - F, name="ag_start_w_ffn_gate")
    ag_wu = _gather_start(wu_s, ag_wg[3], pad_rows=FP - F, name="ag_start_w_ffn_up")
    ag_wd = _gather_start(wd_s, ag_wu[3], pad_rows=FP - F, name="ag_start_w_ffn_down")

    BM = 1024
    bn_big = _pick(N_BIG, 768, LANES)

    h1, rstd1 = _rms_fwd(xs, norm1_g, deps=[ag_wd[3]], name="rms1_fwd")
    proj = _matmul([(h1, win_all)], "nt", bm=BM, bn=bn_big, b_hole=f_rows, name="proj_fwd")
    f_logit = _matmul([(h1, w_f)], "nt", bm=BM, bn=LANES, out_dtype=F32, name="forget_fwd")
    b_pad = jnp.pad(b_forget, ((0, 0), (0, LANES - H_A)))
    c_all, sneg = _fox_gate_fwd(f_logit, b_pad, name="fox_gate_fwd")
    c_heads = c_all[:, :H_A].T
    c_col, c_row = c_heads[:, :, None], c_heads[:, None, :]
    fw_wa = _gather_forward(ag_wa, proj, name="ag_w_branch_a")
    fw_wb = _gather_forward(ag_wb, fw_wa[4], name="ag_w_branch_b")
    fw_wout = _gather_forward(ag_wout, fw_wb[4], name="ag_w_out")
    oa, lse_a = _fox_fwd(proj, c_col, c_row, H_A, deps=[fw_wout[4]], name="fox_fwd")
    fw_wg = _gather_forward(ag_wg, oa, name="ag_w_ffn_gate")
    fw_wu = _gather_forward(ag_wu, fw_wg[4], name="ag_w_ffn_up")

    def heads_q(a):
        return a.reshape(T, HKV, G, DH_B).transpose(1, 2, 0, 3)

    def heads_kv(a):
        return a.reshape(T, HKV, DH_B).transpose(1, 0, 2)

    qb = heads_q(proj[:, 3 * WA:3 * WA + QB])
    kb = heads_kv(proj[:, 3 * WA + QB:3 * WA + QB + KB])
    vb = heads_kv(proj[:, 3 * WA + QB + KB:GATE_COL])
    onehot = _bucket_onehot()
    bias = _bias_table(rel_bias, onehot, name="rel_bias_table").reshape(HKV, G, 2, SWA_BLOCK, SWA_BLOCK)
    sinks_b = jnp.broadcast_to(attn_sinks.reshape(HKV, G, 1, 1), (HKV, G, 1, LANES))
    ob4, lse_b = _swa_fwd(qb, kb, vb, bias, sinks_b, deps=[fw_wu[4]], name="swa_fwd")
    ob = ob4.transpose(2, 0, 1, 3).reshape(T, QB)
    fw_wd = _gather_forward(ag_wd, ob, name="ag_w_ffn_down")

    wa_t = _gather_finish(fw_wa, fw_wd[4], name="ag_wait_w_branch_a")
    wb_t = _gather_finish(fw_wb, wa_t, name="ag_wait_w_branch_b")
    mixed, ya, yb = _branch_mix(oa, ob, wa_t, wb_t, proj, GATE_COL, name="branch_mix")
    wout = _gather_finish(fw_wout, mixed, name="ag_wait_w_out")
    x1 = _matmul([(mixed, wout)], "nn", bm=BM, bn=512, out_dtype=F32, addend=xs, name="out_proj_fwd")

    h2, rstd2 = _rms_fwd(x1, norm2_g, name="rms2_fwd")
    wg_t = _gather_finish(fw_wg, h2, name="ag_wait_w_ffn_gate")
    wu_t = _gather_finish(fw_wu, wg_t, name="ag_wait_w_ffn_up")
    gate, up, hidden = _ffn_up(h2, wg_t, wu_t, name="ffn_up")
    wd = _gather_finish(fw_wd, hidden, name="ag_wait_w_ffn_down")
    tk_f = _pick(FP, 2816, LANES)
    x2 = _matmul([(hidden, wd)], "nn", bm=BM, bn=1024, tk=tk_f, out_dtype=F32, addend=x1, name="ffn_down_fwd")

    dx2, loss_part, g_final = _final_loss(x2, final_g.reshape(1, D), loss_target[0], name="final_loss")
    loss = lax.psum(loss_part[0, 0], MESH_AXES)

    dx2_b = dx2.astype(BF16)
    g_wd = _matmul([(hidden, dx2_b)], "tn", bm=1024, bn=1024, name="ffn_down_bwd_w")
    rs_wd = _scatter_start(g_wd, R_F, loss_part, name="rs_start_w_ffn_down")
    dgate, dup = _matmul([(dx2_b, wd)], "nt", bm=BM, bn=512, extras=[(gate, 0), (up, 0)], epilogue=_swiglu_bwd_tile,
                         n_out=2, deps=[rs_wd[3]], name="ffn_down_bwd_x")
    g_wg = _matmul([(dgate, h2)], "tn", bm=1024, bn=1024, name="ffn_gate_bwd_w")
    rs_wg = _scatter_start(g_wg, R_F, rs_wd[3], name="rs_start_w_ffn_gate")
    g_wu = _matmul([(dup, h2)], "tn", bm=1024, bn=1024, deps=[rs_wg[3]], name="ffn_up_bwd_w")
    rs_wu = _scatter_start(g_wu, R_F, rs_wg[3], name="rs_start_w_ffn_up")
    dh2 = _matmul([(dgate, wg_t), (dup, wu_t)], "nn", bm=BM, bn=1024, tk=_pick(FP, 1408, LANES), out_dtype=F32,
                  deps=[rs_wu[3]], name="ffn_up_bwd_x")
    dx1, g_norm2 = _rms_bwd(dx2, dh2, x1, rstd2, norm2_g, name="rms2_bwd")

    dx1_b = dx1.astype(BF16)
    g_wout = _matmul([(mixed, dx1_b)], "tn", bm=1024, bn=1024, name="out_proj_bwd_w")
    rs_wout = _scatter_start(g_wout, D // N_DEV, rs_wu[3], name="rs_start_w_out")
    bn_mix = _pick(D, 256, LANES)
    assert GATE_COL % bn_mix == 0
    dya, dyb, dga, dgb = _matmul(
        [(dx1_b, wout)], "nt", bm=BM, bn=bn_mix, epilogue=_mix_bwd_tile, n_out=4, deps=[rs_wout[3]],
        extras=[(ya, 0), (yb, 0), (proj, GATE_COL // bn_mix), (proj, (GATE_COL + D) // bn_mix)], name="out_proj_bwd_x")
    g_wa = _matmul([(dya, oa)], "tn", bm=1024, bn=1024, name="branch_a_bwd_w")
    rs_wa = _scatter_start(g_wa, D // N_DEV, rs_wout[3], name="rs_start_w_branch_a")
    g_wb = _matmul([(dyb, ob)], "tn", bm=1024, bn=1024, deps=[rs_wa[3]], name="branch_b_bwd_w")
    rs_wb = _scatter_start(g_wb, D // N_DEV, rs_wa[3], name="rs_start_w_branch_b")
    doa = _matmul([(dya, wa_t)], "nn", bm=BM, bn=1024, deps=[rs_wb[3]], name="branch_a_bwd_x")
    dob = _matmul([(dyb, wb_t)], "nn", bm=BM, bn=1024, name="branch_b_bwd_x")

    delta_a = _rowdot(doa, oa, H_A, name="fox_delta")
    dqa, dka, dva, dc_key, dc_query = _fox_bwd(proj, doa, lse_a, delta_a, c_col, c_row, H_A, name="fox_bwd")
    dc = jnp.pad((dc_key[:, 0, :] + dc_query[:, :, 0]).T, ((0, 0), (0, LANES - H_A)))
    df, g_bf = _fox_gate_bwd(dc, sneg, name="fox_gate_bwd")

    dqb4, dkb3, dvb3, dbias, dsinks = _swa_bwd(qb, kb, vb, ob4, heads_q(dob), lse_b, bias, sinks_b, name="swa_bwd")
    g_relb = _bias_table_bwd(dbias.reshape(H_B, -1), onehot, name="rel_bias_table_bwd")
    dqb = dqb4.transpose(2, 0, 1, 3).reshape(T, QB)
    dkb = dkb3.transpose(1, 0, 2).reshape(T, KB)
    dvb = dvb3.transpose(1, 0, 2).reshape(T, KB)

    dproj = jnp.concatenate([dqa.astype(BF16), dka, dva, dqb, dkb, dvb, dga, dgb], axis=1)
    g_win = _matmul([(dproj, h1)], "tn", bm=bn_big, bn=1024, out_hole=f_rows, name="proj_bwd_w")
    g_wf = _matmul([(df, h1)], "tn", bm=LANES, bn=1024, name="forget_bwd_w")
    g_win = lax.dynamic_update_slice(g_win, g_wf[:H_A], (3 * WA, 0)).reshape(N_DEV, R_IN, D)
    rs_win = _scatter_start(g_win, R_IN, rs_wb[3], name="rs_start_w_in")
    dh_f = _matmul([(df, w_f)], "nn", bm=BM, bn=1024, out_dtype=F32, deps=[rs_win[3]], name="forget_bwd_x")
    tk_big = _pick(math.gcd(3 * WA, N_BIG - 3 * WA), 1536, LANES)
    dh1 = _matmul([(dproj, win_all)], "nn", bm=BM, bn=1024, tk=tk_big, out_dtype=F32, addend=dh_f, b_hole=f_rows,
                  name="proj_bwd_x")
    grad_x, g_norm1 = _rms_bwd(dx1, dh1, xs, rstd1, norm1_g, name="rms1_bwd")

    def update(handle, after, w, m, v, transposed, nm):
        take = (lambda a: a[0].T) if transposed else (lambda a: a[0])
        give = (lambda a: a.T[None]) if transposed else (lambda a: a[None])
        land = _scatter_finish(handle, after, name="rs_wait_" + nm)
        return tuple(give(o) for o in _sum_adamw(land, take(w), take(m), take(v), name="adamw_" + nm))

    big = {}
    big["w_ffn_down"] = update(rs_wd, grad_x, w_ffn_down, m_w_ffn_down, v_w_ffn_down, False, "w_ffn_down")
    big["w_ffn_gate"] = update(rs_wg, big["w_ffn_down"][1], w_ffn_gate, m_w_ffn_gate, v_w_ffn_gate, True, "w_ffn_gate")
    big["w_ffn_up"] = update(rs_wu, big["w_ffn_gate"][1], w_ffn_up, m_w_ffn_up, v_w_ffn_up, True, "w_ffn_up")
    big["w_out"] = update(rs_wout, big["w_ffn_up"][1], w_out, m_w_out, v_w_out, False, "w_out")
    big["w_branch_a"] = update(rs_wa, big["w_out"][1], w_branch_a, m_w_branch_a, v_w_branch_a, True, "w_branch_a")
    big["w_branch_b"] = update(rs_wb, big["w_branch_a"][1], w_branch_b, m_w_branch_b, v_w_branch_b, True, "w_branch_b")
    big["w_in"] = update(rs_win, big["w_branch_b"][1], w_in, m_w_in, v_w_in, True, "w_in")

    n_rb = NUM_BUCKETS * H_B
    assert D >= n_rb and D >= H_A + H_B
    small = jnp.concatenate([
        g_norm1, g_norm2, g_final,
        jnp.pad(jnp.concatenate([g_bf[:, :H_A], dsinks[:, :, 0, 0].reshape(1, H_B)], axis=1), ((0, 0), (0, D - H_A - H_B))),
        jnp.pad(g_relb.reshape(1, n_rb), ((0, 0), (0, D - n_rb))),
        jnp.zeros((3, D), F32)], axis=0)
    small = _all_reduce_small(small, name="ar_small")
    gs = {
        "norm1_g": small[0:1], "norm2_g": small[1:2], "final_g": small[2],
        "b_forget": small[3:4, :H_A], "attn_sinks": small[3:4, H_A:H_A + H_B],
        "rel_bias": small[4, :n_rb].reshape(NUM_BUCKETS, H_B),
    }

    def adam_small(ws, gsm, ms, vs):
        def pack(parts):
            rows = [jnp.pad(p.reshape(1, -1), ((0, 0), (0, D - p.size))) for p in parts]
            return jnp.concatenate(rows + [jnp.ones((8 - len(rows), D), F32)], axis=0)
        d, nm_, nv_ = _adamw(pack(ws), pack(gsm), pack(ms), pack(vs), name="adamw_small")
        unpack = lambda a: [a[i, :p.size].reshape(p.shape) for i, p in enumerate(ws)]
        return unpack(d), unpack(nm_), unpack(nv_)

    small_names = ["norm1_g", "b_forget", "attn_sinks", "rel_bias", "norm2_g", "final_g"]
    small_w = [norm1_g, b_forget, attn_sinks, rel_bias, norm2_g, final_g]
    small_g = [gs[n].reshape(w.shape) for n, w in zip(small_names, small_w)]
    small_m = [m_norm1_g, m_b_forget, m_attn_sinks, m_rel_bias, m_norm2_g, m_final_g]
    small_v = [v_norm1_g, v_b_forget, v_attn_sinks, v_rel_bias, v_norm2_g, v_final_g]
    sd, sm, sv = adam_small(small_w, small_g, small_m, small_v)
    for i, n in enumerate(small_names):
        big[n] = (small_g[i], sd[i], sm[i], sv[i])

    order = ["norm1_g", "w_in", "b_forget", "attn_sinks", "rel_bias", "w_branch_a", "w_branch_b", "w_out",
             "norm2_g", "w_ffn_gate", "w_ffn_up", "w_ffn_down", "final_g"]
    grads = [big[n][0] for n in order]
    deltas = [big[n][1] for n in order]
    new_m = [big[n][2] for n in order]
    new_v = [big[n][3] for n in order]
    return (loss, grad_x[None], *grads, *deltas, *new_m, *new_v)
```

```python
import math

import numpy as np
import jax
import jax.numpy as jnp
from jax import lax
from jax.experimental import pallas as pl
from jax.experimental.pallas import tpu as pltpu

N_DEV = 8
MESH_AXES = ("x", "y", "c")
DH_A = 128
DH_B = 64
SWA_BLOCK = 128
FOX_TILE = 1024
NUM_BUCKETS = 32
MAX_DISTANCE = 128
EPS = 1e-6
ADAM_LR = 0.001
ADAM_B1 = 0.9
ADAM_B2 = 0.999
ADAM_EPS = 1e-08
ADAM_WD = 0.01
ADAM_STEP = 10
NEG = -1e30
LANES = 128
BF16_TILE_ROWS = 16
V7X_VMEM_LIMIT = 56 * 1024 * 1024
F32 = jnp.float32
BF16 = jnp.bfloat16
MESH = pl.DeviceIdType.MESH
ANY = pl.BlockSpec(memory_space=pl.ANY)


def _round_up(n, m):
    return (n + m - 1) // m * m


def _pick(n, target, mult):
    best = None
    for d in range(mult, min(n, target) + 1, mult):
        if n % d == 0:
            best = d
    return n if best is None else best


def _pcall(kernel, *, name, in_specs, dims=None, deps=(), **kw):
    deps = tuple(deps)
    if deps:
        inner, n_in, n_dep = kernel, len(in_specs), len(deps)

        def kernel(*refs):
            inner(*refs[:n_in], *refs[n_in + n_dep:])

        in_specs = list(in_specs) + [ANY] * n_dep
    call = pl.pallas_call(
        kernel, name=name, in_specs=in_specs,
        compiler_params=pltpu.CompilerParams(dimension_semantics=dims, vmem_limit_bytes=V7X_VMEM_LIMIT),
        **kw)
    return lambda *operands: call(*operands, *deps)


def _comm_call(kernel, *, name, **kw):
    return pl.pallas_call(kernel, name=name, **kw)


def _sigmoid(v):
    return 1.0 / (1.0 + jnp.exp(-v))


_DIMS = {"nn": (((1,), (0,)), ((), ())), "nt": (((1,), (1,)), ((), ())), "tn": (((0,), (0,)), ((), ()))}


def _dot(a, b, form="nn"):
    return lax.dot_general(a, b, _DIMS[form], preferred_element_type=F32)


def _matmul(pairs, form, *, bm, bn, tk=None, out_dtype=BF16, addend=None, extras=(), epilogue=None, n_out=1, deps=(),
            b_hole=None, out_hole=None, name):
    a0, b0 = pairs[0]
    if form == "tn":
        K, M = a0.shape
    else:
        M, K = a0.shape
    N = b0.shape[0] if form == "nt" else b0.shape[1]
    if b_hole is not None and form == "nt":
        N -= b_hole[1]
    tk = K if tk is None else tk
    bm, bn = min(bm, M), min(bn, N)
    assert M % bm == 0 and N % bn == 0 and K % tk == 0, (name, M, N, K, bm, bn, tk)
    nk = K // tk
    n_pairs = len(pairs)
    has_add = addend is not None

    def stored_row(first, hole):
        return pl.multiple_of(first + jnp.where(first >= hole[0], hole[1], 0), BF16_TILE_ROWS)

    if form == "tn":
        a_spec = pl.BlockSpec((tk, bm), lambda i, j, k: (k, i))
    else:
        a_spec = pl.BlockSpec((bm, tk), lambda i, j, k: (i, k))
    if b_hole is not None:
        assert len(pairs) == 1 and b_hole[0] % (bn if form == "nt" else tk) == 0 and b_hole[1] % BF16_TILE_ROWS == 0
    if form == "nt":
        if b_hole is None:
            b_spec = pl.BlockSpec((bn, tk), lambda i, j, k: (j, k))
        else:
            b_spec = pl.BlockSpec((pl.Element(bn), pl.Element(tk)),
                                  lambda i, j, k: (stored_row(j * bn, b_hole), pl.multiple_of(k * tk, LANES)))
    elif b_hole is None or form == "tn":
        assert b_hole is None
        b_spec = pl.BlockSpec((tk, bn), lambda i, j, k: (k, j))
    else:
        b_spec = pl.BlockSpec((pl.Element(tk), pl.Element(bn)),
                              lambda i, j, k: (stored_row(k * tk, b_hole), pl.multiple_of(j * bn, LANES)))
    if out_hole is None:
        o_spec = pl.BlockSpec((bm, bn), lambda i, j, k: (i, j))
    else:
        assert out_hole[0] % bm == 0 and out_hole[1] % BF16_TILE_ROWS == 0 and epilogue is None and not has_add
        o_spec = pl.BlockSpec((pl.Element(bm), pl.Element(bn)),
                              lambda i, j, k: (stored_row(i * bm, out_hole), pl.multiple_of(j * bn, LANES)))
    m_stored = M + (out_hole[1] if out_hole is not None else 0)
    n_extra = len(extras)
    n_in = 2 * n_pairs + has_add + n_extra

    def kernel(*refs):
        ab = refs[:2 * n_pairs]
        c_ref = refs[2 * n_pairs] if has_add else None
        extra_refs = refs[2 * n_pairs + has_add:n_in]
        o_refs = refs[n_in:n_in + n_out]

        def partial_sum():
            acc = _dot(ab[0][...], ab[1][...], form)
            for p in range(1, n_pairs):
                acc = acc + _dot(ab[2 * p][...], ab[2 * p + 1][...], form)
            return acc

        def finish(acc):
            if has_add:
                acc = acc + c_ref[...].astype(F32)
            outs = (acc,) if epilogue is None else epilogue(acc, *[r[...] for r in extra_refs])
            for o_ref, val in zip(o_refs, outs, strict=True):
                o_ref[...] = val.astype(o_ref.dtype)

        if nk == 1:
            finish(partial_sum())
        else:
            acc_ref = refs[-1]
            k = pl.program_id(2)

            @pl.when(k == 0)
            def _():
                acc_ref[...] = jnp.zeros_like(acc_ref)

            acc_ref[...] += partial_sum()

            @pl.when(k == nk - 1)
            def _():
                finish(acc_ref[...])

    operands, in_specs = [], []
    for a, b in pairs:
        operands += [a, b]
        in_specs += [a_spec, b_spec]
    if has_add:
        operands.append(addend)
        in_specs.append(o_spec)
    for arr, first in extras:
        operands.append(arr)
        in_specs.append(pl.BlockSpec((bm, bn), lambda i, j, k, first=first: (i, first + j)))
    out_shape = tuple(jax.ShapeDtypeStruct((m_stored, N), out_dtype) for _ in range(n_out))
    out = _pcall(
        kernel, name=name, dims=("parallel", "parallel", "arbitrary"), deps=deps,
        out_shape=out_shape, grid=(M // bm, N // bn, nk), in_specs=in_specs, out_specs=(o_spec,) * n_out,
        scratch_shapes=[pltpu.VMEM((bm, bn), F32)] if nk > 1 else [],
    )(*operands)
    return out[0] if epilogue is None else out


def _rms_fwd(x, g, *, deps=(), name):
    T, D = x.shape
    br = _pick(T, 256, 8)

    def kernel(x_ref, g_ref, h_ref, r_ref):
        xv = x_ref[...]
        rstd = lax.rsqrt(jnp.mean(xv * xv, axis=-1, keepdims=True) + EPS)
        h_ref[...] = (xv * rstd * g_ref[...]).astype(BF16)
        r_ref[...] = rstd

    return _pcall(
        kernel, name=name, dims=("parallel",), deps=deps,
        out_shape=(jax.ShapeDtypeStruct((T, D), BF16), jax.ShapeDtypeStruct((T, 1), F32)),
        grid=(T // br,),
        in_specs=[pl.BlockSpec((br, D), lambda i: (i, 0)), pl.BlockSpec((1, D), lambda i: (0, 0))],
        out_specs=(pl.BlockSpec((br, D), lambda i: (i, 0)), pl.BlockSpec((br, 1), lambda i: (i, 0))),
    )(x, g)


def _rms_bwd(dres, dh, x, rstd, g, *, deps=(), name):
    T, D = x.shape
    br = _pick(T, 256, 8)

    def kernel(dres_ref, dh_ref, x_ref, r_ref, g_ref, dx_ref, gg_ref, dxb_ref):
        @pl.when(pl.program_id(0) == 0)
        def _():
            gg_ref[...] = jnp.zeros_like(gg_ref)

        rstd_v = r_ref[...]
        xhat = x_ref[...] * rstd_v
        dhv = dh_ref[...].astype(F32)
        dxhat = dhv * g_ref[...]
        proj = jnp.mean(dxhat * xhat, axis=-1, keepdims=True)
        dx = dres_ref[...] + rstd_v * (dxhat - xhat * proj)
        dx_ref[...] = dx
        dxb_ref[...] = dx.astype(BF16)
        gg_ref[...] += jnp.sum(dhv * xhat, axis=0, keepdims=True)

    row = pl.BlockSpec((br, D), lambda i: (i, 0))
    vec = pl.BlockSpec((1, D), lambda i: (0, 0))
    return _pcall(
        kernel, name=name, dims=("arbitrary",), deps=deps,
        out_shape=(jax.ShapeDtypeStruct((T, D), F32), jax.ShapeDtypeStruct((1, D), F32),
                   jax.ShapeDtypeStruct((T, D), BF16)),
        grid=(T // br,),
        in_specs=[row, row, row, pl.BlockSpec((br, 1), lambda i: (i, 0)), vec],
        out_specs=(row, vec, row),
    )(dres, dh, x, rstd, g)


def _final_loss(x2, gf, target, *, name):
    T, D = x2.shape
    br = _pick(T, 256, 8)

    def kernel(x_ref, g_ref, t_ref, dx_ref, loss_ref, gg_ref, dxb_ref):
        @pl.when(pl.program_id(0) == 0)
        def _():
            gg_ref[...] = jnp.zeros_like(gg_ref)
            loss_ref[...] = jnp.zeros_like(loss_ref)

        xv = x_ref[...]
        gv = g_ref[...]
        rstd = lax.rsqrt(jnp.mean(xv * xv, axis=-1, keepdims=True) + EPS)
        xhat = xv * rstd
        err = xhat * gv - t_ref[...]
        loss_ref[...] += 0.5 * jnp.sum(jnp.mean(err * err, axis=-1, keepdims=True), axis=0, keepdims=True)
        dy = err * (1.0 / D)
        dxhat = dy * gv
        proj = jnp.mean(dxhat * xhat, axis=-1, keepdims=True)
        dx = rstd * (dxhat - xhat * proj)
        dx_ref[...] = dx
        dxb_ref[...] = dx.astype(BF16)
        gg_ref[...] += jnp.sum(dy * xhat, axis=0, keepdims=True)

    row = pl.BlockSpec((br, D), lambda i: (i, 0))
    vec = pl.BlockSpec((1, D), lambda i: (0, 0))
    return _pcall(
        kernel, name=name, dims=("arbitrary",),
        out_shape=(jax.ShapeDtypeStruct((T, D), F32), jax.ShapeDtypeStruct((1, 1), F32),
                   jax.ShapeDtypeStruct((1, D), F32), jax.ShapeDtypeStruct((T, D), BF16)),
        grid=(T // br,),
        in_specs=[row, vec, row],
        out_specs=(row, pl.BlockSpec((1, 1), lambda i: (0, 0)), vec, row),
    )(x2, gf, target)


def _scan_rows(v, reverse):
    n = v.shape[0]
    row = lax.broadcasted_iota(jnp.int32, v.shape, 0)
    s = 1
    while s < n:
        if reverse:
            v = v + jnp.where(row < n - s, pltpu.roll(v, n - s, 0), 0.0)
        else:
            v = v + jnp.where(row >= s, pltpu.roll(v, s, 0), 0.0)
        s *= 2
    return v


def _fox_gate_fwd(f, bias, *, name):
    T = f.shape[0]
    tb = _pick(T, 512, 8)

    def kernel(f_ref, b_ref, c_ref, s_ref, carry):
        @pl.when(pl.program_id(0) == 0)
        def _():
            carry[...] = jnp.zeros_like(carry)

        fa = f_ref[...] + b_ref[...]
        z = jnp.exp(-jnp.abs(fa))
        log_f = jnp.minimum(fa, 0.0) - jnp.log(1.0 + z)
        s_ref[...] = jnp.where(fa >= 0, z, 1.0) / (1.0 + z)
        c = _scan_rows(log_f, False) + carry[...]
        c_ref[...] = c
        carry[...] = c[tb - 1:tb, :]

    blk = pl.BlockSpec((tb, LANES), lambda i: (i, 0))
    return _pcall(
        kernel, name=name, dims=("arbitrary",),
        out_shape=(jax.ShapeDtypeStruct((T, LANES), F32), jax.ShapeDtypeStruct((T, LANES), F32)),
        grid=(T // tb,),
        in_specs=[blk, pl.BlockSpec((1, LANES), lambda i: (0, 0))],
        out_specs=(blk, blk),
        scratch_shapes=[pltpu.VMEM((1, LANES), F32)],
    )(f, bias)


def _fox_gate_bwd(dc, sneg, *, name):
    T = dc.shape[0]
    tb = _pick(T, 512, 8)
    nb = T // tb

    def kernel(dc_ref, s_ref, df_ref, gb_ref, carry):
        @pl.when(pl.program_id(0) == 0)
        def _():
            carry[...] = jnp.zeros_like(carry)
            gb_ref[...] = jnp.zeros_like(gb_ref)

        dlog = _scan_rows(dc_ref[...], True) + carry[...]
        carry[...] = dlog[0:1, :]
        dfa = dlog * s_ref[...]
        df_ref[...] = dfa.astype(BF16)
        gb_ref[...] += jnp.sum(dfa, axis=0, keepdims=True)

    blk = pl.BlockSpec((tb, LANES), lambda i: (nb - 1 - i, 0))
    return _pcall(
        kernel, name=name, dims=("arbitrary",),
        out_shape=(jax.ShapeDtypeStruct((T, LANES), BF16), jax.ShapeDtypeStruct((1, LANES), F32)),
        grid=(nb,),
        in_specs=[blk, blk],
        out_specs=(blk, pl.BlockSpec((1, LANES), lambda i: (0, 0))),
        scratch_shapes=[pltpu.VMEM((1, LANES), F32)],
    )(dc, sneg)


def _fox_scores(q, k, cq, ck, diagonal):
    s = _dot(q, k, "nt") * (DH_A ** -0.5) + cq - ck
    if diagonal:
        row = lax.broadcasted_iota(jnp.int32, s.shape, 0)
        col = lax.broadcasted_iota(jnp.int32, s.shape, 1)
        s = jnp.where(col <= row, s, NEG)
    return s


def _fox_fwd(proj, c_col, c_row, n_heads, *, deps=(), name):
    T = proj.shape[0]
    H = n_heads
    tq = tk = _pick(T, FOX_TILE, 128)
    nq = T // tq

    def kernel(q_ref, k_ref, v_ref, cq_ref, ck_ref, o_ref, lse_ref, m_sc, l_sc, acc_sc):
        i, j = pl.program_id(1), pl.program_id(2)

        @pl.when(j == 0)
        def _():
            m_sc[...] = jnp.full_like(m_sc, NEG)
            l_sc[...] = jnp.zeros_like(l_sc)
            acc_sc[...] = jnp.zeros_like(acc_sc)

        def tile(diagonal):
            s = _fox_scores(q_ref[...], k_ref[...], cq_ref[...], ck_ref[...], diagonal)
            m_prev = m_sc[...]
            m_new = jnp.maximum(m_prev, jnp.max(s, axis=-1, keepdims=True))
            alpha = jnp.exp(m_prev - m_new)
            p = jnp.exp(s - m_new)
            l_sc[...] = alpha * l_sc[...] + jnp.sum(p, axis=-1, keepdims=True)
            acc_sc[...] = alpha * acc_sc[...] + _dot(p.astype(BF16), v_ref[...])
            m_sc[...] = m_new

        @pl.when(j < i)
        def _():
            tile(False)

        @pl.when(j == i)
        def _():
            tile(True)
            l = l_sc[...]
            o_ref[...] = (acc_sc[...] / l).astype(BF16)
            lse_ref[...] = m_sc[...] + jnp.log(l)

    return _pcall(
        kernel, name=name, dims=("parallel", "parallel", "arbitrary"), deps=deps,
        out_shape=(jax.ShapeDtypeStruct((T, H * DH_A), BF16), jax.ShapeDtypeStruct((H, T, 1), F32)),
        grid=(H, nq, nq),
        in_specs=[
            pl.BlockSpec((tq, DH_A), lambda h, i, j: (i, h)),
            pl.BlockSpec((tk, DH_A), lambda h, i, j: (jnp.minimum(j, i), H + h)),
            pl.BlockSpec((tk, DH_A), lambda h, i, j: (jnp.minimum(j, i), 2 * H + h)),
            pl.BlockSpec((None, tq, 1), lambda h, i, j: (h, i, 0)),
            pl.BlockSpec((None, 1, tk), lambda h, i, j: (h, 0, jnp.minimum(j, i))),
        ],
        out_specs=(pl.BlockSpec((tq, DH_A), lambda h, i, j: (i, h)),
                   pl.BlockSpec((None, tq, 1), lambda h, i, j: (h, i, 0))),
        scratch_shapes=[pltpu.VMEM((tq, 1), F32), pltpu.VMEM((tq, 1), F32), pltpu.VMEM((tq, DH_A), F32)],
    )(proj, proj, proj, c_col, c_row)


def _fox_bwd(proj, o, do, lse, c_col, c_row, n_heads, *, name):
    T = proj.shape[0]
    H = n_heads
    tq = tk = _pick(T, FOX_TILE, 128)
    nq = T // tq
    scale = DH_A ** -0.5

    def kernel(q_ref, k_ref, v_ref, do_ref, lse_ref, o_ref, cq_ref, ck_ref,
               dq_ref, dk_ref, dv_ref, dc_ref, dcq_ref, dk_sc, dv_sc, dc_sc):
        j, i = pl.program_id(1), pl.program_id(2)

        @pl.when((j == 0) & (i == 0))
        def _():
            dq_ref[...] = jnp.zeros_like(dq_ref)
            dcq_ref[...] = jnp.zeros_like(dcq_ref)

        @pl.when(i == 0)
        def _():
            dk_sc[...] = jnp.zeros_like(dk_sc)
            dv_sc[...] = jnp.zeros_like(dv_sc)
            dc_sc[...] = jnp.zeros_like(dc_sc)

        def tile(diagonal):
            q, k, v, dov = q_ref[...], k_ref[...], v_ref[...], do_ref[...]
            s = _fox_scores(q, k, cq_ref[...], ck_ref[...], diagonal)
            p = jnp.exp(s - lse_ref[...])
            dv_sc[...] += _dot(p.astype(BF16), dov, "tn")
            dp = _dot(dov, v, "nt")
            delta = jnp.sum(dov.astype(F32) * o_ref[...].astype(F32), axis=1, keepdims=True)
            ds = p * (dp - delta)
            dc_sc[...] -= jnp.sum(ds, axis=0, keepdims=True)
            dsb = ds.astype(BF16)
            dk_sc[...] += _dot(dsb, q, "tn") * scale
            rows = pl.ds(pl.multiple_of(i * tq, tq), tq)
            dq_ref[rows, :] += _dot(dsb, k) * scale
            dcq_ref[rows, :] += jnp.sum(ds, axis=1, keepdims=True)

        @pl.when(i > j)
        def _():
            tile(False)

        @pl.when(i == j)
        def _():
            tile(True)

        @pl.when(i == nq - 1)
        def _():
            dk_ref[...] = dk_sc[...].astype(BF16)
            dv_ref[...] = dv_sc[...].astype(BF16)
            dc_ref[...] = dc_sc[...]

    qi = lambda j, i: jnp.maximum(i, j)
    return _pcall(
        kernel, name=name, dims=("parallel", "arbitrary", "arbitrary"),
        out_shape=(jax.ShapeDtypeStruct((T, H * DH_A), F32), jax.ShapeDtypeStruct((T, H * DH_A), BF16),
                   jax.ShapeDtypeStruct((T, H * DH_A), BF16), jax.ShapeDtypeStruct((H, 1, T), F32),
                   jax.ShapeDtypeStruct((H, T, 1), F32)),
        grid=(H, nq, nq),
        in_specs=[
            pl.BlockSpec((tq, DH_A), lambda h, j, i: (qi(j, i), h)),
            pl.BlockSpec((tk, DH_A), lambda h, j, i: (j, H + h)),
            pl.BlockSpec((tk, DH_A), lambda h, j, i: (j, 2 * H + h)),
            pl.BlockSpec((tq, DH_A), lambda h, j, i: (qi(j, i), h)),
            pl.BlockSpec((None, tq, 1), lambda h, j, i: (h, qi(j, i), 0)),
            pl.BlockSpec((tq, DH_A), lambda h, j, i: (qi(j, i), h)),
            pl.BlockSpec((None, tq, 1), lambda h, j, i: (h, qi(j, i), 0)),
            pl.BlockSpec((None, 1, tk), lambda h, j, i: (h, 0, j)),
        ],
        out_specs=(pl.BlockSpec((T, DH_A), lambda h, j, i: (0, h)),
                   pl.BlockSpec((tk, DH_A), lambda h, j, i: (j, h)),
                   pl.BlockSpec((tk, DH_A), lambda h, j, i: (j, h)),
                   pl.BlockSpec((None, 1, tk), lambda h, j, i: (h, 0, j)),
                   pl.BlockSpec((None, T, 1), lambda h, j, i: (h, 0, 0))),
        scratch_shapes=[pltpu.VMEM((tk, DH_A), F32), pltpu.VMEM((tk, DH_A), F32), pltpu.VMEM((1, tk), F32)],
    )(proj, proj, proj, do, lse, o, c_col, c_row)


def _t5_bucket(dist):
    max_exact = NUM_BUCKETS // 2
    small = dist < max_exact
    large = max_exact + (np.log(np.maximum(dist, 1) / max_exact) / np.log(MAX_DISTANCE / max_exact)
                         * (NUM_BUCKETS - max_exact)).astype(np.int64)
    large = np.minimum(large, NUM_BUCKETS - 1)
    return np.where(small, dist, large)


def _bucket_onehot():
    ql = np.arange(SWA_BLOCK)[:, None]
    kl = np.arange(2 * SWA_BLOCK)[None, :]
    bucket = _t5_bucket(np.clip(ql + SWA_BLOCK - kl, 0, None))
    bucket = np.stack([bucket[:, :SWA_BLOCK], bucket[:, SWA_BLOCK:]], axis=0)
    onehot = (bucket.reshape(1, -1) == np.arange(NUM_BUCKETS)[:, None])
    return jnp.asarray(onehot, dtype=BF16)


def _split3(v):
    hi = v.astype(BF16)
    r1 = v - hi.astype(F32)
    mid = r1.astype(BF16)
    lo = (r1 - mid.astype(F32)).astype(BF16)
    return hi, mid, lo


def _bias_table(rel_bias, onehot, *, name):
    B, H = rel_bias.shape
    n = onehot.shape[1]
    bc = _pick(n, 8192, LANES)

    def kernel(rb_ref, oh_ref, o_ref):
        hi, mid, lo = _split3(rb_ref[...])
        oh = oh_ref[...]
        o_ref[...] = _dot(hi, oh, "tn") + _dot(mid, oh, "tn") + _dot(lo, oh, "tn")

    return _pcall(
        kernel, name=name, dims=("parallel",),
        out_shape=jax.ShapeDtypeStruct((H, n), F32), grid=(n // bc,),
        in_specs=[pl.BlockSpec((B, H), lambda i: (0, 0)), pl.BlockSpec((B, bc), lambda i: (0, i))],
        out_specs=pl.BlockSpec((H, bc), lambda i: (0, i)),
    )(rel_bias, onehot)


def _bias_table_bwd(dbias, onehot, *, name):
    H, n = dbias.shape
    B = onehot.shape[0]
    bc = _pick(n, 8192, LANES)

    def kernel(db_ref, oh_ref, o_ref):
        @pl.when(pl.program_id(0) == 0)
        def _():
            o_ref[...] = jnp.zeros_like(o_ref)

        hi, mid, lo = _split3(db_ref[...])
        oh = oh_ref[...]
        o_ref[...] += _dot(oh, hi, "nt") + _dot(oh, mid, "nt") + _dot(oh, lo, "nt")

    return _pcall(
        kernel, name=name, dims=("arbitrary",),
        out_shape=jax.ShapeDtypeStruct((B, H), F32), grid=(n // bc,),
        in_specs=[pl.BlockSpec((H, bc), lambda i: (0, i)), pl.BlockSpec((B, bc), lambda i: (0, i))],
        out_specs=pl.BlockSpec((B, H), lambda i: (0, 0)),
    )(dbias, onehot)


def _swa_probs(q2, kp, kc, bias_ref, lse, n, G):
    Q = SWA_BLOCK
    scale = DH_B ** -0.5
    sp = (_dot(q2, kp, "nt") * scale).reshape(G, Q, Q) + bias_ref[:, 0]
    sc = (_dot(q2, kc, "nt") * scale).reshape(G, Q, Q) + bias_ref[:, 1]
    row = lax.broadcasted_iota(jnp.int32, (G, Q, Q), 1)
    col = lax.broadcasted_iota(jnp.int32, (G, Q, Q), 2)
    vis_p = (col > row) & (n > 0)
    vis_c = col <= row
    sp = jnp.where(vis_p, sp, NEG)
    sc = jnp.where(vis_c, sc, NEG)
    if lse is None:
        return sp, sc, vis_p, vis_c
    pp = jnp.where(vis_p, jnp.exp(sp - lse), 0.0)
    pc = jnp.where(vis_c, jnp.exp(sc - lse), 0.0)
    return pp, pc


def _swa_fwd(q, k, v, bias, sinks, *, deps=(), name):
    HKV, G, T, _ = q.shape
    Q = SWA_BLOCK
    N = T // Q

    def kernel(q_ref, kp_ref, kc_ref, vp_ref, vc_ref, b_ref, s_ref, o_ref, lse_ref):
        n = pl.program_id(1)
        q2 = q_ref[...].reshape(G * Q, DH_B)
        sp, sc, vis_p, vis_c = _swa_probs(q2, kp_ref[...], kc_ref[...], b_ref, None, n, G)
        sink = s_ref[...][:, :, 0:1]
        m = jnp.maximum(jnp.maximum(jnp.max(sp, axis=-1, keepdims=True), jnp.max(sc, axis=-1, keepdims=True)), sink)
        pp = jnp.where(vis_p, jnp.exp(sp - m), 0.0)
        pc = jnp.where(vis_c, jnp.exp(sc - m), 0.0)
        denom = jnp.sum(pp, axis=-1, keepdims=True) + jnp.sum(pc, axis=-1, keepdims=True) + jnp.exp(sink - m)
        o = _dot(pp.reshape(G * Q, Q).astype(BF16), vp_ref[...]) + _dot(pc.reshape(G * Q, Q).astype(BF16), vc_ref[...])
        o_ref[...] = (o.reshape(G, Q, DH_B) / denom).astype(BF16)
        lse_ref[...] = m + jnp.log(denom)

    prev = lambda h, n: (h, jnp.maximum(n - 1, 0), 0)
    cur = lambda h, n: (h, n, 0)
    kv = lambda f: pl.BlockSpec((None, Q, DH_B), f)
    return _pcall(
        kernel, name=name, dims=("parallel", "parallel"), deps=deps,
        out_shape=(jax.ShapeDtypeStruct((HKV, G, T, DH_B), BF16), jax.ShapeDtypeStruct((HKV, G, T, 1), F32)),
        grid=(HKV, N),
        in_specs=[pl.BlockSpec((None, G, Q, DH_B), lambda h, n: (h, 0, n, 0)),
                  kv(prev), kv(cur), kv(prev), kv(cur),
                  pl.BlockSpec((None, G, 2, Q, Q), lambda h, n: (h, 0, 0, 0, 0)),
                  pl.BlockSpec((None, G, 1, LANES), lambda h, n: (h, 0, 0, 0))],
        out_specs=(pl.BlockSpec((None, G, Q, DH_B), lambda h, n: (h, 0, n, 0)),
                   pl.BlockSpec((None, G, Q, 1), lambda h, n: (h, 0, n, 0))),
    )(q, k, k, v, v, bias, sinks)


def _swa_bwd(q, k, v, o, do, lse, bias, sinks, *, name):
    HKV, G, T, _ = q.shape
    Q = SWA_BLOCK
    N = T // Q
    scale = DH_B ** -0.5

    def kernel(q_ref, kp_ref, kc_ref, vp_ref, vc_ref, o_ref, do_ref, lse_ref, b_ref, s_ref,
               dq_ref, dk_ref, dv_ref, db_ref, ds_ref, ck_sc, cv_sc):
        n = pl.program_id(1)

        @pl.when(n == 0)
        def _():
            db_ref[...] = jnp.zeros_like(db_ref)
            ds_ref[...] = jnp.zeros_like(ds_ref)
            ck_sc[...] = jnp.zeros_like(ck_sc)
            cv_sc[...] = jnp.zeros_like(cv_sc)

        @pl.when(n < N)
        def _():
            q2 = q_ref[...].reshape(G * Q, DH_B)
            do3 = do_ref[...]
            do2 = do3.reshape(G * Q, DH_B)
            kp, kc, vp, vc = kp_ref[...], kc_ref[...], vp_ref[...], vc_ref[...]
            lse_v = lse_ref[...]
            pp, pc = _swa_probs(q2, kp, kc, b_ref, lse_v, n, G)
            delta = jnp.sum(do3.astype(F32) * o_ref[...].astype(F32), axis=-1, keepdims=True)
            dsp = pp * (_dot(do2, vp, "nt").reshape(G, Q, Q) - delta)
            dsc = pc * (_dot(do2, vc, "nt").reshape(G, Q, Q) - delta)
            p_sink = jnp.exp(s_ref[...][:, :, 0:1] - lse_v)
            ds_ref[...] += jnp.broadcast_to(-jnp.sum(p_sink * delta, axis=1, keepdims=True), (G, 1, LANES))
            db_ref[:, 0] += dsp
            db_ref[:, 1] += dsc
            dsp2 = dsp.reshape(G * Q, Q).astype(BF16)
            dsc2 = dsc.reshape(G * Q, Q).astype(BF16)
            dq = (_dot(dsp2, kp) + _dot(dsc2, kc)) * scale
            dq_ref[...] = dq.reshape(G, Q, DH_B).astype(BF16)
            pp2 = pp.reshape(G * Q, Q).astype(BF16)
            pc2 = pc.reshape(G * Q, Q).astype(BF16)
            dk_ref[...] = (ck_sc[...] + _dot(dsp2, q2, "tn") * scale).astype(BF16)
            dv_ref[...] = (cv_sc[...] + _dot(pp2, do2, "tn")).astype(BF16)
            ck_sc[...] = _dot(dsc2, q2, "tn") * scale
            cv_sc[...] = _dot(pc2, do2, "tn")

        @pl.when(n == N)
        def _():
            dk_ref[...] = ck_sc[...].astype(BF16)
            dv_ref[...] = cv_sc[...].astype(BF16)

    qn = lambda n: jnp.minimum(n, N - 1)
    prev = lambda h, n: (h, jnp.maximum(qn(n) - 1, 0), 0)
    cur = lambda h, n: (h, qn(n), 0)
    out_kv = lambda h, n: (h, jnp.maximum(n - 1, 0), 0)
    kv = lambda f: pl.BlockSpec((None, Q, DH_B), f)
    qspec = pl.BlockSpec((None, G, Q, DH_B), lambda h, n: (h, 0, qn(n), 0))
    bspec = pl.BlockSpec((None, G, 2, Q, Q), lambda h, n: (h, 0, 0, 0, 0))
    sspec = pl.BlockSpec((None, G, 1, LANES), lambda h, n: (h, 0, 0, 0))
    return _pcall(
        kernel, name=name, dims=("parallel", "arbitrary"),
        out_shape=(jax.ShapeDtypeStruct((HKV, G, T, DH_B), BF16), jax.ShapeDtypeStruct((HKV, T, DH_B), BF16),
                   jax.ShapeDtypeStruct((HKV, T, DH_B), BF16), jax.ShapeDtypeStruct((HKV, G, 2, Q, Q), F32),
                   jax.ShapeDtypeStruct((HKV, G, 1, LANES), F32)),
        grid=(HKV, N + 1),
        in_specs=[qspec, kv(prev), kv(cur), kv(prev), kv(cur), qspec, qspec,
                  pl.BlockSpec((None, G, Q, 1), lambda h, n: (h, 0, qn(n), 0)), bspec, sspec],
        out_specs=(qspec, kv(out_kv), kv(out_kv), bspec, sspec),
        scratch_shapes=[pltpu.VMEM((Q, DH_B), F32), pltpu.VMEM((Q, DH_B), F32)],
    )(q, k, k, v, v, o, do, lse, bias, sinks)


def _branch_mix(oa, ob, wa_t, wb_t, proj, gate_col, *, deps=(), name):
    T = oa.shape[0]
    D = wa_t.shape[0]
    bm, bn = _pick(T, 1024, 8), _pick(D, 512, LANES)
    assert gate_col % bn == 0
    ga0, gb0 = gate_col // bn, (gate_col + D) // bn

    def kernel(oa_ref, ob_ref, wa_ref, wb_ref, ga_ref, gb_ref, mix_ref, ya_ref, yb_ref):
        ya = _dot(oa_ref[...], wa_ref[...], "nt")
        yb = _dot(ob_ref[...], wb_ref[...], "nt")
        mix = _sigmoid(ga_ref[...].astype(F32)) * ya + _sigmoid(gb_ref[...].astype(F32)) * yb
        mix_ref[...] = mix.astype(BF16)
        ya_ref[...] = ya.astype(BF16)
        yb_ref[...] = yb.astype(BF16)

    out = pl.BlockSpec((bm, bn), lambda i, j: (i, j))
    return _pcall(
        kernel, name=name, dims=("parallel", "parallel"), deps=deps,
        out_shape=tuple(jax.ShapeDtypeStruct((T, D), BF16) for _ in range(3)),
        grid=(T // bm, D // bn),
        in_specs=[pl.BlockSpec((bm, oa.shape[1]), lambda i, j: (i, 0)),
                  pl.BlockSpec((bm, ob.shape[1]), lambda i, j: (i, 0)),
                  pl.BlockSpec((bn, oa.shape[1]), lambda i, j: (j, 0)),
                  pl.BlockSpec((bn, ob.shape[1]), lambda i, j: (j, 0)),
                  pl.BlockSpec((bm, bn), lambda i, j: (i, ga0 + j)),
                  pl.BlockSpec((bm, bn), lambda i, j: (i, gb0 + j))],
        out_specs=(out, out, out),
    )(oa, ob, wa_t, wb_t, proj, proj)


def _mix_bwd_tile(dmixed, ya, yb, ga, gb):
    sa = _sigmoid(ga.astype(F32))
    sb = _sigmoid(gb.astype(F32))
    return (dmixed * sa, dmixed * sb,
            dmixed * ya.astype(F32) * sa * (1.0 - sa), dmixed * yb.astype(F32) * sb * (1.0 - sb))


def _ffn_up(h, wg_t, wu_t, *, name):
    T, D = h.shape
    FP = wg_t.shape[0]
    bm, bn = _pick(T, 1024, 8), _pick(FP, 512, LANES)

    def kernel(h_ref, wg_ref, wu_ref, g_ref, u_ref, hid_ref):
        hv = h_ref[...]
        g = _dot(hv, wg_ref[...], "nt")
        u = _dot(hv, wu_ref[...], "nt")
        g_ref[...] = g.astype(BF16)
        u_ref[...] = u.astype(BF16)
        hid_ref[...] = (g * _sigmoid(g) * u).astype(BF16)

    out = pl.BlockSpec((bm, bn), lambda i, j: (i, j))
    wspec = pl.BlockSpec((bn, D), lambda i, j: (j, 0))
    return _pcall(
        kernel, name=name, dims=("parallel", "parallel"),
        out_shape=tuple(jax.ShapeDtypeStruct((T, FP), BF16) for _ in range(3)),
        grid=(T // bm, FP // bn),
        in_specs=[pl.BlockSpec((bm, D), lambda i, j: (i, 0)), wspec, wspec],
        out_specs=(out, out, out),
    )(h, wg_t, wu_t)


def _swiglu_bwd_tile(dhid, gate, up):
    g = gate.astype(F32)
    sg = _sigmoid(g)
    return dhid * up.astype(F32) * sg * (1.0 + g * (1.0 - sg)), dhid * g * sg


def _adamw_update(w, g, m, v):
    nm = ADAM_B1 * m + (1.0 - ADAM_B1) * g
    nv = ADAM_B2 * v + (1.0 - ADAM_B2) * (g * g)
    m_hat = nm / (1.0 - ADAM_B1 ** ADAM_STEP)
    v_hat = nv / (1.0 - ADAM_B2 ** ADAM_STEP)
    return -ADAM_LR * (m_hat / (jnp.sqrt(v_hat) + ADAM_EPS) + ADAM_WD * w), nm, nv


def _adamw(w, g, m, v, *, name):
    Rw, Cw = w.shape
    br = _pick(Rw, max(8, (1 << 19) // Cw // 8 * 8), 8)

    def kernel(w_ref, g_ref, m_ref, v_ref, d_ref, nm_ref, nv_ref):
        d_ref[...], nm_ref[...], nv_ref[...] = _adamw_update(w_ref[...], g_ref[...], m_ref[...], v_ref[...])

    blk = pl.BlockSpec((br, Cw), lambda i: (i, 0))
    return _pcall(
        kernel, name=name, dims=("parallel",),
        out_shape=tuple(jax.ShapeDtypeStruct((Rw, Cw), F32) for _ in range(3)),
        grid=(Rw // br,), in_specs=[blk, blk, blk, blk], out_specs=(blk, blk, blk),
    )(w, g, m, v)


def _sum_adamw(land, w, m, v, *, name):
    _, R, C = land.shape
    bc = _pick(C, max(LANES, (1 << 18) // R // LANES * LANES), LANES)

    def kernel(l_ref, w_ref, m_ref, v_ref, g_ref, d_ref, nm_ref, nv_ref):
        g = l_ref[0].astype(F32)
        for k in range(1, N_DEV):
            g = g + l_ref[k].astype(F32)
        g_ref[...] = g
        d_ref[...], nm_ref[...], nv_ref[...] = _adamw_update(w_ref[...], g, m_ref[...], v_ref[...])

    blk = pl.BlockSpec((R, bc), lambda i: (0, i))
    return _pcall(
        kernel, name=name, dims=("parallel",),
        out_shape=tuple(jax.ShapeDtypeStruct((R, C), F32) for _ in range(4)),
        grid=(C // bc,),
        in_specs=[pl.BlockSpec((N_DEV, R, bc), lambda i: (0, 0, i)), blk, blk, blk],
        out_specs=(blk, blk, blk, blk),
    )(land, w, m, v)


def _my_place():
    return lax.axis_index("x"), lax.axis_index("y"), lax.axis_index("c")


def _flip(v, bit):
    return 1 - v if bit else v


HBM_SPEC = pl.BlockSpec(memory_space=pltpu.HBM)
SEM_SPEC = pl.BlockSpec(memory_space=pltpu.SEMAPHORE)
SPLIT_EFFECT = pltpu.SideEffectType.DATAFLOW_SIDE_EFFECTING
N_PEERS = N_DEV - 1


def _hbm(a):
    return pltpu.with_memory_space_constraint(a, pltpu.HBM)


def _peer(x, y, c, k):
    return _flip(x, k & 4), _flip(y, k & 2), _flip(c, k & 1)


def _row_block(ref, idx, R):
    if len(ref.shape) == 3:
        return ref.at[idx]
    return ref.at[pl.ds(pl.multiple_of(idx * R, BF16_TILE_ROWS), R), :]


def _remote_pair(src, dst, arrives, send_sem, recv_sem, to):
    send = pltpu.make_async_remote_copy(src_ref=src, dst_ref=dst, send_sem=send_sem, recv_sem=recv_sem,
                                        device_id=to, device_id_type=MESH)
    recv = pltpu.make_async_remote_copy(src_ref=src, dst_ref=arrives, send_sem=send_sem, recv_sem=recv_sem,
                                        device_id=to, device_id_type=MESH)
    return send, recv


def _split_call(body, arrays, sems, n_through, out_sems, after, token, *, name):
    out_shape = [pltpu.SemaphoreType.DMA((n,)) for n in out_sems]
    out_shape += [pltpu.HBM(a.shape, a.dtype) for a in arrays[:n_through]]
    out_specs = [SEM_SPEC] * len(out_sems) + [HBM_SPEC] * n_through
    if token:
        out_shape.append(jax.ShapeDtypeStruct((8, LANES), F32))
        out_specs.append(pl.BlockSpec(memory_space=pltpu.VMEM))
    out = pl.pallas_call(
        body, name=name, out_shape=tuple(out_shape),
        in_specs=tuple([HBM_SPEC] * len(arrays) + [SEM_SPEC] * len(sems) + [ANY]),
        out_specs=tuple(out_specs),
        input_output_aliases={t: len(out_sems) + t for t in range(n_through)},
        compiler_params=pltpu.CompilerParams(has_side_effects=SPLIT_EFFECT),
    )(*[_hbm(a) for a in arrays], *sems, after)
    n_s = len(out_sems)
    return tuple(out[:n_s]), tuple(out[n_s:n_s + n_through]), (out[-1] if token else None)


def _scatter_copies(src_ref, land_ref, sems, R):
    send_sems, recv_sems, local_sem = sems
    x, y, c = _my_place()
    local = pltpu.make_async_copy(_row_block(src_ref, 4 * x + 2 * y + c, R), land_ref.at[0], local_sem.at[0])
    remote = []
    for k in range(1, N_DEV):
        px, py, pc = _peer(x, y, c, k)
        remote.append(_remote_pair(_row_block(src_ref, 4 * px + 2 * py + pc, R), land_ref.at[k], land_ref.at[k],
                                   send_sems.at[k - 1], recv_sems.at[k - 1], (px, py, pc)))
    return local, remote


def _scatter_start(g, R, after, *, name):
    land = lax.empty((N_DEV, R, g.shape[-1]), g.dtype)

    def body(src_ref, land_ref, after_ref, send_sems, recv_sems, local_sem, src_thru, land_thru, token):
        local, remote = _scatter_copies(src_ref, land_ref, (send_sems, recv_sems, local_sem), R)
        local.start()
        for send, _ in remote:
            send.start()
        token[...] = jnp.zeros_like(token)

    sems, (src_thru, land_thru), token = _split_call(body, [g, land], [], 2, (N_PEERS, N_PEERS, 1), after, True, name=name)
    return sems, src_thru, land_thru, token, R


def _scatter_finish(handle, after, *, name):
    sems, src_thru, land_thru, _, R = handle

    def body(src_ref, land_ref, send_sems, recv_sems, local_sem, after_ref, src_dead, got_ref):
        local, remote = _scatter_copies(src_ref, land_ref, (send_sems, recv_sems, local_sem), R)
        for send, recv in remote:
            send.wait_send()
            recv.wait_recv()
        local.wait()

    return _split_call(body, [src_thru, land_thru], sems, 2, (), after, False, name=name)[1][1]


def _gather_copies(src_ref, land_ref, sems, R):
    send_sems, recv_sems, local_sem = sems
    x, y, c = _my_place()
    mine = _row_block(land_ref, 4 * x + 2 * y + c, R)
    local = pltpu.make_async_copy(src_ref, mine, local_sem.at[0])
    peers = [(x, y, 1 - c), (1 - x, y, c), (x, 1 - y, c), (1 - x, 1 - y, c)]
    remote = [_remote_pair(src_ref, mine, _row_block(land_ref, 4 * px + 2 * py + pc, R),
                           send_sems.at[k], recv_sems.at[k], (px, py, pc)) for k, (px, py, pc) in enumerate(peers)]
    return local, remote


def _forward_copies(land_ref, sems, R):
    send_sems, recv_sems = sems
    x, y, c = _my_place()
    pairs = []
    for j, (px, py) in enumerate([(1 - x, y), (x, 1 - y), (1 - x, 1 - y)]):
        blk = _row_block(land_ref, 4 * px + 2 * py + c, R)
        pairs.append(_remote_pair(blk, blk, _row_block(land_ref, 4 * px + 2 * py + (1 - c), R),
                                  send_sems.at[j], recv_sems.at[j], (x, y, 1 - c)))
    return pairs


def _gather_start(shard, after, *, pad_rows=0, name):
    R, C = shard.shape
    flat = R % BF16_TILE_ROWS == 0
    assert flat or pad_rows == 0
    land = lax.empty((N_DEV * R + pad_rows, C) if flat else (N_DEV, R, C), shard.dtype)
    if pad_rows:
        land = lax.dynamic_update_slice(land, jnp.zeros((pad_rows, C), shard.dtype), (N_DEV * R, 0))

    def body(src_ref, land_ref, after_ref, send_sems, recv_sems, local_sem, src_thru, land_thru, token):
        local, remote = _gather_copies(src_ref, land_ref, (send_sems, recv_sems, local_sem), R)
        local.start()
        for send, _ in remote:
            send.start()
        token[...] = jnp.zeros_like(token)

    sems, (src_thru, land_thru), token = _split_call(body, [shard, land], [], 2, (4, 4, 1), after, True, name=name)
    return sems, src_thru, land_thru, token, R


def _gather_forward(handle, after, *, name):
    sems, src_thru, land_thru, _, R = handle

    def arrived(land_ref, src_ref, send_sems, recv_sems, local_sem, after_ref, land_out):
        _, remote = _gather_copies(src_ref, land_ref, (send_sems, recv_sems, local_sem), R)
        for _, recv in remote[1:]:
            recv.wait_recv()

    def pass_on(land_ref, after_ref, send_sems, recv_sems, land_out, token):
        for send, _ in _forward_copies(land_ref, (send_sems, recv_sems), R):
            send.start()
        token[...] = jnp.zeros_like(token)

    _, (land1,), _ = _split_call(arrived, [land_thru, src_thru], sems, 1, (), after, False, name=name + "_arrived")
    sems2, (land2,), token = _split_call(pass_on, [land1], [], 1, (3, 3), src_thru, True, name=name + "_pass_on")
    return sems, sems2, src_thru, land2, token, R


def _gather_finish(handle2, after, *, name):
    sems, sems2, src_thru, land_thru, _, R = handle2

    def body(land_ref, src_ref, send_sems, recv_sems, local_sem, send2, recv2, after_ref, land_out):
        local, remote = _gather_copies(src_ref, land_ref, (send_sems, recv_sems, local_sem), R)
        for send, _ in remote:
            send.wait_send()
        remote[0][1].wait_recv()
        local.wait()
        for send, recv in _forward_copies(land_ref, (send2, recv2), R):
            send.wait_send()
            recv.wait_recv()

    return _split_call(body, [land_thru, src_thru], [*sems, *sems2], 1, (), after, False, name=name)[1][0]


def _all_reduce_small(v, *, name):
    rows, C = v.shape

    def body(v_ref, o_ref, land_ref, send_sems, recv_sems):
        x, y, c = _my_place()
        me = 4 * x + 2 * y + c
        copies = []
        for k in range(1, N_DEV):
            peer = (_flip(x, k & 4), _flip(y, k & 2), _flip(c, k & 1))
            copies.append(pltpu.make_async_remote_copy(
                src_ref=v_ref, dst_ref=land_ref.at[me],
                send_sem=send_sems.at[k - 1], recv_sem=recv_sems.at[k - 1], device_id=peer, device_id_type=MESH))
        for cp in copies:
            cp.start()
        land_ref[me] = v_ref[...]
        for k in range(1, N_DEV):
            peer_idx = 4 * _flip(x, k & 4) + 2 * _flip(y, k & 2) + _flip(c, k & 1)
            pltpu.make_async_remote_copy(
                src_ref=v_ref, dst_ref=land_ref.at[peer_idx],
                send_sem=send_sems.at[k - 1], recv_sem=recv_sems.at[k - 1],
                device_id=(x, y, c), device_id_type=MESH).wait_recv()
        for cp in copies:
            cp.wait_send()
        acc = land_ref[0]
        for s in range(1, N_DEV):
            acc = acc + land_ref[s]
        o_ref[...] = acc

    vm = pl.BlockSpec(memory_space=pltpu.VMEM)
    return _comm_call(
        body, name=name,
        out_shape=jax.ShapeDtypeStruct((rows, C), F32),
        in_specs=[vm], out_specs=vm,
        scratch_shapes=[pltpu.VMEM((N_DEV, rows, C), F32),
                        pltpu.SemaphoreType.DMA((7,)), pltpu.SemaphoreType.DMA((7,))],
    )(v)


def kernel(x, norm1_g, w_in, b_forget, attn_sinks, rel_bias, w_branch_a, w_branch_b, w_out, norm2_g, w_ffn_gate, w_ffn_up, w_ffn_down, final_g, loss_target, m_norm1_g, m_w_in, m_b_forget, m_attn_sinks, m_rel_bias, m_w_branch_a, m_w_branch_b, m_w_out, m_norm2_g, m_w_ffn_gate, m_w_ffn_up, m_w_ffn_down, m_final_g, v_norm1_g, v_w_in, v_b_forget, v_attn_sinks, v_rel_bias, v_w_branch_a, v_w_branch_b, v_w_out, v_norm2_g, v_w_ffn_gate, v_w_ffn_up, v_w_ffn_down, v_final_g):
    xs = x[0]
    T, D = xs.shape
    H_A, H_B = b_forget.shape[-1], attn_sinks.shape[-1]
    WA, QB = H_A * DH_A, H_B * DH_B
    R_IN = w_in.shape[-1]
    W_IN = N_DEV * R_IN
    KB = (W_IN - 3 * WA - H_A - QB - 2 * D) // 2
    HKV = KB // DH_B
    G = H_B // HKV
    N_BIG = W_IN - H_A
    GATE_COL = 3 * WA + QB + 2 * KB
    R_F = w_ffn_gate.shape[-1]
    F = N_DEV * R_F
    FP = _round_up(F, 512)
    assert H_A <= LANES and T % SWA_BLOCK == 0

    win_s = w_in[0].T.astype(BF16)
    wa_s = w_branch_a[0].T.astype(BF16)
    wb_s = w_branch_b[0].T.astype(BF16)
    wout_s = w_out[0].astype(BF16)
    wg_s = w_ffn_gate[0].T.astype(BF16)
    wu_s = w_ffn_up[0].T.astype(BF16)
    wd_s = w_ffn_down[0].astype(BF16)

    ag_win = _gather_start(win_s, norm1_g, name="ag_start_w_in")
    ag_wa = _gather_start(wa_s, ag_win[3], name="ag_start_w_branch_a")
    ag_wb = _gather_start(wb_s, ag_wa[3], name="ag_start_w_branch_b")
    ag_wout = _gather_start(wout_s, ag_wb[3], name="ag_start_w_out")
    ag_wg = _gather_start(wg_s, ag_wout[3], pad_rows=FP - F, name="ag_start_w_ffn_gate")
    ag_wu = _gather_start(wu_s, ag_wg[3], pad_rows=FP - F, name="ag_start_w_ffn_up")
    ag_wd = _gather_start(wd_s, ag_wu[3], pad_rows=FP - F, name="ag_start_w_ffn_down")

    BM = 1024
    bn_big = _pick(N_BIG, 768, LANES)

    h1, rstd1 = _rms_fwd(xs, norm1_g, deps=[ag_wd[3]], name="rms1_fwd")
    fw_win = _gather_forward(ag_win, h1, name="ag_w_in")
    win_all = _gather_finish(fw_win, fw_win[4], name="ag_wait_w_in").reshape(W_IN, D)
    f_rows = (3 * WA, H_A)
    w_f = jnp.pad(win_all[3 * WA:3 * WA + H_A], ((0, LANES - H_A), (0, 0)))
    proj = _matmul([(h1, win_all)], "nt", bm=BM, bn=bn_big, b_hole=f_rows, name="proj_fwd")
    f_logit = _matmul([(h1, w_f)], "nt", bm=BM, bn=LANES, out_dtype=F32, name="forget_fwd")
    b_pad = jnp.pad(b_forget, ((0, 0), (0, LANES - H_A)))
    c_all, sneg = _fox_gate_fwd(f_logit, b_pad, name="fox_gate_fwd")
    c_heads = c_all[:, :H_A].T
    c_col, c_row = c_heads[:, :, None], c_heads[:, None, :]
    fw_wa = _gather_forward(ag_wa, proj, name="ag_w_branch_a")
    fw_wb = _gather_forward(ag_wb, fw_wa[4], name="ag_w_branch_b")
    fw_wout = _gather_forward(ag_wout, fw_wb[4], name="ag_w_out")
    oa, lse_a = _fox_fwd(proj, c_col, c_row, H_A, deps=[fw_wout[4]], name="fox_fwd")
    fw_wg = _gather_forward(ag_wg, oa, name="ag_w_ffn_gate")
    fw_wu = _gather_forward(ag_wu, fw_wg[4], name="ag_w_ffn_up")

    def heads_q(a):
        return a.reshape(T, HKV, G, DH_B).transpose(1, 2, 0, 3)

    def heads_kv(a):
        return a.reshape(T, HKV, DH_B).transpose(1, 0, 2)

    qb = heads_q(proj[:, 3 * WA:3 * WA + QB])
    kb = heads_kv(proj[:, 3 * WA + QB:3 * WA + QB + KB])
    vb = heads_kv(proj[:, 3 * WA + QB + KB:GATE_COL])
    onehot = _bucket_onehot()
    bias = _bias_table(rel_bias, onehot, name="rel_bias_table").reshape(HKV, G, 2, SWA_BLOCK, SWA_BLOCK)
    sinks_b = jnp.broadcast_to(attn_sinks.reshape(HKV, G, 1, 1), (HKV, G, 1, LANES))
    ob4, lse_b = _swa_fwd(qb, kb, vb, bias, sinks_b, deps=[fw_wu[4]], name="swa_fwd")
    ob = ob4.transpose(2, 0, 1, 3).reshape(T, QB)

    wa_t = _gather_finish(fw_wa, ob, name="ag_wait_w_branch_a")
    wb_t = _gather_finish(fw_wb, wa_t, name="ag_wait_w_branch_b")
    mixed, ya, yb = _branch_mix(oa, ob, wa_t, wb_t, proj, GATE_COL, name="branch_mix")
    fw_wd = _gather_forward(ag_wd, mixed, name="ag_w_ffn_down")
    wout = _gather_finish(fw_wout, fw_wd[4], name="ag_wait_w_out")
    x1 = _matmul([(mixed, wout)], "nn", bm=BM, bn=512, out_dtype=F32, addend=xs, name="out_proj_fwd")

    h2, rstd2 = _rms_fwd(x1, norm2_g, name="rms2_fwd")
    wg_t = _gather_finish(fw_wg, h2, name="ag_wait_w_ffn_gate")
    wu_t = _gather_finish(fw_wu, wg_t, name="ag_wait_w_ffn_up")
    gate, up, hidden = _ffn_up(h2, wg_t, wu_t, name="ffn_up")
    wd = _gather_finish(fw_wd, hidden, name="ag_wait_w_ffn_down")
    tk_f = _pick(FP, 2816, LANES)
    x2 = _matmul([(hidden, wd)], "nn", bm=BM, bn=1024, tk=tk_f, out_dtype=F32, addend=x1, name="ffn_down_fwd")

    dx2, loss_part, g_final, dx2_b = _final_loss(x2, final_g.reshape(1, D), loss_target[0], name="final_loss")
    loss = lax.psum(loss_part[0, 0], MESH_AXES)

    g_wd = _matmul([(hidden, dx2_b)], "tn", bm=1024, bn=1024, name="ffn_down_bwd_w")
    rs_wd = _scatter_start(g_wd, R_F, loss_part, name="rs_start_w_ffn_down")
    dgate, dup = _matmul([(dx2_b, wd)], "nt", bm=BM, bn=512, extras=[(gate, 0), (up, 0)], epilogue=_swiglu_bwd_tile,
                         n_out=2, deps=[rs_wd[3]], name="ffn_down_bwd_x")
    g_wg = _matmul([(dgate, h2)], "tn", bm=1024, bn=1024, name="ffn_gate_bwd_w")
    rs_wg = _scatter_start(g_wg, R_F, rs_wd[3], name="rs_start_w_ffn_gate")
    g_wu = _matmul([(dup, h2)], "tn", bm=1024, bn=1024, deps=[rs_wg[3]], name="ffn_up_bwd_w")
    rs_wu = _scatter_start(g_wu, R_F, rs_wg[3], name="rs_start_w_ffn_up")
    dh2 = _matmul([(dgate, wg_t), (dup, wu_t)], "nn", bm=BM, bn=1024, tk=_pick(FP, 1408, LANES), out_dtype=F32,
                  deps=[rs_wu[3]], name="ffn_up_bwd_x")
    dx1, g_norm2, dx1_b = _rms_bwd(dx2, dh2, x1, rstd2, norm2_g, name="rms2_bwd")

    g_wout = _matmul([(mixed, dx1_b)], "tn", bm=1024, bn=1024, name="out_proj_bwd_w")
    rs_wout = _scatter_start(g_wout, D // N_DEV, rs_wu[3], name="rs_start_w_out")
    bn_mix = _pick(D, 256, LANES)
    assert GATE_COL % bn_mix == 0
    dya, dyb, dga, dgb = _matmul(
        [(dx1_b, wout)], "nt", bm=BM, bn=bn_mix, epilogue=_mix_bwd_tile, n_out=4, deps=[rs_wout[3]],
        extras=[(ya, 0), (yb, 0), (proj, GATE_COL // bn_mix), (proj, (GATE_COL + D) // bn_mix)], name="out_proj_bwd_x")
    g_wa = _matmul([(dya, oa)], "tn", bm=1024, bn=1024, name="branch_a_bwd_w")
    rs_wa = _scatter_start(g_wa, D // N_DEV, rs_wout[3], name="rs_start_w_branch_a")
    g_wb = _matmul([(dyb, ob)], "tn", bm=1024, bn=1024, deps=[rs_wa[3]], name="branch_b_bwd_w")
    rs_wb = _scatter_start(g_wb, D // N_DEV, rs_wa[3], name="rs_start_w_branch_b")
    doa = _matmul([(dya, wa_t)], "nn", bm=BM, bn=1024, deps=[rs_wb[3]], name="branch_a_bwd_x")
    dob = _matmul([(dyb, wb_t)], "nn", bm=BM, bn=1024, name="branch_b_bwd_x")

    dqa, dka, dva, dc_key, dc_query = _fox_bwd(proj, oa, doa, lse_a, c_col, c_row, H_A, name="fox_bwd")
    dc = jnp.pad((dc_key[:, 0, :] + dc_query[:, :, 0]).T, ((0, 0), (0, LANES - H_A)))
    df, g_bf = _fox_gate_bwd(dc, sneg, name="fox_gate_bwd")

    dqb4, dkb3, dvb3, dbias, dsinks = _swa_bwd(qb, kb, vb, ob4, heads_q(dob), lse_b, bias, sinks_b, name="swa_bwd")
    g_relb = _bias_table_bwd(dbias.reshape(H_B, -1), onehot, name="rel_bias_table_bwd")
    dqb = dqb4.transpose(2, 0, 1, 3).reshape(T, QB)
    dkb = dkb3.transpose(1, 0, 2).reshape(T, KB)
    dvb = dvb3.transpose(1, 0, 2).reshape(T, KB)

    dproj = jnp.concatenate([dqa.astype(BF16), dka, dva, dqb, dkb, dvb, dga, dgb], axis=1)
    g_win = _matmul([(dproj, h1)], "tn", bm=bn_big, bn=1024, out_hole=f_rows, name="proj_bwd_w")
    g_wf = _matmul([(df, h1)], "tn", bm=LANES, bn=1024, name="forget_bwd_w")
    g_win = lax.dynamic_update_slice(g_win, g_wf[:H_A], (3 * WA, 0)).reshape(N_DEV, R_IN, D)
    rs_win = _scatter_start(g_win, R_IN, rs_wb[3], name="rs_start_w_in")
    dh_f = _matmul([(df, w_f)], "nn", bm=BM, bn=1024, out_dtype=F32, deps=[rs_win[3]], name="forget_bwd_x")
    tk_big = _pick(math.gcd(3 * WA, N_BIG - 3 * WA), 1536, LANES)
    dh1 = _matmul([(dproj, win_all)], "nn", bm=BM, bn=1024, tk=tk_big, out_dtype=F32, addend=dh_f, b_hole=f_rows,
                  name="proj_bwd_x")
    grad_x, g_norm1, _ = _rms_bwd(dx1, dh1, xs, rstd1, norm1_g, name="rms1_bwd")

    def update(handle, after, w, m, v, transposed, nm):
        take = (lambda a: a[0].T) if transposed else (lambda a: a[0])
        give = (lambda a: a.T[None]) if transposed else (lambda a: a[None])
        land = _scatter_finish(handle, after, name="rs_wait_" + nm)
        return tuple(give(o) for o in _sum_adamw(land, take(w), take(m), take(v), name="adamw_" + nm))

    big = {}
    big["w_ffn_down"] = update(rs_wd, grad_x, w_ffn_down, m_w_ffn_down, v_w_ffn_down, False, "w_ffn_down")
    big["w_ffn_gate"] = update(rs_wg, big["w_ffn_down"][1], w_ffn_gate, m_w_ffn_gate, v_w_ffn_gate, True, "w_ffn_gate")
    big["w_ffn_up"] = update(rs_wu, big["w_ffn_gate"][1], w_ffn_up, m_w_ffn_up, v_w_ffn_up, True, "w_ffn_up")
    big["w_out"] = update(rs_wout, big["w_ffn_up"][1], w_out, m_w_out, v_w_out, False, "w_out")
    big["w_branch_a"] = update(rs_wa, big["w_out"][1], w_branch_a, m_w_branch_a, v_w_branch_a, True, "w_branch_a")
    big["w_branch_b"] = update(rs_wb, big["w_branch_a"][1], w_branch_b, m_w_branch_b, v_w_branch_b, True, "w_branch_b")
    big["w_in"] = update(rs_win, big["w_branch_b"][1], w_in, m_w_in, v_w_in, True, "w_in")

    n_rb = NUM_BUCKETS * H_B
    assert D >= n_rb and D >= H_A + H_B
    small = jnp.concatenate([
        g_norm1, g_norm2, g_final,
        jnp.pad(jnp.concatenate([g_bf[:, :H_A], dsinks[:, :, 0, 0].reshape(1, H_B)], axis=1), ((0, 0), (0, D - H_A - H_B))),
        jnp.pad(g_relb.reshape(1, n_rb), ((0, 0), (0, D - n_rb))),
        jnp.zeros((3, D), F32)], axis=0)
    small = _all_reduce_small(small, name="ar_small")
    gs = {
        "norm1_g": small[0:1], "norm2_g": small[1:2], "final_g": small[2],
        "b_forget": small[3:4, :H_A], "attn_sinks": small[3:4, H_A:H_A + H_B],
        "rel_bias": small[4, :n_rb].reshape(NUM_BUCKETS, H_B),
    }

    def adam_small(ws, gsm, ms, vs):
        def pack(parts):
            rows = [jnp.pad(p.reshape(1, -1), ((0, 0), (0, D - p.size))) for p in parts]
            return jnp.concatenate(rows + [jnp.ones((8 - len(rows), D), F32)], axis=0)
        d, nm_, nv_ = _adamw(pack(ws), pack(gsm), pack(ms), pack(vs), name="adamw_small")
        unpack = lambda a: [a[i, :p.size].reshape(p.shape) for i, p in enumerate(ws)]
        return unpack(d), unpack(nm_), unpack(nv_)

    small_names = ["norm1_g", "b_forget", "attn_sinks", "rel_bias", "norm2_g", "final_g"]
    small_w = [norm1_g, b_forget, attn_sinks, rel_bias, norm2_g, final_g]
    small_g = [gs[n].reshape(w.shape) for n, w in zip(small_names, small_w)]
    small_m = [m_norm1_g, m_b_forget, m_attn_sinks, m_rel_bias, m_norm2_g, m_final_g]
    small_v = [v_norm1_g, v_b_forget, v_attn_sinks, v_rel_bias, v_norm2_g, v_final_g]
    sd, sm, sv = adam_small(small_w, small_g, small_m, small_v)
    for i, n in enumerate(small_names):
        big[n] = (small_g[i], sd[i], sm[i], sv[i])

    order = ["norm1_g", "w_in", "b_forget", "attn_sinks", "rel_bias", "w_branch_a", "w_branch_b", "w_out",
             "norm2_g", "w_ffn_gate", "w_ffn_up", "w_ffn_down", "final_g"]
    grads = [big[n][0] for n in order]
    deltas = [big[n][1] for n in order]
    new_m = [big[n][2] for n in order]
    new_v = [big[n][3] for n in order]
    return (loss, grad_x[None], *grads, *deltas, *new_m, *new_v)
```

```python
import math

import numpy as np
import jax
import jax.numpy as jnp
from jax import lax
from jax.experimental import pallas as pl
from jax.experimental.pallas import tpu as pltpu

N_DEV = 8
MESH_AXES = ("x", "y", "c")
DH_A = 128
DH_B = 64
SWA_BLOCK = 128
FOX_TILE = 1024
NUM_BUCKETS = 32
MAX_DISTANCE = 128
EPS = 1e-6
ADAM_LR = 0.001
ADAM_B1 = 0.9
ADAM_B2 = 0.999
ADAM_EPS = 1e-08
ADAM_WD = 0.01
ADAM_STEP = 10
NEG = -1e30
LANES = 128
BF16_TILE_ROWS = 16
V7X_VMEM_LIMIT = 56 * 1024 * 1024
F32 = jnp.float32
BF16 = jnp.bfloat16
MESH = pl.DeviceIdType.MESH
ANY = pl.BlockSpec(memory_space=pl.ANY)


def _round_up(n, m):
    return (n + m - 1) // m * m


def _pick(n, target, mult):
    best = None
    for d in range(mult, min(n, target) + 1, mult):
        if n % d == 0:
            best = d
    return n if best is None else best


def _pcall(kernel, *, name, in_specs, dims=None, deps=(), **kw):
    deps = tuple(deps)
    if deps:
        inner, n_in, n_dep = kernel, len(in_specs), len(deps)

        def kernel(*refs):
            inner(*refs[:n_in], *refs[n_in + n_dep:])

        in_specs = list(in_specs) + [ANY] * n_dep
    call = pl.pallas_call(
        kernel, name=name, in_specs=in_specs,
        compiler_params=pltpu.CompilerParams(dimension_semantics=dims, vmem_limit_bytes=V7X_VMEM_LIMIT),
        **kw)
    return lambda *operands: call(*operands, *deps)


def _comm_call(kernel, *, name, **kw):
    return pl.pallas_call(kernel, name=name, **kw)


def _sigmoid(v):
    return 1.0 / (1.0 + jnp.exp(-v))


_DIMS = {"nn": (((1,), (0,)), ((), ())), "nt": (((1,), (1,)), ((), ())), "tn": (((0,), (0,)), ((), ()))}


def _dot(a, b, form="nn"):
    return lax.dot_general(a, b, _DIMS[form], preferred_element_type=F32)


def _matmul(pairs, form, *, bm, bn, tk=None, out_dtype=BF16, addend=None, extras=(), epilogue=None, n_out=1, deps=(),
            b_hole=None, out_hole=None, name):
    a0, b0 = pairs[0]
    if form == "tn":
        K, M = a0.shape
    else:
        M, K = a0.shape
    N = b0.shape[0] if form == "nt" else b0.shape[1]
    if b_hole is not None and form == "nt":
        N -= b_hole[1]
    tk = K if tk is None else tk
    bm, bn = min(bm, M), min(bn, N)
    assert M % bm == 0 and N % bn == 0 and K % tk == 0, (name, M, N, K, bm, bn, tk)
    nk = K // tk
    n_pairs = len(pairs)
    has_add = addend is not None

    def stored_row(first, hole):
        return pl.multiple_of(first + jnp.where(first >= hole[0], hole[1], 0), BF16_TILE_ROWS)

    if form == "tn":
        a_spec = pl.BlockSpec((tk, bm), lambda i, j, k: (k, i))
    else:
        a_spec = pl.BlockSpec((bm, tk), lambda i, j, k: (i, k))
    if b_hole is not None:
        assert len(pairs) == 1 and b_hole[0] % (bn if form == "nt" else tk) == 0 and b_hole[1] % BF16_TILE_ROWS == 0
    if form == "nt":
        if b_hole is None:
            b_spec = pl.BlockSpec((bn, tk), lambda i, j, k: (j, k))
        else:
            b_spec = pl.BlockSpec((pl.Element(bn), pl.Element(tk)),
                                  lambda i, j, k: (stored_row(j * bn, b_hole), pl.multiple_of(k * tk, LANES)))
    elif b_hole is None or form == "tn":
        assert b_hole is None
        b_spec = pl.BlockSpec((tk, bn), lambda i, j, k: (k, j))
    else:
        b_spec = pl.BlockSpec((pl.Element(tk), pl.Element(bn)),
                              lambda i, j, k: (stored_row(k * tk, b_hole), pl.multiple_of(j * bn, LANES)))
    if out_hole is None:
        o_spec = pl.BlockSpec((bm, bn), lambda i, j, k: (i, j))
    else:
        assert out_hole[0] % bm == 0 and out_hole[1] % BF16_TILE_ROWS == 0 and epilogue is None and not has_add
        o_spec = pl.BlockSpec((pl.Element(bm), pl.Element(bn)),
                              lambda i, j, k: (stored_row(i * bm, out_hole), pl.multiple_of(j * bn, LANES)))
    m_stored = M + (out_hole[1] if out_hole is not None else 0)
    n_extra = len(extras)
    n_in = 2 * n_pairs + has_add + n_extra

    def kernel(*refs):
        ab = refs[:2 * n_pairs]
        c_ref = refs[2 * n_pairs] if has_add else None
        extra_refs = refs[2 * n_pairs + has_add:n_in]
        o_refs = refs[n_in:n_in + n_out]

        def partial_sum():
            acc = _dot(ab[0][...], ab[1][...], form)
            for p in range(1, n_pairs):
                acc = acc + _dot(ab[2 * p][...], ab[2 * p + 1][...], form)
            return acc

        def finish(acc):
            if has_add:
                acc = acc + c_ref[...].astype(F32)
            outs = (acc,) if epilogue is None else epilogue(acc, *[r[...] for r in extra_refs])
            for o_ref, val in zip(o_refs, outs, strict=True):
                o_ref[...] = val.astype(o_ref.dtype)

        if nk == 1:
            finish(partial_sum())
        else:
            acc_ref = refs[-1]
            k = pl.program_id(2)

            @pl.when(k == 0)
            def _():
                acc_ref[...] = jnp.zeros_like(acc_ref)

            acc_ref[...] += partial_sum()

            @pl.when(k == nk - 1)
            def _():
                finish(acc_ref[...])

    operands, in_specs = [], []
    for a, b in pairs:
        operands += [a, b]
        in_specs += [a_spec, b_spec]
    if has_add:
        operands.append(addend)
        in_specs.append(o_spec)
    for arr, first in extras:
        operands.append(arr)
        in_specs.append(pl.BlockSpec((bm, bn), lambda i, j, k, first=first: (i, first + j)))
    out_shape = tuple(jax.ShapeDtypeStruct((m_stored, N), out_dtype) for _ in range(n_out))
    out = _pcall(
        kernel, name=name, dims=("parallel", "parallel", "arbitrary"), deps=deps,
        out_shape=out_shape, grid=(M // bm, N // bn, nk), in_specs=in_specs, out_specs=(o_spec,) * n_out,
        scratch_shapes=[pltpu.VMEM((bm, bn), F32)] if nk > 1 else [],
    )(*operands)
    return out[0] if epilogue is None else out


def _rms_fwd(x, g, *, deps=(), name):
    T, D = x.shape
    br = _pick(T, 256, 8)

    def kernel(x_ref, g_ref, h_ref, r_ref):
        xv = x_ref[...]
        rstd = lax.rsqrt(jnp.mean(xv * xv, axis=-1, keepdims=True) + EPS)
        h_ref[...] = (xv * rstd * g_ref[...]).astype(BF16)
        r_ref[...] = rstd

    return _pcall(
        kernel, name=name, dims=("parallel",), deps=deps,
        out_shape=(jax.ShapeDtypeStruct((T, D), BF16), jax.ShapeDtypeStruct((T, 1), F32)),
        grid=(T // br,),
        in_specs=[pl.BlockSpec((br, D), lambda i: (i, 0)), pl.BlockSpec((1, D), lambda i: (0, 0))],
        out_specs=(pl.BlockSpec((br, D), lambda i: (i, 0)), pl.BlockSpec((br, 1), lambda i: (i, 0))),
    )(x, g)


def _rms_bwd(dres, dh, x, rstd, g, *, deps=(), name):
    T, D = x.shape
    br = _pick(T, 256, 8)

    def kernel(dres_ref, dh_ref, x_ref, r_ref, g_ref, dx_ref, gg_ref, dxb_ref):
        @pl.when(pl.program_id(0) == 0)
        def _():
            gg_ref[...] = jnp.zeros_like(gg_ref)

        rstd_v = r_ref[...]
        xhat = x_ref[...] * rstd_v
        dhv = dh_ref[...].astype(F32)
        dxhat = dhv * g_ref[...]
        proj = jnp.mean(dxhat * xhat, axis=-1, keepdims=True)
        dx = dres_ref[...] + rstd_v * (dxhat - xhat * proj)
        dx_ref[...] = dx
        dxb_ref[...] = dx.astype(BF16)
        gg_ref[...] += jnp.sum(dhv * xhat, axis=0, keepdims=True)

    row = pl.BlockSpec((br, D), lambda i: (i, 0))
    vec = pl.BlockSpec((1, D), lambda i: (0, 0))
    return _pcall(
        kernel, name=name, dims=("arbitrary",), deps=deps,
        out_shape=(jax.ShapeDtypeStruct((T, D), F32), jax.ShapeDtypeStruct((1, D), F32),
                   jax.ShapeDtypeStruct((T, D), BF16)),
        grid=(T // br,),
        in_specs=[row, row, row, pl.BlockSpec((br, 1), lambda i: (i, 0)), vec],
        out_specs=(row, vec, row),
    )(dres, dh, x, rstd, g)


def _final_loss(x2, gf, target, *, name):
    T, D = x2.shape
    br = _pick(T, 256, 8)

    def kernel(x_ref, g_ref, t_ref, dx_ref, loss_ref, gg_ref, dxb_ref):
        @pl.when(pl.program_id(0) == 0)
        def _():
            gg_ref[...] = jnp.zeros_like(gg_ref)
            loss_ref[...] = jnp.zeros_like(loss_ref)

        xv = x_ref[...]
        gv = g_ref[...]
        rstd = lax.rsqrt(jnp.mean(xv * xv, axis=-1, keepdims=True) + EPS)
        xhat = xv * rstd
        err = xhat * gv - t_ref[...]
        loss_ref[...] += 0.5 * jnp.sum(jnp.mean(err * err, axis=-1, keepdims=True), axis=0, keepdims=True)
        dy = err * (1.0 / D)
        dxhat = dy * gv
        proj = jnp.mean(dxhat * xhat, axis=-1, keepdims=True)
        dx = rstd * (dxhat - xhat * proj)
        dx_ref[...] = dx
        dxb_ref[...] = dx.astype(BF16)
        gg_ref[...] += jnp.sum(dy * xhat, axis=0, keepdims=True)

    row = pl.BlockSpec((br, D), lambda i: (i, 0))
    vec = pl.BlockSpec((1, D), lambda i: (0, 0))
    return _pcall(
        kernel, name=name, dims=("arbitrary",),
        out_shape=(jax.ShapeDtypeStruct((T, D), F32), jax.ShapeDtypeStruct((1, 1), F32),
                   jax.ShapeDtypeStruct((1, D), F32), jax.ShapeDtypeStruct((T, D), BF16)),
        grid=(T // br,),
        in_specs=[row, vec, row],
        out_specs=(row, pl.BlockSpec((1, 1), lambda i: (0, 0)), vec, row),
    )(x2, gf, target)


def _scan_rows(v, reverse):
    n = v.shape[0]
    row = lax.broadcasted_iota(jnp.int32, v.shape, 0)
    s = 1
    while s < n:
        if reverse:
            v = v + jnp.where(row < n - s, pltpu.roll(v, n - s, 0), 0.0)
        else:
            v = v + jnp.where(row >= s, pltpu.roll(v, s, 0), 0.0)
        s *= 2
    return v


def _fox_gate_fwd(f, bias, *, name):
    T = f.shape[0]
    tb = _pick(T, 512, 8)

    def kernel(f_ref, b_ref, c_ref, s_ref, carry):
        @pl.when(pl.program_id(0) == 0)
        def _():
            carry[...] = jnp.zeros_like(carry)

        fa = f_ref[...] + b_ref[...]
        z = jnp.exp(-jnp.abs(fa))
        log_f = jnp.minimum(fa, 0.0) - jnp.log(1.0 + z)
        s_ref[...] = jnp.where(fa >= 0, z, 1.0) / (1.0 + z)
        c = _scan_rows(log_f, False) + carry[...]
        c_ref[...] = c
        carry[...] = c[tb - 1:tb, :]

    blk = pl.BlockSpec((tb, LANES), lambda i: (i, 0))
    return _pcall(
        kernel, name=name, dims=("arbitrary",),
        out_shape=(jax.ShapeDtypeStruct((T, LANES), F32), jax.ShapeDtypeStruct((T, LANES), F32)),
        grid=(T // tb,),
        in_specs=[blk, pl.BlockSpec((1, LANES), lambda i: (0, 0))],
        out_specs=(blk, blk),
        scratch_shapes=[pltpu.VMEM((1, LANES), F32)],
    )(f, bias)


def _fox_gate_bwd(dc, sneg, *, name):
    T = dc.shape[0]
    tb = _pick(T, 512, 8)
    nb = T // tb

    def kernel(dc_ref, s_ref, df_ref, gb_ref, carry):
        @pl.when(pl.program_id(0) == 0)
        def _():
            carry[...] = jnp.zeros_like(carry)
            gb_ref[...] = jnp.zeros_like(gb_ref)

        dlog = _scan_rows(dc_ref[...], True) + carry[...]
        carry[...] = dlog[0:1, :]
        dfa = dlog * s_ref[...]
        df_ref[...] = dfa.astype(BF16)
        gb_ref[...] += jnp.sum(dfa, axis=0, keepdims=True)

    blk = pl.BlockSpec((tb, LANES), lambda i: (nb - 1 - i, 0))
    return _pcall(
        kernel, name=name, dims=("arbitrary",),
        out_shape=(jax.ShapeDtypeStruct((T, LANES), BF16), jax.ShapeDtypeStruct((1, LANES), F32)),
        grid=(nb,),
        in_specs=[blk, blk],
        out_specs=(blk, pl.BlockSpec((1, LANES), lambda i: (0, 0))),
        scratch_shapes=[pltpu.VMEM((1, LANES), F32)],
    )(dc, sneg)


def _fox_scores(q, k, cq, ck, diagonal):
    s = _dot(q, k, "nt") * (DH_A ** -0.5) + cq - ck
    if diagonal:
        row = lax.broadcasted_iota(jnp.int32, s.shape, 0)
        col = lax.broadcasted_iota(jnp.int32, s.shape, 1)
        s = jnp.where(col <= row, s, NEG)
    return s


def _fox_fwd(proj, c_col, c_row, n_heads, *, deps=(), name):
    T = proj.shape[0]
    H = n_heads
    tq = tk = _pick(T, FOX_TILE, 128)
    nq = T // tq

    def kernel(q_ref, k_ref, v_ref, cq_ref, ck_ref, o_ref, lse_ref, m_sc, l_sc, acc_sc):
        i, j = pl.program_id(1), pl.program_id(2)

        @pl.when(j == 0)
        def _():
            m_sc[...] = jnp.full_like(m_sc, NEG)
            l_sc[...] = jnp.zeros_like(l_sc)
            acc_sc[...] = jnp.zeros_like(acc_sc)

        def tile(diagonal):
            s = _fox_scores(q_ref[...], k_ref[...], cq_ref[...], ck_ref[...], diagonal)
            m_prev = m_sc[...]
            m_new = jnp.maximum(m_prev, jnp.max(s, axis=-1, keepdims=True))
            alpha = jnp.exp(m_prev - m_new)
            p = jnp.exp(s - m_new)
            l_sc[...] = alpha * l_sc[...] + jnp.sum(p, axis=-1, keepdims=True)
            acc_sc[...] = alpha * acc_sc[...] + _dot(p.astype(BF16), v_ref[...])
            m_sc[...] = m_new

        @pl.when(j < i)
        def _():
            tile(False)

        @pl.when(j == i)
        def _():
            tile(True)
            l = l_sc[...]
            o_ref[...] = (acc_sc[...] / l).astype(BF16)
            lse_ref[...] = m_sc[...] + jnp.log(l)

    return _pcall(
        kernel, name=name, dims=("parallel", "parallel", "arbitrary"), deps=deps,
        out_shape=(jax.ShapeDtypeStruct((T, H * DH_A), BF16), jax.ShapeDtypeStruct((H, T, 1), F32)),
        grid=(H, nq, nq),
        in_specs=[
            pl.BlockSpec((tq, DH_A), lambda h, i, j: (i, h)),
            pl.BlockSpec((tk, DH_A), lambda h, i, j: (jnp.minimum(j, i), H + h)),
            pl.BlockSpec((tk, DH_A), lambda h, i, j: (jnp.minimum(j, i), 2 * H + h)),
            pl.BlockSpec((None, tq, 1), lambda h, i, j: (h, i, 0)),
            pl.BlockSpec((None, 1, tk), lambda h, i, j: (h, 0, jnp.minimum(j, i))),
        ],
        out_specs=(pl.BlockSpec((tq, DH_A), lambda h, i, j: (i, h)),
                   pl.BlockSpec((None, tq, 1), lambda h, i, j: (h, i, 0))),
        scratch_shapes=[pltpu.VMEM((tq, 1), F32), pltpu.VMEM((tq, 1), F32), pltpu.VMEM((tq, DH_A), F32)],
    )(proj, proj, proj, c_col, c_row)


def _fox_bwd(proj, o, do, lse, c_col, c_row, n_heads, *, name):
    T = proj.shape[0]
    H = n_heads
    tq = tk = _pick(T, FOX_TILE, 128)
    nq = T // tq
    scale = DH_A ** -0.5

    def kernel(q_ref, k_ref, v_ref, do_ref, lse_ref, o_ref, cq_ref, ck_ref,
               dq_ref, dk_ref, dv_ref, dc_ref, dcq_ref, dk_sc, dv_sc, dc_sc):
        j, i = pl.program_id(1), pl.program_id(2)

        @pl.when((j == 0) & (i == 0))
        def _():
            dq_ref[...] = jnp.zeros_like(dq_ref)
            dcq_ref[...] = jnp.zeros_like(dcq_ref)

        @pl.when(i == 0)
        def _():
            dk_sc[...] = jnp.zeros_like(dk_sc)
            dv_sc[...] = jnp.zeros_like(dv_sc)
            dc_sc[...] = jnp.zeros_like(dc_sc)

        def tile(diagonal):
            q, k, v, dov = q_ref[...], k_ref[...], v_ref[...], do_ref[...]
            s = _fox_scores(q, k, cq_ref[...], ck_ref[...], diagonal)
            p = jnp.exp(s - lse_ref[...])
            dv_sc[...] += _dot(p.astype(BF16), dov, "tn")
            dp = _dot(dov, v, "nt")
            delta = jnp.sum(dov.astype(F32) * o_ref[...].astype(F32), axis=1, keepdims=True)
            ds = p * (dp - delta)
            dc_sc[...] -= jnp.sum(ds, axis=0, keepdims=True)
            dsb = ds.astype(BF16)
            dk_sc[...] += _dot(dsb, q, "tn") * scale
            rows = pl.ds(pl.multiple_of(i * tq, tq), tq)
            dq_ref[rows, :] += _dot(dsb, k) * scale
            dcq_ref[rows, :] += jnp.sum(ds, axis=1, keepdims=True)

        @pl.when(i > j)
        def _():
            tile(False)

        @pl.when(i == j)
        def _():
            tile(True)

        @pl.when(i == nq - 1)
        def _():
            dk_ref[...] = dk_sc[...].astype(BF16)
            dv_ref[...] = dv_sc[...].astype(BF16)
            dc_ref[...] = dc_sc[...]

    qi = lambda j, i: jnp.maximum(i, j)
    return _pcall(
        kernel, name=name, dims=("parallel", "arbitrary", "arbitrary"),
        out_shape=(jax.ShapeDtypeStruct((T, H * DH_A), F32), jax.ShapeDtypeStruct((T, H * DH_A), BF16),
                   jax.ShapeDtypeStruct((T, H * DH_A), BF16), jax.ShapeDtypeStruct((H, 1, T), F32),
                   jax.ShapeDtypeStruct((H, T, 1), F32)),
        grid=(H, nq, nq),
        in_specs=[
            pl.BlockSpec((tq, DH_A), lambda h, j, i: (qi(j, i), h)),
            pl.BlockSpec((tk, DH_A), lambda h, j, i: (j, H + h)),
            pl.BlockSpec((tk, DH_A), lambda h, j, i: (j, 2 * H + h)),
            pl.BlockSpec((tq, DH_A), lambda h, j, i: (qi(j, i), h)),
            pl.BlockSpec((None, tq, 1), lambda h, j, i: (h, qi(j, i), 0)),
            pl.BlockSpec((tq, DH_A), lambda h, j, i: (qi(j, i), h)),
            pl.BlockSpec((None, tq, 1), lambda h, j, i: (h, qi(j, i), 0)),
            pl.BlockSpec((None, 1, tk), lambda h, j, i: (h, 0, j)),
        ],
        out_specs=(pl.BlockSpec((T, DH_A), lambda h, j, i: (0, h)),
                   pl.BlockSpec((tk, DH_A), lambda h, j, i: (j, h)),
                   pl.BlockSpec((tk, DH_A), lambda h, j, i: (j, h)),
                   pl.BlockSpec((None, 1, tk), lambda h, j, i: (h, 0, j)),
                   pl.BlockSpec((None, T, 1), lambda h, j, i: (h, 0, 0))),
        scratch_shapes=[pltpu.VMEM((tk, DH_A), F32), pltpu.VMEM((tk, DH_A), F32), pltpu.VMEM((1, tk), F32)],
    )(proj, proj, proj, do, lse, o, c_col, c_row)


def _t5_bucket(dist):
    max_exact = NUM_BUCKETS // 2
    small = dist < max_exact
    large = max_exact + (np.log(np.maximum(dist, 1) / max_exact) / np.log(MAX_DISTANCE / max_exact)
                         * (NUM_BUCKETS - max_exact)).astype(np.int64)
    large = np.minimum(large, NUM_BUCKETS - 1)
    return np.where(small, dist, large)


def _bucket_onehot():
    ql = np.arange(SWA_BLOCK)[:, None]
    kl = np.arange(2 * SWA_BLOCK)[None, :]
    bucket = _t5_bucket(np.clip(ql + SWA_BLOCK - kl, 0, None))
    bucket = np.stack([bucket[:, :SWA_BLOCK], bucket[:, SWA_BLOCK:]], axis=0)
    onehot = (bucket.reshape(1, -1) == np.arange(NUM_BUCKETS)[:, None])
    return jnp.asarray(onehot, dtype=BF16)


def _split3(v):
    hi = v.astype(BF16)
    r1 = v - hi.astype(F32)
    mid = r1.astype(BF16)
    lo = (r1 - mid.astype(F32)).astype(BF16)
    return hi, mid, lo


def _bias_table(rel_bias, onehot, *, name):
    B, H = rel_bias.shape
    n = onehot.shape[1]
    bc = _pick(n, 8192, LANES)

    def kernel(rb_ref, oh_ref, o_ref):
        hi, mid, lo = _split3(rb_ref[...])
        oh = oh_ref[...]
        o_ref[...] = _dot(hi, oh, "tn") + _dot(mid, oh, "tn") + _dot(lo, oh, "tn")

    return _pcall(
        kernel, name=name, dims=("parallel",),
        out_shape=jax.ShapeDtypeStruct((H, n), F32), grid=(n // bc,),
        in_specs=[pl.BlockSpec((B, H), lambda i: (0, 0)), pl.BlockSpec((B, bc), lambda i: (0, i))],
        out_specs=pl.BlockSpec((H, bc), lambda i: (0, i)),
    )(rel_bias, onehot)


def _bias_table_bwd(dbias, onehot, *, name):
    H, n = dbias.shape
    B = onehot.shape[0]
    bc = _pick(n, 8192, LANES)

    def kernel(db_ref, oh_ref, o_ref):
        @pl.when(pl.program_id(0) == 0)
        def _():
            o_ref[...] = jnp.zeros_like(o_ref)

        hi, mid, lo = _split3(db_ref[...])
        oh = oh_ref[...]
        o_ref[...] += _dot(oh, hi, "nt") + _dot(oh, mid, "nt") + _dot(oh, lo, "nt")

    return _pcall(
        kernel, name=name, dims=("arbitrary",),
        out_shape=jax.ShapeDtypeStruct((B, H), F32), grid=(n // bc,),
        in_specs=[pl.BlockSpec((H, bc), lambda i: (0, i)), pl.BlockSpec((B, bc), lambda i: (0, i))],
        out_specs=pl.BlockSpec((B, H), lambda i: (0, 0)),
    )(dbias, onehot)


def _swa_probs(q2, kp, kc, bias_ref, lse, n, G):
    Q = SWA_BLOCK
    scale = DH_B ** -0.5
    sp = (_dot(q2, kp, "nt") * scale).reshape(G, Q, Q) + bias_ref[:, 0]
    sc = (_dot(q2, kc, "nt") * scale).reshape(G, Q, Q) + bias_ref[:, 1]
    row = lax.broadcasted_iota(jnp.int32, (G, Q, Q), 1)
    col = lax.broadcasted_iota(jnp.int32, (G, Q, Q), 2)
    vis_p = (col > row) & (n > 0)
    vis_c = col <= row
    sp = jnp.where(vis_p, sp, NEG)
    sc = jnp.where(vis_c, sc, NEG)
    if lse is None:
        return sp, sc, vis_p, vis_c
    pp = jnp.where(vis_p, jnp.exp(sp - lse), 0.0)
    pc = jnp.where(vis_c, jnp.exp(sc - lse), 0.0)
    return pp, pc


def _swa_fwd(q, k, v, bias, sinks, *, deps=(), name):
    HKV, G, T, _ = q.shape
    Q = SWA_BLOCK
    N = T // Q

    def kernel(q_ref, kp_ref, kc_ref, vp_ref, vc_ref, b_ref, s_ref, o_ref, lse_ref):
        n = pl.program_id(1)
        q2 = q_ref[...].reshape(G * Q, DH_B)
        sp, sc, vis_p, vis_c = _swa_probs(q2, kp_ref[...], kc_ref[...], b_ref, None, n, G)
        sink = s_ref[...][:, :, 0:1]
        m = jnp.maximum(jnp.maximum(jnp.max(sp, axis=-1, keepdims=True), jnp.max(sc, axis=-1, keepdims=True)), sink)
        pp = jnp.where(vis_p, jnp.exp(sp - m), 0.0)
        pc = jnp.where(vis_c, jnp.exp(sc - m), 0.0)
        denom = jnp.sum(pp, axis=-1, keepdims=True) + jnp.sum(pc, axis=-1, keepdims=True) + jnp.exp(sink - m)
        o = _dot(pp.reshape(G * Q, Q).astype(BF16), vp_ref[...]) + _dot(pc.reshape(G * Q, Q).astype(BF16), vc_ref[...])
        o_ref[...] = (o.reshape(G, Q, DH_B) / denom).astype(BF16)
        lse_ref[...] = m + jnp.log(denom)

    prev = lambda h, n: (h, jnp.maximum(n - 1, 0), 0)
    cur = lambda h, n: (h, n, 0)
    kv = lambda f: pl.BlockSpec((None, Q, DH_B), f)
    return _pcall(
        kernel, name=name, dims=("parallel", "parallel"), deps=deps,
        out_shape=(jax.ShapeDtypeStruct((HKV, G, T, DH_B), BF16), jax.ShapeDtypeStruct((HKV, G, T, 1), F32)),
        grid=(HKV, N),
        in_specs=[pl.BlockSpec((None, G, Q, DH_B), lambda h, n: (h, 0, n, 0)),
                  kv(prev), kv(cur), kv(prev), kv(cur),
                  pl.BlockSpec((None, G, 2, Q, Q), lambda h, n: (h, 0, 0, 0, 0)),
                  pl.BlockSpec((None, G, 1, LANES), lambda h, n: (h, 0, 0, 0))],
        out_specs=(pl.BlockSpec((None, G, Q, DH_B), lambda h, n: (h, 0, n, 0)),
                   pl.BlockSpec((None, G, Q, 1), lambda h, n: (h, 0, n, 0))),
    )(q, k, k, v, v, bias, sinks)


def _swa_bwd(q, k, v, o, do, lse, bias, sinks, *, name):
    HKV, G, T, _ = q.shape
    Q = SWA_BLOCK
    N = T // Q
    scale = DH_B ** -0.5

    def kernel(q_ref, kp_ref, kc_ref, vp_ref, vc_ref, o_ref, do_ref, lse_ref, b_ref, s_ref,
               dq_ref, dk_ref, dv_ref, db_ref, ds_ref, ck_sc, cv_sc):
        n = pl.program_id(1)

        @pl.when(n == 0)
        def _():
            db_ref[...] = jnp.zeros_like(db_ref)
            ds_ref[...] = jnp.zeros_like(ds_ref)
            ck_sc[...] = jnp.zeros_like(ck_sc)
            cv_sc[...] = jnp.zeros_like(cv_sc)

        @pl.when(n < N)
        def _():
            q2 = q_ref[...].reshape(G * Q, DH_B)
            do3 = do_ref[...]
            do2 = do3.reshape(G * Q, DH_B)
            kp, kc, vp, vc = kp_ref[...], kc_ref[...], vp_ref[...], vc_ref[...]
            lse_v = lse_ref[...]
            pp, pc = _swa_probs(q2, kp, kc, b_ref, lse_v, n, G)
            delta = jnp.sum(do3.astype(F32) * o_ref[...].astype(F32), axis=-1, keepdims=True)
            dsp = pp * (_dot(do2, vp, "nt").reshape(G, Q, Q) - delta)
            dsc = pc * (_dot(do2, vc, "nt").reshape(G, Q, Q) - delta)
            p_sink = jnp.exp(s_ref[...][:, :, 0:1] - lse_v)
            ds_ref[...] += jnp.broadcast_to(-jnp.sum(p_sink * delta, axis=1, keepdims=True), (G, 1, LANES))
            db_ref[:, 0] += dsp
            db_ref[:, 1] += dsc
            dsp2 = dsp.reshape(G * Q, Q).astype(BF16)
            dsc2 = dsc.reshape(G * Q, Q).astype(BF16)
            dq = (_dot(dsp2, kp) + _dot(dsc2, kc)) * scale
            dq_ref[...] = dq.reshape(G, Q, DH_B).astype(BF16)
            pp2 = pp.reshape(G * Q, Q).astype(BF16)
            pc2 = pc.reshape(G * Q, Q).astype(BF16)
            dk_ref[...] = (ck_sc[...] + _dot(dsp2, q2, "tn") * scale).astype(BF16)
            dv_ref[...] = (cv_sc[...] + _dot(pp2, do2, "tn")).astype(BF16)
            ck_sc[...] = _dot(dsc2, q2, "tn") * scale
            cv_sc[...] = _dot(pc2, do2, "tn")

        @pl.when(n == N)
        def _():
            dk_ref[...] = ck_sc[...].astype(BF16)
            dv_ref[...] = cv_sc[...].astype(BF16)

    qn = lambda n: jnp.minimum(n, N - 1)
    prev = lambda h, n: (h, jnp.maximum(qn(n) - 1, 0), 0)
    cur = lambda h, n: (h, qn(n), 0)
    out_kv = lambda h, n: (h, jnp.maximum(n - 1, 0), 0)
    kv = lambda f: pl.BlockSpec((None, Q, DH_B), f)
    qspec = pl.BlockSpec((None, G, Q, DH_B), lambda h, n: (h, 0, qn(n), 0))
    bspec = pl.BlockSpec((None, G, 2, Q, Q), lambda h, n: (h, 0, 0, 0, 0))
    sspec = pl.BlockSpec((None, G, 1, LANES), lambda h, n: (h, 0, 0, 0))
    return _pcall(
        kernel, name=name, dims=("parallel", "arbitrary"),
        out_shape=(jax.ShapeDtypeStruct((HKV, G, T, DH_B), BF16), jax.ShapeDtypeStruct((HKV, T, DH_B), BF16),
                   jax.ShapeDtypeStruct((HKV, T, DH_B), BF16), jax.ShapeDtypeStruct((HKV, G, 2, Q, Q), F32),
                   jax.ShapeDtypeStruct((HKV, G, 1, LANES), F32)),
        grid=(HKV, N + 1),
        in_specs=[qspec, kv(prev), kv(cur), kv(prev), kv(cur), qspec, qspec,
                  pl.BlockSpec((None, G, Q, 1), lambda h, n: (h, 0, qn(n), 0)), bspec, sspec],
        out_specs=(qspec, kv(out_kv), kv(out_kv), bspec, sspec),
        scratch_shapes=[pltpu.VMEM((Q, DH_B), F32), pltpu.VMEM((Q, DH_B), F32)],
    )(q, k, k, v, v, o, do, lse, bias, sinks)


def _branch_mix(oa, ob, wa_t, wb_t, proj, gate_col, *, deps=(), name):
    T = oa.shape[0]
    D = wa_t.shape[0]
    bm, bn = _pick(T, 1024, 8), _pick(D, 512, LANES)
    assert gate_col % bn == 0
    ga0, gb0 = gate_col // bn, (gate_col + D) // bn

    def kernel(oa_ref, ob_ref, wa_ref, wb_ref, ga_ref, gb_ref, mix_ref, ya_ref, yb_ref):
        ya = _dot(oa_ref[...], wa_ref[...], "nt")
        yb = _dot(ob_ref[...], wb_ref[...], "nt")
        mix = _sigmoid(ga_ref[...].astype(F32)) * ya + _sigmoid(gb_ref[...].astype(F32)) * yb
        mix_ref[...] = mix.astype(BF16)
        ya_ref[...] = ya.astype(BF16)
        yb_ref[...] = yb.astype(BF16)

    out = pl.BlockSpec((bm, bn), lambda i, j: (i, j))
    return _pcall(
        kernel, name=name, dims=("parallel", "parallel"), deps=deps,
        out_shape=tuple(jax.ShapeDtypeStruct((T, D), BF16) for _ in range(3)),
        grid=(T // bm, D // bn),
        in_specs=[pl.BlockSpec((bm, oa.shape[1]), lambda i, j: (i, 0)),
                  pl.BlockSpec((bm, ob.shape[1]), lambda i, j: (i, 0)),
                  pl.BlockSpec((bn, oa.shape[1]), lambda i, j: (j, 0)),
                  pl.BlockSpec((bn, ob.shape[1]), lambda i, j: (j, 0)),
                  pl.BlockSpec((bm, bn), lambda i, j: (i, ga0 + j)),
                  pl.BlockSpec((bm, bn), lambda i, j: (i, gb0 + j))],
        out_specs=(out, out, out),
    )(oa, ob, wa_t, wb_t, proj, proj)


def _mix_bwd_tile(dmixed, ya, yb, ga, gb):
    sa = _sigmoid(ga.astype(F32))
    sb = _sigmoid(gb.astype(F32))
    return (dmixed * sa, dmixed * sb,
            dmixed * ya.astype(F32) * sa * (1.0 - sa), dmixed * yb.astype(F32) * sb * (1.0 - sb))


def _ffn_up(h, wg_t, wu_t, *, name):
    T, D = h.shape
    FP = wg_t.shape[0]
    bm, bn = _pick(T, 1024, 8), _pick(FP, 512, LANES)

    def kernel(h_ref, wg_ref, wu_ref, g_ref, u_ref, hid_ref):
        hv = h_ref[...]
        g = _dot(hv, wg_ref[...], "nt")
        u = _dot(hv, wu_ref[...], "nt")
        g_ref[...] = g.astype(BF16)
        u_ref[...] = u.astype(BF16)
        hid_ref[...] = (g * _sigmoid(g) * u).astype(BF16)

    out = pl.BlockSpec((bm, bn), lambda i, j: (i, j))
    wspec = pl.BlockSpec((bn, D), lambda i, j: (j, 0))
    return _pcall(
        kernel, name=name, dims=("parallel", "parallel"),
        out_shape=tuple(jax.ShapeDtypeStruct((T, FP), BF16) for _ in range(3)),
        grid=(T // bm, FP // bn),
        in_specs=[pl.BlockSpec((bm, D), lambda i, j: (i, 0)), wspec, wspec],
        out_specs=(out, out, out),
    )(h, wg_t, wu_t)


def _swiglu_bwd_tile(dhid, gate, up):
    g = gate.astype(F32)
    sg = _sigmoid(g)
    return dhid * up.astype(F32) * sg * (1.0 + g * (1.0 - sg)), dhid * g * sg


def _adamw_update(w, g, m, v):
    nm = ADAM_B1 * m + (1.0 - ADAM_B1) * g
    nv = ADAM_B2 * v + (1.0 - ADAM_B2) * (g * g)
    m_hat = nm / (1.0 - ADAM_B1 ** ADAM_STEP)
    v_hat = nv / (1.0 - ADAM_B2 ** ADAM_STEP)
    return -ADAM_LR * (m_hat / (jnp.sqrt(v_hat) + ADAM_EPS) + ADAM_WD * w), nm, nv


def _adamw(w, g, m, v, *, deps=(), name):
    Rw, Cw = w.shape
    br = _pick(Rw, max(8, (1 << 19) // Cw // 8 * 8), 8)

    def kernel(w_ref, g_ref, m_ref, v_ref, d_ref, nm_ref, nv_ref):
        d_ref[...], nm_ref[...], nv_ref[...] = _adamw_update(w_ref[...], g_ref[...], m_ref[...], v_ref[...])

    blk = pl.BlockSpec((br, Cw), lambda i: (i, 0))
    return _pcall(
        kernel, name=name, dims=("parallel",), deps=deps,
        out_shape=tuple(jax.ShapeDtypeStruct((Rw, Cw), F32) for _ in range(3)),
        grid=(Rw // br,), in_specs=[blk, blk, blk, blk], out_specs=(blk, blk, blk),
    )(w, g, m, v)


def _sum_adamw(land, w, m, v, *, name):
    _, R, C = land.shape
    bc = _pick(C, max(LANES, (1 << 18) // R // LANES * LANES), LANES)

    def kernel(l_ref, w_ref, m_ref, v_ref, g_ref, d_ref, nm_ref, nv_ref):
        g = l_ref[0].astype(F32)
        for k in range(1, N_DEV):
            g = g + l_ref[k].astype(F32)
        g_ref[...] = g
        d_ref[...], nm_ref[...], nv_ref[...] = _adamw_update(w_ref[...], g, m_ref[...], v_ref[...])

    blk = pl.BlockSpec((R, bc), lambda i: (0, i))
    return _pcall(
        kernel, name=name, dims=("parallel",),
        out_shape=tuple(jax.ShapeDtypeStruct((R, C), F32) for _ in range(4)),
        grid=(C // bc,),
        in_specs=[pl.BlockSpec((N_DEV, R, bc), lambda i: (0, 0, i)), blk, blk, blk],
        out_specs=(blk, blk, blk, blk),
    )(land, w, m, v)


def _my_place():
    return lax.axis_index("x"), lax.axis_index("y"), lax.axis_index("c")


def _flip(v, bit):
    return 1 - v if bit else v


HBM_SPEC = pl.BlockSpec(memory_space=pltpu.HBM)
SEM_SPEC = pl.BlockSpec(memory_space=pltpu.SEMAPHORE)
SPLIT_EFFECT = pltpu.SideEffectType.DATAFLOW_SIDE_EFFECTING
N_PEERS = N_DEV - 1


def _hbm(a):
    return pltpu.with_memory_space_constraint(a, pltpu.HBM)


def _peer(x, y, c, k):
    return _flip(x, k & 4), _flip(y, k & 2), _flip(c, k & 1)


def _row_block(ref, idx, R):
    if len(ref.shape) == 3:
        return ref.at[idx]
    return ref.at[pl.ds(pl.multiple_of(idx * R, BF16_TILE_ROWS), R), :]


def _remote_pair(src, dst, arrives, send_sem, recv_sem, to):
    send = pltpu.make_async_remote_copy(src_ref=src, dst_ref=dst, send_sem=send_sem, recv_sem=recv_sem,
                                        device_id=to, device_id_type=MESH)
    recv = pltpu.make_async_remote_copy(src_ref=src, dst_ref=arrives, send_sem=send_sem, recv_sem=recv_sem,
                                        device_id=to, device_id_type=MESH)
    return send, recv


def _split_call(body, arrays, sems, n_through, out_sems, after, token, *, name):
    out_shape = [pltpu.SemaphoreType.DMA((n,)) for n in out_sems]
    out_shape += [pltpu.HBM(a.shape, a.dtype) for a in arrays[:n_through]]
    out_specs = [SEM_SPEC] * len(out_sems) + [HBM_SPEC] * n_through
    if token:
        out_shape.append(jax.ShapeDtypeStruct((8, LANES), F32))
        out_specs.append(pl.BlockSpec(memory_space=pltpu.VMEM))
    out = pl.pallas_call(
        body, name=name, out_shape=tuple(out_shape),
        in_specs=tuple([HBM_SPEC] * len(arrays) + [SEM_SPEC] * len(sems) + [ANY]),
        out_specs=tuple(out_specs),
        input_output_aliases={t: len(out_sems) + t for t in range(n_through)},
        compiler_params=pltpu.CompilerParams(has_side_effects=SPLIT_EFFECT),
    )(*[_hbm(a) for a in arrays], *sems, after)
    n_s = len(out_sems)
    return tuple(out[:n_s]), tuple(out[n_s:n_s + n_through]), (out[-1] if token else None)


def _scatter_copies(src_ref, land_ref, sems, R):
    send_sems, recv_sems, local_sem = sems
    x, y, c = _my_place()
    local = pltpu.make_async_copy(_row_block(src_ref, 4 * x + 2 * y + c, R), land_ref.at[0], local_sem.at[0])
    remote = []
    for k in range(1, N_DEV):
        px, py, pc = _peer(x, y, c, k)
        remote.append(_remote_pair(_row_block(src_ref, 4 * px + 2 * py + pc, R), land_ref.at[k], land_ref.at[k],
                                   send_sems.at[k - 1], recv_sems.at[k - 1], (px, py, pc)))
    return local, remote


def _scatter_start(g, R, after, *, name):
    land = lax.empty((N_DEV, R, g.shape[-1]), g.dtype)

    def body(src_ref, land_ref, after_ref, send_sems, recv_sems, local_sem, src_thru, land_thru, token):
        local, remote = _scatter_copies(src_ref, land_ref, (send_sems, recv_sems, local_sem), R)
        local.start()
        for send, _ in remote:
            send.start()
        token[...] = jnp.zeros_like(token)

    sems, (src_thru, land_thru), token = _split_call(body, [g, land], [], 2, (N_PEERS, N_PEERS, 1), after, True, name=name)
    return sems, src_thru, land_thru, token, R


def _scatter_finish(handle, after, *, name):
    sems, src_thru, land_thru, _, R = handle

    def body(src_ref, land_ref, send_sems, recv_sems, local_sem, after_ref, src_dead, got_ref):
        local, remote = _scatter_copies(src_ref, land_ref, (send_sems, recv_sems, local_sem), R)
        for send, recv in remote:
            send.wait_send()
            recv.wait_recv()
        local.wait()

    return _split_call(body, [src_thru, land_thru], sems, 2, (), after, False, name=name)[1][1]


def _gather_copies(src_ref, land_ref, sems, R):
    send_sems, recv_sems, local_sem = sems
    x, y, c = _my_place()
    mine = _row_block(land_ref, 4 * x + 2 * y + c, R)
    local = pltpu.make_async_copy(src_ref, mine, local_sem.at[0])
    peers = [(x, y, 1 - c), (1 - x, y, c), (x, 1 - y, c), (1 - x, 1 - y, c)]
    remote = [_remote_pair(src_ref, mine, _row_block(land_ref, 4 * px + 2 * py + pc, R),
                           send_sems.at[k], recv_sems.at[k], (px, py, pc)) for k, (px, py, pc) in enumerate(peers)]
    return local, remote


def _forward_copies(land_ref, sems, R):
    send_sems, recv_sems = sems
    x, y, c = _my_place()
    pairs = []
    for j, (px, py) in enumerate([(1 - x, y), (x, 1 - y), (1 - x, 1 - y)]):
        blk = _row_block(land_ref, 4 * px + 2 * py + c, R)
        pairs.append(_remote_pair(blk, blk, _row_block(land_ref, 4 * px + 2 * py + (1 - c), R),
                                  send_sems.at[j], recv_sems.at[j], (x, y, 1 - c)))
    return pairs


def _gather_start(shard, after, *, pad_rows=0, name):
    R, C = shard.shape
    flat = R % BF16_TILE_ROWS == 0
    assert flat or pad_rows == 0
    land = lax.empty((N_DEV * R + pad_rows, C) if flat else (N_DEV, R, C), shard.dtype)
    if pad_rows:
        land = lax.dynamic_update_slice(land, jnp.zeros((pad_rows, C), shard.dtype), (N_DEV * R, 0))

    def body(src_ref, land_ref, after_ref, send_sems, recv_sems, local_sem, src_thru, land_thru, token):
        local, remote = _gather_copies(src_ref, land_ref, (send_sems, recv_sems, local_sem), R)
        local.start()
        for send, _ in remote:
            send.start()
        token[...] = jnp.zeros_like(token)

    sems, (src_thru, land_thru), token = _split_call(body, [shard, land], [], 2, (4, 4, 1), after, True, name=name)
    return sems, src_thru, land_thru, token, R


def _gather_forward(handle, after, *, name):
    sems, src_thru, land_thru, _, R = handle

    def arrived(land_ref, src_ref, send_sems, recv_sems, local_sem, after_ref, land_out):
        _, remote = _gather_copies(src_ref, land_ref, (send_sems, recv_sems, local_sem), R)
        for _, recv in remote[1:]:
            recv.wait_recv()

    def pass_on(land_ref, after_ref, send_sems, recv_sems, land_out, token):
        for send, _ in _forward_copies(land_ref, (send_sems, recv_sems), R):
            send.start()
        token[...] = jnp.zeros_like(token)

    _, (land1,), _ = _split_call(arrived, [land_thru, src_thru], sems, 1, (), after, False, name=name + "_arrived")
    sems2, (land2,), token = _split_call(pass_on, [land1], [], 1, (3, 3), src_thru, True, name=name + "_pass_on")
    return sems, sems2, src_thru, land2, token, R


def _gather_finish(handle2, after, *, name):
    sems, sems2, src_thru, land_thru, _, R = handle2

    def body(land_ref, src_ref, send_sems, recv_sems, local_sem, send2, recv2, after_ref, land_out):
        local, remote = _gather_copies(src_ref, land_ref, (send_sems, recv_sems, local_sem), R)
        for send, _ in remote:
            send.wait_send()
        remote[0][1].wait_recv()
        local.wait()
        for send, recv in _forward_copies(land_ref, (send2, recv2), R):
            send.wait_send()
            recv.wait_recv()

    return _split_call(body, [land_thru, src_thru], [*sems, *sems2], 1, (), after, False, name=name)[1][0]


def _all_reduce_small(v, *, name):
    rows, C = v.shape

    def body(v_ref, o_ref, land_ref, send_sems, recv_sems):
        x, y, c = _my_place()
        me = 4 * x + 2 * y + c
        copies = []
        for k in range(1, N_DEV):
            peer = (_flip(x, k & 4), _flip(y, k & 2), _flip(c, k & 1))
            copies.append(pltpu.make_async_remote_copy(
                src_ref=v_ref, dst_ref=land_ref.at[me],
                send_sem=send_sems.at[k - 1], recv_sem=recv_sems.at[k - 1], device_id=peer, device_id_type=MESH))
        for cp in copies:
            cp.start()
        land_ref[me] = v_ref[...]
        for k in range(1, N_DEV):
            peer_idx = 4 * _flip(x, k & 4) + 2 * _flip(y, k & 2) + _flip(c, k & 1)
            pltpu.make_async_remote_copy(
                src_ref=v_ref, dst_ref=land_ref.at[peer_idx],
                send_sem=send_sems.at[k - 1], recv_sem=recv_sems.at[k - 1],
                device_id=(x, y, c), device_id_type=MESH).wait_recv()
        for cp in copies:
            cp.wait_send()
        acc = land_ref[0]
        for s in range(1, N_DEV):
            acc = acc + land_ref[s]
        o_ref[...] = acc

    vm = pl.BlockSpec(memory_space=pltpu.VMEM)
    return _comm_call(
        body, name=name,
        out_shape=jax.ShapeDtypeStruct((rows, C), F32),
        in_specs=[vm], out_specs=vm,
        scratch_shapes=[pltpu.VMEM((N_DEV, rows, C), F32),
                        pltpu.SemaphoreType.DMA((7,)), pltpu.SemaphoreType.DMA((7,))],
    )(v)


def kernel(x, norm1_g, w_in, b_forget, attn_sinks, rel_bias, w_branch_a, w_branch_b, w_out, norm2_g, w_ffn_gate, w_ffn_up, w_ffn_down, final_g, loss_target, m_norm1_g, m_w_in, m_b_forget, m_attn_sinks, m_rel_bias, m_w_branch_a, m_w_branch_b, m_w_out, m_norm2_g, m_w_ffn_gate, m_w_ffn_up, m_w_ffn_down, m_final_g, v_norm1_g, v_w_in, v_b_forget, v_attn_sinks, v_rel_bias, v_w_branch_a, v_w_branch_b, v_w_out, v_norm2_g, v_w_ffn_gate, v_w_ffn_up, v_w_ffn_down, v_final_g):
    xs = x[0]
    T, D = xs.shape
    H_A, H_B = b_forget.shape[-1], attn_sinks.shape[-1]
    WA, QB = H_A * DH_A, H_B * DH_B
    R_IN = w_in.shape[-1]
    W_IN = N_DEV * R_IN
    KB = (W_IN - 3 * WA - H_A - QB - 2 * D) // 2
    HKV = KB // DH_B
    G = H_B // HKV
    N_BIG = W_IN - H_A
    GATE_COL = 3 * WA + QB + 2 * KB
    R_F = w_ffn_gate.shape[-1]
    F = N_DEV * R_F
    FP = _round_up(F, 512)
    assert H_A <= LANES and T % SWA_BLOCK == 0

    win_s = w_in[0].T.astype(BF16)
    wa_s = w_branch_a[0].T.astype(BF16)
    wb_s = w_branch_b[0].T.astype(BF16)
    wout_s = w_out[0].astype(BF16)
    wg_s = w_ffn_gate[0].T.astype(BF16)
    wu_s = w_ffn_up[0].T.astype(BF16)
    wd_s = w_ffn_down[0].astype(BF16)

    ag_win = _gather_start(win_s, norm1_g, name="ag_start_w_in")
    ag_wa = _gather_start(wa_s, ag_win[3], name="ag_start_w_branch_a")
    ag_wb = _gather_start(wb_s, ag_wa[3], name="ag_start_w_branch_b")
    ag_wout = _gather_start(wout_s, ag_wb[3], name="ag_start_w_out")
    ag_wg = _gather_start(wg_s, ag_wout[3], pad_rows=FP - F, name="ag_start_w_ffn_gate")
    ag_wu = _gather_start(wu_s, ag_wg[3], pad_rows=FP - F, name="ag_start_w_ffn_up")
    ag_wd = _gather_start(wd_s, ag_wu[3], pad_rows=FP - F, name="ag_start_w_ffn_down")

    BM = 1024
    bn_big = _pick(N_BIG, 768, LANES)

    rows_first = lambda *a: tuple(t[0].T for t in a)
    win_rows = rows_first(w_in, m_w_in, v_w_in)
    h1, rstd1 = _rms_fwd(xs, norm1_g, deps=[ag_wd[3], *win_rows], name="rms1_fwd")
    fw_win = _gather_forward(ag_win, h1, name="ag_w_in")
    win_all = _gather_finish(fw_win, fw_win[4], name="ag_wait_w_in").reshape(W_IN, D)
    f_rows = (3 * WA, H_A)
    w_f = jnp.pad(win_all[3 * WA:3 * WA + H_A], ((0, LANES - H_A), (0, 0)))
    proj = _matmul([(h1, win_all)], "nt", bm=BM, bn=bn_big, b_hole=f_rows, name="proj_fwd")
    f_logit = _matmul([(h1, w_f)], "nt", bm=BM, bn=LANES, out_dtype=F32, name="forget_fwd")
    b_pad = jnp.pad(b_forget, ((0, 0), (0, LANES - H_A)))
    c_all, sneg = _fox_gate_fwd(f_logit, b_pad, name="fox_gate_fwd")
    c_heads = c_all[:, :H_A].T
    c_col, c_row = c_heads[:, :, None], c_heads[:, None, :]
    fw_wa = _gather_forward(ag_wa, proj, name="ag_w_branch_a")
    fw_wb = _gather_forward(ag_wb, fw_wa[4], name="ag_w_branch_b")
    fw_wout = _gather_forward(ag_wout, fw_wb[4], name="ag_w_out")
    oa, lse_a = _fox_fwd(proj, c_col, c_row, H_A, deps=[fw_wout[4]], name="fox_fwd")
    fw_wg = _gather_forward(ag_wg, oa, name="ag_w_ffn_gate")
    fw_wu = _gather_forward(ag_wu, fw_wg[4], name="ag_w_ffn_up")

    def heads_q(a):
        return a.reshape(T, HKV, G, DH_B).transpose(1, 2, 0, 3)

    def heads_kv(a):
        return a.reshape(T, HKV, DH_B).transpose(1, 0, 2)

    qb = heads_q(proj[:, 3 * WA:3 * WA + QB])
    kb = heads_kv(proj[:, 3 * WA + QB:3 * WA + QB + KB])
    vb = heads_kv(proj[:, 3 * WA + QB + KB:GATE_COL])
    onehot = _bucket_onehot()
    bias = _bias_table(rel_bias, onehot, name="rel_bias_table").reshape(HKV, G, 2, SWA_BLOCK, SWA_BLOCK)
    sinks_b = jnp.broadcast_to(attn_sinks.reshape(HKV, G, 1, 1), (HKV, G, 1, LANES))
    ob4, lse_b = _swa_fwd(qb, kb, vb, bias, sinks_b, deps=[fw_wu[4]], name="swa_fwd")
    ob = ob4.transpose(2, 0, 1, 3).reshape(T, QB)

    wa_t = _gather_finish(fw_wa, ob, name="ag_wait_w_branch_a")
    wb_t = _gather_finish(fw_wb, wa_t, name="ag_wait_w_branch_b")
    mixed, ya, yb = _branch_mix(oa, ob, wa_t, wb_t, proj, GATE_COL, name="branch_mix")
    fw_wd = _gather_forward(ag_wd, mixed, name="ag_w_ffn_down")
    wout = _gather_finish(fw_wout, fw_wd[4], name="ag_wait_w_out")
    x1 = _matmul([(mixed, wout)], "nn", bm=BM, bn=512, out_dtype=F32, addend=xs, name="out_proj_fwd")

    h2, rstd2 = _rms_fwd(x1, norm2_g, name="rms2_fwd")
    wg_t = _gather_finish(fw_wg, h2, name="ag_wait_w_ffn_gate")
    wu_t = _gather_finish(fw_wu, wg_t, name="ag_wait_w_ffn_up")
    gate, up, hidden = _ffn_up(h2, wg_t, wu_t, name="ffn_up")
    wd = _gather_finish(fw_wd, hidden, name="ag_wait_w_ffn_down")
    tk_f = _pick(FP, 2816, LANES)
    x2 = _matmul([(hidden, wd)], "nn", bm=BM, bn=1024, tk=tk_f, out_dtype=F32, addend=x1, name="ffn_down_fwd")

    dx2, loss_part, g_final, dx2_b = _final_loss(x2, final_g.reshape(1, D), loss_target[0], name="final_loss")
    loss = lax.psum(loss_part[0, 0], MESH_AXES)

    g_wd = _matmul([(hidden, dx2_b)], "tn", bm=1024, bn=1024, name="ffn_down_bwd_w")
    rs_wd = _scatter_start(g_wd, R_F, loss_part, name="rs_start_w_ffn_down")
    dgate, dup = _matmul([(dx2_b, wd)], "nt", bm=BM, bn=512, extras=[(gate, 0), (up, 0)], epilogue=_swiglu_bwd_tile,
                         n_out=2, deps=[rs_wd[3]], name="ffn_down_bwd_x")
    g_wg = _matmul([(dgate, h2)], "tn", bm=1024, bn=1024, name="ffn_gate_bwd_w")
    rs_wg = _scatter_start(g_wg, R_F, rs_wd[3], name="rs_start_w_ffn_gate")
    g_wu = _matmul([(dup, h2)], "tn", bm=1024, bn=1024, deps=[rs_wg[3]], name="ffn_up_bwd_w")
    rs_wu = _scatter_start(g_wu, R_F, rs_wg[3], name="rs_start_w_ffn_up")
    dh2 = _matmul([(dgate, wg_t), (dup, wu_t)], "nn", bm=BM, bn=1024, tk=_pick(FP, 1408, LANES), out_dtype=F32,
                  deps=[rs_wu[3]], name="ffn_up_bwd_x")
    dx1, g_norm2, dx1_b = _rms_bwd(dx2, dh2, x1, rstd2, norm2_g, name="rms2_bwd")

    g_wout = _matmul([(mixed, dx1_b)], "tn", bm=1024, bn=1024, name="out_proj_bwd_w")
    rs_wout = _scatter_start(g_wout, D // N_DEV, rs_wu[3], name="rs_start_w_out")
    bn_mix = _pick(D, 256, LANES)
    assert GATE_COL % bn_mix == 0
    dya, dyb, dga, dgb = _matmul(
        [(dx1_b, wout)], "nt", bm=BM, bn=bn_mix, epilogue=_mix_bwd_tile, n_out=4, deps=[rs_wout[3]],
        extras=[(ya, 0), (yb, 0), (proj, GATE_COL // bn_mix), (proj, (GATE_COL + D) // bn_mix)], name="out_proj_bwd_x")
    g_wa = _matmul([(dya, oa)], "tn", bm=1024, bn=1024, name="branch_a_bwd_w")
    rs_wa = _scatter_start(g_wa, D // N_DEV, rs_wout[3], name="rs_start_w_branch_a")
    g_wb = _matmul([(dyb, ob)], "tn", bm=1024, bn=1024, deps=[rs_wa[3]], name="branch_b_bwd_w")
    rs_wb = _scatter_start(g_wb, D // N_DEV, rs_wa[3], name="rs_start_w_branch_b")
    doa = _matmul([(dya, wa_t)], "nn", bm=BM, bn=1024, deps=[rs_wb[3]], name="branch_a_bwd_x")
    dob = _matmul([(dyb, wb_t)], "nn", bm=BM, bn=1024, name="branch_b_bwd_x")

    dqa, dka, dva, dc_key, dc_query = _fox_bwd(proj, oa, doa, lse_a, c_col, c_row, H_A, name="fox_bwd")
    dc = jnp.pad((dc_key[:, 0, :] + dc_query[:, :, 0]).T, ((0, 0), (0, LANES - H_A)))
    df, g_bf = _fox_gate_bwd(dc, sneg, name="fox_gate_bwd")

    dqb4, dkb3, dvb3, dbias, dsinks = _swa_bwd(qb, kb, vb, ob4, heads_q(dob), lse_b, bias, sinks_b, name="swa_bwd")
    g_relb = _bias_table_bwd(dbias.reshape(H_B, -1), onehot, name="rel_bias_table_bwd")
    dqb = dqb4.transpose(2, 0, 1, 3).reshape(T, QB)
    dkb = dkb3.transpose(1, 0, 2).reshape(T, KB)
    dvb = dvb3.transpose(1, 0, 2).reshape(T, KB)

    dproj = jnp.concatenate([dqa.astype(BF16), dka, dva, dqb, dkb, dvb, dga, dgb], axis=1)
    g_win = _matmul([(dproj, h1)], "tn", bm=bn_big, bn=1024, out_hole=f_rows, name="proj_bwd_w")
    g_wf = _matmul([(df, h1)], "tn", bm=LANES, bn=1024, name="forget_bwd_w")
    g_win = lax.dynamic_update_slice(g_win, g_wf[:H_A], (3 * WA, 0)).reshape(N_DEV, R_IN, D)
    rs_win = _scatter_start(g_win, R_IN, rs_wb[3], name="rs_start_w_in")
    dh_f = _matmul([(df, w_f)], "nn", bm=BM, bn=1024, out_dtype=F32, deps=[rs_win[3]], name="forget_bwd_x")
    tk_big = _pick(math.gcd(3 * WA, N_BIG - 3 * WA), 1536, LANES)
    dh1 = _matmul([(dproj, win_all)], "nn", bm=BM, bn=1024, tk=tk_big, out_dtype=F32, addend=dh_f, b_hole=f_rows,
                  name="proj_bwd_x")
    grad_x, g_norm1, _ = _rms_bwd(dx1, dh1, xs, rstd1, norm1_g, name="rms1_bwd")

    def update(handle, after, wmv, transposed, nm):
        give = (lambda a: a.T[None]) if transposed else (lambda a: a[None])
        land = _scatter_finish(handle, after, name="rs_wait_" + nm)
        return tuple(give(o) for o in _sum_adamw(land, *wmv, name="adamw_" + nm))

    as_is = lambda *a: tuple(t[0] for t in a)
    big = {}
    big["w_ffn_down"] = update(rs_wd, grad_x, as_is(w_ffn_down, m_w_ffn_down, v_w_ffn_down), False, "w_ffn_down")
    big["w_ffn_gate"] = update(rs_wg, big["w_ffn_down"][1], rows_first(w_ffn_gate, m_w_ffn_gate, v_w_ffn_gate), True,
                               "w_ffn_gate")
    big["w_ffn_up"] = update(rs_wu, big["w_ffn_gate"][1], rows_first(w_ffn_up, m_w_ffn_up, v_w_ffn_up), True, "w_ffn_up")
    big["w_out"] = update(rs_wout, big["w_ffn_up"][1], as_is(w_out, m_w_out, v_w_out), False, "w_out")
    big["w_branch_a"] = update(rs_wa, big["w_out"][1], rows_first(w_branch_a, m_w_branch_a, v_w_branch_a), True,
                               "w_branch_a")
    big["w_branch_b"] = update(rs_wb, big["w_branch_a"][1], rows_first(w_branch_b, m_w_branch_b, v_w_branch_b), True,
                               "w_branch_b")

    n_rb = NUM_BUCKETS * H_B
    assert D >= n_rb and D >= H_A + H_B
    small = jnp.concatenate([
        g_norm1, g_norm2, g_final,
        jnp.pad(jnp.concatenate([g_bf[:, :H_A], dsinks[:, :, 0, 0].reshape(1, H_B)], axis=1), ((0, 0), (0, D - H_A - H_B))),
        jnp.pad(g_relb.reshape(1, n_rb), ((0, 0), (0, D - n_rb))),
        jnp.zeros((3, D), F32)], axis=0)
    small = _all_reduce_small(small, name="ar_small")
    gs = {
        "norm1_g": small[0:1], "norm2_g": small[1:2], "final_g": small[2],
        "b_forget": small[3:4, :H_A], "attn_sinks": small[3:4, H_A:H_A + H_B],
        "rel_bias": small[4, :n_rb].reshape(NUM_BUCKETS, H_B),
    }

    def adam_small(ws, gsm, ms, vs):
        def pack(parts):
            rows = [jnp.pad(p.reshape(1, -1), ((0, 0), (0, D - p.size))) for p in parts]
            return jnp.concatenate(rows + [jnp.ones((8 - len(rows), D), F32)], axis=0)
        d, nm_, nv_ = _adamw(pack(ws), pack(gsm), pack(ms), pack(vs), deps=[big["w_branch_b"][1]], name="adamw_small")
        unpack = lambda a: [a[i, :p.size].reshape(p.shape) for i, p in enumerate(ws)]
        return unpack(d), unpack(nm_), unpack(nv_)

    small_names = ["norm1_g", "b_forget", "attn_sinks", "rel_bias", "norm2_g", "final_g"]
    small_w = [norm1_g, b_forget, attn_sinks, rel_bias, norm2_g, final_g]
    small_g = [gs[n].reshape(w.shape) for n, w in zip(small_names, small_w)]
    small_m = [m_norm1_g, m_b_forget, m_attn_sinks, m_rel_bias, m_norm2_g, m_final_g]
    small_v = [v_norm1_g, v_b_forget, v_attn_sinks, v_rel_bias, v_norm2_g, v_final_g]
    sd, sm, sv = adam_small(small_w, small_g, small_m, small_v)
    for i, n in enumerate(small_names):
        big[n] = (small_g[i], sd[i], sm[i], sv[i])

    big["w_in"] = update(rs_win, sd[0], win_rows, True, "w_in")

    order = ["norm1_g", "w_in", "b_forget", "attn_sinks", "rel_bias", "w_branch_a", "w_branch_b", "w_out",
             "norm2_g", "w_ffn_gate", "w_ffn_up", "w_ffn_down", "final_g"]
    grads = [big[n][0] for n in order]
    deltas = [big[n][1] for n in order]
    new_m = [big[n][2] for n in order]
    new_v = [big[n][3] for n in order]
    return (loss, grad_x[None], *grads, *deltas, *new_m, *new_v)
```

```python
import math

import numpy as np
import jax
import jax.numpy as jnp
from jax import lax
from jax.experimental import pallas as pl
from jax.experimental.pallas import tpu as pltpu

N_DEV = 8
MESH_AXES = ("x", "y", "c")
DH_A = 128
DH_B = 64
SWA_BLOCK = 128
FOX_TILE = 1024
NUM_BUCKETS = 32
MAX_DISTANCE = 128
EPS = 1e-6
ADAM_LR = 0.001
ADAM_B1 = 0.9
ADAM_B2 = 0.999
ADAM_EPS = 1e-08
ADAM_WD = 0.01
ADAM_STEP = 10
NEG = -1e30
LANES = 128
BF16_TILE_ROWS = 16
V7X_VMEM_LIMIT = 56 * 1024 * 1024
F32 = jnp.float32
BF16 = jnp.bfloat16
MESH = pl.DeviceIdType.MESH
ANY = pl.BlockSpec(memory_space=pl.ANY)


def _round_up(n, m):
    return (n + m - 1) // m * m


def _pick(n, target, mult):
    best = None
    for d in range(mult, min(n, target) + 1, mult):
        if n % d == 0:
            best = d
    return n if best is None else best


def _pcall(kernel, *, name, in_specs, dims=None, deps=(), **kw):
    deps = tuple(deps)
    if deps:
        inner, n_in, n_dep = kernel, len(in_specs), len(deps)

        def kernel(*refs):
            inner(*refs[:n_in], *refs[n_in + n_dep:])

        in_specs = list(in_specs) + [ANY] * n_dep
    call = pl.pallas_call(
        kernel, name=name, in_specs=in_specs,
        compiler_params=pltpu.CompilerParams(dimension_semantics=dims, vmem_limit_bytes=V7X_VMEM_LIMIT),
        **kw)
    return lambda *operands: call(*operands, *deps)


def _comm_call(kernel, *, name, **kw):
    return pl.pallas_call(kernel, name=name, **kw)


def _sigmoid(v):
    return 1.0 / (1.0 + jnp.exp(-v))


_DIMS = {"nn": (((1,), (0,)), ((), ())), "nt": (((1,), (1,)), ((), ())), "tn": (((0,), (0,)), ((), ()))}


def _dot(a, b, form="nn"):
    return lax.dot_general(a, b, _DIMS[form], preferred_element_type=F32)


def _matmul(pairs, form, *, bm, bn, tk=None, out_dtype=BF16, addend=None, extras=(), epilogue=None, n_out=1, deps=(),
            b_hole=None, out_hole=None, name):
    a0, b0 = pairs[0]
    if form == "tn":
        K, M = a0.shape
    else:
        M, K = a0.shape
    N = b0.shape[0] if form == "nt" else b0.shape[1]
    if b_hole is not None and form == "nt":
        N -= b_hole[1]
    tk = K if tk is None else tk
    bm, bn = min(bm, M), min(bn, N)
    assert M % bm == 0 and N % bn == 0 and K % tk == 0, (name, M, N, K, bm, bn, tk)
    nk = K // tk
    n_pairs = len(pairs)
    has_add = addend is not None

    def stored_row(first, hole):
        return pl.multiple_of(first + jnp.where(first >= hole[0], hole[1], 0), BF16_TILE_ROWS)

    if form == "tn":
        a_spec = pl.BlockSpec((tk, bm), lambda i, j, k: (k, i))
    else:
        a_spec = pl.BlockSpec((bm, tk), lambda i, j, k: (i, k))
    if b_hole is not None:
        assert len(pairs) == 1 and b_hole[0] % (bn if form == "nt" else tk) == 0 and b_hole[1] % BF16_TILE_ROWS == 0
    if form == "nt":
        if b_hole is None:
            b_spec = pl.BlockSpec((bn, tk), lambda i, j, k: (j, k))
        else:
            b_spec = pl.BlockSpec((pl.Element(bn), pl.Element(tk)),
                                  lambda i, j, k: (stored_row(j * bn, b_hole), pl.multiple_of(k * tk, LANES)))
    elif b_hole is None or form == "tn":
        assert b_hole is None
        b_spec = pl.BlockSpec((tk, bn), lambda i, j, k: (k, j))
    else:
        b_spec = pl.BlockSpec((pl.Element(tk), pl.Element(bn)),
                              lambda i, j, k: (stored_row(k * tk, b_hole), pl.multiple_of(j * bn, LANES)))
    if out_hole is None:
        o_spec = pl.BlockSpec((bm, bn), lambda i, j, k: (i, j))
    else:
        assert out_hole[0] % bm == 0 and out_hole[1] % BF16_TILE_ROWS == 0 and epilogue is None and not has_add
        o_spec = pl.BlockSpec((pl.Element(bm), pl.Element(bn)),
                              lambda i, j, k: (stored_row(i * bm, out_hole), pl.multiple_of(j * bn, LANES)))
    m_stored = M + (out_hole[1] if out_hole is not None else 0)
    n_extra = len(extras)
    n_in = 2 * n_pairs + has_add + n_extra

    def kernel(*refs):
        ab = refs[:2 * n_pairs]
        c_ref = refs[2 * n_pairs] if has_add else None
        extra_refs = refs[2 * n_pairs + has_add:n_in]
        o_refs = refs[n_in:n_in + n_out]

        def partial_sum():
            acc = _dot(ab[0][...], ab[1][...], form)
            for p in range(1, n_pairs):
                acc = acc + _dot(ab[2 * p][...], ab[2 * p + 1][...], form)
            return acc

        def finish(acc):
            if has_add:
                acc = acc + c_ref[...].astype(F32)
            outs = (acc,) if epilogue is None else epilogue(acc, *[r[...] for r in extra_refs])
            for o_ref, val in zip(o_refs, outs, strict=True):
                o_ref[...] = val.astype(o_ref.dtype)

        if nk == 1:
            finish(partial_sum())
        else:
            acc_ref = refs[-1]
            k = pl.program_id(2)

            @pl.when(k == 0)
            def _():
                acc_ref[...] = jnp.zeros_like(acc_ref)

            acc_ref[...] += partial_sum()

            @pl.when(k == nk - 1)
            def _():
                finish(acc_ref[...])

    operands, in_specs = [], []
    for a, b in pairs:
        operands += [a, b]
        in_specs += [a_spec, b_spec]
    if has_add:
        operands.append(addend)
        in_specs.append(o_spec)
    for arr, first in extras:
        operands.append(arr)
        in_specs.append(pl.BlockSpec((bm, bn), lambda i, j, k, first=first: (i, first + j)))
    out_shape = tuple(jax.ShapeDtypeStruct((m_stored, N), out_dtype) for _ in range(n_out))
    out = _pcall(
        kernel, name=name, dims=("parallel", "parallel", "arbitrary"), deps=deps,
        out_shape=out_shape, grid=(M // bm, N // bn, nk), in_specs=in_specs, out_specs=(o_spec,) * n_out,
        scratch_shapes=[pltpu.VMEM((bm, bn), F32)] if nk > 1 else [],
    )(*operands)
    return out[0] if epilogue is None else out


def _rms_fwd(x, g, *, deps=(), name):
    T, D = x.shape
    br = _pick(T, 256, 8)

    def kernel(x_ref, g_ref, h_ref, r_ref):
        xv = x_ref[...]
        rstd = lax.rsqrt(jnp.mean(xv * xv, axis=-1, keepdims=True) + EPS)
        h_ref[...] = (xv * rstd * g_ref[...]).astype(BF16)
        r_ref[...] = rstd

    return _pcall(
        kernel, name=name, dims=("parallel",), deps=deps,
        out_shape=(jax.ShapeDtypeStruct((T, D), BF16), jax.ShapeDtypeStruct((T, 1), F32)),
        grid=(T // br,),
        in_specs=[pl.BlockSpec((br, D), lambda i: (i, 0)), pl.BlockSpec((1, D), lambda i: (0, 0))],
        out_specs=(pl.BlockSpec((br, D), lambda i: (i, 0)), pl.BlockSpec((br, 1), lambda i: (i, 0))),
    )(x, g)


def _rms_bwd(dres, dh, x, rstd, g, *, deps=(), name):
    T, D = x.shape
    br = _pick(T, 256, 8)

    def kernel(dres_ref, dh_ref, x_ref, r_ref, g_ref, dx_ref, gg_ref, dxb_ref):
        @pl.when(pl.program_id(0) == 0)
        def _():
            gg_ref[...] = jnp.zeros_like(gg_ref)

        rstd_v = r_ref[...]
        xhat = x_ref[...] * rstd_v
        dhv = dh_ref[...].astype(F32)
        dxhat = dhv * g_ref[...]
        proj = jnp.mean(dxhat * xhat, axis=-1, keepdims=True)
        dx = dres_ref[...] + rstd_v * (dxhat - xhat * proj)
        dx_ref[...] = dx
        dxb_ref[...] = dx.astype(BF16)
        gg_ref[...] += jnp.sum(dhv * xhat, axis=0, keepdims=True)

    row = pl.BlockSpec((br, D), lambda i: (i, 0))
    vec = pl.BlockSpec((1, D), lambda i: (0, 0))
    return _pcall(
        kernel, name=name, dims=("arbitrary",), deps=deps,
        out_shape=(jax.ShapeDtypeStruct((T, D), F32), jax.ShapeDtypeStruct((1, D), F32),
                   jax.ShapeDtypeStruct((T, D), BF16)),
        grid=(T // br,),
        in_specs=[row, row, row, pl.BlockSpec((br, 1), lambda i: (i, 0)), vec],
        out_specs=(row, vec, row),
    )(dres, dh, x, rstd, g)


def _final_loss(x2, gf, target, *, name):
    T, D = x2.shape
    br = _pick(T, 256, 8)

    def kernel(x_ref, g_ref, t_ref, dx_ref, loss_ref, gg_ref, dxb_ref):
        @pl.when(pl.program_id(0) == 0)
        def _():
            gg_ref[...] = jnp.zeros_like(gg_ref)
            loss_ref[...] = jnp.zeros_like(loss_ref)

        xv = x_ref[...]
        gv = g_ref[...]
        rstd = lax.rsqrt(jnp.mean(xv * xv, axis=-1, keepdims=True) + EPS)
        xhat = xv * rstd
        err = xhat * gv - t_ref[...]
        loss_ref[...] += 0.5 * jnp.sum(jnp.mean(err * err, axis=-1, keepdims=True), axis=0, keepdims=True)
        dy = err * (1.0 / D)
        dxhat = dy * gv
        proj = jnp.mean(dxhat * xhat, axis=-1, keepdims=True)
        dx = rstd * (dxhat - xhat * proj)
        dx_ref[...] = dx
        dxb_ref[...] = dx.astype(BF16)
        gg_ref[...] += jnp.sum(dy * xhat, axis=0, keepdims=True)

    row = pl.BlockSpec((br, D), lambda i: (i, 0))
    vec = pl.BlockSpec((1, D), lambda i: (0, 0))
    return _pcall(
        kernel, name=name, dims=("arbitrary",),
        out_shape=(jax.ShapeDtypeStruct((T, D), F32), jax.ShapeDtypeStruct((1, 1), F32),
                   jax.ShapeDtypeStruct((1, D), F32), jax.ShapeDtypeStruct((T, D), BF16)),
        grid=(T // br,),
        in_specs=[row, vec, row],
        out_specs=(row, pl.BlockSpec((1, 1), lambda i: (0, 0)), vec, row),
    )(x2, gf, target)


def _scan_rows(v, reverse):
    n = v.shape[0]
    row = lax.broadcasted_iota(jnp.int32, v.shape, 0)
    s = 1
    while s < n:
        if reverse:
            v = v + jnp.where(row < n - s, pltpu.roll(v, n - s, 0), 0.0)
        else:
            v = v + jnp.where(row >= s, pltpu.roll(v, s, 0), 0.0)
        s *= 2
    return v


def _fox_gate_fwd(f, bias, *, name):
    T = f.shape[0]
    tb = _pick(T, 512, 8)

    def kernel(f_ref, b_ref, c_ref, s_ref, carry):
        @pl.when(pl.program_id(0) == 0)
        def _():
            carry[...] = jnp.zeros_like(carry)

        fa = f_ref[...] + b_ref[...]
        z = jnp.exp(-jnp.abs(fa))
        log_f = jnp.minimum(fa, 0.0) - jnp.log(1.0 + z)
        s_ref[...] = jnp.where(fa >= 0, z, 1.0) / (1.0 + z)
        c = _scan_rows(log_f, False) + carry[...]
        c_ref[...] = c
        carry[...] = c[tb - 1:tb, :]

    blk = pl.BlockSpec((tb, LANES), lambda i: (i, 0))
    return _pcall(
        kernel, name=name, dims=("arbitrary",),
        out_shape=(jax.ShapeDtypeStruct((T, LANES), F32), jax.ShapeDtypeStruct((T, LANES), F32)),
        grid=(T // tb,),
        in_specs=[blk, pl.BlockSpec((1, LANES), lambda i: (0, 0))],
        out_specs=(blk, blk),
        scratch_shapes=[pltpu.VMEM((1, LANES), F32)],
    )(f, bias)


def _fox_gate_bwd(dc, sneg, *, name):
    T = dc.shape[0]
    tb = _pick(T, 512, 8)
    nb = T // tb

    def kernel(dc_ref, s_ref, df_ref, gb_ref, carry):
        @pl.when(pl.program_id(0) == 0)
        def _():
            carry[...] = jnp.zeros_like(carry)
            gb_ref[...] = jnp.zeros_like(gb_ref)

        dlog = _scan_rows(dc_ref[...], True) + carry[...]
        carry[...] = dlog[0:1, :]
        dfa = dlog * s_ref[...]
        df_ref[...] = dfa.astype(BF16)
        gb_ref[...] += jnp.sum(dfa, axis=0, keepdims=True)

    blk = pl.BlockSpec((tb, LANES), lambda i: (nb - 1 - i, 0))
    return _pcall(
        kernel, name=name, dims=("arbitrary",),
        out_shape=(jax.ShapeDtypeStruct((T, LANES), BF16), jax.ShapeDtypeStruct((1, LANES), F32)),
        grid=(nb,),
        in_specs=[blk, blk],
        out_specs=(blk, pl.BlockSpec((1, LANES), lambda i: (0, 0))),
        scratch_shapes=[pltpu.VMEM((1, LANES), F32)],
    )(dc, sneg)


def _fox_scores(q, k, cq, ck, diagonal):
    s = _dot(q, k, "nt") * (DH_A ** -0.5) + cq - ck
    if diagonal:
        row = lax.broadcasted_iota(jnp.int32, s.shape, 0)
        col = lax.broadcasted_iota(jnp.int32, s.shape, 1)
        s = jnp.where(col <= row, s, NEG)
    return s


def _fox_fwd(proj, c_col, c_row, n_heads, *, deps=(), name):
    T = proj.shape[0]
    H = n_heads
    tq = tk = _pick(T, FOX_TILE, 128)
    nq = T // tq

    def kernel(q_ref, k_ref, v_ref, cq_ref, ck_ref, o_ref, lse_ref, m_sc, l_sc, acc_sc):
        i, j = pl.program_id(1), pl.program_id(2)

        @pl.when(j == 0)
        def _():
            m_sc[...] = jnp.full_like(m_sc, NEG)
            l_sc[...] = jnp.zeros_like(l_sc)
            acc_sc[...] = jnp.zeros_like(acc_sc)

        def tile(diagonal):
            s = _fox_scores(q_ref[...], k_ref[...], cq_ref[...], ck_ref[...], diagonal)
            m_prev = m_sc[...]
            m_new = jnp.maximum(m_prev, jnp.max(s, axis=-1, keepdims=True))
            alpha = jnp.exp(m_prev - m_new)
            p = jnp.exp(s - m_new)
            l_sc[...] = alpha * l_sc[...] + jnp.sum(p, axis=-1, keepdims=True)
            acc_sc[...] = alpha * acc_sc[...] + _dot(p.astype(BF16), v_ref[...])
            m_sc[...] = m_new

        @pl.when(j < i)
        def _():
            tile(False)

        @pl.when(j == i)
        def _():
            tile(True)
            l = l_sc[...]
            o_ref[...] = (acc_sc[...] / l).astype(BF16)
            lse_ref[...] = m_sc[...] + jnp.log(l)

    return _pcall(
        kernel, name=name, dims=("parallel", "parallel", "arbitrary"), deps=deps,
        out_shape=(jax.ShapeDtypeStruct((T, H * DH_A), BF16), jax.ShapeDtypeStruct((H, T, 1), F32)),
        grid=(H, nq, nq),
        in_specs=[
            pl.BlockSpec((tq, DH_A), lambda h, i, j: (i, h)),
            pl.BlockSpec((tk, DH_A), lambda h, i, j: (jnp.minimum(j, i), H + h)),
            pl.BlockSpec((tk, DH_A), lambda h, i, j: (jnp.minimum(j, i), 2 * H + h)),
            pl.BlockSpec((None, tq, 1), lambda h, i, j: (h, i, 0)),
            pl.BlockSpec((None, 1, tk), lambda h, i, j: (h, 0, jnp.minimum(j, i))),
        ],
        out_specs=(pl.BlockSpec((tq, DH_A), lambda h, i, j: (i, h)),
                   pl.BlockSpec((None, tq, 1), lambda h, i, j: (h, i, 0))),
        scratch_shapes=[pltpu.VMEM((tq, 1), F32), pltpu.VMEM((tq, 1), F32), pltpu.VMEM((tq, DH_A), F32)],
    )(proj, proj, proj, c_col, c_row)


def _fox_bwd(proj, o, do, lse, c_col, c_row, n_heads, *, name):
    T = proj.shape[0]
    H = n_heads
    tq = tk = _pick(T, FOX_TILE, 128)
    nq = T // tq
    scale = DH_A ** -0.5

    def kernel(q_ref, k_ref, v_ref, do_ref, lse_ref, o_ref, cq_ref, ck_ref,
               dq_ref, dk_ref, dv_ref, dc_ref, dcq_ref, dk_sc, dv_sc, dc_sc):
        j, i = pl.program_id(1), pl.program_id(2)

        @pl.when((j == 0) & (i == 0))
        def _():
            dq_ref[...] = jnp.zeros_like(dq_ref)
            dcq_ref[...] = jnp.zeros_like(dcq_ref)

        @pl.when(i == 0)
        def _():
            dk_sc[...] = jnp.zeros_like(dk_sc)
            dv_sc[...] = jnp.zeros_like(dv_sc)
            dc_sc[...] = jnp.zeros_like(dc_sc)

        def tile(diagonal):
            q, k, v, dov = q_ref[...], k_ref[...], v_ref[...], do_ref[...]
            s = _fox_scores(q, k, cq_ref[...], ck_ref[...], diagonal)
            p = jnp.exp(s - lse_ref[...])
            dv_sc[...] += _dot(p.astype(BF16), dov, "tn")
            dp = _dot(dov, v, "nt")
            delta = jnp.sum(dov.astype(F32) * o_ref[...].astype(F32), axis=1, keepdims=True)
            ds = p * (dp - delta)
            dc_sc[...] -= jnp.sum(ds, axis=0, keepdims=True)
            dsb = ds.astype(BF16)
            dk_sc[...] += _dot(dsb, q, "tn") * scale
            rows = pl.ds(pl.multiple_of(i * tq, tq), tq)
            dq_ref[rows, :] += _dot(dsb, k) * scale
            dcq_ref[rows, :] += jnp.sum(ds, axis=1, keepdims=True)

        @pl.when(i > j)
        def _():
            tile(False)

        @pl.when(i == j)
        def _():
            tile(True)

        @pl.when(i == nq - 1)
        def _():
            dk_ref[...] = dk_sc[...].astype(BF16)
            dv_ref[...] = dv_sc[...].astype(BF16)
            dc_ref[...] = dc_sc[...]

    qi = lambda j, i: jnp.maximum(i, j)
    return _pcall(
        kernel, name=name, dims=("parallel", "arbitrary", "arbitrary"),
        out_shape=(jax.ShapeDtypeStruct((T, H * DH_A), F32), jax.ShapeDtypeStruct((T, H * DH_A), BF16),
                   jax.ShapeDtypeStruct((T, H * DH_A), BF16), jax.ShapeDtypeStruct((H, 1, T), F32),
                   jax.ShapeDtypeStruct((H, T, 1), F32)),
        grid=(H, nq, nq),
        in_specs=[
            pl.BlockSpec((tq, DH_A), lambda h, j, i: (qi(j, i), h)),
            pl.BlockSpec((tk, DH_A), lambda h, j, i: (j, H + h)),
            pl.BlockSpec((tk, DH_A), lambda h, j, i: (j, 2 * H + h)),
            pl.BlockSpec((tq, DH_A), lambda h, j, i: (qi(j, i), h)),
            pl.BlockSpec((None, tq, 1), lambda h, j, i: (h, qi(j, i), 0)),
            pl.BlockSpec((tq, DH_A), lambda h, j, i: (qi(j, i), h)),
            pl.BlockSpec((None, tq, 1), lambda h, j, i: (h, qi(j, i), 0)),
            pl.BlockSpec((None, 1, tk), lambda h, j, i: (h, 0, j)),
        ],
        out_specs=(pl.BlockSpec((T, DH_A), lambda h, j, i: (0, h)),
                   pl.BlockSpec((tk, DH_A), lambda h, j, i: (j, h)),
                   pl.BlockSpec((tk, DH_A), lambda h, j, i: (j, h)),
                   pl.BlockSpec((None, 1, tk), lambda h, j, i: (h, 0, j)),
                   pl.BlockSpec((None, T, 1), lambda h, j, i: (h, 0, 0))),
        scratch_shapes=[pltpu.VMEM((tk, DH_A), F32), pltpu.VMEM((tk, DH_A), F32), pltpu.VMEM((1, tk), F32)],
    )(proj, proj, proj, do, lse, o, c_col, c_row)


def _t5_bucket(dist):
    max_exact = NUM_BUCKETS // 2
    small = dist < max_exact
    large = max_exact + (np.log(np.maximum(dist, 1) / max_exact) / np.log(MAX_DISTANCE / max_exact)
                         * (NUM_BUCKETS - max_exact)).astype(np.int64)
    large = np.minimum(large, NUM_BUCKETS - 1)
    return np.where(small, dist, large)


def _bucket_onehot():
    ql = np.arange(SWA_BLOCK)[:, None]
    kl = np.arange(2 * SWA_BLOCK)[None, :]
    bucket = _t5_bucket(np.clip(ql + SWA_BLOCK - kl, 0, None))
    bucket = np.stack([bucket[:, :SWA_BLOCK], bucket[:, SWA_BLOCK:]], axis=0)
    onehot = (bucket.reshape(1, -1) == np.arange(NUM_BUCKETS)[:, None])
    return jnp.asarray(onehot, dtype=BF16)


def _split3(v):
    hi = v.astype(BF16)
    r1 = v - hi.astype(F32)
    mid = r1.astype(BF16)
    lo = (r1 - mid.astype(F32)).astype(BF16)
    return hi, mid, lo


def _bias_table(rel_bias, onehot, *, name):
    B, H = rel_bias.shape
    n = onehot.shape[1]
    bc = _pick(n, 8192, LANES)

    def kernel(rb_ref, oh_ref, o_ref):
        hi, mid, lo = _split3(rb_ref[...])
        oh = oh_ref[...]
        o_ref[...] = _dot(hi, oh, "tn") + _dot(mid, oh, "tn") + _dot(lo, oh, "tn")

    return _pcall(
        kernel, name=name, dims=("parallel",),
        out_shape=jax.ShapeDtypeStruct((H, n), F32), grid=(n // bc,),
        in_specs=[pl.BlockSpec((B, H), lambda i: (0, 0)), pl.BlockSpec((B, bc), lambda i: (0, i))],
        out_specs=pl.BlockSpec((H, bc), lambda i: (0, i)),
    )(rel_bias, onehot)


def _bias_table_bwd(dbias, onehot, *, name):
    H, n = dbias.shape
    B = onehot.shape[0]
    bc = _pick(n, 8192, LANES)

    def kernel(db_ref, oh_ref, o_ref):
        @pl.when(pl.program_id(0) == 0)
        def _():
            o_ref[...] = jnp.zeros_like(o_ref)

        hi, mid, lo = _split3(db_ref[...])
        oh = oh_ref[...]
        o_ref[...] += _dot(oh, hi, "nt") + _dot(oh, mid, "nt") + _dot(oh, lo, "nt")

    return _pcall(
        kernel, name=name, dims=("arbitrary",),
        out_shape=jax.ShapeDtypeStruct((B, H), F32), grid=(n // bc,),
        in_specs=[pl.BlockSpec((H, bc), lambda i: (0, i)), pl.BlockSpec((B, bc), lambda i: (0, i))],
        out_specs=pl.BlockSpec((B, H), lambda i: (0, 0)),
    )(dbias, onehot)


def _swa_probs(q2, kp, kc, bias_ref, lse, n, G):
    Q = SWA_BLOCK
    scale = DH_B ** -0.5
    sp = (_dot(q2, kp, "nt") * scale).reshape(G, Q, Q) + bias_ref[:, 0]
    sc = (_dot(q2, kc, "nt") * scale).reshape(G, Q, Q) + bias_ref[:, 1]
    row = lax.broadcasted_iota(jnp.int32, (G, Q, Q), 1)
    col = lax.broadcasted_iota(jnp.int32, (G, Q, Q), 2)
    vis_p = (col > row) & (n > 0)
    vis_c = col <= row
    sp = jnp.where(vis_p, sp, NEG)
    sc = jnp.where(vis_c, sc, NEG)
    if lse is None:
        return sp, sc, vis_p, vis_c
    pp = jnp.where(vis_p, jnp.exp(sp - lse), 0.0)
    pc = jnp.where(vis_c, jnp.exp(sc - lse), 0.0)
    return pp, pc


def _swa_fwd(q, k, v, bias, sinks, *, deps=(), name):
    HKV, G, T, _ = q.shape
    Q = SWA_BLOCK
    N = T // Q

    def kernel(q_ref, kp_ref, kc_ref, vp_ref, vc_ref, b_ref, s_ref, o_ref, lse_ref):
        n = pl.program_id(1)
        q2 = q_ref[...].reshape(G * Q, DH_B)
        sp, sc, vis_p, vis_c = _swa_probs(q2, kp_ref[...], kc_ref[...], b_ref, None, n, G)
        sink = s_ref[...][:, :, 0:1]
        m = jnp.maximum(jnp.maximum(jnp.max(sp, axis=-1, keepdims=True), jnp.max(sc, axis=-1, keepdims=True)), sink)
        pp = jnp.where(vis_p, jnp.exp(sp - m), 0.0)
        pc = jnp.where(vis_c, jnp.exp(sc - m), 0.0)
        denom = jnp.sum(pp, axis=-1, keepdims=True) + jnp.sum(pc, axis=-1, keepdims=True) + jnp.exp(sink - m)
        o = _dot(pp.reshape(G * Q, Q).astype(BF16), vp_ref[...]) + _dot(pc.reshape(G * Q, Q).astype(BF16), vc_ref[...])
        o_ref[...] = (o.reshape(G, Q, DH_B) / denom).astype(BF16)
        lse_ref[...] = m + jnp.log(denom)

    prev = lambda h, n: (h, jnp.maximum(n - 1, 0), 0)
    cur = lambda h, n: (h, n, 0)
    kv = lambda f: pl.BlockSpec((None, Q, DH_B), f)
    return _pcall(
        kernel, name=name, dims=("parallel", "parallel"), deps=deps,
        out_shape=(jax.ShapeDtypeStruct((HKV, G, T, DH_B), BF16), jax.ShapeDtypeStruct((HKV, G, T, 1), F32)),
        grid=(HKV, N),
        in_specs=[pl.BlockSpec((None, G, Q, DH_B), lambda h, n: (h, 0, n, 0)),
                  kv(prev), kv(cur), kv(prev), kv(cur),
                  pl.BlockSpec((None, G, 2, Q, Q), lambda h, n: (h, 0, 0, 0, 0)),
                  pl.BlockSpec((None, G, 1, LANES), lambda h, n: (h, 0, 0, 0))],
        out_specs=(pl.BlockSpec((None, G, Q, DH_B), lambda h, n: (h, 0, n, 0)),
                   pl.BlockSpec((None, G, Q, 1), lambda h, n: (h, 0, n, 0))),
    )(q, k, k, v, v, bias, sinks)


def _swa_bwd(q, k, v, o, do, lse, bias, sinks, *, name):
    HKV, G, T, _ = q.shape
    Q = SWA_BLOCK
    N = T // Q
    scale = DH_B ** -0.5

    def kernel(q_ref, kp_ref, kc_ref, vp_ref, vc_ref, o_ref, do_ref, lse_ref, b_ref, s_ref,
               dq_ref, dk_ref, dv_ref, db_ref, ds_ref, ck_sc, cv_sc):
        n = pl.program_id(1)

        @pl.when(n == 0)
        def _():
            db_ref[...] = jnp.zeros_like(db_ref)
            ds_ref[...] = jnp.zeros_like(ds_ref)
            ck_sc[...] = jnp.zeros_like(ck_sc)
            cv_sc[...] = jnp.zeros_like(cv_sc)

        @pl.when(n < N)
        def _():
            q2 = q_ref[...].reshape(G * Q, DH_B)
            do3 = do_ref[...]
            do2 = do3.reshape(G * Q, DH_B)
            kp, kc, vp, vc = kp_ref[...], kc_ref[...], vp_ref[...], vc_ref[...]
            lse_v = lse_ref[...]
            pp, pc = _swa_probs(q2, kp, kc, b_ref, lse_v, n, G)
            delta = jnp.sum(do3.astype(F32) * o_ref[...].astype(F32), axis=-1, keepdims=True)
            dsp = pp * (_dot(do2, vp, "nt").reshape(G, Q, Q) - delta)
            dsc = pc * (_dot(do2, vc, "nt").reshape(G, Q, Q) - delta)
            p_sink = jnp.exp(s_ref[...][:, :, 0:1] - lse_v)
            ds_ref[...] += jnp.broadcast_to(-jnp.sum(p_sink * delta, axis=1, keepdims=True), (G, 1, LANES))
            db_ref[:, 0] += dsp
            db_ref[:, 1] += dsc
            dsp2 = dsp.reshape(G * Q, Q).astype(BF16)
            dsc2 = dsc.reshape(G * Q, Q).astype(BF16)
            dq = (_dot(dsp2, kp) + _dot(dsc2, kc)) * scale
            dq_ref[...] = dq.reshape(G, Q, DH_B).astype(BF16)
            pp2 = pp.reshape(G * Q, Q).astype(BF16)
            pc2 = pc.reshape(G * Q, Q).astype(BF16)
            dk_ref[...] = (ck_sc[...] + _dot(dsp2, q2, "tn") * scale).astype(BF16)
            dv_ref[...] = (cv_sc[...] + _dot(pp2, do2, "tn")).astype(BF16)
            ck_sc[...] = _dot(dsc2, q2, "tn") * scale
            cv_sc[...] = _dot(pc2, do2, "tn")

        @pl.when(n == N)
        def _():
            dk_ref[...] = ck_sc[...].astype(BF16)
            dv_ref[...] = cv_sc[...].astype(BF16)

    qn = lambda n: jnp.minimum(n, N - 1)
    prev = lambda h, n: (h, jnp.maximum(qn(n) - 1, 0), 0)
    cur = lambda h, n: (h, qn(n), 0)
    out_kv = lambda h, n: (h, jnp.maximum(n - 1, 0), 0)
    kv = lambda f: pl.BlockSpec((None, Q, DH_B), f)
    qspec = pl.BlockSpec((None, G, Q, DH_B), lambda h, n: (h, 0, qn(n), 0))
    bspec = pl.BlockSpec((None, G, 2, Q, Q), lambda h, n: (h, 0, 0, 0, 0))
    sspec = pl.BlockSpec((None, G, 1, LANES), lambda h, n: (h, 0, 0, 0))
    return _pcall(
        kernel, name=name, dims=("parallel", "arbitrary"),
        out_shape=(jax.ShapeDtypeStruct((HKV, G, T, DH_B), BF16), jax.ShapeDtypeStruct((HKV, T, DH_B), BF16),
                   jax.ShapeDtypeStruct((HKV, T, DH_B), BF16), jax.ShapeDtypeStruct((HKV, G, 2, Q, Q), F32),
                   jax.ShapeDtypeStruct((HKV, G, 1, LANES), F32)),
        grid=(HKV, N + 1),
        in_specs=[qspec, kv(prev), kv(cur), kv(prev), kv(cur), qspec, qspec,
                  pl.BlockSpec((None, G, Q, 1), lambda h, n: (h, 0, qn(n), 0)), bspec, sspec],
        out_specs=(qspec, kv(out_kv), kv(out_kv), bspec, sspec),
        scratch_shapes=[pltpu.VMEM((Q, DH_B), F32), pltpu.VMEM((Q, DH_B), F32)],
    )(q, k, k, v, v, o, do, lse, bias, sinks)


def _branch_mix(oa, ob, wa_t, wb_t, proj, gate_col, *, deps=(), name):
    T = oa.shape[0]
    D = wa_t.shape[0]
    bm, bn = _pick(T, 1024, 8), _pick(D, 512, LANES)
    assert gate_col % bn == 0
    ga0, gb0 = gate_col // bn, (gate_col + D) // bn

    def kernel(oa_ref, ob_ref, wa_ref, wb_ref, ga_ref, gb_ref, mix_ref, ya_ref, yb_ref):
        ya = _dot(oa_ref[...], wa_ref[...], "nt")
        yb = _dot(ob_ref[...], wb_ref[...], "nt")
        mix = _sigmoid(ga_ref[...].astype(F32)) * ya + _sigmoid(gb_ref[...].astype(F32)) * yb
        mix_ref[...] = mix.astype(BF16)
        ya_ref[...] = ya.astype(BF16)
        yb_ref[...] = yb.astype(BF16)

    out = pl.BlockSpec((bm, bn), lambda i, j: (i, j))
    return _pcall(
        kernel, name=name, dims=("parallel", "parallel"), deps=deps,
        out_shape=tuple(jax.ShapeDtypeStruct((T, D), BF16) for _ in range(3)),
        grid=(T // bm, D // bn),
        in_specs=[pl.BlockSpec((bm, oa.shape[1]), lambda i, j: (i, 0)),
                  pl.BlockSpec((bm, ob.shape[1]), lambda i, j: (i, 0)),
                  pl.BlockSpec((bn, oa.shape[1]), lambda i, j: (j, 0)),
                  pl.BlockSpec((bn, ob.shape[1]), lambda i, j: (j, 0)),
                  pl.BlockSpec((bm, bn), lambda i, j: (i, ga0 + j)),
                  pl.BlockSpec((bm, bn), lambda i, j: (i, gb0 + j))],
        out_specs=(out, out, out),
    )(oa, ob, wa_t, wb_t, proj, proj)


def _mix_bwd_tile(dmixed, ya, yb, ga, gb):
    sa = _sigmoid(ga.astype(F32))
    sb = _sigmoid(gb.astype(F32))
    return (dmixed * sa, dmixed * sb,
            dmixed * ya.astype(F32) * sa * (1.0 - sa), dmixed * yb.astype(F32) * sb * (1.0 - sb))


def _ffn_up(h, wg_t, wu_t, *, name):
    T, D = h.shape
    FP = wg_t.shape[0]
    bm, bn = _pick(T, 1024, 8), _pick(FP, 512, LANES)

    def kernel(h_ref, wg_ref, wu_ref, g_ref, u_ref, hid_ref):
        hv = h_ref[...]
        g = _dot(hv, wg_ref[...], "nt")
        u = _dot(hv, wu_ref[...], "nt")
        g_ref[...] = g.astype(BF16)
        u_ref[...] = u.astype(BF16)
        hid_ref[...] = (g * _sigmoid(g) * u).astype(BF16)

    out = pl.BlockSpec((bm, bn), lambda i, j: (i, j))
    wspec = pl.BlockSpec((bn, D), lambda i, j: (j, 0))
    return _pcall(
        kernel, name=name, dims=("parallel", "parallel"),
        out_shape=tuple(jax.ShapeDtypeStruct((T, FP), BF16) for _ in range(3)),
        grid=(T // bm, FP // bn),
        in_specs=[pl.BlockSpec((bm, D), lambda i, j: (i, 0)), wspec, wspec],
        out_specs=(out, out, out),
    )(h, wg_t, wu_t)


def _swiglu_bwd_tile(dhid, gate, up):
    g = gate.astype(F32)
    sg = _sigmoid(g)
    return dhid * up.astype(F32) * sg * (1.0 + g * (1.0 - sg)), dhid * g * sg


def _adamw_update(w, g, m, v):
    nm = ADAM_B1 * m + (1.0 - ADAM_B1) * g
    nv = ADAM_B2 * v + (1.0 - ADAM_B2) * (g * g)
    m_hat = nm / (1.0 - ADAM_B1 ** ADAM_STEP)
    v_hat = nv / (1.0 - ADAM_B2 ** ADAM_STEP)
    return -ADAM_LR * (m_hat / (jnp.sqrt(v_hat) + ADAM_EPS) + ADAM_WD * w), nm, nv


def _adamw(w, g, m, v, *, name):
    Rw, Cw = w.shape
    br = _pick(Rw, max(8, (1 << 19) // Cw // 8 * 8), 8)

    def kernel(w_ref, g_ref, m_ref, v_ref, d_ref, nm_ref, nv_ref):
        d_ref[...], nm_ref[...], nv_ref[...] = _adamw_update(w_ref[...], g_ref[...], m_ref[...], v_ref[...])

    blk = pl.BlockSpec((br, Cw), lambda i: (i, 0))
    return _pcall(
        kernel, name=name, dims=("parallel",),
        out_shape=tuple(jax.ShapeDtypeStruct((Rw, Cw), F32) for _ in range(3)),
        grid=(Rw // br,), in_specs=[blk, blk, blk, blk], out_specs=(blk, blk, blk),
    )(w, g, m, v)


def _sum_adamw(land, w, m, v, *, name):
    _, R, C = land.shape
    bc = _pick(C, max(LANES, (1 << 18) // R // LANES * LANES), LANES)

    def kernel(l_ref, w_ref, m_ref, v_ref, g_ref, d_ref, nm_ref, nv_ref):
        g = l_ref[0].astype(F32)
        for k in range(1, N_DEV):
            g = g + l_ref[k].astype(F32)
        g_ref[...] = g
        d_ref[...], nm_ref[...], nv_ref[...] = _adamw_update(w_ref[...], g, m_ref[...], v_ref[...])

    blk = pl.BlockSpec((R, bc), lambda i: (0, i))
    return _pcall(
        kernel, name=name, dims=("parallel",),
        out_shape=tuple(jax.ShapeDtypeStruct((R, C), F32) for _ in range(4)),
        grid=(C // bc,),
        in_specs=[pl.BlockSpec((N_DEV, R, bc), lambda i: (0, 0, i)), blk, blk, blk],
        out_specs=(blk, blk, blk, blk),
    )(land, w, m, v)


def _my_place():
    return lax.axis_index("x"), lax.axis_index("y"), lax.axis_index("c")


def _flip(v, bit):
    return 1 - v if bit else v


HBM_SPEC = pl.BlockSpec(memory_space=pltpu.HBM)
SEM_SPEC = pl.BlockSpec(memory_space=pltpu.SEMAPHORE)
SPLIT_EFFECT = pltpu.SideEffectType.DATAFLOW_SIDE_EFFECTING
N_PEERS = N_DEV - 1


def _hbm(a):
    return pltpu.with_memory_space_constraint(a, pltpu.HBM)


def _peer(x, y, c, k):
    return _flip(x, k & 4), _flip(y, k & 2), _flip(c, k & 1)


def _row_block(ref, idx, R):
    if len(ref.shape) == 3:
        return ref.at[idx]
    return ref.at[pl.ds(pl.multiple_of(idx * R, BF16_TILE_ROWS), R), :]


def _remote_pair(src, dst, arrives, send_sem, recv_sem, to):
    send = pltpu.make_async_remote_copy(src_ref=src, dst_ref=dst, send_sem=send_sem, recv_sem=recv_sem,
                                        device_id=to, device_id_type=MESH)
    recv = pltpu.make_async_remote_copy(src_ref=src, dst_ref=arrives, send_sem=send_sem, recv_sem=recv_sem,
                                        device_id=to, device_id_type=MESH)
    return send, recv


def _split_call(body, arrays, sems, n_through, out_sems, after, token, *, name):
    out_shape = [pltpu.SemaphoreType.DMA((n,)) for n in out_sems]
    out_shape += [pltpu.HBM(a.shape, a.dtype) for a in arrays[:n_through]]
    out_specs = [SEM_SPEC] * len(out_sems) + [HBM_SPEC] * n_through
    if token:
        out_shape.append(jax.ShapeDtypeStruct((8, LANES), F32))
        out_specs.append(pl.BlockSpec(memory_space=pltpu.VMEM))
    out = pl.pallas_call(
        body, name=name, out_shape=tuple(out_shape),
        in_specs=tuple([HBM_SPEC] * len(arrays) + [SEM_SPEC] * len(sems) + [ANY]),
        out_specs=tuple(out_specs),
        input_output_aliases={t: len(out_sems) + t for t in range(n_through)},
        compiler_params=pltpu.CompilerParams(has_side_effects=SPLIT_EFFECT),
    )(*[_hbm(a) for a in arrays], *sems, after)
    n_s = len(out_sems)
    return tuple(out[:n_s]), tuple(out[n_s:n_s + n_through]), (out[-1] if token else None)


def _scatter_copies(src_ref, land_ref, sems, R):
    send_sems, recv_sems, local_sem = sems
    x, y, c = _my_place()
    local = pltpu.make_async_copy(_row_block(src_ref, 4 * x + 2 * y + c, R), land_ref.at[0], local_sem.at[0])
    remote = []
    for k in range(1, N_DEV):
        px, py, pc = _peer(x, y, c, k)
        remote.append(_remote_pair(_row_block(src_ref, 4 * px + 2 * py + pc, R), land_ref.at[k], land_ref.at[k],
                                   send_sems.at[k - 1], recv_sems.at[k - 1], (px, py, pc)))
    return local, remote


def _scatter_start(g, R, after, *, name):
    land = lax.empty((N_DEV, R, g.shape[-1]), g.dtype)

    def body(src_ref, land_ref, after_ref, send_sems, recv_sems, local_sem, src_thru, land_thru, token):
        local, remote = _scatter_copies(src_ref, land_ref, (send_sems, recv_sems, local_sem), R)
        local.start()
        for send, _ in remote:
            send.start()
        token[...] = jnp.zeros_like(token)

    sems, (src_thru, land_thru), token = _split_call(body, [g, land], [], 2, (N_PEERS, N_PEERS, 1), after, True, name=name)
    return sems, src_thru, land_thru, token, R


def _scatter_finish(handle, after, *, name):
    sems, src_thru, land_thru, _, R = handle

    def body(src_ref, land_ref, send_sems, recv_sems, local_sem, after_ref, src_dead, got_ref):
        local, remote = _scatter_copies(src_ref, land_ref, (send_sems, recv_sems, local_sem), R)
        for send, recv in remote:
            send.wait_send()
            recv.wait_recv()
        local.wait()

    return _split_call(body, [src_thru, land_thru], sems, 2, (), after, False, name=name)[1][1]


def _gather_copies(src_ref, land_ref, sems, R):
    send_sems, recv_sems, local_sem = sems
    x, y, c = _my_place()
    mine = _row_block(land_ref, 4 * x + 2 * y + c, R)
    local = pltpu.make_async_copy(src_ref, mine, local_sem.at[0])
    peers = [(x, y, 1 - c), (1 - x, y, c), (x, 1 - y, c), (1 - x, 1 - y, c)]
    remote = [_remote_pair(src_ref, mine, _row_block(land_ref, 4 * px + 2 * py + pc, R),
                           send_sems.at[k], recv_sems.at[k], (px, py, pc)) for k, (px, py, pc) in enumerate(peers)]
    return local, remote


def _forward_copies(land_ref, sems, R):
    send_sems, recv_sems = sems
    x, y, c = _my_place()
    pairs = []
    for j, (px, py) in enumerate([(1 - x, y), (x, 1 - y), (1 - x, 1 - y)]):
        blk = _row_block(land_ref, 4 * px + 2 * py + c, R)
        pairs.append(_remote_pair(blk, blk, _row_block(land_ref, 4 * px + 2 * py + (1 - c), R),
                                  send_sems.at[j], recv_sems.at[j], (x, y, 1 - c)))
    return pairs


def _gather_start(shard, after, *, pad_rows=0, name):
    R, C = shard.shape
    flat = R % BF16_TILE_ROWS == 0
    assert flat or pad_rows == 0
    land = lax.empty((N_DEV * R + pad_rows, C) if flat else (N_DEV, R, C), shard.dtype)
    if pad_rows:
        land = lax.dynamic_update_slice(land, jnp.zeros((pad_rows, C), shard.dtype), (N_DEV * R, 0))

    def body(src_ref, land_ref, after_ref, send_sems, recv_sems, local_sem, src_thru, land_thru, token):
        local, remote = _gather_copies(src_ref, land_ref, (send_sems, recv_sems, local_sem), R)
        local.start()
        for send, _ in remote:
            send.start()
        token[...] = jnp.zeros_like(token)

    sems, (src_thru, land_thru), token = _split_call(body, [shard, land], [], 2, (4, 4, 1), after, True, name=name)
    return sems, src_thru, land_thru, token, R


def _gather_forward(handle, after, *, name):
    sems, src_thru, land_thru, _, R = handle

    def arrived(land_ref, src_ref, send_sems, recv_sems, local_sem, after_ref, land_out):
        _, remote = _gather_copies(src_ref, land_ref, (send_sems, recv_sems, local_sem), R)
        for _, recv in remote[1:]:
            recv.wait_recv()

    def pass_on(land_ref, after_ref, send_sems, recv_sems, land_out, token):
        for send, _ in _forward_copies(land_ref, (send_sems, recv_sems), R):
            send.start()
        token[...] = jnp.zeros_like(token)

    _, (land1,), _ = _split_call(arrived, [land_thru, src_thru], sems, 1, (), after, False, name=name + "_arrived")
    sems2, (land2,), token = _split_call(pass_on, [land1], [], 1, (3, 3), src_thru, True, name=name + "_pass_on")
    return sems, sems2, src_thru, land2, token, R


def _gather_finish(handle2, after, *, name):
    sems, sems2, src_thru, land_thru, _, R = handle2

    def body(land_ref, src_ref, send_sems, recv_sems, local_sem, send2, recv2, after_ref, land_out):
        local, remote = _gather_copies(src_ref, land_ref, (send_sems, recv_sems, local_sem), R)
        for send, _ in remote:
            send.wait_send()
        remote[0][1].wait_recv()
        local.wait()
        for send, recv in _forward_copies(land_ref, (send2, recv2), R):
            send.wait_send()
            recv.wait_recv()

    return _split_call(body, [land_thru, src_thru], [*sems, *sems2], 1, (), after, False, name=name)[1][0]


def _all_reduce_small(v, *, name):
    rows, C = v.shape

    def body(v_ref, o_ref, land_ref, send_sems, recv_sems):
        x, y, c = _my_place()
        me = 4 * x + 2 * y + c
        copies = []
        for k in range(1, N_DEV):
            peer = (_flip(x, k & 4), _flip(y, k & 2), _flip(c, k & 1))
            copies.append(pltpu.make_async_remote_copy(
                src_ref=v_ref, dst_ref=land_ref.at[me],
                send_sem=send_sems.at[k - 1], recv_sem=recv_sems.at[k - 1], device_id=peer, device_id_type=MESH))
        for cp in copies:
            cp.start()
        land_ref[me] = v_ref[...]
        for k in range(1, N_DEV):
            peer_idx = 4 * _flip(x, k & 4) + 2 * _flip(y, k & 2) + _flip(c, k & 1)
            pltpu.make_async_remote_copy(
                src_ref=v_ref, dst_ref=land_ref.at[peer_idx],
                send_sem=send_sems.at[k - 1], recv_sem=recv_sems.at[k - 1],
                device_id=(x, y, c), device_id_type=MESH).wait_recv()
        for cp in copies:
            cp.wait_send()
        acc = land_ref[0]
        for s in range(1, N_DEV):
            acc = acc + land_ref[s]
        o_ref[...] = acc

    vm = pl.BlockSpec(memory_space=pltpu.VMEM)
    return _comm_call(
        body, name=name,
        out_shape=jax.ShapeDtypeStruct((rows, C), F32),
        in_specs=[vm], out_specs=vm,
        scratch_shapes=[pltpu.VMEM((N_DEV, rows, C), F32),
                        pltpu.SemaphoreType.DMA((7,)), pltpu.SemaphoreType.DMA((7,))],
    )(v)


def kernel(x, norm1_g, w_in, b_forget, attn_sinks, rel_bias, w_branch_a, w_branch_b, w_out, norm2_g, w_ffn_gate, w_ffn_up, w_ffn_down, final_g, loss_target, m_norm1_g, m_w_in, m_b_forget, m_attn_sinks, m_rel_bias, m_w_branch_a, m_w_branch_b, m_w_out, m_norm2_g, m_w_ffn_gate, m_w_ffn_up, m_w_ffn_down, m_final_g, v_norm1_g, v_w_in, v_b_forget, v_attn_sinks, v_rel_bias, v_w_branch_a, v_w_branch_b, v_w_out, v_norm2_g, v_w_ffn_gate, v_w_ffn_up, v_w_ffn_down, v_final_g):
    xs = x[0]
    T, D = xs.shape
    H_A, H_B = b_forget.shape[-1], attn_sinks.shape[-1]
    WA, QB = H_A * DH_A, H_B * DH_B
    R_IN = w_in.shape[-1]
    W_IN = N_DEV * R_IN
    KB = (W_IN - 3 * WA - H_A - QB - 2 * D) // 2
    HKV = KB // DH_B
    G = H_B // HKV
    N_BIG = W_IN - H_A
    GATE_COL = 3 * WA + QB + 2 * KB
    R_F = w_ffn_gate.shape[-1]
    F = N_DEV * R_F
    FP = _round_up(F, 512)
    assert H_A <= LANES and T % SWA_BLOCK == 0

    win_s = w_in[0].T.astype(BF16)
    wa_s = w_branch_a[0].T.astype(BF16)
    wb_s = w_branch_b[0].T.astype(BF16)
    wout_s = w_out[0].astype(BF16)
    wg_s = w_ffn_gate[0].T.astype(BF16)
    wu_s = w_ffn_up[0].T.astype(BF16)
    wd_s = w_ffn_down[0].astype(BF16)

    ag_win = _gather_start(win_s, norm1_g, name="ag_start_w_in")
    ag_wa = _gather_start(wa_s, ag_win[3], name="ag_start_w_branch_a")
    ag_wb = _gather_start(wb_s, ag_wa[3], name="ag_start_w_branch_b")
    ag_wout = _gather_start(wout_s, ag_wb[3], name="ag_start_w_out")
    ag_wg = _gather_start(wg_s, ag_wout[3], pad_rows=FP - F, name="ag_start_w_ffn_gate")
    ag_wu = _gather_start(wu_s, ag_wg[3], pad_rows=FP - F, name="ag_start_w_ffn_up")
    ag_wd = _gather_start(wd_s, ag_wu[3], pad_rows=FP - F, name="ag_start_w_ffn_down")

    BM = 1024
    bn_big = _pick(N_BIG, 768, LANES)

    h1, rstd1 = _rms_fwd(xs, norm1_g, deps=[ag_wd[3]], name="rms1_fwd")
    fw_win = _gather_forward(ag_win, h1, name="ag_w_in")
    win_all = _gather_finish(fw_win, fw_win[4], name="ag_wait_w_in").reshape(W_IN, D)
    f_rows = (3 * WA, H_A)
    w_f = jnp.pad(win_all[3 * WA:3 * WA + H_A], ((0, LANES - H_A), (0, 0)))
    proj = _matmul([(h1, win_all)], "nt", bm=BM, bn=bn_big, b_hole=f_rows, name="proj_fwd")
    f_logit = _matmul([(h1, w_f)], "nt", bm=BM, bn=LANES, out_dtype=F32, name="forget_fwd")
    b_pad = jnp.pad(b_forget, ((0, 0), (0, LANES - H_A)))
    c_all, sneg = _fox_gate_fwd(f_logit, b_pad, name="fox_gate_fwd")
    c_heads = c_all[:, :H_A].T
    c_col, c_row = c_heads[:, :, None], c_heads[:, None, :]
    fw_wa = _gather_forward(ag_wa, proj, name="ag_w_branch_a")
    fw_wb = _gather_forward(ag_wb, fw_wa[4], name="ag_w_branch_b")
    fw_wout = _gather_forward(ag_wout, fw_wb[4], name="ag_w_out")
    oa, lse_a = _fox_fwd(proj, c_col, c_row, H_A, deps=[fw_wout[4]], name="fox_fwd")
    fw_wg = _gather_forward(ag_wg, oa, name="ag_w_ffn_gate")
    fw_wu = _gather_forward(ag_wu, fw_wg[4], name="ag_w_ffn_up")

    def heads_q(a):
        return a.reshape(T, HKV, G, DH_B).transpose(1, 2, 0, 3)

    def heads_kv(a):
        return a.reshape(T, HKV, DH_B).transpose(1, 0, 2)

    qb = heads_q(proj[:, 3 * WA:3 * WA + QB])
    kb = heads_kv(proj[:, 3 * WA + QB:3 * WA + QB + KB])
    vb = heads_kv(proj[:, 3 * WA + QB + KB:GATE_COL])
    onehot = _bucket_onehot()
    bias = _bias_table(rel_bias, onehot, name="rel_bias_table").reshape(HKV, G, 2, SWA_BLOCK, SWA_BLOCK)
    sinks_b = jnp.broadcast_to(attn_sinks.reshape(HKV, G, 1, 1), (HKV, G, 1, LANES))
    ob4, lse_b = _swa_fwd(qb, kb, vb, bias, sinks_b, deps=[fw_wu[4]], name="swa_fwd")
    ob = ob4.transpose(2, 0, 1, 3).reshape(T, QB)

    wa_t = _gather_finish(fw_wa, ob, name="ag_wait_w_branch_a")
    wb_t = _gather_finish(fw_wb, wa_t, name="ag_wait_w_branch_b")
    mixed, ya, yb = _branch_mix(oa, ob, wa_t, wb_t, proj, GATE_COL, name="branch_mix")
    fw_wd = _gather_forward(ag_wd, mixed, name="ag_w_ffn_down")
    wout = _gather_finish(fw_wout, fw_wd[4], name="ag_wait_w_out")
    x1 = _matmul([(mixed, wout)], "nn", bm=BM, bn=512, out_dtype=F32, addend=xs, name="out_proj_fwd")

    h2, rstd2 = _rms_fwd(x1, norm2_g, name="rms2_fwd")
    wg_t = _gather_finish(fw_wg, h2, name="ag_wait_w_ffn_gate")
    wu_t = _gather_finish(fw_wu, wg_t, name="ag_wait_w_ffn_up")
    gate, up, hidden = _ffn_up(h2, wg_t, wu_t, name="ffn_up")
    wd = _gather_finish(fw_wd, hidden, name="ag_wait_w_ffn_down")
    tk_f = _pick(FP, 2816, LANES)
    x2 = _matmul([(hidden, wd)], "nn", bm=BM, bn=1024, tk=tk_f, out_dtype=F32, addend=x1, name="ffn_down_fwd")

    dx2, loss_part, g_final, dx2_b = _final_loss(x2, final_g.reshape(1, D), loss_target[0], name="final_loss")
    loss = lax.psum(loss_part[0, 0], MESH_AXES)

    g_wd = _matmul([(hidden, dx2_b)], "tn", bm=1024, bn=1024, name="ffn_down_bwd_w")
    rs_wd = _scatter_start(g_wd, R_F, loss_part, name="rs_start_w_ffn_down")
    dgate, dup = _matmul([(dx2_b, wd)], "nt", bm=BM, bn=512, extras=[(gate, 0), (up, 0)], epilogue=_swiglu_bwd_tile,
                         n_out=2, deps=[rs_wd[3]], name="ffn_down_bwd_x")
    g_wg = _matmul([(dgate, h2)], "tn", bm=1024, bn=1024, name="ffn_gate_bwd_w")
    rs_wg = _scatter_start(g_wg, R_F, rs_wd[3], name="rs_start_w_ffn_gate")
    g_wu = _matmul([(dup, h2)], "tn", bm=1024, bn=1024, deps=[rs_wg[3]], name="ffn_up_bwd_w")
    rs_wu = _scatter_start(g_wu, R_F, rs_wg[3], name="rs_start_w_ffn_up")
    dh2 = _matmul([(dgate, wg_t), (dup, wu_t)], "nn", bm=BM, bn=512, tk=_pick(FP, 2816, LANES), out_dtype=F32,
                  deps=[rs_wu[3]], name="ffn_up_bwd_x")
    dx1, g_norm2, dx1_b = _rms_bwd(dx2, dh2, x1, rstd2, norm2_g, name="rms2_bwd")

    g_wout = _matmul([(mixed, dx1_b)], "tn", bm=1024, bn=1024, name="out_proj_bwd_w")
    rs_wout = _scatter_start(g_wout, D // N_DEV, rs_wu[3], name="rs_start_w_out")
    bn_mix = _pick(D, 256, LANES)
    assert GATE_COL % bn_mix == 0
    dya, dyb, dga, dgb = _matmul(
        [(dx1_b, wout)], "nt", bm=BM, bn=bn_mix, epilogue=_mix_bwd_tile, n_out=4, deps=[rs_wout[3]],
        extras=[(ya, 0), (yb, 0), (proj, GATE_COL // bn_mix), (proj, (GATE_COL + D) // bn_mix)], name="out_proj_bwd_x")
    g_wa = _matmul([(dya, oa)], "tn", bm=1024, bn=1024, name="branch_a_bwd_w")
    rs_wa = _scatter_start(g_wa, D // N_DEV, rs_wout[3], name="rs_start_w_branch_a")
    g_wb = _matmul([(dyb, ob)], "tn", bm=1024, bn=1024, deps=[rs_wa[3]], name="branch_b_bwd_w")
    rs_wb = _scatter_start(g_wb, D // N_DEV, rs_wa[3], name="rs_start_w_branch_b")
    doa = _matmul([(dya, wa_t)], "nn", bm=BM, bn=1024, deps=[rs_wb[3]], name="branch_a_bwd_x")
    dob = _matmul([(dyb, wb_t)], "nn", bm=BM, bn=1024, name="branch_b_bwd_x")

    dqa, dka, dva, dc_key, dc_query = _fox_bwd(proj, oa, doa, lse_a, c_col, c_row, H_A, name="fox_bwd")
    dc = jnp.pad((dc_key[:, 0, :] + dc_query[:, :, 0]).T, ((0, 0), (0, LANES - H_A)))
    df, g_bf = _fox_gate_bwd(dc, sneg, name="fox_gate_bwd")

    dqb4, dkb3, dvb3, dbias, dsinks = _swa_bwd(qb, kb, vb, ob4, heads_q(dob), lse_b, bias, sinks_b, name="swa_bwd")
    g_relb = _bias_table_bwd(dbias.reshape(H_B, -1), onehot, name="rel_bias_table_bwd")
    dqb = dqb4.transpose(2, 0, 1, 3).reshape(T, QB)
    dkb = dkb3.transpose(1, 0, 2).reshape(T, KB)
    dvb = dvb3.transpose(1, 0, 2).reshape(T, KB)

    dproj = jnp.concatenate([dqa.astype(BF16), dka, dva, dqb, dkb, dvb, dga, dgb], axis=1)
    g_win = _matmul([(dproj, h1)], "tn", bm=bn_big, bn=1024, out_hole=f_rows, name="proj_bwd_w")
    g_wf = _matmul([(df, h1)], "tn", bm=LANES, bn=1024, name="forget_bwd_w")
    g_win = lax.dynamic_update_slice(g_win, g_wf[:H_A], (3 * WA, 0)).reshape(N_DEV, R_IN, D)
    rs_win = _scatter_start(g_win, R_IN, rs_wb[3], name="rs_start_w_in")
    dh_f = _matmul([(df, w_f)], "nn", bm=BM, bn=1024, out_dtype=F32, deps=[rs_win[3]], name="forget_bwd_x")
    tk_big = _pick(math.gcd(3 * WA, N_BIG - 3 * WA), 1536, LANES)
    dh1 = _matmul([(dproj, win_all)], "nn", bm=BM, bn=1024, tk=tk_big, out_dtype=F32, addend=dh_f, b_hole=f_rows,
                  name="proj_bwd_x")
    grad_x, g_norm1, _ = _rms_bwd(dx1, dh1, xs, rstd1, norm1_g, name="rms1_bwd")

    def update(handle, after, w, m, v, transposed, nm):
        take = (lambda a: a[0].T) if transposed else (lambda a: a[0])
        give = (lambda a: a.T[None]) if transposed else (lambda a: a[None])
        land = _scatter_finish(handle, after, name="rs_wait_" + nm)
        return tuple(give(o) for o in _sum_adamw(land, take(w), take(m), take(v), name="adamw_" + nm))

    big = {}
    big["w_ffn_down"] = update(rs_wd, grad_x, w_ffn_down, m_w_ffn_down, v_w_ffn_down, False, "w_ffn_down")
    big["w_ffn_gate"] = update(rs_wg, big["w_ffn_down"][1], w_ffn_gate, m_w_ffn_gate, v_w_ffn_gate, True, "w_ffn_gate")
    big["w_ffn_up"] = update(rs_wu, big["w_ffn_gate"][1], w_ffn_up, m_w_ffn_up, v_w_ffn_up, True, "w_ffn_up")
    big["w_out"] = update(rs_wout, big["w_ffn_up"][1], w_out, m_w_out, v_w_out, False, "w_out")
    big["w_branch_a"] = update(rs_wa, big["w_out"][1], w_branch_a, m_w_branch_a, v_w_branch_a, True, "w_branch_a")
    big["w_branch_b"] = update(rs_wb, big["w_branch_a"][1], w_branch_b, m_w_branch_b, v_w_branch_b, True, "w_branch_b")
    big["w_in"] = update(rs_win, big["w_branch_b"][1], w_in, m_w_in, v_w_in, True, "w_in")

    n_rb = NUM_BUCKETS * H_B
    assert D >= n_rb and D >= H_A + H_B
    small = jnp.concatenate([
        g_norm1, g_norm2, g_final,
        jnp.pad(jnp.concatenate([g_bf[:, :H_A], dsinks[:, :, 0, 0].reshape(1, H_B)], axis=1), ((0, 0), (0, D - H_A - H_B))),
        jnp.pad(g_relb.reshape(1, n_rb), ((0, 0), (0, D - n_rb))),
        jnp.zeros((3, D), F32)], axis=0)
    small = _all_reduce_small(small, name="ar_small")
    gs = {
        "norm1_g": small[0:1], "norm2_g": small[1:2], "final_g": small[2],
        "b_forget": small[3:4, :H_A], "attn_sinks": small[3:4, H_A:H_A + H_B],
        "rel_bias": small[4, :n_rb].reshape(NUM_BUCKETS, H_B),
    }

    def adam_small(ws, gsm, ms, vs):
        def pack(parts):
            rows = [jnp.pad(p.reshape(1, -1), ((0, 0), (0, D - p.size))) for p in parts]
            return jnp.concatenate(rows + [jnp.ones((8 - len(rows), D), F32)], axis=0)
        d, nm_, nv_ = _adamw(pack(ws), pack(gsm), pack(ms), pack(vs), name="adamw_small")
        unpack = lambda a: [a[i, :p.size].reshape(p.shape) for i, p in enumerate(ws)]
        return unpack(d), unpack(nm_), unpack(nv_)

    small_names = ["norm1_g", "b_forget", "attn_sinks", "rel_bias", "norm2_g", "final_g"]
    small_w = [norm1_g, b_forget, attn_sinks, rel_bias, norm2_g, final_g]
    small_g = [gs[n].reshape(w.shape) for n, w in zip(small_names, small_w)]
    small_m = [m_norm1_g, m_b_forget, m_attn_sinks, m_rel_bias, m_norm2_g, m_final_g]
    small_v = [v_norm1_g, v_b_forget, v_attn_sinks, v_rel_bias, v_norm2_g, v_final_g]
    sd, sm, sv = adam_small(small_w, small_g, small_m, small_v)
    for i, n in enumerate(small_names):
        big[n] = (small_g[i], sd[i], sm[i], sv[i])

    order = ["norm1_g", "w_in", "b_forget", "attn_sinks", "rel_bias", "w_branch_a", "w_branch_b", "w_out",
             "norm2_g", "w_ffn_gate", "w_ffn_up", "w_ffn_down", "final_g"]
    grads = [big[n][0] for n in order]
    deltas = [big[n][1] for n in order]
    new_m = [big[n][2] for n in order]
    new_v = [big[n][3] for n in order]
    return (loss, grad_x[None], *grads, *deltas, *new_m, *new_v)
```

```python
import math

import numpy as np
import jax
import jax.numpy as jnp
from jax import lax
from jax.experimental import pallas as pl
from jax.experimental.pallas import tpu as pltpu

N_DEV = 8
MESH_AXES = ("x", "y", "c")
DH_A = 128
DH_B = 64
SWA_BLOCK = 128
FOX_TILE = 1024
NUM_BUCKETS = 32
MAX_DISTANCE = 128
EPS = 1e-6
ADAM_LR = 0.001
ADAM_B1 = 0.9
ADAM_B2 = 0.999
ADAM_EPS = 1e-08
ADAM_WD = 0.01
ADAM_STEP = 10
NEG = -1e30
LANES = 128
BF16_TILE_ROWS = 16
V7X_VMEM_LIMIT = 56 * 1024 * 1024
F32 = jnp.float32
BF16 = jnp.bfloat16
MESH = pl.DeviceIdType.MESH
ANY = pl.BlockSpec(memory_space=pl.ANY)


def _round_up(n, m):
    return (n + m - 1) // m * m


def _pick(n, target, mult):
    best = None
    for d in range(mult, min(n, target) + 1, mult):
        if n % d == 0:
            best = d
    return n if best is None else best


def _pcall(kernel, *, name, in_specs, dims=None, deps=(), **kw):
    deps = tuple(deps)
    if deps:
        inner, n_in, n_dep = kernel, len(in_specs), len(deps)

        def kernel(*refs):
            inner(*refs[:n_in], *refs[n_in + n_dep:])

        in_specs = list(in_specs) + [ANY] * n_dep
    call = pl.pallas_call(
        kernel, name=name, in_specs=in_specs,
        compiler_params=pltpu.CompilerParams(dimension_semantics=dims, vmem_limit_bytes=V7X_VMEM_LIMIT),
        **kw)
    return lambda *operands: call(*operands, *deps)


def _comm_call(kernel, *, name, **kw):
    return pl.pallas_call(kernel, name=name, **kw)


def _sigmoid(v):
    return 1.0 / (1.0 + jnp.exp(-v))


_DIMS = {"nn": (((1,), (0,)), ((), ())), "nt": (((1,), (1,)), ((), ())), "tn": (((0,), (0,)), ((), ()))}


def _dot(a, b, form="nn"):
    return lax.dot_general(a, b, _DIMS[form], preferred_element_type=F32)


def _matmul(pairs, form, *, bm, bn, tk=None, out_dtype=BF16, addend=None, extras=(), epilogue=None, n_out=1, deps=(),
            b_hole=None, out_hole=None, name):
    a0, b0 = pairs[0]
    if form == "tn":
        K, M = a0.shape
    else:
        M, K = a0.shape
    N = b0.shape[0] if form == "nt" else b0.shape[1]
    if b_hole is not None and form == "nt":
        N -= b_hole[1]
    tk = K if tk is None else tk
    bm, bn = min(bm, M), min(bn, N)
    assert M % bm == 0 and N % bn == 0 and K % tk == 0, (name, M, N, K, bm, bn, tk)
    nk = K // tk
    n_pairs = len(pairs)
    has_add = addend is not None

    def stored_row(first, hole):
        return pl.multiple_of(first + jnp.where(first >= hole[0], hole[1], 0), BF16_TILE_ROWS)

    if form == "tn":
        a_spec = pl.BlockSpec((tk, bm), lambda i, j, k: (k, i))
    else:
        a_spec = pl.BlockSpec((bm, tk), lambda i, j, k: (i, k))
    if b_hole is not None:
        assert len(pairs) == 1 and b_hole[0] % (bn if form == "nt" else tk) == 0 and b_hole[1] % BF16_TILE_ROWS == 0
    if form == "nt":
        if b_hole is None:
            b_spec = pl.BlockSpec((bn, tk), lambda i, j, k: (j, k))
        else:
            b_spec = pl.BlockSpec((pl.Element(bn), pl.Element(tk)),
                                  lambda i, j, k: (stored_row(j * bn, b_hole), pl.multiple_of(k * tk, LANES)))
    elif b_hole is None or form == "tn":
        assert b_hole is None
        b_spec = pl.BlockSpec((tk, bn), lambda i, j, k: (k, j))
    else:
        b_spec = pl.BlockSpec((pl.Element(tk), pl.Element(bn)),
                              lambda i, j, k: (stored_row(k * tk, b_hole), pl.multiple_of(j * bn, LANES)))
    if out_hole is None:
        o_spec = pl.BlockSpec((bm, bn), lambda i, j, k: (i, j))
    else:
        assert out_hole[0] % bm == 0 and out_hole[1] % BF16_TILE_ROWS == 0 and epilogue is None and not has_add
        o_spec = pl.BlockSpec((pl.Element(bm), pl.Element(bn)),
                              lambda i, j, k: (stored_row(i * bm, out_hole), pl.multiple_of(j * bn, LANES)))
    m_stored = M + (out_hole[1] if out_hole is not None else 0)
    n_extra = len(extras)
    n_in = 2 * n_pairs + has_add + n_extra

    def kernel(*refs):
        ab = refs[:2 * n_pairs]
        c_ref = refs[2 * n_pairs] if has_add else None
        extra_refs = refs[2 * n_pairs + has_add:n_in]
        o_refs = refs[n_in:n_in + n_out]

        def partial_sum():
            acc = _dot(ab[0][...], ab[1][...], form)
            for p in range(1, n_pairs):
                acc = acc + _dot(ab[2 * p][...], ab[2 * p + 1][...], form)
            return acc

        def finish(acc):
            if has_add:
                acc = acc + c_ref[...].astype(F32)
            outs = (acc,) if epilogue is None else epilogue(acc, *[r[...] for r in extra_refs])
            for o_ref, val in zip(o_refs, outs, strict=True):
                o_ref[...] = val.astype(o_ref.dtype)

        if nk == 1:
            finish(partial_sum())
        else:
            acc_ref = refs[-1]
            k = pl.program_id(2)

            @pl.when(k == 0)
            def _():
                acc_ref[...] = jnp.zeros_like(acc_ref)

            acc_ref[...] += partial_sum()

            @pl.when(k == nk - 1)
            def _():
                finish(acc_ref[...])

    operands, in_specs = [], []
    for a, b in pairs:
        operands += [a, b]
        in_specs += [a_spec, b_spec]
    if has_add:
        operands.append(addend)
        in_specs.append(o_spec)
    for arr, first in extras:
        operands.append(arr)
        in_specs.append(pl.BlockSpec((bm, bn), lambda i, j, k, first=first: (i, first + j)))
    out_shape = tuple(jax.ShapeDtypeStruct((m_stored, N), out_dtype) for _ in range(n_out))
    out = _pcall(
        kernel, name=name, dims=("parallel", "parallel", "arbitrary"), deps=deps,
        out_shape=out_shape, grid=(M // bm, N // bn, nk), in_specs=in_specs, out_specs=(o_spec,) * n_out,
        scratch_shapes=[pltpu.VMEM((bm, bn), F32)] if nk > 1 else [],
    )(*operands)
    return out[0] if epilogue is None else out


def _rms_fwd(x, g, *, deps=(), name):
    T, D = x.shape
    br = _pick(T, 256, 8)

    def kernel(x_ref, g_ref, h_ref, r_ref):
        xv = x_ref[...]
        rstd = lax.rsqrt(jnp.mean(xv * xv, axis=-1, keepdims=True) + EPS)
        h_ref[...] = (xv * rstd * g_ref[...]).astype(BF16)
        r_ref[...] = rstd

    return _pcall(
        kernel, name=name, dims=("parallel",), deps=deps,
        out_shape=(jax.ShapeDtypeStruct((T, D), BF16), jax.ShapeDtypeStruct((T, 1), F32)),
        grid=(T // br,),
        in_specs=[pl.BlockSpec((br, D), lambda i: (i, 0)), pl.BlockSpec((1, D), lambda i: (0, 0))],
        out_specs=(pl.BlockSpec((br, D), lambda i: (i, 0)), pl.BlockSpec((br, 1), lambda i: (i, 0))),
    )(x, g)


def _rms_bwd(dres, dh, x, rstd, g, *, deps=(), name):
    T, D = x.shape
    br = _pick(T, 256, 8)

    def kernel(dres_ref, dh_ref, x_ref, r_ref, g_ref, dx_ref, gg_ref, dxb_ref):
        @pl.when(pl.program_id(0) == 0)
        def _():
            gg_ref[...] = jnp.zeros_like(gg_ref)

        rstd_v = r_ref[...]
        xhat = x_ref[...] * rstd_v
        dhv = dh_ref[...].astype(F32)
        dxhat = dhv * g_ref[...]
        proj = jnp.mean(dxhat * xhat, axis=-1, keepdims=True)
        dx = dres_ref[...] + rstd_v * (dxhat - xhat * proj)
        dx_ref[...] = dx
        dxb_ref[...] = dx.astype(BF16)
        gg_ref[...] += jnp.sum(dhv * xhat, axis=0, keepdims=True)

    row = pl.BlockSpec((br, D), lambda i: (i, 0))
    vec = pl.BlockSpec((1, D), lambda i: (0, 0))
    return _pcall(
        kernel, name=name, dims=("arbitrary",), deps=deps,
        out_shape=(jax.ShapeDtypeStruct((T, D), F32), jax.ShapeDtypeStruct((1, D), F32),
                   jax.ShapeDtypeStruct((T, D), BF16)),
        grid=(T // br,),
        in_specs=[row, row, row, pl.BlockSpec((br, 1), lambda i: (i, 0)), vec],
        out_specs=(row, vec, row),
    )(dres, dh, x, rstd, g)


def _final_loss(x2, gf, target, *, name):
    T, D = x2.shape
    br = _pick(T, 256, 8)

    def kernel(x_ref, g_ref, t_ref, dx_ref, loss_ref, gg_ref, dxb_ref):
        @pl.when(pl.program_id(0) == 0)
        def _():
            gg_ref[...] = jnp.zeros_like(gg_ref)
            loss_ref[...] = jnp.zeros_like(loss_ref)

        xv = x_ref[...]
        gv = g_ref[...]
        rstd = lax.rsqrt(jnp.mean(xv * xv, axis=-1, keepdims=True) + EPS)
        xhat = xv * rstd
        err = xhat * gv - t_ref[...]
        loss_ref[...] += 0.5 * jnp.sum(jnp.mean(err * err, axis=-1, keepdims=True), axis=0, keepdims=True)
        dy = err * (1.0 / D)
        dxhat = dy * gv
        proj = jnp.mean(dxhat * xhat, axis=-1, keepdims=True)
        dx = rstd * (dxhat - xhat * proj)
        dx_ref[...] = dx
        dxb_ref[...] = dx.astype(BF16)
        gg_ref[...] += jnp.sum(dy * xhat, axis=0, keepdims=True)

    row = pl.BlockSpec((br, D), lambda i: (i, 0))
    vec = pl.BlockSpec((1, D), lambda i: (0, 0))
    return _pcall(
        kernel, name=name, dims=("arbitrary",),
        out_shape=(jax.ShapeDtypeStruct((T, D), F32), jax.ShapeDtypeStruct((1, 1), F32),
                   jax.ShapeDtypeStruct((1, D), F32), jax.ShapeDtypeStruct((T, D), BF16)),
        grid=(T // br,),
        in_specs=[row, vec, row],
        out_specs=(row, pl.BlockSpec((1, 1), lambda i: (0, 0)), vec, row),
    )(x2, gf, target)


def _scan_rows(v, reverse):
    n = v.shape[0]
    row = lax.broadcasted_iota(jnp.int32, v.shape, 0)
    s = 1
    while s < n:
        if reverse:
            v = v + jnp.where(row < n - s, pltpu.roll(v, n - s, 0), 0.0)
        else:
            v = v + jnp.where(row >= s, pltpu.roll(v, s, 0), 0.0)
        s *= 2
    return v


def _fox_gate_fwd(f, bias, *, name):
    T = f.shape[0]
    tb = _pick(T, 512, 8)

    def kernel(f_ref, b_ref, c_ref, s_ref, carry):
        @pl.when(pl.program_id(0) == 0)
        def _():
            carry[...] = jnp.zeros_like(carry)

        fa = f_ref[...] + b_ref[...]
        z = jnp.exp(-jnp.abs(fa))
        log_f = jnp.minimum(fa, 0.0) - jnp.log(1.0 + z)
        s_ref[...] = jnp.where(fa >= 0, z, 1.0) / (1.0 + z)
        c = _scan_rows(log_f, False) + carry[...]
        c_ref[...] = c
        carry[...] = c[tb - 1:tb, :]

    blk = pl.BlockSpec((tb, LANES), lambda i: (i, 0))
    return _pcall(
        kernel, name=name, dims=("arbitrary",),
        out_shape=(jax.ShapeDtypeStruct((T, LANES), F32), jax.ShapeDtypeStruct((T, LANES), F32)),
        grid=(T // tb,),
        in_specs=[blk, pl.BlockSpec((1, LANES), lambda i: (0, 0))],
        out_specs=(blk, blk),
        scratch_shapes=[pltpu.VMEM((1, LANES), F32)],
    )(f, bias)


def _fox_gate_bwd(dc, sneg, *, name):
    T = dc.shape[0]
    tb = _pick(T, 512, 8)
    nb = T // tb

    def kernel(dc_ref, s_ref, df_ref, gb_ref, carry):
        @pl.when(pl.program_id(0) == 0)
        def _():
            carry[...] = jnp.zeros_like(carry)
            gb_ref[...] = jnp.zeros_like(gb_ref)

        dlog = _scan_rows(dc_ref[...], True) + carry[...]
        carry[...] = dlog[0:1, :]
        dfa = dlog * s_ref[...]
        df_ref[...] = dfa.astype(BF16)
        gb_ref[...] += jnp.sum(dfa, axis=0, keepdims=True)

    blk = pl.BlockSpec((tb, LANES), lambda i: (nb - 1 - i, 0))
    return _pcall(
        kernel, name=name, dims=("arbitrary",),
        out_shape=(jax.ShapeDtypeStruct((T, LANES), BF16), jax.ShapeDtypeStruct((1, LANES), F32)),
        grid=(nb,),
        in_specs=[blk, blk],
        out_specs=(blk, pl.BlockSpec((1, LANES), lambda i: (0, 0))),
        scratch_shapes=[pltpu.VMEM((1, LANES), F32)],
    )(dc, sneg)


def _fox_scores(q, k, cq, ck, diagonal):
    s = _dot(q, k, "nt") * (DH_A ** -0.5) + cq - ck
    if diagonal:
        row = lax.broadcasted_iota(jnp.int32, s.shape, 0)
        col = lax.broadcasted_iota(jnp.int32, s.shape, 1)
        s = jnp.where(col <= row, s, NEG)
    return s


def _fox_fwd(proj, c_col, c_row, n_heads, *, deps=(), name):
    T = proj.shape[0]
    H = n_heads
    tq = tk = _pick(T, FOX_TILE, 128)
    nq = T // tq

    def kernel(q_ref, k_ref, v_ref, cq_ref, ck_ref, o_ref, lse_ref, m_sc, l_sc, acc_sc):
        i, j = pl.program_id(1), pl.program_id(2)

        @pl.when(j == 0)
        def _():
            m_sc[...] = jnp.full_like(m_sc, NEG)
            l_sc[...] = jnp.zeros_like(l_sc)
            acc_sc[...] = jnp.zeros_like(acc_sc)

        def tile(diagonal):
            s = _fox_scores(q_ref[...], k_ref[...], cq_ref[...], ck_ref[...], diagonal)
            m_prev = m_sc[...]
            m_new = jnp.maximum(m_prev, jnp.max(s, axis=-1, keepdims=True))
            alpha = jnp.exp(m_prev - m_new)
            p = jnp.exp(s - m_new)
            l_sc[...] = alpha * l_sc[...] + jnp.sum(p, axis=-1, keepdims=True)
            acc_sc[...] = alpha * acc_sc[...] + _dot(p.astype(BF16), v_ref[...])
            m_sc[...] = m_new

        @pl.when(j < i)
        def _():
            tile(False)

        @pl.when(j == i)
        def _():
            tile(True)
            l = l_sc[...]
            o_ref[...] = (acc_sc[...] / l).astype(BF16)
            lse_ref[...] = m_sc[...] + jnp.log(l)

    return _pcall(
        kernel, name=name, dims=("parallel", "parallel", "arbitrary"), deps=deps,
        out_shape=(jax.ShapeDtypeStruct((T, H * DH_A), BF16), jax.ShapeDtypeStruct((H, T, 1), F32)),
        grid=(H, nq, nq),
        in_specs=[
            pl.BlockSpec((tq, DH_A), lambda h, i, j: (i, h)),
            pl.BlockSpec((tk, DH_A), lambda h, i, j: (jnp.minimum(j, i), H + h)),
            pl.BlockSpec((tk, DH_A), lambda h, i, j: (jnp.minimum(j, i), 2 * H + h)),
            pl.BlockSpec((None, tq, 1), lambda h, i, j: (h, i, 0)),
            pl.BlockSpec((None, 1, tk), lambda h, i, j: (h, 0, jnp.minimum(j, i))),
        ],
        out_specs=(pl.BlockSpec((tq, DH_A), lambda h, i, j: (i, h)),
                   pl.BlockSpec((None, tq, 1), lambda h, i, j: (h, i, 0))),
        scratch_shapes=[pltpu.VMEM((tq, 1), F32), pltpu.VMEM((tq, 1), F32), pltpu.VMEM((tq, DH_A), F32)],
    )(proj, proj, proj, c_col, c_row)


def _fox_bwd(proj, o, do, lse, c_col, c_row, n_heads, *, name):
    T = proj.shape[0]
    H = n_heads
    tq = tk = _pick(T, FOX_TILE, 128)
    nq = T // tq
    scale = DH_A ** -0.5

    def kernel(q_ref, k_ref, v_ref, do_ref, lse_ref, o_ref, cq_ref, ck_ref,
               dq_ref, dk_ref, dv_ref, dc_ref, dcq_ref, dk_sc, dv_sc, dc_sc):
        j, i = pl.program_id(1), pl.program_id(2)

        @pl.when((j == 0) & (i == 0))
        def _():
            dq_ref[...] = jnp.zeros_like(dq_ref)
            dcq_ref[...] = jnp.zeros_like(dcq_ref)

        @pl.when(i == 0)
        def _():
            dk_sc[...] = jnp.zeros_like(dk_sc)
            dv_sc[...] = jnp.zeros_like(dv_sc)
            dc_sc[...] = jnp.zeros_like(dc_sc)

        def tile(diagonal):
            q, k, v, dov = q_ref[...], k_ref[...], v_ref[...], do_ref[...]
            s = _fox_scores(q, k, cq_ref[...], ck_ref[...], diagonal)
            p = jnp.exp(s - lse_ref[...])
            dv_sc[...] += _dot(p.astype(BF16), dov, "tn")
            dp = _dot(dov, v, "nt")
            delta = jnp.sum(dov.astype(F32) * o_ref[...].astype(F32), axis=1, keepdims=True)
            ds = p * (dp - delta)
            dc_sc[...] -= jnp.sum(ds, axis=0, keepdims=True)
            dsb = ds.astype(BF16)
            dk_sc[...] += _dot(dsb, q, "tn") * scale
            rows = pl.ds(pl.multiple_of(i * tq, tq), tq)
            dq_ref[rows, :] += _dot(dsb, k) * scale
            dcq_ref[rows, :] += jnp.sum(ds, axis=1, keepdims=True)

        @pl.when(i > j)
        def _():
            tile(False)

        @pl.when(i == j)
        def _():
            tile(True)

        @pl.when(i == nq - 1)
        def _():
            dk_ref[...] = dk_sc[...].astype(BF16)
            dv_ref[...] = dv_sc[...].astype(BF16)
            dc_ref[...] = dc_sc[...]

    qi = lambda j, i: jnp.maximum(i, j)
    return _pcall(
        kernel, name=name, dims=("parallel", "arbitrary", "arbitrary"),
        out_shape=(jax.ShapeDtypeStruct((T, H * DH_A), F32), jax.ShapeDtypeStruct((T, H * DH_A), BF16),
                   jax.ShapeDtypeStruct((T, H * DH_A), BF16), jax.ShapeDtypeStruct((H, 1, T), F32),
                   jax.ShapeDtypeStruct((H, T, 1), F32)),
        grid=(H, nq, nq),
        in_specs=[
            pl.BlockSpec((tq, DH_A), lambda h, j, i: (qi(j, i), h)),
            pl.BlockSpec((tk, DH_A), lambda h, j, i: (j, H + h)),
            pl.BlockSpec((tk, DH_A), lambda h, j, i: (j, 2 * H + h)),
            pl.BlockSpec((tq, DH_A), lambda h, j, i: (qi(j, i), h)),
            pl.BlockSpec((None, tq, 1), lambda h, j, i: (h, qi(j, i), 0)),
            pl.BlockSpec((tq, DH_A), lambda h, j, i: (qi(j, i), h)),
            pl.BlockSpec((None, tq, 1), lambda h, j, i: (h, qi(j, i), 0)),
            pl.BlockSpec((None, 1, tk), lambda h, j, i: (h, 0, j)),
        ],
        out_specs=(pl.BlockSpec((T, DH_A), lambda h, j, i: (0, h)),
                   pl.BlockSpec((tk, DH_A), lambda h, j, i: (j, h)),
                   pl.BlockSpec((tk, DH_A), lambda h, j, i: (j, h)),
                   pl.BlockSpec((None, 1, tk), lambda h, j, i: (h, 0, j)),
                   pl.BlockSpec((None, T, 1), lambda h, j, i: (h, 0, 0))),
        scratch_shapes=[pltpu.VMEM((tk, DH_A), F32), pltpu.VMEM((tk, DH_A), F32), pltpu.VMEM((1, tk), F32)],
    )(proj, proj, proj, do, lse, o, c_col, c_row)


def _t5_bucket(dist):
    max_exact = NUM_BUCKETS // 2
    small = dist < max_exact
    large = max_exact + (np.log(np.maximum(dist, 1) / max_exact) / np.log(MAX_DISTANCE / max_exact)
                         * (NUM_BUCKETS - max_exact)).astype(np.int64)
    large = np.minimum(large, NUM_BUCKETS - 1)
    return np.where(small, dist, large)


def _bucket_onehot():
    ql = np.arange(SWA_BLOCK)[:, None]
    kl = np.arange(2 * SWA_BLOCK)[None, :]
    bucket = _t5_bucket(np.clip(ql + SWA_BLOCK - kl, 0, None))
    bucket = np.stack([bucket[:, :SWA_BLOCK], bucket[:, SWA_BLOCK:]], axis=0)
    onehot = (bucket.reshape(1, -1) == np.arange(NUM_BUCKETS)[:, None])
    return jnp.asarray(onehot, dtype=BF16)


def _split3(v):
    hi = v.astype(BF16)
    r1 = v - hi.astype(F32)
    mid = r1.astype(BF16)
    lo = (r1 - mid.astype(F32)).astype(BF16)
    return hi, mid, lo


def _bias_table(rel_bias, onehot, *, name):
    B, H = rel_bias.shape
    n = onehot.shape[1]
    bc = _pick(n, 8192, LANES)

    def kernel(rb_ref, oh_ref, o_ref):
        hi, mid, lo = _split3(rb_ref[...])
        oh = oh_ref[...]
        o_ref[...] = _dot(hi, oh, "tn") + _dot(mid, oh, "tn") + _dot(lo, oh, "tn")

    return _pcall(
        kernel, name=name, dims=("parallel",),
        out_shape=jax.ShapeDtypeStruct((H, n), F32), grid=(n // bc,),
        in_specs=[pl.BlockSpec((B, H), lambda i: (0, 0)), pl.BlockSpec((B, bc), lambda i: (0, i))],
        out_specs=pl.BlockSpec((H, bc), lambda i: (0, i)),
    )(rel_bias, onehot)


def _bias_table_bwd(dbias, onehot, *, name):
    H, n = dbias.shape
    B = onehot.shape[0]
    bc = _pick(n, 8192, LANES)

    def kernel(db_ref, oh_ref, o_ref):
        @pl.when(pl.program_id(0) == 0)
        def _():
            o_ref[...] = jnp.zeros_like(o_ref)

        hi, mid, lo = _split3(db_ref[...])
        oh = oh_ref[...]
        o_ref[...] += _dot(oh, hi, "nt") + _dot(oh, mid, "nt") + _dot(oh, lo, "nt")

    return _pcall(
        kernel, name=name, dims=("arbitrary",),
        out_shape=jax.ShapeDtypeStruct((B, H), F32), grid=(n // bc,),
        in_specs=[pl.BlockSpec((H, bc), lambda i: (0, i)), pl.BlockSpec((B, bc), lambda i: (0, i))],
        out_specs=pl.BlockSpec((B, H), lambda i: (0, 0)),
    )(dbias, onehot)


def _swa_probs(q2, kp, kc, bias_ref, lse, n, G):
    Q = SWA_BLOCK
    scale = DH_B ** -0.5
    sp = (_dot(q2, kp, "nt") * scale).reshape(G, Q, Q) + bias_ref[:, 0]
    sc = (_dot(q2, kc, "nt") * scale).reshape(G, Q, Q) + bias_ref[:, 1]
    row = lax.broadcasted_iota(jnp.int32, (G, Q, Q), 1)
    col = lax.broadcasted_iota(jnp.int32, (G, Q, Q), 2)
    vis_p = (col > row) & (n > 0)
    vis_c = col <= row
    sp = jnp.where(vis_p, sp, NEG)
    sc = jnp.where(vis_c, sc, NEG)
    if lse is None:
        return sp, sc, vis_p, vis_c
    pp = jnp.where(vis_p, jnp.exp(sp - lse), 0.0)
    pc = jnp.where(vis_c, jnp.exp(sc - lse), 0.0)
    return pp, pc


def _swa_fwd(q, k, v, bias, sinks, *, deps=(), name):
    HKV, G, T, _ = q.shape
    Q = SWA_BLOCK
    N = T // Q

    def kernel(q_ref, kp_ref, kc_ref, vp_ref, vc_ref, b_ref, s_ref, o_ref, lse_ref):
        n = pl.program_id(1)
        q2 = q_ref[...].reshape(G * Q, DH_B)
        sp, sc, vis_p, vis_c = _swa_probs(q2, kp_ref[...], kc_ref[...], b_ref, None, n, G)
        sink = s_ref[...][:, :, 0:1]
        m = jnp.maximum(jnp.maximum(jnp.max(sp, axis=-1, keepdims=True), jnp.max(sc, axis=-1, keepdims=True)), sink)
        pp = jnp.where(vis_p, jnp.exp(sp - m), 0.0)
        pc = jnp.where(vis_c, jnp.exp(sc - m), 0.0)
        denom = jnp.sum(pp, axis=-1, keepdims=True) + jnp.sum(pc, axis=-1, keepdims=True) + jnp.exp(sink - m)
        o = _dot(pp.reshape(G * Q, Q).astype(BF16), vp_ref[...]) + _dot(pc.reshape(G * Q, Q).astype(BF16), vc_ref[...])
        o_ref[...] = (o.reshape(G, Q, DH_B) / denom).astype(BF16)
        lse_ref[...] = m + jnp.log(denom)

    prev = lambda h, n: (h, jnp.maximum(n - 1, 0), 0)
    cur = lambda h, n: (h, n, 0)
    kv = lambda f: pl.BlockSpec((None, Q, DH_B), f)
    return _pcall(
        kernel, name=name, dims=("parallel", "parallel"), deps=deps,
        out_shape=(jax.ShapeDtypeStruct((HKV, G, T, DH_B), BF16), jax.ShapeDtypeStruct((HKV, G, T, 1), F32)),
        grid=(HKV, N),
        in_specs=[pl.BlockSpec((None, G, Q, DH_B), lambda h, n: (h, 0, n, 0)),
                  kv(prev), kv(cur), kv(prev), kv(cur),
                  pl.BlockSpec((None, G, 2, Q, Q), lambda h, n: (h, 0, 0, 0, 0)),
                  pl.BlockSpec((None, G, 1, LANES), lambda h, n: (h, 0, 0, 0))],
        out_specs=(pl.BlockSpec((None, G, Q, DH_B), lambda h, n: (h, 0, n, 0)),
                   pl.BlockSpec((None, G, Q, 1), lambda h, n: (h, 0, n, 0))),
    )(q, k, k, v, v, bias, sinks)


def _swa_bwd(q, k, v, o, do, lse, bias, sinks, *, name):
    HKV, G, T, _ = q.shape
    Q = SWA_BLOCK
    N = T // Q
    scale = DH_B ** -0.5

    def kernel(q_ref, kp_ref, kc_ref, vp_ref, vc_ref, o_ref, do_ref, lse_ref, b_ref, s_ref,
               dq_ref, dk_ref, dv_ref, db_ref, ds_ref, ck_sc, cv_sc):
        n = pl.program_id(1)

        @pl.when(n == 0)
        def _():
            db_ref[...] = jnp.zeros_like(db_ref)
            ds_ref[...] = jnp.zeros_like(ds_ref)
            ck_sc[...] = jnp.zeros_like(ck_sc)
            cv_sc[...] = jnp.zeros_like(cv_sc)

        @pl.when(n < N)
        def _():
            q2 = q_ref[...].reshape(G * Q, DH_B)
            do3 = do_ref[...]
            do2 = do3.reshape(G * Q, DH_B)
            kp, kc, vp, vc = kp_ref[...], kc_ref[...], vp_ref[...], vc_ref[...]
            lse_v = lse_ref[...]
            pp, pc = _swa_probs(q2, kp, kc, b_ref, lse_v, n, G)
            delta = jnp.sum(do3.astype(F32) * o_ref[...].astype(F32), axis=-1, keepdims=True)
            dsp = pp * (_dot(do2, vp, "nt").reshape(G, Q, Q) - delta)
            dsc = pc * (_dot(do2, vc, "nt").reshape(G, Q, Q) - delta)
            p_sink = jnp.exp(s_ref[...][:, :, 0:1] - lse_v)
            ds_ref[...] += jnp.broadcast_to(-jnp.sum(p_sink * delta, axis=1, keepdims=True), (G, 1, LANES))
            db_ref[:, 0] += dsp
            db_ref[:, 1] += dsc
            dsp2 = dsp.reshape(G * Q, Q).astype(BF16)
            dsc2 = dsc.reshape(G * Q, Q).astype(BF16)
            dq = (_dot(dsp2, kp) + _dot(dsc2, kc)) * scale
            dq_ref[...] = dq.reshape(G, Q, DH_B).astype(BF16)
            pp2 = pp.reshape(G * Q, Q).astype(BF16)
            pc2 = pc.reshape(G * Q, Q).astype(BF16)
            dk_ref[...] = (ck_sc[...] + _dot(dsp2, q2, "tn") * scale).astype(BF16)
            dv_ref[...] = (cv_sc[...] + _dot(pp2, do2, "tn")).astype(BF16)
            ck_sc[...] = _dot(dsc2, q2, "tn") * scale
            cv_sc[...] = _dot(pc2, do2, "tn")

        @pl.when(n == N)
        def _():
            dk_ref[...] = ck_sc[...].astype(BF16)
            dv_ref[...] = cv_sc[...].astype(BF16)

    qn = lambda n: jnp.minimum(n, N - 1)
    prev = lambda h, n: (h, jnp.maximum(qn(n) - 1, 0), 0)
    cur = lambda h, n: (h, qn(n), 0)
    out_kv = lambda h, n: (h, jnp.maximum(n - 1, 0), 0)
    kv = lambda f: pl.BlockSpec((None, Q, DH_B), f)
    qspec = pl.BlockSpec((None, G, Q, DH_B), lambda h, n: (h, 0, qn(n), 0))
    bspec = pl.BlockSpec((None, G, 2, Q, Q), lambda h, n: (h, 0, 0, 0, 0))
    sspec = pl.BlockSpec((None, G, 1, LANES), lambda h, n: (h, 0, 0, 0))
    return _pcall(
        kernel, name=name, dims=("parallel", "arbitrary"),
        out_shape=(jax.ShapeDtypeStruct((HKV, G, T, DH_B), BF16), jax.ShapeDtypeStruct((HKV, T, DH_B), BF16),
                   jax.ShapeDtypeStruct((HKV, T, DH_B), BF16), jax.ShapeDtypeStruct((HKV, G, 2, Q, Q), F32),
                   jax.ShapeDtypeStruct((HKV, G, 1, LANES), F32)),
        grid=(HKV, N + 1),
        in_specs=[qspec, kv(prev), kv(cur), kv(prev), kv(cur), qspec, qspec,
                  pl.BlockSpec((None, G, Q, 1), lambda h, n: (h, 0, qn(n), 0)), bspec, sspec],
        out_specs=(qspec, kv(out_kv), kv(out_kv), bspec, sspec),
        scratch_shapes=[pltpu.VMEM((Q, DH_B), F32), pltpu.VMEM((Q, DH_B), F32)],
    )(q, k, k, v, v, o, do, lse, bias, sinks)


def _branch_mix(oa, ob, wa_t, wb_t, proj, gate_col, *, deps=(), name):
    T = oa.shape[0]
    D = wa_t.shape[0]
    bm, bn = _pick(T, 1024, 8), _pick(D, 512, LANES)
    assert gate_col % bn == 0
    ga0, gb0 = gate_col // bn, (gate_col + D) // bn

    def kernel(oa_ref, ob_ref, wa_ref, wb_ref, ga_ref, gb_ref, mix_ref, ya_ref, yb_ref):
        ya = _dot(oa_ref[...], wa_ref[...], "nt")
        yb = _dot(ob_ref[...], wb_ref[...], "nt")
        mix = _sigmoid(ga_ref[...].astype(F32)) * ya + _sigmoid(gb_ref[...].astype(F32)) * yb
        mix_ref[...] = mix.astype(BF16)
        ya_ref[...] = ya.astype(BF16)
        yb_ref[...] = yb.astype(BF16)

    out = pl.BlockSpec((bm, bn), lambda i, j: (i, j))
    return _pcall(
        kernel, name=name, dims=("parallel", "parallel"), deps=deps,
        out_shape=tuple(jax.ShapeDtypeStruct((T, D), BF16) for _ in range(3)),
        grid=(T // bm, D // bn),
        in_specs=[pl.BlockSpec((bm, oa.shape[1]), lambda i, j: (i, 0)),
                  pl.BlockSpec((bm, ob.shape[1]), lambda i, j: (i, 0)),
                  pl.BlockSpec((bn, oa.shape[1]), lambda i, j: (j, 0)),
                  pl.BlockSpec((bn, ob.shape[1]), lambda i, j: (j, 0)),
                  pl.BlockSpec((bm, bn), lambda i, j: (i, ga0 + j)),
                  pl.BlockSpec((bm, bn), lambda i, j: (i, gb0 + j))],
        out_specs=(out, out, out),
    )(oa, ob, wa_t, wb_t, proj, proj)


def _mix_bwd_tile(dmixed, ya, yb, ga, gb):
    sa = _sigmoid(ga.astype(F32))
    sb = _sigmoid(gb.astype(F32))
    return (dmixed * sa, dmixed * sb,
            dmixed * ya.astype(F32) * sa * (1.0 - sa), dmixed * yb.astype(F32) * sb * (1.0 - sb))


def _ffn_up(h, wg_t, wu_t, *, name):
    T, D = h.shape
    FP = wg_t.shape[0]
    bm, bn = _pick(T, 1024, 8), _pick(FP, 512, LANES)

    def kernel(h_ref, wg_ref, wu_ref, g_ref, u_ref, hid_ref):
        hv = h_ref[...]
        g = _dot(hv, wg_ref[...], "nt")
        u = _dot(hv, wu_ref[...], "nt")
        g_ref[...] = g.astype(BF16)
        u_ref[...] = u.astype(BF16)
        hid_ref[...] = (g * _sigmoid(g) * u).astype(BF16)

    out = pl.BlockSpec((bm, bn), lambda i, j: (i, j))
    wspec = pl.BlockSpec((bn, D), lambda i, j: (j, 0))
    return _pcall(
        kernel, name=name, dims=("parallel", "parallel"),
        out_shape=tuple(jax.ShapeDtypeStruct((T, FP), BF16) for _ in range(3)),
        grid=(T // bm, FP // bn),
        in_specs=[pl.BlockSpec((bm, D), lambda i, j: (i, 0)), wspec, wspec],
        out_specs=(out, out, out),
    )(h, wg_t, wu_t)


def _swiglu_bwd_tile(dhid, gate, up):
    g = gate.astype(F32)
    sg = _sigmoid(g)
    return dhid * up.astype(F32) * sg * (1.0 + g * (1.0 - sg)), dhid * g * sg


def _adamw_update(w, g, m, v):
    nm = ADAM_B1 * m + (1.0 - ADAM_B1) * g
    nv = ADAM_B2 * v + (1.0 - ADAM_B2) * (g * g)
    m_hat = nm / (1.0 - ADAM_B1 ** ADAM_STEP)
    v_hat = nv / (1.0 - ADAM_B2 ** ADAM_STEP)
    return -ADAM_LR * (m_hat / (jnp.sqrt(v_hat) + ADAM_EPS) + ADAM_WD * w), nm, nv


def _adamw(w, g, m, v, *, name):
    Rw, Cw = w.shape
    br = _pick(Rw, max(8, (1 << 19) // Cw // 8 * 8), 8)

    def kernel(w_ref, g_ref, m_ref, v_ref, d_ref, nm_ref, nv_ref):
        d_ref[...], nm_ref[...], nv_ref[...] = _adamw_update(w_ref[...], g_ref[...], m_ref[...], v_ref[...])

    blk = pl.BlockSpec((br, Cw), lambda i: (i, 0))
    return _pcall(
        kernel, name=name, dims=("parallel",),
        out_shape=tuple(jax.ShapeDtypeStruct((Rw, Cw), F32) for _ in range(3)),
        grid=(Rw // br,), in_specs=[blk, blk, blk, blk], out_specs=(blk, blk, blk),
    )(w, g, m, v)


def _sum_adamw(land, w, m, v, *, name):
    _, R, C = land.shape
    bc = _pick(C, max(LANES, (1 << 18) // R // LANES * LANES), LANES)

    def kernel(l_ref, w_ref, m_ref, v_ref, g_ref, d_ref, nm_ref, nv_ref):
        g = l_ref[0].astype(F32)
        for k in range(1, N_DEV):
            g = g + l_ref[k].astype(F32)
        g_ref[...] = g
        d_ref[...], nm_ref[...], nv_ref[...] = _adamw_update(w_ref[...], g, m_ref[...], v_ref[...])

    blk = pl.BlockSpec((R, bc), lambda i: (0, i))
    return _pcall(
        kernel, name=name, dims=("parallel",),
        out_shape=tuple(jax.ShapeDtypeStruct((R, C), F32) for _ in range(4)),
        grid=(C // bc,),
        in_specs=[pl.BlockSpec((N_DEV, R, bc), lambda i: (0, 0, i)), blk, blk, blk],
        out_specs=(blk, blk, blk, blk),
    )(land, w, m, v)


def _my_place():
    return lax.axis_index("x"), lax.axis_index("y"), lax.axis_index("c")


def _flip(v, bit):
    return 1 - v if bit else v


HBM_SPEC = pl.BlockSpec(memory_space=pltpu.HBM)
SEM_SPEC = pl.BlockSpec(memory_space=pltpu.SEMAPHORE)
SPLIT_EFFECT = pltpu.SideEffectType.DATAFLOW_SIDE_EFFECTING
N_PEERS = N_DEV - 1


def _hbm(a):
    return pltpu.with_memory_space_constraint(a, pltpu.HBM)


def _peer(x, y, c, k):
    return _flip(x, k & 4), _flip(y, k & 2), _flip(c, k & 1)


def _row_block(ref, idx, R):
    if len(ref.shape) == 3:
        return ref.at[idx]
    return ref.at[pl.ds(pl.multiple_of(idx * R, BF16_TILE_ROWS), R), :]


def _remote_pair(src, dst, arrives, send_sem, recv_sem, to):
    send = pltpu.make_async_remote_copy(src_ref=src, dst_ref=dst, send_sem=send_sem, recv_sem=recv_sem,
                                        device_id=to, device_id_type=MESH)
    recv = pltpu.make_async_remote_copy(src_ref=src, dst_ref=arrives, send_sem=send_sem, recv_sem=recv_sem,
                                        device_id=to, device_id_type=MESH)
    return send, recv


def _split_call(body, arrays, sems, n_through, out_sems, after, token, *, name):
    out_shape = [pltpu.SemaphoreType.DMA((n,)) for n in out_sems]
    out_shape += [pltpu.HBM(a.shape, a.dtype) for a in arrays[:n_through]]
    out_specs = [SEM_SPEC] * len(out_sems) + [HBM_SPEC] * n_through
    if token:
        out_shape.append(jax.ShapeDtypeStruct((8, LANES), F32))
        out_specs.append(pl.BlockSpec(memory_space=pltpu.VMEM))
    out = pl.pallas_call(
        body, name=name, out_shape=tuple(out_shape),
        in_specs=tuple([HBM_SPEC] * len(arrays) + [SEM_SPEC] * len(sems) + [ANY]),
        out_specs=tuple(out_specs),
        input_output_aliases={t: len(out_sems) + t for t in range(n_through)},
        compiler_params=pltpu.CompilerParams(has_side_effects=SPLIT_EFFECT),
    )(*[_hbm(a) for a in arrays], *sems, after)
    n_s = len(out_sems)
    return tuple(out[:n_s]), tuple(out[n_s:n_s + n_through]), (out[-1] if token else None)


def _scatter_copies(src_ref, land_ref, sems, R):
    send_sems, recv_sems, local_sem = sems
    x, y, c = _my_place()
    local = pltpu.make_async_copy(_row_block(src_ref, 4 * x + 2 * y + c, R), land_ref.at[0], local_sem.at[0])
    remote = []
    for k in range(1, N_DEV):
        px, py, pc = _peer(x, y, c, k)
        remote.append(_remote_pair(_row_block(src_ref, 4 * px + 2 * py + pc, R), land_ref.at[k], land_ref.at[k],
                                   send_sems.at[k - 1], recv_sems.at[k - 1], (px, py, pc)))
    return local, remote


def _scatter_start(g, R, after, *, name):
    land = lax.empty((N_DEV, R, g.shape[-1]), g.dtype)

    def body(src_ref, land_ref, after_ref, send_sems, recv_sems, local_sem, src_thru, land_thru, token):
        local, remote = _scatter_copies(src_ref, land_ref, (send_sems, recv_sems, local_sem), R)
        local.start()
        for send, _ in remote:
            send.start()
        token[...] = jnp.zeros_like(token)

    sems, (src_thru, land_thru), token = _split_call(body, [g, land], [], 2, (N_PEERS, N_PEERS, 1), after, True, name=name)
    return sems, src_thru, land_thru, token, R


def _scatter_finish(handle, after, *, name):
    sems, src_thru, land_thru, _, R = handle

    def body(src_ref, land_ref, send_sems, recv_sems, local_sem, after_ref, src_dead, got_ref):
        local, remote = _scatter_copies(src_ref, land_ref, (send_sems, recv_sems, local_sem), R)
        for send, recv in remote:
            send.wait_send()
            recv.wait_recv()
        local.wait()

    return _split_call(body, [src_thru, land_thru], sems, 2, (), after, False, name=name)[1][1]


def _gather_copies(src_ref, land_ref, sems, R):
    send_sems, recv_sems, local_sem = sems
    x, y, c = _my_place()
    mine = _row_block(land_ref, 4 * x + 2 * y + c, R)
    local = pltpu.make_async_copy(src_ref, mine, local_sem.at[0])
    peers = [(x, y, 1 - c), (1 - x, y, c), (x, 1 - y, c), (1 - x, 1 - y, c)]
    remote = [_remote_pair(src_ref, mine, _row_block(land_ref, 4 * px + 2 * py + pc, R),
                           send_sems.at[k], recv_sems.at[k], (px, py, pc)) for k, (px, py, pc) in enumerate(peers)]
    return local, remote


def _forward_copies(land_ref, sems, R):
    send_sems, recv_sems = sems
    x, y, c = _my_place()
    pairs = []
    for j, (px, py) in enumerate([(1 - x, y), (x, 1 - y), (1 - x, 1 - y)]):
        blk = _row_block(land_ref, 4 * px + 2 * py + c, R)
        pairs.append(_remote_pair(blk, blk, _row_block(land_ref, 4 * px + 2 * py + (1 - c), R),
                                  send_sems.at[j], recv_sems.at[j], (x, y, 1 - c)))
    return pairs


def _gather_start(shard, after, *, pad_rows=0, name):
    R, C = shard.shape
    flat = R % BF16_TILE_ROWS == 0
    assert flat or pad_rows == 0
    land = lax.empty((N_DEV * R + pad_rows, C) if flat else (N_DEV, R, C), shard.dtype)
    if pad_rows:
        land = lax.dynamic_update_slice(land, jnp.zeros((pad_rows, C), shard.dtype), (N_DEV * R, 0))

    def body(src_ref, land_ref, after_ref, send_sems, recv_sems, local_sem, src_thru, land_thru, token):
        local, remote = _gather_copies(src_ref, land_ref, (send_sems, recv_sems, local_sem), R)
        local.start()
        for send, _ in remote:
            send.start()
        token[...] = jnp.zeros_like(token)

    sems, (src_thru, land_thru), token = _split_call(body, [shard, land], [], 2, (4, 4, 1), after, True, name=name)
    return sems, src_thru, land_thru, token, R


def _gather_forward(handle, after, *, name):
    sems, src_thru, land_thru, _, R = handle

    def arrived(land_ref, src_ref, send_sems, recv_sems, local_sem, after_ref, land_out):
        _, remote = _gather_copies(src_ref, land_ref, (send_sems, recv_sems, local_sem), R)
        for _, recv in remote[1:]:
            recv.wait_recv()

    def pass_on(land_ref, after_ref, send_sems, recv_sems, land_out, token):
        for send, _ in _forward_copies(land_ref, (send_sems, recv_sems), R):
            send.start()
        token[...] = jnp.zeros_like(token)

    _, (land1,), _ = _split_call(arrived, [land_thru, src_thru], sems, 1, (), after, False, name=name + "_arrived")
    sems2, (land2,), token = _split_call(pass_on, [land1], [], 1, (3, 3), src_thru, True, name=name + "_pass_on")
    return sems, sems2, src_thru, land2, token, R


def _gather_finish(handle2, after, *, name):
    sems, sems2, src_thru, land_thru, _, R = handle2

    def body(land_ref, src_ref, send_sems, recv_sems, local_sem, send2, recv2, after_ref, land_out):
        local, remote = _gather_copies(src_ref, land_ref, (send_sems, recv_sems, local_sem), R)
        for send, _ in remote:
            send.wait_send()
        remote[0][1].wait_recv()
        local.wait()
        for send, recv in _forward_copies(land_ref, (send2, recv2), R):
            send.wait_send()
            recv.wait_recv()

    return _split_call(body, [land_thru, src_thru], [*sems, *sems2], 1, (), after, False, name=name)[1][0]


def _all_reduce_small(v, *, name):
    rows, C = v.shape

    def body(v_ref, o_ref, land_ref, send_sems, recv_sems):
        x, y, c = _my_place()
        me = 4 * x + 2 * y + c
        copies = []
        for k in range(1, N_DEV):
            peer = (_flip(x, k & 4), _flip(y, k & 2), _flip(c, k & 1))
            copies.append(pltpu.make_async_remote_copy(
                src_ref=v_ref, dst_ref=land_ref.at[me],
                send_sem=send_sems.at[k - 1], recv_sem=recv_sems.at[k - 1], device_id=peer, device_id_type=MESH))
        for cp in copies:
            cp.start()
        land_ref[me] = v_ref[...]
        for k in range(1, N_DEV):
            peer_idx = 4 * _flip(x, k & 4) + 2 * _flip(y, k & 2) + _flip(c, k & 1)
            pltpu.make_async_remote_copy(
                src_ref=v_ref, dst_ref=land_ref.at[peer_idx],
                send_sem=send_sems.at[k - 1], recv_sem=recv_sems.at[k - 1],
                device_id=(x, y, c), device_id_type=MESH).wait_recv()
        for cp in copies:
            cp.wait_send()
        acc = land_ref[0]
        for s in range(1, N_DEV):
            acc = acc + land_ref[s]
        o_ref[...] = acc

    vm = pl.BlockSpec(memory_space=pltpu.VMEM)
    return _comm_call(
        body, name=name,
        out_shape=jax.ShapeDtypeStruct((rows, C), F32),
        in_specs=[vm], out_specs=vm,
        scratch_shapes=[pltpu.VMEM((N_DEV, rows, C), F32),
                        pltpu.SemaphoreType.DMA((7,)), pltpu.SemaphoreType.DMA((7,))],
    )(v)


def kernel(x, norm1_g, w_in, b_forget, attn_sinks, rel_bias, w_branch_a, w_branch_b, w_out, norm2_g, w_ffn_gate, w_ffn_up, w_ffn_down, final_g, loss_target, m_norm1_g, m_w_in, m_b_forget, m_attn_sinks, m_rel_bias, m_w_branch_a, m_w_branch_b, m_w_out, m_norm2_g, m_w_ffn_gate, m_w_ffn_up, m_w_ffn_down, m_final_g, v_norm1_g, v_w_in, v_b_forget, v_attn_sinks, v_rel_bias, v_w_branch_a, v_w_branch_b, v_w_out, v_norm2_g, v_w_ffn_gate, v_w_ffn_up, v_w_ffn_down, v_final_g):
    xs = x[0]
    T, D = xs.shape
    H_A, H_B = b_forget.shape[-1], attn_sinks.shape[-1]
    WA, QB = H_A * DH_A, H_B * DH_B
    R_IN = w_in.shape[-1]
    W_IN = N_DEV * R_IN
    KB = (W_IN - 3 * WA - H_A - QB - 2 * D) // 2
    HKV = KB // DH_B
    G = H_B // HKV
    N_BIG = W_IN - H_A
    GATE_COL = 3 * WA + QB + 2 * KB
    R_F = w_ffn_gate.shape[-1]
    F = N_DEV * R_F
    FP = _round_up(F, 512)
    assert H_A <= LANES and T % SWA_BLOCK == 0

    win_s = w_in[0].T.astype(BF16)
    wa_s = w_branch_a[0].T.astype(BF16)
    wb_s = w_branch_b[0].T.astype(BF16)
    wout_s = w_out[0].astype(BF16)
    wg_s = w_ffn_gate[0].T.astype(BF16)
    wu_s = w_ffn_up[0].T.astype(BF16)
    wd_s = w_ffn_down[0].astype(BF16)

    ag_win = _gather_start(win_s, norm1_g, name="ag_start_w_in")
    ag_wa = _gather_start(wa_s, ag_win[3], name="ag_start_w_branch_a")
    ag_wb = _gather_start(wb_s, ag_wa[3], name="ag_start_w_branch_b")
    ag_wout = _gather_start(wout_s, ag_wb[3], name="ag_start_w_out")
    ag_wg = _gather_start(wg_s, ag_wout[3], pad_rows=FP - F, name="ag_start_w_ffn_gate")
    ag_wu = _gather_start(wu_s, ag_wg[3], pad_rows=FP - F, name="ag_start_w_ffn_up")
    ag_wd = _gather_start(wd_s, ag_wu[3], pad_rows=FP - F, name="ag_start_w_ffn_down")

    BM = 1024
    bn_big = _pick(N_BIG, 768, LANES)

    h1, rstd1 = _rms_fwd(xs, norm1_g, deps=[ag_wd[3]], name="rms1_fwd")
    fw_win = _gather_forward(ag_win, h1, name="ag_w_in")
    win_all = _gather_finish(fw_win, fw_win[4], name="ag_wait_w_in").reshape(W_IN, D)
    f_rows = (3 * WA, H_A)
    w_f = jnp.pad(win_all[3 * WA:3 * WA + H_A], ((0, LANES - H_A), (0, 0)))
    proj = _matmul([(h1, win_all)], "nt", bm=BM, bn=bn_big, b_hole=f_rows, name="proj_fwd")
    f_logit = _matmul([(h1, w_f)], "nt", bm=BM, bn=LANES, out_dtype=F32, name="forget_fwd")
    b_pad = jnp.pad(b_forget, ((0, 0), (0, LANES - H_A)))
    c_all, sneg = _fox_gate_fwd(f_logit, b_pad, name="fox_gate_fwd")
    c_heads = c_all[:, :H_A].T
    c_col, c_row = c_heads[:, :, None], c_heads[:, None, :]
    fw_wa = _gather_forward(ag_wa, proj, name="ag_w_branch_a")
    fw_wb = _gather_forward(ag_wb, fw_wa[4], name="ag_w_branch_b")
    fw_wout = _gather_forward(ag_wout, fw_wb[4], name="ag_w_out")
    oa, lse_a = _fox_fwd(proj, c_col, c_row, H_A, deps=[fw_wout[4]], name="fox_fwd")
    fw_wg = _gather_forward(ag_wg, oa, name="ag_w_ffn_gate")
    fw_wu = _gather_forward(ag_wu, fw_wg[4], name="ag_w_ffn_up")

    def heads_q(a):
        return a.reshape(T, HKV, G, DH_B).transpose(1, 2, 0, 3)

    def heads_kv(a):
        return a.reshape(T, HKV, DH_B).transpose(1, 0, 2)

    qb = heads_q(proj[:, 3 * WA:3 * WA + QB])
    kb = heads_kv(proj[:, 3 * WA + QB:3 * WA + QB + KB])
    vb = heads_kv(proj[:, 3 * WA + QB + KB:GATE_COL])
    onehot = _bucket_onehot()
    bias = _bias_table(rel_bias, onehot, name="rel_bias_table").reshape(HKV, G, 2, SWA_BLOCK, SWA_BLOCK)
    sinks_b = jnp.broadcast_to(attn_sinks.reshape(HKV, G, 1, 1), (HKV, G, 1, LANES))
    ob4, lse_b = _swa_fwd(qb, kb, vb, bias, sinks_b, deps=[fw_wu[4]], name="swa_fwd")
    ob = ob4.transpose(2, 0, 1, 3).reshape(T, QB)

    wa_t = _gather_finish(fw_wa, ob, name="ag_wait_w_branch_a")
    wb_t = _gather_finish(fw_wb, wa_t, name="ag_wait_w_branch_b")
    mixed, ya, yb = _branch_mix(oa, ob, wa_t, wb_t, proj, GATE_COL, name="branch_mix")
    fw_wd = _gather_forward(ag_wd, mixed, name="ag_w_ffn_down")
    wout = _gather_finish(fw_wout, fw_wd[4], name="ag_wait_w_out")
    x1 = _matmul([(mixed, wout)], "nn", bm=BM, bn=512, out_dtype=F32, addend=xs, name="out_proj_fwd")

    h2, rstd2 = _rms_fwd(x1, norm2_g, name="rms2_fwd")
    wg_t = _gather_finish(fw_wg, h2, name="ag_wait_w_ffn_gate")
    wu_t = _gather_finish(fw_wu, wg_t, name="ag_wait_w_ffn_up")
    gate, up, hidden = _ffn_up(h2, wg_t, wu_t, name="ffn_up")
    wd = _gather_finish(fw_wd, hidden, name="ag_wait_w_ffn_down")
    tk_f = _pick(FP, 2816, LANES)
    x2 = _matmul([(hidden, wd)], "nn", bm=BM, bn=1024, tk=tk_f, out_dtype=F32, addend=x1, name="ffn_down_fwd")

    dx2, loss_part, g_final, dx2_b = _final_loss(x2, final_g.reshape(1, D), loss_target[0], name="final_loss")
    loss = lax.psum(loss_part[0, 0], MESH_AXES)

    g_wd = _matmul([(hidden, dx2_b)], "tn", bm=1024, bn=1024, name="ffn_down_bwd_w")
    rs_wd = _scatter_start(g_wd, R_F, loss_part, name="rs_start_w_ffn_down")
    dgate, dup = _matmul([(dx2_b, wd)], "nt", bm=512, bn=1024, extras=[(gate, 0), (up, 0)], epilogue=_swiglu_bwd_tile,
                         n_out=2, deps=[rs_wd[3]], name="ffn_down_bwd_x")
    g_wg = _matmul([(dgate, h2)], "tn", bm=1024, bn=1024, name="ffn_gate_bwd_w")
    rs_wg = _scatter_start(g_wg, R_F, rs_wd[3], name="rs_start_w_ffn_gate")
    g_wu = _matmul([(dup, h2)], "tn", bm=1024, bn=1024, deps=[rs_wg[3]], name="ffn_up_bwd_w")
    rs_wu = _scatter_start(g_wu, R_F, rs_wg[3], name="rs_start_w_ffn_up")
    dh2 = _matmul([(dgate, wg_t), (dup, wu_t)], "nn", bm=BM, bn=512, tk=_pick(FP, 2816, LANES), out_dtype=F32,
                  deps=[rs_wu[3]], name="ffn_up_bwd_x")
    dx1, g_norm2, dx1_b = _rms_bwd(dx2, dh2, x1, rstd2, norm2_g, name="rms2_bwd")

    g_wout = _matmul([(mixed, dx1_b)], "tn", bm=1024, bn=1024, name="out_proj_bwd_w")
    rs_wout = _scatter_start(g_wout, D // N_DEV, rs_wu[3], name="rs_start_w_out")
    bn_mix = _pick(D, 256, LANES)
    assert GATE_COL % bn_mix == 0
    dya, dyb, dga, dgb = _matmul(
        [(dx1_b, wout)], "nt", bm=BM, bn=bn_mix, epilogue=_mix_bwd_tile, n_out=4, deps=[rs_wout[3]],
        extras=[(ya, 0), (yb, 0), (proj, GATE_COL // bn_mix), (proj, (GATE_COL + D) // bn_mix)], name="out_proj_bwd_x")
    g_wa = _matmul([(dya, oa)], "tn", bm=1024, bn=1024, name="branch_a_bwd_w")
    rs_wa = _scatter_start(g_wa, D // N_DEV, rs_wout[3], name="rs_start_w_branch_a")
    g_wb = _matmul([(dyb, ob)], "tn", bm=1024, bn=1024, deps=[rs_wa[3]], name="branch_b_bwd_w")
    rs_wb = _scatter_start(g_wb, D // N_DEV, rs_wa[3], name="rs_start_w_branch_b")
    doa = _matmul([(dya, wa_t)], "nn", bm=BM, bn=1024, deps=[rs_wb[3]], name="branch_a_bwd_x")
    dob = _matmul([(dyb, wb_t)], "nn", bm=BM, bn=1024, name="branch_b_bwd_x")

    dqa, dka, dva, dc_key, dc_query = _fox_bwd(proj, oa, doa, lse_a, c_col, c_row, H_A, name="fox_bwd")
    dc = jnp.pad((dc_key[:, 0, :] + dc_query[:, :, 0]).T, ((0, 0), (0, LANES - H_A)))
    df, g_bf = _fox_gate_bwd(dc, sneg, name="fox_gate_bwd")

    dqb4, dkb3, dvb3, dbias, dsinks = _swa_bwd(qb, kb, vb, ob4, heads_q(dob), lse_b, bias, sinks_b, name="swa_bwd")
    g_relb = _bias_table_bwd(dbias.reshape(H_B, -1), onehot, name="rel_bias_table_bwd")
    dqb = dqb4.transpose(2, 0, 1, 3).reshape(T, QB)
    dkb = dkb3.transpose(1, 0, 2).reshape(T, KB)
    dvb = dvb3.transpose(1, 0, 2).reshape(T, KB)

    dproj = jnp.concatenate([dqa.astype(BF16), dka, dva, dqb, dkb, dvb, dga, dgb], axis=1)
    g_win = _matmul([(dproj, h1)], "tn", bm=bn_big, bn=1024, out_hole=f_rows, name="proj_bwd_w")
    g_wf = _matmul([(df, h1)], "tn", bm=LANES, bn=1024, name="forget_bwd_w")
    g_win = lax.dynamic_update_slice(g_win, g_wf[:H_A], (3 * WA, 0)).reshape(N_DEV, R_IN, D)
    rs_win = _scatter_start(g_win, R_IN, rs_wb[3], name="rs_start_w_in")
    dh_f = _matmul([(df, w_f)], "nn", bm=BM, bn=1024, out_dtype=F32, deps=[rs_win[3]], name="forget_bwd_x")
    tk_big = _pick(math.gcd(3 * WA, N_BIG - 3 * WA), 1536, LANES)
    dh1 = _matmul([(dproj, win_all)], "nn", bm=BM, bn=1024, tk=tk_big, out_dtype=F32, addend=dh_f, b_hole=f_rows,
                  name="proj_bwd_x")
    grad_x, g_norm1, _ = _rms_bwd(dx1, dh1, xs, rstd1, norm1_g, name="rms1_bwd")

    def update(handle, after, w, m, v, transposed, nm):
        take = (lambda a: a[0].T) if transposed else (lambda a: a[0])
        give = (lambda a: a.T[None]) if transposed else (lambda a: a[None])
        land = _scatter_finish(handle, after, name="rs_wait_" + nm)
        return tuple(give(o) for o in _sum_adamw(land, take(w), take(m), take(v), name="adamw_" + nm))

    big = {}
    big["w_ffn_down"] = update(rs_wd, grad_x, w_ffn_down, m_w_ffn_down, v_w_ffn_down, False, "w_ffn_down")
    big["w_ffn_gate"] = update(rs_wg, big["w_ffn_down"][1], w_ffn_gate, m_w_ffn_gate, v_w_ffn_gate, True, "w_ffn_gate")
    big["w_ffn_up"] = update(rs_wu, big["w_ffn_gate"][1], w_ffn_up, m_w_ffn_up, v_w_ffn_up, True, "w_ffn_up")
    big["w_out"] = update(rs_wout, big["w_ffn_up"][1], w_out, m_w_out, v_w_out, False, "w_out")
    big["w_branch_a"] = update(rs_wa, big["w_out"][1], w_branch_a, m_w_branch_a, v_w_branch_a, True, "w_branch_a")
    big["w_branch_b"] = update(rs_wb, big["w_branch_a"][1], w_branch_b, m_w_branch_b, v_w_branch_b, True, "w_branch_b")
    big["w_in"] = update(rs_win, big["w_branch_b"][1], w_in, m_w_in, v_w_in, True, "w_in")

    n_rb = NUM_BUCKETS * H_B
    assert D >= n_rb and D >= H_A + H_B
    small = jnp.concatenate([
        g_norm1, g_norm2, g_final,
        jnp.pad(jnp.concatenate([g_bf[:, :H_A], dsinks[:, :, 0, 0].reshape(1, H_B)], axis=1), ((0, 0), (0, D - H_A - H_B))),
        jnp.pad(g_relb.reshape(1, n_rb), ((0, 0), (0, D - n_rb))),
        jnp.zeros((3, D), F32)], axis=0)
    small = _all_reduce_small(small, name="ar_small")
    gs = {
        "norm1_g": small[0:1], "norm2_g": small[1:2], "final_g": small[2],
        "b_forget": small[3:4, :H_A], "attn_sinks": small[3:4, H_A:H_A + H_B],
        "rel_bias": small[4, :n_rb].reshape(NUM_BUCKETS, H_B),
    }

    def adam_small(ws, gsm, ms, vs):
        def pack(parts):
            rows = [jnp.pad(p.reshape(1, -1), ((0, 0), (0, D - p.size))) for p in parts]
            return jnp.concatenate(rows + [jnp.ones((8 - len(rows), D), F32)], axis=0)
        d, nm_, nv_ = _adamw(pack(ws), pack(gsm), pack(ms), pack(vs), name="adamw_small")
        unpack = lambda a: [a[i, :p.size].reshape(p.shape) for i, p in enumerate(ws)]
        return unpack(d), unpack(nm_), unpack(nv_)

    small_names = ["norm1_g", "b_forget", "attn_sinks", "rel_bias", "norm2_g", "final_g"]
    small_w = [norm1_g, b_forget, attn_sinks, rel_bias, norm2_g, final_g]
    small_g = [gs[n].reshape(w.shape) for n, w in zip(small_names, small_w)]
    small_m = [m_norm1_g, m_b_forget, m_attn_sinks, m_rel_bias, m_norm2_g, m_final_g]
    small_v = [v_norm1_g, v_b_forget, v_attn_sinks, v_rel_bias, v_norm2_g, v_final_g]
    sd, sm, sv = adam_small(small_w, small_g, small_m, small_v)
    for i, n in enumerate(small_names):
        big[n] = (small_g[i], sd[i], sm[i], sv[i])

    order = ["norm1_g", "w_in", "b_forget", "attn_sinks", "rel_bias", "w_branch_a", "w_branch_b", "w_out",
             "norm2_g", "w_ffn_gate", "w_ffn_up", "w_ffn_down", "final_g"]
    grads = [big[n][0] for n in order]
    deltas = [big[n][1] for n in order]
    new_m = [big[n][2] for n in order]
    new_v = [big[n][3] for n in order]
    return (loss, grad_x[None], *grads, *deltas, *new_m, *new_v)
```
